```python
import math
import jax, jax.numpy as jnp
from jax import lax
import numpy as np

D_MODEL = 1024
BATCH = 2
SEQ = 8192
DEPTH = 4

HEAD_DIM = 64
N_HEADS = D_MODEL // HEAD_DIM
A_GROUPS = ((128, 1), (512, 4), (2048, 16))
A_HEADS_PER_GROUP = N_HEADS // 4
A_HEADS = A_HEADS_PER_GROUP * len(A_GROUPS)
B_HEADS = N_HEADS - A_HEADS
GRID_W = 64
NA_ROWS_MAX = 8
NA_COLS = 16
NA_QCOLS = 16
T5_BUCKETS = 32
T5_MAX_DIST = 1024
CONV_WIDTH = 31
N_EXPERTS = 16
N_GROUPS = 4
TOP_K = 2
D_EXPERT = 512
ALPHA = (2 * DEPTH) ** 0.25
BETA = (8 * DEPTH) ** -0.25
LN_EPS = 1e-5
NEG = -1e30

kernel_name = "hybrid_dilated_natten_conformer_moe_encoder"


def layer_norm(x, g, b):
    xf = x.astype(jnp.float32)
    mu = xf.mean(-1, keepdims=True)
    var = jnp.square(xf - mu).mean(-1, keepdims=True)
    return ((xf - mu) * lax.rsqrt(var + LN_EPS) * g + b).astype(x.dtype)


def t5_bucket(rel):
    nb = T5_BUCKETS // 2
    max_exact = nb // 2
    ret = jnp.where(rel > 0, nb, 0)
    n = jnp.abs(rel)
    nf = jnp.maximum(n, 1).astype(jnp.float32)
    large = max_exact + (jnp.log(nf / max_exact) / math.log(T5_MAX_DIST / max_exact)
                         * (nb - max_exact)).astype(jnp.int32)
    large = jnp.minimum(large, nb - 1)
    return ret + jnp.where(n < max_exact, n, large)


def dilated_attention(q, k, v, bias_tab, window, dil):
    b, h, s, dh = q.shape
    n_side = window // (2 * dil)
    qb = n_side
    L = s // dil
    nblk = -(-L // qb)
    Lp = nblk * qb

    def strided(t):
        return t.reshape(b, h, L, dil, dh).transpose(0, 1, 3, 2, 4)

    qs = jnp.pad(strided(q), ((0, 0),) * 3 + ((0, Lp - L), (0, 0)))
    qs = qs.reshape(b, h, dil, nblk, qb, dh)
    pad_kv = ((0, 0),) * 3 + ((qb, Lp - L + qb), (0, 0))

    def band(t):
        t = jnp.pad(strided(t), pad_kv).reshape(b, h, dil, nblk + 2, qb, dh)
        return jnp.concatenate([t[:, :, :, :-2], t[:, :, :, 1:-1], t[:, :, :, 2:]], axis=4)

    kb, vb = band(k), band(v)
    rel = jnp.arange(3 * qb)[None, :] - qb - jnp.arange(qb)[:, None]
    band_ok = jnp.abs(rel) <= n_side
    bias = jnp.moveaxis(bias_tab[t5_bucket(rel * dil)], -1, 0)
    kpos = jnp.arange(nblk)[:, None] * qb - qb + jnp.arange(3 * qb)[None, :]
    in_range = (kpos >= 0) & (kpos < L)
    valid = band_ok[None] & in_range[:, None, :]
    logits = jnp.einsum('bhrnqd,bhrnkd->bhrnqk', qs, kb).astype(jnp.float32)
    logits = jnp.where(valid, logits + bias[None, :, None, None], NEG)
    lse = jax.nn.logsumexp(logits, axis=-1)
    p = jnp.exp(logits - lse[..., None]).astype(v.dtype)
    out = jnp.einsum('bhrnqk,bhrnkd->bhrnqd', p, vb)
    out = out.reshape(b, h, dil, Lp, dh)[:, :, :, :L].transpose(0, 1, 3, 2, 4).reshape(b, h, s, dh)
    lse = lse.reshape(b, h, dil, Lp)[:, :, :, :L].transpose(0, 1, 3, 2).reshape(b, h, s)
    return out, lse


def neighborhood_attention(q, k, v, rpb):
    b, h, s, dh = q.shape
    rows = s // GRID_W
    kh = min(NA_ROWS_MAX, rows)
    n_cb = GRID_W // NA_QCOLS
    kspan = NA_QCOLS + NA_COLS
    r = jnp.arange(rows)
    row_idx = jnp.clip(r - kh // 2, 0, rows - kh)[:, None] + jnp.arange(kh)[None, :]
    cb = jnp.arange(n_cb)
    col_idx = (jnp.clip(cb * NA_QCOLS - NA_COLS // 2, 0, GRID_W - kspan)[:, None]
               + jnp.arange(kspan)[None, :])
    qcol = cb[:, None] * NA_QCOLS + jnp.arange(NA_QCOLS)[None, :]
    win0 = jnp.clip(qcol - NA_COLS // 2, 0, GRID_W - NA_COLS)[..., None]
    kc = col_idx[:, None, :]
    col_ok = (kc >= win0) & (kc < win0 + NA_COLS)
    dc = kc - qcol[..., None]
    dr = row_idx - r[:, None]

    def gather(t):
        g = t.reshape(b, h, rows, GRID_W, dh)[:, :, row_idx]
        g = g[:, :, :, :, col_idx]
        return g.transpose(0, 1, 2, 4, 3, 5, 6).reshape(b, h, rows, n_cb, kh * kspan, dh)

    kg, vg = gather(k), gather(v)
    qg = q.reshape(b, h, rows, n_cb, NA_QCOLS, dh)
    bias = rpb[:, (dr + NA_ROWS_MAX - 1)[:, None, None, :, None],
               jnp.clip(dc + NA_COLS - 1, 0, 2 * NA_COLS - 2)[None, :, :, None, :]]
    bias = bias.reshape(h, rows, n_cb, NA_QCOLS, kh * kspan)
    valid = jnp.broadcast_to(col_ok[:, :, None, :], (n_cb, NA_QCOLS, kh, kspan)).reshape(n_cb, NA_QCOLS, kh * kspan)
    logits = jnp.einsum('bhrcqd,bhrckd->bhrcqk', qg, kg).astype(jnp.float32)
    logits = jnp.where(valid, logits + bias[None], NEG)
    p = jax.nn.softmax(logits, axis=-1).astype(v.dtype)
    out = jnp.einsum('bhrcqk,bhrckd->bhrcqd', p, vg)
    return out.reshape(b, h, s, dh)


def attention_mixer(x, w_in, w_out, rel_bias, rpb):
    b, s, _ = x.shape
    da, db = A_HEADS * HEAD_DIM, B_HEADS * HEAD_DIM
    proj = x @ w_in
    qa, ka, va, qn, kn, vn = jnp.split(proj, [da, 2 * da, 3 * da, 3 * da + db, 3 * da + 2 * db], axis=-1)
    scale = HEAD_DIM ** -0.5

    def heads(t):
        return t.reshape(b, s, -1, HEAD_DIM).transpose(0, 2, 1, 3)

    qa, ka, va = heads(qa) * scale, heads(ka), heads(va)
    outs, lses = [], []
    for g, (window, dil) in enumerate(A_GROUPS):
        sl = slice(g * A_HEADS_PER_GROUP, (g + 1) * A_HEADS_PER_GROUP)
        o, l = dilated_attention(qa[:, sl], ka[:, sl], va[:, sl], rel_bias[:, sl], window, dil)
        outs.append(o)
        lses.append(l)
    wts = jax.nn.softmax(jnp.stack(lses), axis=0)
    oa = jnp.concatenate([o * wts[g][..., None].astype(o.dtype) for g, o in enumerate(outs)], axis=1)
    ob = neighborhood_attention(heads(qn) * scale, heads(kn), heads(vn), rpb)
    o = jnp.concatenate([oa, ob], axis=1).transpose(0, 2, 1, 3).reshape(b, s, D_MODEL)
    return o @ w_out


def conv_module(x, w_pw1, b_pw1, w_dw, b_dw, g, bb, w_pw2, b_pw2):
    u = x @ w_pw1 + b_pw1
    a, gate = jnp.split(u, 2, axis=-1)
    u = a * jax.nn.sigmoid(gate)
    u = lax.conv_general_dilated(u, w_dw[:, None, :], window_strides=(1,),
                                 padding=[(CONV_WIDTH // 2, CONV_WIDTH // 2)],
                                 dimension_numbers=('NWC', 'WIO', 'NWC'),
                                 feature_group_count=D_MODEL) + b_dw
    u = jax.nn.silu(layer_norm(u, g, bb))
    return u @ w_pw2 + b_pw2


def moe(x, w_router, router_bias, w_gate, w_up, w_down):
    b, s, d = x.shape
    xt = x.reshape(-1, d)
    scores = jax.nn.sigmoid((xt @ w_router).astype(jnp.float32))
    sel = scores + router_bias.astype(jnp.float32)
    epg = N_EXPERTS // N_GROUPS
    grp = lax.top_k(sel.reshape(-1, N_GROUPS, epg), TOP_K)[0].sum(-1)
    g_idx = jnp.argmax(grp, axis=-1)
    in_grp = (jnp.arange(N_EXPERTS) // epg)[None, :] == g_idx[:, None]
    _, e_idx = lax.top_k(jnp.where(in_grp, sel, NEG), TOP_K)
    w = jnp.take_along_axis(scores, e_idx, axis=-1)
    w = w / w.sum(-1, keepdims=True)
    gates = (jax.nn.one_hot(e_idx, N_EXPERTS, dtype=jnp.float32) * w[..., None]).sum(1).astype(x.dtype)
    y = jnp.zeros_like(xt)
    for e in range(N_EXPERTS):
        hid = jax.nn.silu(xt @ w_gate[e]) * (xt @ w_up[e])
        y = y + gates[:, e:e + 1] * (hid @ w_down[e])
    return y.reshape(b, s, d)


def setup_inputs(seed: int = 0) -> dict:
    key = jax.random.key(seed)
    ks = jax.random.split(key, 24)
    f32 = jnp.float32
    d = D_MODEL
    n_even = (DEPTH + 1) // 2
    n_odd = DEPTH // 2

    def nrm(k, shape, scale):
        return jax.random.normal(k, shape, f32) * scale

    return {
        'x': nrm(ks[0], (BATCH, SEQ, d), 1.0),
        'w_in_attn': nrm(ks[1], (n_even, d, 3 * d), d ** -0.5),
        'w_out_attn': nrm(ks[2], (n_even, d, d), d ** -0.5 * BETA),
        'rel_bias': nrm(ks[3], (T5_BUCKETS, A_HEADS), 0.2),
        'rpb_2d': nrm(ks[4], (n_even, B_HEADS, 2 * NA_ROWS_MAX - 1, 2 * NA_COLS - 1), 0.2),
        'w_pw1': nrm(ks[5], (n_odd, d, 2 * d), d ** -0.5),
        'b_pw1': nrm(ks[6], (n_odd, 2 * d), 0.02),
        'w_dw': nrm(ks[7], (n_odd, CONV_WIDTH, d), CONV_WIDTH ** -0.5),
        'b_dw': nrm(ks[8], (n_odd, d), 0.02),
        'conv_ln_g': 1.0 + nrm(ks[9], (n_odd, d), 0.05),
        'conv_ln_b': nrm(ks[10], (n_odd, d), 0.02),
        'w_pw2': nrm(ks[11], (n_odd, d, d), d ** -0.5 * BETA),
        'b_pw2': nrm(ks[12], (n_odd, d), 0.02),
        'ln_mix_g': 1.0 + nrm(ks[13], (DEPTH, d), 0.05),
        'ln_mix_b': nrm(ks[14], (DEPTH, d), 0.02),
        'ln_ffn_g': 1.0 + nrm(ks[15], (DEPTH, d), 0.05),
        'ln_ffn_b': nrm(ks[16], (DEPTH, d), 0.02),
        'w_router': nrm(ks[17], (d, N_EXPERTS), d ** -0.5),
        'router_bias': nrm(ks[18], (N_EXPERTS,), 0.01),
        'w_gate': nrm(ks[19], (DEPTH, N_EXPERTS, d, D_EXPERT), d ** -0.5),
        'w_up': nrm(ks[20], (DEPTH, N_EXPERTS, d, D_EXPERT), d ** -0.5),
        'w_down': nrm(ks[21], (DEPTH, N_EXPERTS, D_EXPERT, d), D_EXPERT ** -0.5 * BETA),
    }


def reference(x, w_in_attn, w_out_attn, rel_bias, rpb_2d, w_pw1, b_pw1, w_dw, b_dw,
              conv_ln_g, conv_ln_b, w_pw2, b_pw2, ln_mix_g, ln_mix_b, ln_ffn_g, ln_ffn_b,
              w_router, router_bias, w_gate, w_up, w_down):
    for layer in range(DEPTH):
        i = layer // 2
        if layer % 2 == 0:
            m = attention_mixer(x, w_in_attn[i], w_out_attn[i], rel_bias, rpb_2d[i])
        else:
            m = conv_module(x, w_pw1[i], b_pw1[i], w_dw[i], b_dw[i], conv_ln_g[i], conv_ln_b[i],
                            w_pw2[i], b_pw2[i])
        x = layer_norm(ALPHA * x + m, ln_mix_g[layer], ln_mix_b[layer])
        f = moe(x, w_router, router_bias, w_gate[layer], w_up[layer], w_down[layer])
        x = layer_norm(ALPHA * x + f, ln_ffn_g[layer], ln_ffn_b[layer])
    return x
```

```python
import functools
import math

import jax
import jax.numpy as jnp
import numpy as np
from jax import lax
from jax.experimental import pallas as pl
from jax.experimental.pallas import tpu as pltpu

D_MODEL = 1024
HEAD_DIM = 64
GROUP_HEADS = 4
GROUP_COLS = GROUP_HEADS * HEAD_DIM
DILATIONS = (1, 4, 16)
QUERY_BLOCK = 64
GRID_W = 64
NA_ROWS = 8
NA_COLS = 16
T5_BUCKETS = 32
T5_MAX_DIST = 1024
CONV_WIDTH = 31
CONV_HALO = 16
N_EXPERTS = 16
GROUP_EXPERTS = 4
D_EXPERT = 512
DEPTH = 4
ALPHA = (2 * DEPTH) ** 0.25
LN_EPS = 1e-5
NEG = -1e30

LANES = 128
ROW_CHUNKS = D_MODEL // LANES
GROUP_CHUNKS = GROUP_COLS // LANES
SEQ_TILE = 1024
EXPERT_TILE = 256
COMBINE_TILE = 256
VMEM_LIMIT = 56 * 1024 * 1024

F32 = jnp.float32
BF16 = jnp.bfloat16


def _params(semantics, vmem=VMEM_LIMIT):
    return pltpu.CompilerParams(dimension_semantics=semantics, vmem_limit_bytes=vmem)


def _matmul_kernel(x_ref, w_ref, o_ref):
    y = jnp.dot(x_ref[...].astype(BF16), w_ref[...], preferred_element_type=F32)
    for c in range(o_ref.shape[0]):
        o_ref[c] = y[:, c * LANES:(c + 1) * LANES]


def _matmul(x, w, tm=1024, tn=768):
    n, k = x.shape
    _, m = w.shape
    return pl.pallas_call(
        _matmul_kernel,
        grid=(n // tm, m // tn),
        in_specs=[pl.BlockSpec((tm, k), lambda i, j: (i, 0)),
                  pl.BlockSpec((k, tn), lambda i, j: (0, j))],
        out_specs=pl.BlockSpec((tn // LANES, tm, LANES), lambda i, j: (j, i, 0)),
        out_shape=jax.ShapeDtypeStruct((m // LANES, n, LANES), F32),
        compiler_params=_params(("parallel", "arbitrary")),
        name="qkv_proj",
    )(x, w)


def _glu_kernel(x_ref, wa_ref, wg_ref, ba_ref, bg_ref, o_ref):
    xb = x_ref[...].astype(BF16)
    a = jnp.dot(xb, wa_ref[...], preferred_element_type=F32) + ba_ref[...]
    g = jnp.dot(xb, wg_ref[...], preferred_element_type=F32) + bg_ref[...]
    o_ref[...] = a * jax.nn.sigmoid(g)


def _pw1_glu(x, w, b, tm=1024, tn=512):
    n, k = x.shape
    half = w.shape[1] // 2
    nj = half // tn
    return pl.pallas_call(
        _glu_kernel,
        grid=(n // tm, nj),
        in_specs=[pl.BlockSpec((tm, k), lambda i, j: (i, 0)),
                  pl.BlockSpec((k, tn), lambda i, j: (0, j)),
                  pl.BlockSpec((k, tn), lambda i, j: (0, j + nj)),
                  pl.BlockSpec((1, tn), lambda i, j: (0, j)),
                  pl.BlockSpec((1, tn), lambda i, j: (0, j + nj))],
        out_specs=pl.BlockSpec((tm, tn), lambda i, j: (i, j)),
        out_shape=jax.ShapeDtypeStruct((n, half), F32),
        compiler_params=_params(("parallel", "arbitrary")),
        name="pw1_glu",
    )(x, w, w, b, b)


def _head_masks(rows):
    lane = lax.broadcasted_iota(jnp.int32, (rows, GROUP_COLS), 1)
    return [(lane >= HEAD_DIM * h) & (lane < HEAD_DIM * (h + 1)) for h in range(GROUP_HEADS)]


def _stack_heads(q, masks):
    return jnp.concatenate([jnp.where(m, q, 0.0) for m in masks], axis=0)


def _unstack_heads(o4, masks, rows):
    out = jnp.where(masks[0], o4[0:rows], 0.0)
    for h in range(1, GROUP_HEADS):
        out = out + jnp.where(masks[h], o4[h * rows:(h + 1) * rows], 0.0)
    return out


def _softmax_pv(logits, v, with_lse):
    m = jnp.max(logits, axis=-1, keepdims=True)
    p = jnp.exp(logits - m)
    s = jnp.sum(p, axis=-1, keepdims=True)
    o4 = jnp.dot(p.astype(BF16), v, preferred_element_type=F32) * (1.0 / s)
    lse = (m + jnp.log(s)) if with_lse else None
    return o4, lse


def _dilated_kernel(q_ref, kp_ref, kc_ref, kn_ref, vp_ref, vc_ref, vn_ref, bias_ref,
                    o_ref, l_ref, *, dil):
    t = pl.program_id(1)
    chunk = QUERY_BLOCK * dil
    n_chunks = SEQ_TILE // chunk
    qb = QUERY_BLOCK
    masks = _head_masks(qb)
    col = lax.broadcasted_iota(jnp.int32, (GROUP_HEADS * qb, 3 * qb), 1)
    no_prev = jnp.where(col < qb, NEG, 0.0) * (t == 0).astype(F32)
    no_next = jnp.where(col >= 2 * qb, NEG, 0.0) * (t == pl.num_programs(1) - 1).astype(F32)
    bias = bias_ref[...]

    def sl(start):
        return pl.ds(start, qb) if dil == 1 else pl.ds(start, qb, stride=dil)

    def rows(ref, start):
        return jnp.concatenate([ref[c, sl(start), :] for c in range(GROUP_CHUNKS)], axis=-1)

    def window(p_ref, c_ref, n_ref, c, r):
        parts = []
        for cc in (c - 1, c, c + 1):
            if cc < 0:
                parts.append(rows(p_ref, r))
            elif cc >= n_chunks:
                parts.append(rows(n_ref, r))
            else:
                parts.append(rows(c_ref, cc * chunk + r))
        return jnp.concatenate(parts, axis=0).astype(BF16)

    for c in range(n_chunks):
        for r in range(dil):
            base = c * chunk + r
            q4 = _stack_heads(rows(q_ref, base), masks).astype(BF16)
            k = window(kp_ref, kc_ref, kn_ref, c, r)
            v = window(vp_ref, vc_ref, vn_ref, c, r)
            logits = lax.dot_general(q4, k, (((1,), (1,)), ((), ())),
                                     preferred_element_type=F32) + bias
            if c == 0:
                logits = logits + no_prev
            if c == n_chunks - 1:
                logits = logits + no_next
            o4, lse = _softmax_pv(logits, v, True)
            o = _unstack_heads(o4, masks, qb)
            lb = _unstack_heads(jnp.broadcast_to(lse, (GROUP_HEADS * qb, GROUP_COLS)), masks, qb)
            for ch in range(GROUP_CHUNKS):
                o_ref[ch, sl(base), :] = o[:, ch * LANES:(ch + 1) * LANES]
                l_ref[ch, sl(base), :] = lb[:, ch * LANES:(ch + 1) * LANES]


def _tile_spec(col_block, seq):
    tiles = seq // SEQ_TILE
    return pl.BlockSpec((GROUP_CHUNKS, SEQ_TILE, LANES), lambda b, t: (col_block, b * tiles + t, 0))


def _halo_specs(col_block, halo, seq):
    per_tile = SEQ_TILE // halo
    per_seq = seq // halo
    return [
        pl.BlockSpec((GROUP_CHUNKS, halo, LANES),
                     lambda b, t: (col_block, b * per_seq + jnp.maximum(t * per_tile - 1, 0), 0)),
        _tile_spec(col_block, seq),
        pl.BlockSpec((GROUP_CHUNKS, halo, LANES),
                     lambda b, t: (col_block, b * per_seq + jnp.minimum((t + 1) * per_tile, per_seq - 1), 0)),
    ]


def _dilated_attention(proj, bias, group, bsz, seq):
    dil = DILATIONS[group]
    n_groups = len(DILATIONS)
    q_col, k_col, v_col = group, n_groups + group, 2 * n_groups + group
    out_spec = _tile_spec(0, seq)
    out_sds = jax.ShapeDtypeStruct((GROUP_CHUNKS, bsz * seq, LANES), F32)
    return pl.pallas_call(
        functools.partial(_dilated_kernel, dil=dil),
        grid=(bsz, seq // SEQ_TILE),
        in_specs=[_tile_spec(q_col, seq)]
        + _halo_specs(k_col, QUERY_BLOCK * dil, seq) + _halo_specs(v_col, QUERY_BLOCK * dil, seq)
        + [pl.BlockSpec(bias.shape, lambda b, t: (0, 0))],
        out_specs=[out_spec, out_spec],
        out_shape=[out_sds, out_sds],
        compiler_params=_params(("parallel", "arbitrary")),
        name=f"dilated_attn_d{dil}",
    )(proj, proj, proj, proj, proj, proj, proj, bias)


NA_HALO = 256
NA_KEYS = NA_ROWS * GRID_W


def _na_kernel(q_ref, kp_ref, kc_ref, kn_ref, vp_ref, vc_ref, vn_ref, bias_ref, o_ref,
               kwin, vwin, *, grid_rows):
    t = pl.program_id(1)
    tile_rows = SEQ_TILE // GRID_W
    half = NA_ROWS // 2
    for win, p_ref, c_ref, n_ref in ((kwin, kp_ref, kc_ref, kn_ref), (vwin, vp_ref, vc_ref, vn_ref)):
        for ch in range(GROUP_CHUNKS):
            lanes = slice(ch * LANES, (ch + 1) * LANES)
            win[0:NA_HALO, lanes] = p_ref[ch].astype(BF16)
            win[NA_HALO:NA_HALO + SEQ_TILE, lanes] = c_ref[ch].astype(BF16)
            win[NA_HALO + SEQ_TILE:, lanes] = n_ref[ch].astype(BF16)
    masks = _head_masks(GRID_W)
    for i in range(tile_rows):
        row = t * tile_rows + i
        row0 = jnp.clip(row - half, 0, grid_rows - NA_ROWS)
        off = pl.multiple_of((row0 - (t * tile_rows - half)) * GRID_W, GRID_W)
        q = jnp.concatenate([q_ref[ch, i * GRID_W:(i + 1) * GRID_W, :] for ch in range(GROUP_CHUNKS)], axis=-1)
        q4 = _stack_heads(q, masks).astype(BF16)
        k = kwin[pl.ds(off, NA_KEYS), :]
        v = vwin[pl.ds(off, NA_KEYS), :]
        logits = lax.dot_general(q4, k, (((1,), (1,)), ((), ())),
                                 preferred_element_type=F32) + bias_ref[row - row0]
        o4, _ = _softmax_pv(logits, v, False)
        o = _unstack_heads(o4, masks, GRID_W)
        for ch in range(GROUP_CHUNKS):
            o_ref[ch, i * GRID_W:(i + 1) * GRID_W, :] = o[:, ch * LANES:(ch + 1) * LANES]


def _neighborhood_attention(proj, bias, bsz, seq):
    base = 3 * len(DILATIONS)
    win_rows = SEQ_TILE + 2 * NA_HALO
    return pl.pallas_call(
        functools.partial(_na_kernel, grid_rows=seq // GRID_W),
        grid=(bsz, seq // SEQ_TILE),
        in_specs=[_tile_spec(base, seq)]
        + _halo_specs(base + 1, NA_HALO, seq) + _halo_specs(base + 2, NA_HALO, seq)
        + [pl.BlockSpec(bias.shape, lambda b, t: (0, 0, 0))],
        out_specs=_tile_spec(0, seq),
        out_shape=jax.ShapeDtypeStruct((GROUP_CHUNKS, bsz * seq, LANES), F32),
        scratch_shapes=[pltpu.VMEM((win_rows, GROUP_COLS), BF16),
                        pltpu.VMEM((win_rows, GROUP_COLS), BF16)],
        compiler_params=_params(("parallel", "arbitrary")),
        name="neighborhood_attn",
    )(proj, proj, proj, proj, proj, proj, proj, bias)


def _t5_bucket_table(dil):
    qb = QUERY_BLOCK
    rel = (np.arange(3 * qb)[None, :] - qb - np.arange(qb)[:, None])
    nb = T5_BUCKETS // 2
    max_exact = nb // 2
    n = np.abs(rel * dil)
    nf = np.maximum(n, 1).astype(np.float32)
    large = max_exact + (np.log(nf / np.float32(max_exact)) / np.float32(math.log(T5_MAX_DIST / max_exact))
                         * np.float32(nb - max_exact)).astype(np.int32)
    large = np.minimum(large, nb - 1)
    bucket = np.where(rel * dil > 0, nb, 0) + np.where(n < max_exact, n, large)
    return bucket.astype(np.int32), np.abs(rel) <= qb


def _dilated_bias(rel_bias, group):
    bucket, band = _t5_bucket_table(DILATIONS[group])
    tab = rel_bias[:, group * GROUP_HEADS:(group + 1) * GROUP_HEADS]
    bias = jnp.moveaxis(tab[bucket], -1, 0)
    bias = jnp.where(band[None], bias, NEG)
    return bias.reshape(GROUP_HEADS * QUERY_BLOCK, 3 * QUERY_BLOCK).astype(F32)


def _na_bias(rpb):
    qc = np.arange(GRID_W)[:, None]
    kc = np.arange(GRID_W)[None, :]
    win0 = np.clip(qc - NA_COLS // 2, 0, GRID_W - NA_COLS)
    col_ok = (kc >= win0) & (kc < win0 + NA_COLS)
    dc = np.clip(kc - qc + NA_COLS - 1, 0, 2 * NA_COLS - 2)
    shift = np.arange(NA_ROWS)[:, None]
    kr = np.arange(NA_ROWS)[None, :]
    dr = kr - shift + NA_ROWS - 1
    b = rpb[:, dr[:, None, :, None], dc[None, :, None, :]]
    b = jnp.where(col_ok[None, None, :, None, :], b, NEG)
    b = jnp.transpose(b, (1, 0, 2, 3, 4))
    return b.reshape(NA_ROWS, GROUP_HEADS * GRID_W, NA_KEYS).astype(F32)


def _layer_norm(z, g, b):
    mu = jnp.mean(z, axis=-1, keepdims=True)
    zc = z - mu
    var = jnp.mean(zc * zc, axis=-1, keepdims=True)
    return zc * lax.rsqrt(var + LN_EPS) * g + b


def _row(x, i):
    return x[i:i + 1, :]


def _route(logits, rbias):
    scores = jax.nn.sigmoid(logits)
    sel = scores + rbias
    ge = GROUP_EXPERTS
    n_groups = N_EXPERTS // ge
    best_sum, best_g = None, None
    for g in range(n_groups):
        vals = [_row(sel, g * ge + j) for j in range(ge)]
        top2 = None
        for a in range(ge):
            for b in range(a + 1, ge):
                pair = vals[a] + vals[b]
                top2 = pair if top2 is None else jnp.maximum(top2, pair)
        if best_sum is None:
            best_sum, best_g = top2, jnp.zeros_like(top2, dtype=jnp.int32)
        else:
            upd = top2 > best_sum
            best_g = jnp.where(upd, g, best_g)
            best_sum = jnp.where(upd, top2, best_sum)

    def in_group(x, j):
        out = _row(x, j)
        for g in range(1, n_groups):
            out = jnp.where(best_g == g, _row(x, g * ge + j), out)
        return out

    cand = [in_group(sel, j) for j in range(ge)]
    cand_score = [in_group(scores, j) for j in range(ge)]
    v1, j1, w1 = cand[0], jnp.zeros_like(best_g), cand_score[0]
    for j in range(1, ge):
        upd = cand[j] > v1
        v1 = jnp.where(upd, cand[j], v1)
        j1 = jnp.where(upd, j, j1)
        w1 = jnp.where(upd, cand_score[j], w1)
    v2 = jnp.full_like(v1, -jnp.inf)
    j2, w2 = jnp.zeros_like(best_g), jnp.zeros_like(w1)
    for j in range(ge):
        upd = (cand[j] > v2) & (j1 != j)
        v2 = jnp.where(upd, cand[j], v2)
        j2 = jnp.where(upd, j, j2)
        w2 = jnp.where(upd, cand_score[j], w2)
    wsum = w1 + w2
    e1 = (best_g * ge + j1).astype(F32)
    e2 = (best_g * ge + j2).astype(F32)
    zero = jnp.zeros_like(w1)
    return jnp.concatenate([e1, e2, w1 / wsum, w2 / wsum, zero, zero, zero, zero], axis=0)


def _post_tail(m, x_ref, g_ref, b_ref, wr_ref, rb_ref, x1_ref, x1r_ref, route_ref):
    tm = m.shape[0]
    xn = _layer_norm(ALPHA * x_ref[...] + m, g_ref[...], b_ref[...])
    x1_ref[...] = xn
    for j in range(ROW_CHUNKS):
        x1r_ref[pl.ds(j, tm, stride=ROW_CHUNKS), :] = xn[:, j * LANES:(j + 1) * LANES]
    xh = xn.astype(BF16)
    xl = (xn - xh.astype(F32)).astype(BF16)
    wr = wr_ref[...]
    contract = (((1,), (1,)), ((), ()))
    a = lax.dot_general(wr, xh, contract, preferred_element_type=F32)
    c = lax.dot_general(wr[0:N_EXPERTS], xl, contract, preferred_element_type=F32)
    logits = a[0:N_EXPERTS] + a[N_EXPERTS:] + c
    route_ref[...] = _route(logits, rb_ref[...])


def _post_attn_kernel(o0_ref, l0_ref, o1_ref, l1_ref, o2_ref, l2_ref, ob_ref, w_ref,
                      x_ref, g_ref, b_ref, wr_ref, rb_ref, x1_ref, x1r_ref, route_ref):
    def wide(ref):
        return jnp.concatenate([ref[ch] for ch in range(GROUP_CHUNKS)], axis=-1)

    l0, l1, l2 = wide(l0_ref), wide(l1_ref), wide(l2_ref)
    mx = jnp.maximum(jnp.maximum(l0, l1), l2)
    e0, e1, e2 = jnp.exp(l0 - mx), jnp.exp(l1 - mx), jnp.exp(l2 - mx)
    inv = 1.0 / (e0 + e1 + e2)
    lhs = jnp.concatenate([wide(o0_ref) * (e0 * inv), wide(o1_ref) * (e1 * inv),
                           wide(o2_ref) * (e2 * inv), wide(ob_ref)], axis=-1).astype(BF16)
    m = jnp.dot(lhs, w_ref[...], preferred_element_type=F32)
    _post_tail(m, x_ref, g_ref, b_ref, wr_ref, rb_ref, x1_ref, x1r_ref, route_ref)


def _post_conv_kernel(u_ref, w_ref, wb_ref, x_ref, g_ref, b_ref, wr_ref, rb_ref,
                      x1_ref, x1r_ref, route_ref):
    m = jnp.dot(u_ref[...], w_ref[...], preferred_element_type=F32) + wb_ref[...]
    _post_tail(m, x_ref, g_ref, b_ref, wr_ref, rb_ref, x1_ref, x1r_ref, route_ref)


def _post_mixer(kernel, name, lhs_list, consts_front, x, g, b, wr, rb, tm=512):
    n, d = x.shape
    row = lambda i: (i, 0)
    const = lambda i: (0, 0)
    in_specs = [pl.BlockSpec((tm, a.shape[1]), row) if a.ndim == 2
                else pl.BlockSpec((a.shape[0], tm, LANES), lambda i: (0, i, 0)) for a in lhs_list]
    in_specs += [pl.BlockSpec(c.shape, const) for c in consts_front]
    in_specs += [pl.BlockSpec((tm, d), row)]
    in_specs += [pl.BlockSpec(c.shape, const) for c in (g, b, wr, rb)]
    return pl.pallas_call(
        kernel,
        grid=(n // tm,),
        in_specs=in_specs,
        out_specs=[pl.BlockSpec((tm, d), row),
                   pl.BlockSpec((tm * ROW_CHUNKS, LANES), row),
                   pl.BlockSpec((8, tm), lambda i: (0, i))],
        out_shape=[jax.ShapeDtypeStruct((n, d), F32),
                   jax.ShapeDtypeStruct((n * ROW_CHUNKS, LANES), F32),
                   jax.ShapeDtypeStruct((8, n), F32)],
        compiler_params=_params(("parallel",)),
        name=name,
    )(*lhs_list, *consts_front, x, g, b, wr, rb)


def _dwconv_kernel(up_ref, uc_ref, un_ref, w_ref, cb_ref, g_ref, b_ref, o_ref, win, acc, *, ts):
    t = pl.program_id(1)
    has_prev = (t > 0).astype(F32)
    has_next = (t < pl.num_programs(1) - 1).astype(F32)
    for j in range(ROW_CHUNKS):
        lanes = slice(j * LANES, (j + 1) * LANES)
        win[j, 0:CONV_HALO, :] = up_ref[:, lanes] * has_prev
        win[j, CONV_HALO:CONV_HALO + ts, :] = uc_ref[:, lanes]
        win[j, CONV_HALO + ts:, :] = un_ref[:, lanes] * has_next

    def body(j, carry):
        a = jnp.zeros((ts, LANES), F32)
        for tap in range(CONV_WIDTH):
            start = CONV_HALO - CONV_WIDTH // 2 + tap
            a = a + win[j, pl.ds(start, ts), :] * w_ref[j, tap:tap + 1, :]
        acc[j] = a
        return carry

    lax.fori_loop(0, ROW_CHUNKS, body, 0)
    u = jnp.concatenate([acc[j] for j in range(ROW_CHUNKS)], axis=-1) + cb_ref[...]
    y = _layer_norm(u, g_ref[...], b_ref[...])
    o_ref[...] = (y * jax.nn.sigmoid(y)).astype(o_ref.dtype)


def _dwconv_ln_silu(u, w_dw, b_dw, g, b, ts=256):
    bsz, seq, d = u.shape
    per_tile = ts // CONV_HALO
    last = seq // CONV_HALO - 1
    w3 = jnp.pad(w_dw, ((0, 32 - CONV_WIDTH), (0, 0))).reshape(32, ROW_CHUNKS, LANES).transpose(1, 0, 2)
    vec = pl.BlockSpec((1, d), lambda bb, t: (0, 0))
    return pl.pallas_call(
        functools.partial(_dwconv_kernel, ts=ts),
        grid=(bsz, seq // ts),
        in_specs=[pl.BlockSpec((None, CONV_HALO, d), lambda bb, t: (bb, jnp.maximum(t * per_tile - 1, 0), 0)),
                  pl.BlockSpec((None, ts, d), lambda bb, t: (bb, t, 0)),
                  pl.BlockSpec((None, CONV_HALO, d), lambda bb, t: (bb, jnp.minimum((t + 1) * per_tile, last), 0)),
                  pl.BlockSpec(w3.shape, lambda bb, t: (0, 0, 0)),
                  vec, vec, vec],
        out_specs=pl.BlockSpec((None, ts, d), lambda bb, t: (bb, t, 0)),
        out_shape=jax.ShapeDtypeStruct((bsz, seq, d), BF16),
        scratch_shapes=[pltpu.VMEM((ROW_CHUNKS, ts + 2 * CONV_HALO, LANES), F32),
                        pltpu.VMEM((ROW_CHUNKS, ts, LANES), F32)],
        compiler_params=_params(("parallel", "arbitrary")),
        name="dwconv_ln_silu",
    )(u, u, u, w3, b_dw, g, b)


def _row_gather_start(src_hbm, dst, sem, idx_ref, first, count):
    def body(r, carry):
        tok = idx_ref[first + r]
        pltpu.make_async_copy(src_hbm.at[pl.ds(pl.multiple_of(tok * ROW_CHUNKS, ROW_CHUNKS), ROW_CHUNKS)],
                              dst.at[pl.ds(pl.multiple_of(r * ROW_CHUNKS, ROW_CHUNKS), ROW_CHUNKS)],
                              sem).start()
        return carry
    lax.fori_loop(0, count, body, 0, unroll=8)


def _row_gather_wait(src_hbm, dst, sem):
    pltpu.make_async_copy(src_hbm.at[pl.ds(0, dst.shape[0])], dst, sem).wait()


def _expert_kernel(tile_expert_ref, n_tiles_ref, src_ref, x_hbm, wg_ref, wu_ref, wd_ref,
                   y_ref, xbuf, sem):
    del tile_expert_ref
    i = pl.program_id(0)
    n_tiles = n_tiles_ref[0]
    slot = i % 2
    tm = EXPERT_TILE

    @pl.when(i == 0)
    def _():
        _row_gather_start(x_hbm, xbuf.at[0], sem.at[0], src_ref, 0, tm)

    @pl.when(i + 1 < n_tiles)
    def _():
        _row_gather_start(x_hbm, xbuf.at[1 - slot], sem.at[1 - slot], src_ref, (i + 1) * tm, tm)

    @pl.when(i < n_tiles)
    def _():
        buf = xbuf.at[slot]
        _row_gather_wait(x_hbm, buf, sem.at[slot])
        x = jnp.concatenate([buf[pl.ds(j, tm, stride=ROW_CHUNKS), :] for j in range(ROW_CHUNKS)],
                            axis=-1).astype(BF16)
        hg = jnp.dot(x, wg_ref[...], preferred_element_type=F32)
        hu = jnp.dot(x, wu_ref[...], preferred_element_type=F32)
        hid = (hg * jax.nn.sigmoid(hg) * hu).astype(BF16)
        y = jnp.dot(hid, wd_ref[...], preferred_element_type=F32)
        for j in range(ROW_CHUNKS):
            y_ref[pl.ds(j, tm, stride=ROW_CHUNKS), :] = y[:, j * LANES:(j + 1) * LANES]

    @pl.when(i >= n_tiles)
    def _():
        y_ref[...] = jnp.zeros_like(y_ref)


def _experts(x1r, w_gate, w_up, w_down, tile_expert, n_tiles, src_token):
    tm = EXPERT_TILE
    n_rows = src_token.shape[0]
    d, de = w_gate.shape[1], w_gate.shape[2]
    grid_spec = pltpu.PrefetchScalarGridSpec(
        num_scalar_prefetch=3,
        grid=(n_rows // tm,),
        in_specs=[pl.BlockSpec(memory_space=pl.ANY),
                  pl.BlockSpec((None, d, de), lambda i, te, nt, src: (te[i], 0, 0)),
                  pl.BlockSpec((None, d, de), lambda i, te, nt, src: (te[i], 0, 0)),
                  pl.BlockSpec((None, de, d), lambda i, te, nt, src: (te[i], 0, 0))],
        out_specs=pl.BlockSpec((tm * ROW_CHUNKS, LANES), lambda i, te, nt, src: (i, 0)),
        scratch_shapes=[pltpu.VMEM((2, tm * ROW_CHUNKS, LANES), F32),
                        pltpu.SemaphoreType.DMA((2,))],
    )
    return pl.pallas_call(
        _expert_kernel,
        grid_spec=grid_spec,
        out_shape=jax.ShapeDtypeStruct((n_rows * ROW_CHUNKS, LANES), F32),
        compiler_params=_params(("arbitrary",)),
        name="moe_experts",
    )(tile_expert, n_tiles, src_token, x1r, w_gate, w_up, w_down)


def _combine_kernel(dest_ref, y_hbm, x_ref, route_ref, g_ref, b_ref, o_ref, ybuf, sem, *, n_tokens):
    i = pl.program_id(0)
    n_steps = pl.num_programs(0)
    slot = i % 2
    tm = COMBINE_TILE

    def start(step, s):
        for k in range(2):
            _row_gather_start(y_hbm, ybuf.at[s, k], sem.at[s, k], dest_ref, k * n_tokens + step * tm, tm)

    @pl.when(i == 0)
    def _():
        start(0, 0)

    @pl.when(i + 1 < n_steps)
    def _():
        start(i + 1, 1 - slot)

    for k in range(2):
        _row_gather_wait(y_hbm, ybuf.at[slot, k], sem.at[slot, k])
    eye = (lax.broadcasted_iota(jnp.int32, (tm, tm), 0) == lax.broadcasted_iota(jnp.int32, (tm, tm), 1))
    route = route_ref[...]
    gate = [jnp.sum(jnp.where(eye, route[2 + k:3 + k, :], 0.0), axis=-1, keepdims=True) for k in range(2)]
    parts = []
    for j in range(ROW_CHUNKS):
        y0 = ybuf[slot, 0, pl.ds(j, tm, stride=ROW_CHUNKS), :]
        y1 = ybuf[slot, 1, pl.ds(j, tm, stride=ROW_CHUNKS), :]
        parts.append(gate[0] * y0 + gate[1] * y1)
    f = jnp.concatenate(parts, axis=-1)
    o_ref[...] = _layer_norm(ALPHA * x_ref[...] + f, g_ref[...], b_ref[...])


def _combine(y_rows, x1, route, dest, g, b):
    n, d = x1.shape
    tm = COMBINE_TILE
    grid_spec = pltpu.PrefetchScalarGridSpec(
        num_scalar_prefetch=1,
        grid=(n // tm,),
        in_specs=[pl.BlockSpec(memory_space=pl.ANY),
                  pl.BlockSpec((tm, d), lambda i, dst: (i, 0)),
                  pl.BlockSpec((8, tm), lambda i, dst: (0, i)),
                  pl.BlockSpec((1, d), lambda i, dst: (0, 0)),
                  pl.BlockSpec((1, d), lambda i, dst: (0, 0))],
        out_specs=pl.BlockSpec((tm, d), lambda i, dst: (i, 0)),
        scratch_shapes=[pltpu.VMEM((2, 2, tm * ROW_CHUNKS, LANES), F32),
                        pltpu.SemaphoreType.DMA((2, 2))],
    )
    return pl.pallas_call(
        functools.partial(_combine_kernel, n_tokens=n),
        grid_spec=grid_spec,
        out_shape=jax.ShapeDtypeStruct((n, d), F32),
        compiler_params=_params(("arbitrary",)),
        name="moe_combine",
    )(dest, y_rows, x1, route, g, b)


def _dispatch_plan(route, n_tokens):
    tm = EXPERT_TILE
    n_slots = 2 * n_tokens
    n_rows = n_slots + N_EXPERTS * tm
    expert = route[0:2].astype(jnp.int32).reshape(n_slots)
    onehot = (expert[:, None] == jnp.arange(N_EXPERTS, dtype=jnp.int32)[None, :]).astype(jnp.int32)
    csum = jnp.cumsum(onehot, axis=0)
    pos = jnp.sum(csum * onehot, axis=1) - 1
    counts = csum[-1]
    padded = ((counts + tm - 1) // tm) * tm
    seg_end = jnp.cumsum(padded)
    seg_start = seg_end - padded
    dest = (seg_start[expert] + pos).astype(jnp.int32)
    token = jnp.tile(jnp.arange(n_tokens, dtype=jnp.int32), 2)
    src_token = jnp.zeros((n_rows,), jnp.int32).at[dest].set(token)
    tile_start = jnp.arange(n_rows // tm, dtype=jnp.int32) * tm
    tile_expert = jnp.minimum(jnp.searchsorted(seg_end, tile_start, side="right"), N_EXPERTS - 1).astype(jnp.int32)
    n_tiles = (seg_end[-1] // tm).astype(jnp.int32).reshape(1)
    return tile_expert, n_tiles, src_token, dest


def _moe(x1, x1r, route, w_gate, w_up, w_down, g, b):
    n = x1.shape[0]
    tile_expert, n_tiles, src_token, dest = _dispatch_plan(route, n)
    y_rows = _experts(x1r, w_gate, w_up, w_down, tile_expert, n_tiles, src_token)
    return _combine(y_rows, x1, route, dest, g, b)


def _vec(v):
    return v.reshape(1, -1).astype(F32)


def kernel(x, w_in_attn, w_out_attn, rel_bias, rpb_2d, w_pw1, b_pw1, w_dw, b_dw, conv_ln_g, conv_ln_b, w_pw2, b_pw2, ln_mix_g, ln_mix_b, ln_ffn_g, ln_ffn_b, w_router, router_bias, w_gate, w_up, w_down):
    bsz, seq, d = x.shape
    n = bsz * seq
    h = x.reshape(n, d)
    wr_t = w_router.T.astype(F32)
    wr_hi = wr_t.astype(BF16)
    wr_lo = (wr_t - wr_hi.astype(F32)).astype(BF16)
    wr = jnp.concatenate([wr_hi, wr_lo], axis=0)
    rb = router_bias.reshape(N_EXPERTS, 1).astype(F32)
    n_dil_cols = len(DILATIONS) * GROUP_COLS
    q_cols = np.zeros((3 * d,), np.float32) + 1.0
    q_cols[0:n_dil_cols] = HEAD_DIM ** -0.5
    q_cols[3 * n_dil_cols:3 * n_dil_cols + GROUP_COLS] = HEAD_DIM ** -0.5

    for layer in range(DEPTH):
        i = layer // 2
        g_mix, b_mix = _vec(ln_mix_g[layer]), _vec(ln_mix_b[layer])
        if layer % 2 == 0:
            w_in = (w_in_attn[i] * q_cols[None, :]).astype(BF16)
            proj = _matmul(h, w_in)
            lhs = []
            for grp in range(len(DILATIONS)):
                lhs += _dilated_attention(proj, _dilated_bias(rel_bias, grp), grp, bsz, seq)
            lhs.append(_neighborhood_attention(proj, _na_bias(rpb_2d[i]), bsz, seq))
            x1, x1r, route = _post_mixer(_post_attn_kernel, "post_attn", lhs,
                                         [w_out_attn[i].astype(BF16)], h, g_mix, b_mix, wr, rb)
        else:
            u = _pw1_glu(h, w_pw1[i].astype(BF16), _vec(b_pw1[i]))
            u = _dwconv_ln_silu(u.reshape(bsz, seq, d), w_dw[i].astype(F32), _vec(b_dw[i]),
                                _vec(conv_ln_g[i]), _vec(conv_ln_b[i]))
            x1, x1r, route = _post_mixer(_post_conv_kernel, "post_conv", [u.reshape(n, d)],
                                         [w_pw2[i].astype(BF16), _vec(b_pw2[i])], h, g_mix, b_mix, wr, rb)
        h = _moe(x1, x1r, route, w_gate[layer].astype(BF16), w_up[layer].astype(BF16),
                 w_down[layer].astype(BF16), _vec(ln_ffn_g[layer]), _vec(ln_ffn_b[layer]))
    return h.reshape(bsz, seq, d)
```

```python
import functools
import math

import jax
import jax.numpy as jnp
import numpy as np
from jax import lax
from jax.experimental import pallas as pl
from jax.experimental.pallas import tpu as pltpu

D_MODEL = 1024
HEAD_DIM = 64
GROUP_HEADS = 4
GROUP_COLS = GROUP_HEADS * HEAD_DIM
DILATIONS = (1, 4, 16)
QUERY_BLOCK = 64
GRID_W = 64
NA_ROWS = 8
NA_COLS = 16
T5_BUCKETS = 32
T5_MAX_DIST = 1024
CONV_WIDTH = 31
CONV_HALO = 16
N_EXPERTS = 16
GROUP_EXPERTS = 4
D_EXPERT = 512
DEPTH = 4
ALPHA = (2 * DEPTH) ** 0.25
LN_EPS = 1e-5
NEG = -1e30

LANES = 128
ROW_CHUNKS = D_MODEL // LANES
GROUP_CHUNKS = GROUP_COLS // LANES
SEQ_TILE = 1024
EXPERT_TILE = 256
COMBINE_TILE = 256
VMEM_LIMIT = 56 * 1024 * 1024

F32 = jnp.float32
BF16 = jnp.bfloat16


def _params(semantics, vmem=VMEM_LIMIT):
    return pltpu.CompilerParams(dimension_semantics=semantics, vmem_limit_bytes=vmem)


def _matmul_kernel(x_ref, w_ref, o_ref):
    y = jnp.dot(x_ref[...].astype(BF16), w_ref[...], preferred_element_type=F32)
    for c in range(o_ref.shape[0]):
        o_ref[c] = y[:, c * LANES:(c + 1) * LANES]


def _matmul(x, w, tm=1024, tn=768):
    n, k = x.shape
    _, m = w.shape
    return pl.pallas_call(
        _matmul_kernel,
        grid=(n // tm, m // tn),
        in_specs=[pl.BlockSpec((tm, k), lambda i, j: (i, 0)),
                  pl.BlockSpec((k, tn), lambda i, j: (0, j))],
        out_specs=pl.BlockSpec((tn // LANES, tm, LANES), lambda i, j: (j, i, 0)),
        out_shape=jax.ShapeDtypeStruct((m // LANES, n, LANES), F32),
        compiler_params=_params(("parallel", "arbitrary")),
        name="qkv_proj",
    )(x, w)


def _glu_kernel(x_ref, wa_ref, wg_ref, ba_ref, bg_ref, o_ref):
    xb = x_ref[...].astype(BF16)
    a = jnp.dot(xb, wa_ref[...], preferred_element_type=F32) + ba_ref[...]
    g = jnp.dot(xb, wg_ref[...], preferred_element_type=F32) + bg_ref[...]
    o_ref[...] = a * jax.nn.sigmoid(g)


def _pw1_glu(x, w, b, tm=1024, tn=512):
    n, k = x.shape
    half = w.shape[1] // 2
    nj = half // tn
    return pl.pallas_call(
        _glu_kernel,
        grid=(n // tm, nj),
        in_specs=[pl.BlockSpec((tm, k), lambda i, j: (i, 0)),
                  pl.BlockSpec((k, tn), lambda i, j: (0, j)),
                  pl.BlockSpec((k, tn), lambda i, j: (0, j + nj)),
                  pl.BlockSpec((1, tn), lambda i, j: (0, j)),
                  pl.BlockSpec((1, tn), lambda i, j: (0, j + nj))],
        out_specs=pl.BlockSpec((tm, tn), lambda i, j: (i, j)),
        out_shape=jax.ShapeDtypeStruct((n, half), F32),
        compiler_params=_params(("parallel", "arbitrary")),
        name="pw1_glu",
    )(x, w, w, b, b)


def _head_masks(rows):
    lane = lax.broadcasted_iota(jnp.int32, (rows, GROUP_COLS), 1)
    return [(lane >= HEAD_DIM * h) & (lane < HEAD_DIM * (h + 1)) for h in range(GROUP_HEADS)]


def _stack_heads(q, masks):
    return jnp.concatenate([jnp.where(m, q, 0.0) for m in masks], axis=0)


def _unstack_heads(o4, masks, rows):
    out = jnp.where(masks[0], o4[0:rows], 0.0)
    for h in range(1, GROUP_HEADS):
        out = out + jnp.where(masks[h], o4[h * rows:(h + 1) * rows], 0.0)
    return out


def _softmax_pv(logits, v, with_lse):
    m = jnp.max(logits, axis=-1, keepdims=True)
    p = jnp.exp(logits - m)
    s = jnp.sum(p, axis=-1, keepdims=True)
    o4 = jnp.dot(p.astype(BF16), v, preferred_element_type=F32) * (1.0 / s)
    lse = (m + jnp.log(s)) if with_lse else None
    return o4, lse


def _dilated_kernel(q_ref, kp_ref, kc_ref, kn_ref, vp_ref, vc_ref, vn_ref, bias_ref,
                    o_ref, l_ref, *, dil):
    t = pl.program_id(1)
    chunk = QUERY_BLOCK * dil
    n_chunks = SEQ_TILE // chunk
    qb = QUERY_BLOCK
    masks = _head_masks(qb)
    col = lax.broadcasted_iota(jnp.int32, (GROUP_HEADS * qb, 3 * qb), 1)
    no_prev = jnp.where(col < qb, NEG, 0.0) * (t == 0).astype(F32)
    no_next = jnp.where(col >= 2 * qb, NEG, 0.0) * (t == pl.num_programs(1) - 1).astype(F32)
    bias = bias_ref[...]

    def sl(start):
        return pl.ds(start, qb) if dil == 1 else pl.ds(start, qb, stride=dil)

    def rows(ref, start):
        return jnp.concatenate([ref[c, sl(start), :] for c in range(GROUP_CHUNKS)], axis=-1)

    def window(p_ref, c_ref, n_ref, c, r):
        parts = []
        for cc in (c - 1, c, c + 1):
            if cc < 0:
                parts.append(rows(p_ref, r))
            elif cc >= n_chunks:
                parts.append(rows(n_ref, r))
            else:
                parts.append(rows(c_ref, cc * chunk + r))
        return jnp.concatenate(parts, axis=0).astype(BF16)

    for c in range(n_chunks):
        for r in range(dil):
            base = c * chunk + r
            q4 = _stack_heads(rows(q_ref, base), masks).astype(BF16)
            k = window(kp_ref, kc_ref, kn_ref, c, r)
            v = window(vp_ref, vc_ref, vn_ref, c, r)
            logits = lax.dot_general(q4, k, (((1,), (1,)), ((), ())),
                                     preferred_element_type=F32) + bias
            if c == 0:
                logits = logits + no_prev
            if c == n_chunks - 1:
                logits = logits + no_next
            o4, lse = _softmax_pv(logits, v, True)
            o = _unstack_heads(o4, masks, qb)
            lb = _unstack_heads(jnp.broadcast_to(lse, (GROUP_HEADS * qb, GROUP_COLS)), masks, qb)
            for ch in range(GROUP_CHUNKS):
                o_ref[ch, sl(base), :] = o[:, ch * LANES:(ch + 1) * LANES]
                l_ref[ch, sl(base), :] = lb[:, ch * LANES:(ch + 1) * LANES]


def _tile_spec(col_block, seq):
    tiles = seq // SEQ_TILE
    return pl.BlockSpec((GROUP_CHUNKS, SEQ_TILE, LANES), lambda b, t: (col_block, b * tiles + t, 0))


def _halo_specs(col_block, halo, seq):
    per_tile = SEQ_TILE // halo
    per_seq = seq // halo
    return [
        pl.BlockSpec((GROUP_CHUNKS, halo, LANES),
                     lambda b, t: (col_block, b * per_seq + jnp.maximum(t * per_tile - 1, 0), 0)),
        _tile_spec(col_block, seq),
        pl.BlockSpec((GROUP_CHUNKS, halo, LANES),
                     lambda b, t: (col_block, b * per_seq + jnp.minimum((t + 1) * per_tile, per_seq - 1), 0)),
    ]


def _dilated_attention(proj, bias, group, bsz, seq):
    dil = DILATIONS[group]
    n_groups = len(DILATIONS)
    q_col, k_col, v_col = group, n_groups + group, 2 * n_groups + group
    out_spec = _tile_spec(0, seq)
    out_sds = jax.ShapeDtypeStruct((GROUP_CHUNKS, bsz * seq, LANES), F32)
    return pl.pallas_call(
        functools.partial(_dilated_kernel, dil=dil),
        grid=(bsz, seq // SEQ_TILE),
        in_specs=[_tile_spec(q_col, seq)]
        + _halo_specs(k_col, QUERY_BLOCK * dil, seq) + _halo_specs(v_col, QUERY_BLOCK * dil, seq)
        + [pl.BlockSpec(bias.shape, lambda b, t: (0, 0))],
        out_specs=[out_spec, out_spec],
        out_shape=[out_sds, out_sds],
        compiler_params=_params(("parallel", "arbitrary")),
        name=f"dilated_attn_d{dil}",
    )(proj, proj, proj, proj, proj, proj, proj, bias)


NA_HALO = 256
NA_KEYS = NA_ROWS * GRID_W


def _na_kernel(q_ref, kp_ref, kc_ref, kn_ref, vp_ref, vc_ref, vn_ref, bias_ref, o_ref,
               kwin, vwin, *, grid_rows):
    t = pl.program_id(1)
    tile_rows = SEQ_TILE // GRID_W
    half = NA_ROWS // 2
    for win, p_ref, c_ref, n_ref in ((kwin, kp_ref, kc_ref, kn_ref), (vwin, vp_ref, vc_ref, vn_ref)):
        for ch in range(GROUP_CHUNKS):
            lanes = slice(ch * LANES, (ch + 1) * LANES)
            win[0:NA_HALO, lanes] = p_ref[ch].astype(BF16)
            win[NA_HALO:NA_HALO + SEQ_TILE, lanes] = c_ref[ch].astype(BF16)
            win[NA_HALO + SEQ_TILE:, lanes] = n_ref[ch].astype(BF16)
    masks = _head_masks(GRID_W)
    for i in range(tile_rows):
        row = t * tile_rows + i
        row0 = jnp.clip(row - half, 0, grid_rows - NA_ROWS)
        off = pl.multiple_of((row0 - (t * tile_rows - half)) * GRID_W, GRID_W)
        q = jnp.concatenate([q_ref[ch, i * GRID_W:(i + 1) * GRID_W, :] for ch in range(GROUP_CHUNKS)], axis=-1)
        q4 = _stack_heads(q, masks).astype(BF16)
        k = kwin[pl.ds(off, NA_KEYS), :]
        v = vwin[pl.ds(off, NA_KEYS), :]
        logits = lax.dot_general(q4, k, (((1,), (1,)), ((), ())),
                                 preferred_element_type=F32) + bias_ref[row - row0]
        o4, _ = _softmax_pv(logits, v, False)
        o = _unstack_heads(o4, masks, GRID_W)
        for ch in range(GROUP_CHUNKS):
            o_ref[ch, i * GRID_W:(i + 1) * GRID_W, :] = o[:, ch * LANES:(ch + 1) * LANES]


def _neighborhood_attention(proj, bias, bsz, seq):
    base = 3 * len(DILATIONS)
    win_rows = SEQ_TILE + 2 * NA_HALO
    return pl.pallas_call(
        functools.partial(_na_kernel, grid_rows=seq // GRID_W),
        grid=(bsz, seq // SEQ_TILE),
        in_specs=[_tile_spec(base, seq)]
        + _halo_specs(base + 1, NA_HALO, seq) + _halo_specs(base + 2, NA_HALO, seq)
        + [pl.BlockSpec(bias.shape, lambda b, t: (0, 0, 0))],
        out_specs=_tile_spec(0, seq),
        out_shape=jax.ShapeDtypeStruct((GROUP_CHUNKS, bsz * seq, LANES), F32),
        scratch_shapes=[pltpu.VMEM((win_rows, GROUP_COLS), BF16),
                        pltpu.VMEM((win_rows, GROUP_COLS), BF16)],
        compiler_params=_params(("parallel", "arbitrary")),
        name="neighborhood_attn",
    )(proj, proj, proj, proj, proj, proj, proj, bias)


def _t5_bucket_table(dil):
    qb = QUERY_BLOCK
    rel = (np.arange(3 * qb)[None, :] - qb - np.arange(qb)[:, None])
    nb = T5_BUCKETS // 2
    max_exact = nb // 2
    n = np.abs(rel * dil)
    nf = np.maximum(n, 1).astype(np.float32)
    large = max_exact + (np.log(nf / np.float32(max_exact)) / np.float32(math.log(T5_MAX_DIST / max_exact))
                         * np.float32(nb - max_exact)).astype(np.int32)
    large = np.minimum(large, nb - 1)
    bucket = np.where(rel * dil > 0, nb, 0) + np.where(n < max_exact, n, large)
    return bucket.astype(np.int32), np.abs(rel) <= qb


def _dilated_bias(rel_bias, group):
    bucket, band = _t5_bucket_table(DILATIONS[group])
    tab = rel_bias[:, group * GROUP_HEADS:(group + 1) * GROUP_HEADS].astype(F32)
    onehot = (bucket[None, :, :] == np.arange(T5_BUCKETS)[:, None, None]).astype(np.float32)
    bias = jnp.einsum("bh,bqk->hqk", tab, onehot, precision=lax.Precision.HIGHEST)
    bias = jnp.where(band[None], bias, NEG)
    return bias.reshape(GROUP_HEADS * QUERY_BLOCK, 3 * QUERY_BLOCK)


def _na_bias(rpb):
    qc = np.arange(GRID_W)[:, None]
    kc = np.arange(GRID_W)[None, :]
    win0 = np.clip(qc - NA_COLS // 2, 0, GRID_W - NA_COLS)
    col_ok = (kc >= win0) & (kc < win0 + NA_COLS)
    dc = np.clip(kc - qc + NA_COLS - 1, 0, 2 * NA_COLS - 2)
    shift = np.arange(NA_ROWS)[:, None]
    kr = np.arange(NA_ROWS)[None, :]
    dr = kr - shift + NA_ROWS - 1
    pick_row = (dr[:, :, None] == np.arange(2 * NA_ROWS - 1)[None, None, :]).astype(np.float32)
    pick_col = (dc[None, :, :] == np.arange(2 * NA_COLS - 1)[:, None, None]).astype(np.float32)
    hi = lax.Precision.HIGHEST
    rows = jnp.einsum("hac,dka->hdkc", rpb.astype(F32), pick_row, precision=hi)
    b = jnp.einsum("hdkc,cqx->dhqkx", rows, pick_col, precision=hi)
    b = jnp.where(col_ok[None, None, :, None, :], b, NEG)
    return b.reshape(NA_ROWS, GROUP_HEADS * GRID_W, NA_KEYS)


def _layer_norm(z, g, b):
    mu = jnp.mean(z, axis=-1, keepdims=True)
    zc = z - mu
    var = jnp.mean(zc * zc, axis=-1, keepdims=True)
    return zc * lax.rsqrt(var + LN_EPS) * g + b


def _row(x, i):
    return x[i:i + 1, :]


def _route(logits, rbias):
    scores = jax.nn.sigmoid(logits)
    sel = scores + rbias
    ge = GROUP_EXPERTS
    n_groups = N_EXPERTS // ge
    best_sum, best_g = None, None
    for g in range(n_groups):
        vals = [_row(sel, g * ge + j) for j in range(ge)]
        top2 = None
        for a in range(ge):
            for b in range(a + 1, ge):
                pair = vals[a] + vals[b]
                top2 = pair if top2 is None else jnp.maximum(top2, pair)
        if best_sum is None:
            best_sum, best_g = top2, jnp.zeros_like(top2, dtype=jnp.int32)
        else:
            upd = top2 > best_sum
            best_g = jnp.where(upd, g, best_g)
            best_sum = jnp.where(upd, top2, best_sum)

    def in_group(x, j):
        out = _row(x, j)
        for g in range(1, n_groups):
            out = jnp.where(best_g == g, _row(x, g * ge + j), out)
        return out

    cand = [in_group(sel, j) for j in range(ge)]
    cand_score = [in_group(scores, j) for j in range(ge)]
    v1, j1, w1 = cand[0], jnp.zeros_like(best_g), cand_score[0]
    for j in range(1, ge):
        upd = cand[j] > v1
        v1 = jnp.where(upd, cand[j], v1)
        j1 = jnp.where(upd, j, j1)
        w1 = jnp.where(upd, cand_score[j], w1)
    v2 = jnp.full_like(v1, -jnp.inf)
    j2, w2 = jnp.zeros_like(best_g), jnp.zeros_like(w1)
    for j in range(ge):
        upd = (cand[j] > v2) & (j1 != j)
        v2 = jnp.where(upd, cand[j], v2)
        j2 = jnp.where(upd, j, j2)
        w2 = jnp.where(upd, cand_score[j], w2)
    wsum = w1 + w2
    e1 = (best_g * ge + j1).astype(F32)
    e2 = (best_g * ge + j2).astype(F32)
    zero = jnp.zeros_like(w1)
    return jnp.concatenate([e1, e2, w1 / wsum, w2 / wsum, zero, zero, zero, zero], axis=0)


def _post_tail(m, x_ref, g_ref, b_ref, wr_ref, rb_ref, x1_ref, x1r_ref, route_ref):
    tm = m.shape[0]
    xn = _layer_norm(ALPHA * x_ref[...] + m, g_ref[...], b_ref[...])
    x1_ref[...] = xn
    for j in range(ROW_CHUNKS):
        x1r_ref[pl.ds(j, tm, stride=ROW_CHUNKS), :] = xn[:, j * LANES:(j + 1) * LANES]
    xh = xn.astype(BF16)
    xl = (xn - xh.astype(F32)).astype(BF16)
    wr = wr_ref[...]
    contract = (((1,), (1,)), ((), ()))
    a = lax.dot_general(wr, xh, contract, preferred_element_type=F32)
    c = lax.dot_general(wr[0:N_EXPERTS], xl, contract, preferred_element_type=F32)
    logits = a[0:N_EXPERTS] + a[N_EXPERTS:] + c
    route_ref[...] = _route(logits, rb_ref[...])


def _post_attn_kernel(o0_ref, l0_ref, o1_ref, l1_ref, o2_ref, l2_ref, ob_ref, w_ref,
                      x_ref, g_ref, b_ref, wr_ref, rb_ref, x1_ref, x1r_ref, route_ref):
    def wide(ref):
        return jnp.concatenate([ref[ch] for ch in range(GROUP_CHUNKS)], axis=-1)

    l0, l1, l2 = wide(l0_ref), wide(l1_ref), wide(l2_ref)
    mx = jnp.maximum(jnp.maximum(l0, l1), l2)
    e0, e1, e2 = jnp.exp(l0 - mx), jnp.exp(l1 - mx), jnp.exp(l2 - mx)
    inv = 1.0 / (e0 + e1 + e2)
    lhs = jnp.concatenate([wide(o0_ref) * (e0 * inv), wide(o1_ref) * (e1 * inv),
                           wide(o2_ref) * (e2 * inv), wide(ob_ref)], axis=-1).astype(BF16)
    m = jnp.dot(lhs, w_ref[...], preferred_element_type=F32)
    _post_tail(m, x_ref, g_ref, b_ref, wr_ref, rb_ref, x1_ref, x1r_ref, route_ref)


def _post_conv_kernel(u_ref, w_ref, wb_ref, x_ref, g_ref, b_ref, wr_ref, rb_ref,
                      x1_ref, x1r_ref, route_ref):
    m = jnp.dot(u_ref[...], w_ref[...], preferred_element_type=F32) + wb_ref[...]
    _post_tail(m, x_ref, g_ref, b_ref, wr_ref, rb_ref, x1_ref, x1r_ref, route_ref)


def _post_mixer(kernel, name, lhs_list, consts_front, x, g, b, wr, rb, tm=512):
    n, d = x.shape
    row = lambda i: (i, 0)
    const = lambda i: (0, 0)
    in_specs = [pl.BlockSpec((tm, a.shape[1]), row) if a.ndim == 2
                else pl.BlockSpec((a.shape[0], tm, LANES), lambda i: (0, i, 0)) for a in lhs_list]
    in_specs += [pl.BlockSpec(c.shape, const) for c in consts_front]
    in_specs += [pl.BlockSpec((tm, d), row)]
    in_specs += [pl.BlockSpec(c.shape, const) for c in (g, b, wr, rb)]
    return pl.pallas_call(
        kernel,
        grid=(n // tm,),
        in_specs=in_specs,
        out_specs=[pl.BlockSpec((tm, d), row),
                   pl.BlockSpec((tm * ROW_CHUNKS, LANES), row),
                   pl.BlockSpec((8, tm), lambda i: (0, i))],
        out_shape=[jax.ShapeDtypeStruct((n, d), F32),
                   jax.ShapeDtypeStruct((n * ROW_CHUNKS, LANES), F32),
                   jax.ShapeDtypeStruct((8, n), F32)],
        compiler_params=_params(("parallel",)),
        name=name,
    )(*lhs_list, *consts_front, x, g, b, wr, rb)


def _dwconv_kernel(up_ref, uc_ref, un_ref, w_ref, cb_ref, g_ref, b_ref, o_ref, win, acc, *, ts):
    t = pl.program_id(1)
    has_prev = (t > 0).astype(F32)
    has_next = (t < pl.num_programs(1) - 1).astype(F32)
    for j in range(ROW_CHUNKS):
        lanes = slice(j * LANES, (j + 1) * LANES)
        win[j, 0:CONV_HALO, :] = up_ref[:, lanes] * has_prev
        win[j, CONV_HALO:CONV_HALO + ts, :] = uc_ref[:, lanes]
        win[j, CONV_HALO + ts:, :] = un_ref[:, lanes] * has_next

    def body(j, carry):
        a = jnp.zeros((ts, LANES), F32)
        for tap in range(CONV_WIDTH):
            start = CONV_HALO - CONV_WIDTH // 2 + tap
            a = a + win[j, pl.ds(start, ts), :] * w_ref[j, tap:tap + 1, :]
        acc[j] = a
        return carry

    lax.fori_loop(0, ROW_CHUNKS, body, 0)
    u = jnp.concatenate([acc[j] for j in range(ROW_CHUNKS)], axis=-1) + cb_ref[...]
    y = _layer_norm(u, g_ref[...], b_ref[...])
    o_ref[...] = (y * jax.nn.sigmoid(y)).astype(o_ref.dtype)


def _dwconv_ln_silu(u, w_dw, b_dw, g, b, ts=256):
    bsz, seq, d = u.shape
    per_tile = ts // CONV_HALO
    last = seq // CONV_HALO - 1
    w3 = jnp.pad(w_dw, ((0, 32 - CONV_WIDTH), (0, 0))).reshape(32, ROW_CHUNKS, LANES).transpose(1, 0, 2)
    vec = pl.BlockSpec((1, d), lambda bb, t: (0, 0))
    return pl.pallas_call(
        functools.partial(_dwconv_kernel, ts=ts),
        grid=(bsz, seq // ts),
        in_specs=[pl.BlockSpec((None, CONV_HALO, d), lambda bb, t: (bb, jnp.maximum(t * per_tile - 1, 0), 0)),
                  pl.BlockSpec((None, ts, d), lambda bb, t: (bb, t, 0)),
                  pl.BlockSpec((None, CONV_HALO, d), lambda bb, t: (bb, jnp.minimum((t + 1) * per_tile, last), 0)),
                  pl.BlockSpec(w3.shape, lambda bb, t: (0, 0, 0)),
                  vec, vec, vec],
        out_specs=pl.BlockSpec((None, ts, d), lambda bb, t: (bb, t, 0)),
        out_shape=jax.ShapeDtypeStruct((bsz, seq, d), BF16),
        scratch_shapes=[pltpu.VMEM((ROW_CHUNKS, ts + 2 * CONV_HALO, LANES), F32),
                        pltpu.VMEM((ROW_CHUNKS, ts, LANES), F32)],
        compiler_params=_params(("parallel", "arbitrary")),
        name="dwconv_ln_silu",
    )(u, u, u, w3, b_dw, g, b)


def _row_gather_start(src_hbm, dst, sem, idx_ref, first, count):
    def body(r, carry):
        tok = idx_ref[first + r]
        pltpu.make_async_copy(src_hbm.at[pl.ds(pl.multiple_of(tok * ROW_CHUNKS, ROW_CHUNKS), ROW_CHUNKS)],
                              dst.at[pl.ds(pl.multiple_of(r * ROW_CHUNKS, ROW_CHUNKS), ROW_CHUNKS)],
                              sem).start()
        return carry
    lax.fori_loop(0, count, body, 0, unroll=8)


def _row_gather_wait(src_hbm, dst, sem):
    pltpu.make_async_copy(src_hbm.at[pl.ds(0, dst.shape[0])], dst, sem).wait()


def _dispatch_kernel(dest_ref, tail_ref, n_tiles_ref, x_hbm, xs_hbm, zeros, sem, zsem, *, n_tokens, tm):
    i = pl.program_id(0)
    n_steps = pl.num_programs(0)
    slot = i % 2
    tile_rows = EXPERT_TILE * ROW_CHUNKS
    all_tiles = xs_hbm.shape[0] // tile_rows

    def zero_tile(first_row):
        start = pl.multiple_of(first_row * ROW_CHUNKS, ROW_CHUNKS)
        return pltpu.make_async_copy(zeros, xs_hbm.at[pl.ds(start, tile_rows)], zsem)

    @pl.when(i == 0)
    def _():
        zeros[...] = jnp.zeros_like(zeros)
        for e in range(N_EXPERTS):
            zero_tile(tail_ref[e]).start()
        for e in range(N_EXPERTS):
            zero_tile(tail_ref[e]).wait()

        def unused(j, carry):
            cp = zero_tile(j * EXPERT_TILE)
            cp.start()
            cp.wait()
            return carry
        lax.fori_loop(n_tiles_ref[0], all_tiles, unused, 0)

    def body(r, carry):
        src = x_hbm.at[pl.ds(pl.multiple_of((i * tm + r) * ROW_CHUNKS, ROW_CHUNKS), ROW_CHUNKS)]
        for k in range(2):
            d = dest_ref[k * n_tokens + i * tm + r]
            pltpu.make_async_copy(src, xs_hbm.at[pl.ds(pl.multiple_of(d * ROW_CHUNKS, ROW_CHUNKS), ROW_CHUNKS)],
                                  sem.at[slot]).start()
        return carry
    lax.fori_loop(0, tm, body, 0, unroll=8)

    def wait_step(s):
        rows = 2 * tm * ROW_CHUNKS
        pltpu.make_async_copy(x_hbm.at[pl.ds(0, rows)], xs_hbm.at[pl.ds(0, rows)], sem.at[s]).wait()

    @pl.when(i > 0)
    def _():
        wait_step(1 - slot)

    @pl.when(i == n_steps - 1)
    def _():
        wait_step(slot)


def _dispatch(x1r, dest, tail_start, n_tiles, n_rows, tm=512):
    n_tokens = x1r.shape[0] // ROW_CHUNKS
    grid_spec = pltpu.PrefetchScalarGridSpec(
        num_scalar_prefetch=3,
        grid=(n_tokens // tm,),
        in_specs=[pl.BlockSpec(memory_space=pl.ANY)],
        out_specs=pl.BlockSpec(memory_space=pl.ANY),
        scratch_shapes=[pltpu.VMEM((EXPERT_TILE * ROW_CHUNKS, LANES), F32),
                        pltpu.SemaphoreType.DMA((2,)),
                        pltpu.SemaphoreType.DMA(())],
    )
    return pl.pallas_call(
        functools.partial(_dispatch_kernel, n_tokens=n_tokens, tm=tm),
        grid_spec=grid_spec,
        out_shape=jax.ShapeDtypeStruct((n_rows * ROW_CHUNKS, LANES), F32),
        compiler_params=_params(("arbitrary",)),
        name="moe_dispatch",
    )(dest, tail_start, n_tiles, x1r)


def _expert_kernel(tile_expert_ref, n_tiles_ref, x_ref, wg_ref, wu_ref, wd_ref, y_ref,
                   wg_bf, wu_bf, wd_bf):
    i = pl.program_id(0)
    tm = EXPERT_TILE
    prev = tile_expert_ref[jnp.maximum(i - 1, 0)]

    @pl.when((i == 0) | (tile_expert_ref[i] != prev))
    def _():
        wg_bf[...] = wg_ref[...].astype(BF16)
        wu_bf[...] = wu_ref[...].astype(BF16)
        wd_bf[...] = wd_ref[...].astype(BF16)

    @pl.when(i < n_tiles_ref[0])
    def _():
        x = jnp.concatenate([x_ref[pl.ds(j, tm, stride=ROW_CHUNKS), :] for j in range(ROW_CHUNKS)],
                            axis=-1).astype(BF16)
        hg = jnp.dot(x, wg_bf[...], preferred_element_type=F32)
        hu = jnp.dot(x, wu_bf[...], preferred_element_type=F32)
        hid = (hg * jax.nn.sigmoid(hg) * hu).astype(BF16)
        y = jnp.dot(hid, wd_bf[...], preferred_element_type=F32)
        for j in range(ROW_CHUNKS):
            y_ref[pl.ds(j, tm, stride=ROW_CHUNKS), :] = y[:, j * LANES:(j + 1) * LANES]

    @pl.when(i >= n_tiles_ref[0])
    def _():
        y_ref[...] = jnp.zeros_like(y_ref)


def _experts(xs, w_gate, w_up, w_down, layer, tile_expert, n_tiles):
    tm = EXPERT_TILE
    n_rows = xs.shape[0] // ROW_CHUNKS
    d, de = w_gate.shape[2], w_gate.shape[3]
    row_map = lambda i, te, nt: (jnp.minimum(i, jnp.maximum(nt[0] - 1, 0)), 0)
    w_map = lambda i, te, nt: (layer, te[i], 0, 0)
    grid_spec = pltpu.PrefetchScalarGridSpec(
        num_scalar_prefetch=2,
        grid=(n_rows // tm,),
        in_specs=[pl.BlockSpec((tm * ROW_CHUNKS, LANES), row_map),
                  pl.BlockSpec((None, None, d, de), w_map),
                  pl.BlockSpec((None, None, d, de), w_map),
                  pl.BlockSpec((None, None, de, d), w_map)],
        out_specs=pl.BlockSpec((tm * ROW_CHUNKS, LANES), lambda i, te, nt: (i, 0)),
        scratch_shapes=[pltpu.VMEM((d, de), BF16), pltpu.VMEM((d, de), BF16), pltpu.VMEM((de, d), BF16)],
    )
    return pl.pallas_call(
        _expert_kernel,
        grid_spec=grid_spec,
        out_shape=jax.ShapeDtypeStruct((n_rows * ROW_CHUNKS, LANES), F32),
        compiler_params=_params(("arbitrary",)),
        name="moe_experts",
    )(tile_expert, n_tiles, xs, w_gate, w_up, w_down)


def _combine_kernel(dest_ref, y_hbm, x_ref, route_ref, g_ref, b_ref, o_ref, ybuf, sem, *, n_tokens):
    i = pl.program_id(0)
    n_steps = pl.num_programs(0)
    slot = i % 2
    tm = COMBINE_TILE

    def start(step, s):
        for k in range(2):
            _row_gather_start(y_hbm, ybuf.at[s, k], sem.at[s, k], dest_ref, k * n_tokens + step * tm, tm)

    @pl.when(i == 0)
    def _():
        start(0, 0)

    @pl.when(i + 1 < n_steps)
    def _():
        start(i + 1, 1 - slot)

    for k in range(2):
        _row_gather_wait(y_hbm, ybuf.at[slot, k], sem.at[slot, k])
    eye = (lax.broadcasted_iota(jnp.int32, (tm, tm), 0) == lax.broadcasted_iota(jnp.int32, (tm, tm), 1))
    route = route_ref[...]
    gate = [jnp.sum(jnp.where(eye, route[2 + k:3 + k, :], 0.0), axis=-1, keepdims=True) for k in range(2)]
    parts = []
    for j in range(ROW_CHUNKS):
        y0 = ybuf[slot, 0, pl.ds(j, tm, stride=ROW_CHUNKS), :]
        y1 = ybuf[slot, 1, pl.ds(j, tm, stride=ROW_CHUNKS), :]
        parts.append(gate[0] * y0 + gate[1] * y1)
    f = jnp.concatenate(parts, axis=-1)
    o_ref[...] = _layer_norm(ALPHA * x_ref[...] + f, g_ref[...], b_ref[...])


def _combine(y_rows, x1, route, dest, g, b):
    n, d = x1.shape
    tm = COMBINE_TILE
    grid_spec = pltpu.PrefetchScalarGridSpec(
        num_scalar_prefetch=1,
        grid=(n // tm,),
        in_specs=[pl.BlockSpec(memory_space=pl.ANY),
                  pl.BlockSpec((tm, d), lambda i, dst: (i, 0)),
                  pl.BlockSpec((8, tm), lambda i, dst: (0, i)),
                  pl.BlockSpec((1, d), lambda i, dst: (0, 0)),
                  pl.BlockSpec((1, d), lambda i, dst: (0, 0))],
        out_specs=pl.BlockSpec((tm, d), lambda i, dst: (i, 0)),
        scratch_shapes=[pltpu.VMEM((2, 2, tm * ROW_CHUNKS, LANES), F32),
                        pltpu.SemaphoreType.DMA((2, 2))],
    )
    return pl.pallas_call(
        functools.partial(_combine_kernel, n_tokens=n),
        grid_spec=grid_spec,
        out_shape=jax.ShapeDtypeStruct((n, d), F32),
        compiler_params=_params(("arbitrary",)),
        name="moe_combine",
    )(dest, y_rows, x1, route, g, b)


def _dispatch_plan(route, n_tokens):
    tm = EXPERT_TILE
    blk = LANES
    n_slots = 2 * n_tokens
    n_rows = n_slots + N_EXPERTS * tm
    hi = lax.Precision.HIGHEST
    expert = route[0:2].reshape(n_slots)
    onehot = (expert[:, None] == jnp.arange(N_EXPERTS, dtype=F32)[None, :]).astype(F32)
    blocks = onehot.reshape(n_slots // blk, blk, N_EXPERTS)
    tri = np.tril(np.ones((blk, blk), np.float32))
    local = jnp.einsum("ij,bjk->bik", tri, blocks, precision=hi)
    totals = local[:, -1, :]
    strict = np.tril(np.ones((n_slots // blk,) * 2, np.float32), -1)
    before = jnp.dot(strict, totals, precision=hi)
    rank = jnp.sum((local + before[:, None, :]) * blocks, axis=-1).reshape(n_slots) - 1.0
    counts = before[-1] + totals[-1]
    padded = jnp.ceil(counts / tm) * tm
    seg_end = jnp.dot(np.tril(np.ones((N_EXPERTS,) * 2, np.float32)), padded, precision=hi)
    seg_start = seg_end - padded
    dest = (jnp.sum(onehot * seg_start[None, :], axis=-1) + rank).astype(jnp.int32)
    tile_start = jnp.arange(n_rows // tm, dtype=F32) * tm
    tile_expert = jnp.sum((seg_end[None, :] <= tile_start[:, None]).astype(jnp.int32), axis=-1)
    tile_expert = jnp.minimum(tile_expert, N_EXPERTS - 1).astype(jnp.int32)
    n_tiles = (seg_end[-1] / tm).astype(jnp.int32).reshape(1)
    tail_start = jnp.maximum(seg_end - tm, 0.0).astype(jnp.int32)
    return tile_expert, n_tiles, tail_start, dest, n_rows


def _moe(x1, x1r, route, w_gate, w_up, w_down, layer, g, b):
    n = x1.shape[0]
    tile_expert, n_tiles, tail_start, dest, n_rows = _dispatch_plan(route, n)
    xs = _dispatch(x1r, dest, tail_start, n_tiles, n_rows)
    y_rows = _experts(xs, w_gate, w_up, w_down, layer, tile_expert, n_tiles)
    return _combine(y_rows, x1, route, dest, g, b)


def _vec(v):
    return v.reshape(1, -1).astype(F32)


def kernel(x, w_in_attn, w_out_attn, rel_bias, rpb_2d, w_pw1, b_pw1, w_dw, b_dw, conv_ln_g, conv_ln_b, w_pw2, b_pw2, ln_mix_g, ln_mix_b, ln_ffn_g, ln_ffn_b, w_router, router_bias, w_gate, w_up, w_down):
    bsz, seq, d = x.shape
    n = bsz * seq
    h = x.reshape(n, d)
    wr_t = w_router.T.astype(F32)
    wr_hi = wr_t.astype(BF16)
    wr_lo = (wr_t - wr_hi.astype(F32)).astype(BF16)
    wr = jnp.concatenate([wr_hi, wr_lo], axis=0)
    rb = router_bias.reshape(N_EXPERTS, 1).astype(F32)
    n_dil_cols = len(DILATIONS) * GROUP_COLS
    q_cols = np.zeros((3 * d,), np.float32) + 1.0
    q_cols[0:n_dil_cols] = HEAD_DIM ** -0.5
    q_cols[3 * n_dil_cols:3 * n_dil_cols + GROUP_COLS] = HEAD_DIM ** -0.5

    for layer in range(DEPTH):
        i = layer // 2
        g_mix, b_mix = _vec(ln_mix_g[layer]), _vec(ln_mix_b[layer])
        if layer % 2 == 0:
            w_in = (w_in_attn[i] * q_cols[None, :]).astype(BF16)
            proj = _matmul(h, w_in)
            lhs = []
            for grp in range(len(DILATIONS)):
                lhs += _dilated_attention(proj, _dilated_bias(rel_bias, grp), grp, bsz, seq)
            lhs.append(_neighborhood_attention(proj, _na_bias(rpb_2d[i]), bsz, seq))
            x1, x1r, route = _post_mixer(_post_attn_kernel, "post_attn", lhs,
                                         [w_out_attn[i].astype(BF16)], h, g_mix, b_mix, wr, rb)
        else:
            u = _pw1_glu(h, w_pw1[i].astype(BF16), _vec(b_pw1[i]))
            u = _dwconv_ln_silu(u.reshape(bsz, seq, d), w_dw[i].astype(F32), _vec(b_dw[i]),
                                _vec(conv_ln_g[i]), _vec(conv_ln_b[i]))
            x1, x1r, route = _post_mixer(_post_conv_kernel, "post_conv", [u.reshape(n, d)],
                                         [w_pw2[i].astype(BF16), _vec(b_pw2[i])], h, g_mix, b_mix, wr, rb)
        h = _moe(x1, x1r, route, w_gate, w_up, w_down, layer,
                 _vec(ln_ffn_g[layer]), _vec(ln_ffn_b[layer]))
    return h.reshape(bsz, seq, d)
```

```python
import functools
import math

import jax
import jax.numpy as jnp
import numpy as np
from jax import lax
from jax.experimental import pallas as pl
from jax.experimental.pallas import tpu as pltpu

D_MODEL = 1024
HEAD_DIM = 64
GROUP_HEADS = 4
GROUP_COLS = GROUP_HEADS * HEAD_DIM
DILATIONS = (1, 4, 16)
QUERY_BLOCK = 64
GRID_W = 64
NA_ROWS = 8
NA_COLS = 16
T5_BUCKETS = 32
T5_MAX_DIST = 1024
CONV_WIDTH = 31
CONV_HALO = 16
N_EXPERTS = 16
GROUP_EXPERTS = 4
D_EXPERT = 512
DEPTH = 4
ALPHA = (2 * DEPTH) ** 0.25
LN_EPS = 1e-5
NEG = -1e30

LANES = 128
ROW_CHUNKS = D_MODEL // LANES
GROUP_CHUNKS = GROUP_COLS // LANES
SEQ_TILE = 1024
EXPERT_TILE = 256
COMBINE_TILE = 256
VMEM_LIMIT = 56 * 1024 * 1024

F32 = jnp.float32
BF16 = jnp.bfloat16


def _params(semantics, vmem=VMEM_LIMIT):
    return pltpu.CompilerParams(dimension_semantics=semantics, vmem_limit_bytes=vmem)


def _matmul_kernel(x_ref, w_ref, o_ref):
    y = jnp.dot(x_ref[...].astype(BF16), w_ref[...], preferred_element_type=F32)
    for c in range(o_ref.shape[0]):
        o_ref[c] = y[:, c * LANES:(c + 1) * LANES]


def _matmul(x, w, tm=1024, tn=768):
    n, k = x.shape
    _, m = w.shape
    return pl.pallas_call(
        _matmul_kernel,
        grid=(n // tm, m // tn),
        in_specs=[pl.BlockSpec((tm, k), lambda i, j: (i, 0)),
                  pl.BlockSpec((k, tn), lambda i, j: (0, j))],
        out_specs=pl.BlockSpec((tn // LANES, tm, LANES), lambda i, j: (j, i, 0)),
        out_shape=jax.ShapeDtypeStruct((m // LANES, n, LANES), F32),
        compiler_params=_params(("parallel", "arbitrary")),
        name="qkv_proj",
    )(x, w)


def _glu_kernel(x_ref, wa_ref, wg_ref, ba_ref, bg_ref, o_ref):
    xb = x_ref[...].astype(BF16)
    a = jnp.dot(xb, wa_ref[...], preferred_element_type=F32) + ba_ref[...]
    g = jnp.dot(xb, wg_ref[...], preferred_element_type=F32) + bg_ref[...]
    o_ref[...] = a * jax.nn.sigmoid(g)


def _pw1_glu(x, w, b, tm=1024, tn=512):
    n, k = x.shape
    half = w.shape[1] // 2
    nj = half // tn
    return pl.pallas_call(
        _glu_kernel,
        grid=(n // tm, nj),
        in_specs=[pl.BlockSpec((tm, k), lambda i, j: (i, 0)),
                  pl.BlockSpec((k, tn), lambda i, j: (0, j)),
                  pl.BlockSpec((k, tn), lambda i, j: (0, j + nj)),
                  pl.BlockSpec((1, tn), lambda i, j: (0, j)),
                  pl.BlockSpec((1, tn), lambda i, j: (0, j + nj))],
        out_specs=pl.BlockSpec((tm, tn), lambda i, j: (i, j)),
        out_shape=jax.ShapeDtypeStruct((n, half), F32),
        compiler_params=_params(("parallel", "arbitrary")),
        name="pw1_glu",
    )(x, w, w, b, b)


def _head_masks(rows):
    lane = lax.broadcasted_iota(jnp.int32, (rows, GROUP_COLS), 1)
    return [(lane >= HEAD_DIM * h) & (lane < HEAD_DIM * (h + 1)) for h in range(GROUP_HEADS)]


def _stack_heads(q, masks):
    return jnp.concatenate([jnp.where(m, q, 0.0) for m in masks], axis=0)


def _unstack_heads(o4, masks, rows):
    out = jnp.where(masks[0], o4[0:rows], 0.0)
    for h in range(1, GROUP_HEADS):
        out = out + jnp.where(masks[h], o4[h * rows:(h + 1) * rows], 0.0)
    return out


def _softmax_pv(logits, v, with_lse):
    m = jnp.max(logits, axis=-1, keepdims=True)
    p = jnp.exp(logits - m)
    s = jnp.sum(p, axis=-1, keepdims=True)
    o4 = jnp.dot(p.astype(BF16), v, preferred_element_type=F32) * (1.0 / s)
    lse = (m + jnp.log(s)) if with_lse else None
    return o4, lse


def _dilated_kernel(q_ref, kp_ref, kc_ref, kn_ref, vp_ref, vc_ref, vn_ref, bias_ref,
                    o_ref, l_ref, *, dil):
    t = pl.program_id(1)
    chunk = QUERY_BLOCK * dil
    n_chunks = SEQ_TILE // chunk
    qb = QUERY_BLOCK
    masks = _head_masks(qb)
    col = lax.broadcasted_iota(jnp.int32, (GROUP_HEADS * qb, 3 * qb), 1)
    no_prev = jnp.where(col < qb, NEG, 0.0) * (t == 0).astype(F32)
    no_next = jnp.where(col >= 2 * qb, NEG, 0.0) * (t == pl.num_programs(1) - 1).astype(F32)
    bias = bias_ref[...]

    def sl(start):
        return pl.ds(start, qb) if dil == 1 else pl.ds(start, qb, stride=dil)

    def rows(ref, start):
        return jnp.concatenate([ref[c, sl(start), :] for c in range(GROUP_CHUNKS)], axis=-1)

    def window(p_ref, c_ref, n_ref, c, r):
        parts = []
        for cc in (c - 1, c, c + 1):
            if cc < 0:
                parts.append(rows(p_ref, r))
            elif cc >= n_chunks:
                parts.append(rows(n_ref, r))
            else:
                parts.append(rows(c_ref, cc * chunk + r))
        return jnp.concatenate(parts, axis=0).astype(BF16)

    for c in range(n_chunks):
        for r in range(dil):
            base = c * chunk + r
            q4 = _stack_heads(rows(q_ref, base), masks).astype(BF16)
            k = window(kp_ref, kc_ref, kn_ref, c, r)
            v = window(vp_ref, vc_ref, vn_ref, c, r)
            logits = lax.dot_general(q4, k, (((1,), (1,)), ((), ())),
                                     preferred_element_type=F32) + bias
            if c == 0:
                logits = logits + no_prev
            if c == n_chunks - 1:
                logits = logits + no_next
            o4, lse = _softmax_pv(logits, v, True)
            o = _unstack_heads(o4, masks, qb)
            lb = _unstack_heads(jnp.broadcast_to(lse, (GROUP_HEADS * qb, GROUP_COLS)), masks, qb)
            for ch in range(GROUP_CHUNKS):
                o_ref[ch, sl(base), :] = o[:, ch * LANES:(ch + 1) * LANES]
                l_ref[ch, sl(base), :] = lb[:, ch * LANES:(ch + 1) * LANES]


def _tile_spec(col_block, seq):
    tiles = seq // SEQ_TILE
    return pl.BlockSpec((GROUP_CHUNKS, SEQ_TILE, LANES), lambda b, t: (col_block, b * tiles + t, 0))


def _halo_specs(col_block, halo, seq):
    per_tile = SEQ_TILE // halo
    per_seq = seq // halo
    return [
        pl.BlockSpec((GROUP_CHUNKS, halo, LANES),
                     lambda b, t: (col_block, b * per_seq + jnp.maximum(t * per_tile - 1, 0), 0)),
        _tile_spec(col_block, seq),
        pl.BlockSpec((GROUP_CHUNKS, halo, LANES),
                     lambda b, t: (col_block, b * per_seq + jnp.minimum((t + 1) * per_tile, per_seq - 1), 0)),
    ]


def _dilated_attention(proj, bias, group, bsz, seq):
    dil = DILATIONS[group]
    n_groups = len(DILATIONS)
    q_col, k_col, v_col = group, n_groups + group, 2 * n_groups + group
    out_spec = _tile_spec(0, seq)
    out_sds = jax.ShapeDtypeStruct((GROUP_CHUNKS, bsz * seq, LANES), F32)
    return pl.pallas_call(
        functools.partial(_dilated_kernel, dil=dil),
        grid=(bsz, seq // SEQ_TILE),
        in_specs=[_tile_spec(q_col, seq)]
        + _halo_specs(k_col, QUERY_BLOCK * dil, seq) + _halo_specs(v_col, QUERY_BLOCK * dil, seq)
        + [pl.BlockSpec(bias.shape, lambda b, t: (0, 0))],
        out_specs=[out_spec, out_spec],
        out_shape=[out_sds, out_sds],
        compiler_params=_params(("parallel", "arbitrary")),
        name=f"dilated_attn_d{dil}",
    )(proj, proj, proj, proj, proj, proj, proj, bias)


NA_HALO = 256
NA_KEYS = NA_ROWS * GRID_W


def _na_kernel(q_ref, kp_ref, kc_ref, kn_ref, vp_ref, vc_ref, vn_ref, bias_ref, o_ref,
               kwin, vwin, *, grid_rows):
    t = pl.program_id(1)
    tile_rows = SEQ_TILE // GRID_W
    half = NA_ROWS // 2
    for win, p_ref, c_ref, n_ref in ((kwin, kp_ref, kc_ref, kn_ref), (vwin, vp_ref, vc_ref, vn_ref)):
        for ch in range(GROUP_CHUNKS):
            lanes = slice(ch * LANES, (ch + 1) * LANES)
            win[0:NA_HALO, lanes] = p_ref[ch].astype(BF16)
            win[NA_HALO:NA_HALO + SEQ_TILE, lanes] = c_ref[ch].astype(BF16)
            win[NA_HALO + SEQ_TILE:, lanes] = n_ref[ch].astype(BF16)
    masks = _head_masks(GRID_W)
    for i in range(tile_rows):
        row = t * tile_rows + i
        row0 = jnp.clip(row - half, 0, grid_rows - NA_ROWS)
        off = pl.multiple_of((row0 - (t * tile_rows - half)) * GRID_W, GRID_W)
        q = jnp.concatenate([q_ref[ch, i * GRID_W:(i + 1) * GRID_W, :] for ch in range(GROUP_CHUNKS)], axis=-1)
        q4 = _stack_heads(q, masks).astype(BF16)
        k = kwin[pl.ds(off, NA_KEYS), :]
        v = vwin[pl.ds(off, NA_KEYS), :]
        logits = lax.dot_general(q4, k, (((1,), (1,)), ((), ())),
                                 preferred_element_type=F32) + bias_ref[row - row0]
        o4, _ = _softmax_pv(logits, v, False)
        o = _unstack_heads(o4, masks, GRID_W)
        for ch in range(GROUP_CHUNKS):
            o_ref[ch, i * GRID_W:(i + 1) * GRID_W, :] = o[:, ch * LANES:(ch + 1) * LANES]


def _neighborhood_attention(proj, bias, bsz, seq):
    base = 3 * len(DILATIONS)
    win_rows = SEQ_TILE + 2 * NA_HALO
    return pl.pallas_call(
        functools.partial(_na_kernel, grid_rows=seq // GRID_W),
        grid=(bsz, seq // SEQ_TILE),
        in_specs=[_tile_spec(base, seq)]
        + _halo_specs(base + 1, NA_HALO, seq) + _halo_specs(base + 2, NA_HALO, seq)
        + [pl.BlockSpec(bias.shape, lambda b, t: (0, 0, 0))],
        out_specs=_tile_spec(0, seq),
        out_shape=jax.ShapeDtypeStruct((GROUP_CHUNKS, bsz * seq, LANES), F32),
        scratch_shapes=[pltpu.VMEM((win_rows, GROUP_COLS), BF16),
                        pltpu.VMEM((win_rows, GROUP_COLS), BF16)],
        compiler_params=_params(("parallel", "arbitrary")),
        name="neighborhood_attn",
    )(proj, proj, proj, proj, proj, proj, proj, bias)


def _t5_bucket_table(dil):
    qb = QUERY_BLOCK
    rel = (np.arange(3 * qb)[None, :] - qb - np.arange(qb)[:, None])
    nb = T5_BUCKETS // 2
    max_exact = nb // 2
    n = np.abs(rel * dil)
    nf = np.maximum(n, 1).astype(np.float32)
    large = max_exact + (np.log(nf / np.float32(max_exact)) / np.float32(math.log(T5_MAX_DIST / max_exact))
                         * np.float32(nb - max_exact)).astype(np.int32)
    large = np.minimum(large, nb - 1)
    bucket = np.where(rel * dil > 0, nb, 0) + np.where(n < max_exact, n, large)
    return bucket.astype(np.int32), np.abs(rel) <= qb


def _dilated_bias(rel_bias, group):
    bucket, band = _t5_bucket_table(DILATIONS[group])
    tab = rel_bias[:, group * GROUP_HEADS:(group + 1) * GROUP_HEADS].astype(F32)
    onehot = (bucket[None, :, :] == np.arange(T5_BUCKETS)[:, None, None]).astype(np.float32)
    bias = jnp.einsum("bh,bqk->hqk", tab, onehot, precision=lax.Precision.HIGHEST)
    bias = jnp.where(band[None], bias, NEG)
    return bias.reshape(GROUP_HEADS * QUERY_BLOCK, 3 * QUERY_BLOCK)


def _na_bias(rpb):
    qc = np.arange(GRID_W)[:, None]
    kc = np.arange(GRID_W)[None, :]
    win0 = np.clip(qc - NA_COLS // 2, 0, GRID_W - NA_COLS)
    col_ok = (kc >= win0) & (kc < win0 + NA_COLS)
    dc = np.clip(kc - qc + NA_COLS - 1, 0, 2 * NA_COLS - 2)
    shift = np.arange(NA_ROWS)[:, None]
    kr = np.arange(NA_ROWS)[None, :]
    dr = kr - shift + NA_ROWS - 1
    pick_row = (dr[:, :, None] == np.arange(2 * NA_ROWS - 1)[None, None, :]).astype(np.float32)
    pick_col = (dc[None, :, :] == np.arange(2 * NA_COLS - 1)[:, None, None]).astype(np.float32)
    hi = lax.Precision.HIGHEST
    rows = jnp.einsum("hac,dka->hdkc", rpb.astype(F32), pick_row, precision=hi)
    b = jnp.einsum("hdkc,cqx->dhqkx", rows, pick_col, precision=hi)
    b = jnp.where(col_ok[None, None, :, None, :], b, NEG)
    return b.reshape(NA_ROWS, GROUP_HEADS * GRID_W, NA_KEYS)


def _layer_norm(z, g, b):
    mu = jnp.mean(z, axis=-1, keepdims=True)
    zc = z - mu
    var = jnp.mean(zc * zc, axis=-1, keepdims=True)
    return zc * lax.rsqrt(var + LN_EPS) * g + b


def _row(x, i):
    return x[i:i + 1, :]


def _route(logits, rbias):
    scores = jax.nn.sigmoid(logits)
    sel = scores + rbias
    ge = GROUP_EXPERTS
    n_groups = N_EXPERTS // ge
    best_sum, best_g = None, None
    for g in range(n_groups):
        vals = [_row(sel, g * ge + j) for j in range(ge)]
        top2 = None
        for a in range(ge):
            for b in range(a + 1, ge):
                pair = vals[a] + vals[b]
                top2 = pair if top2 is None else jnp.maximum(top2, pair)
        if best_sum is None:
            best_sum, best_g = top2, jnp.zeros_like(top2, dtype=jnp.int32)
        else:
            upd = top2 > best_sum
            best_g = jnp.where(upd, g, best_g)
            best_sum = jnp.where(upd, top2, best_sum)

    def in_group(x, j):
        out = _row(x, j)
        for g in range(1, n_groups):
            out = jnp.where(best_g == g, _row(x, g * ge + j), out)
        return out

    cand = [in_group(sel, j) for j in range(ge)]
    cand_score = [in_group(scores, j) for j in range(ge)]
    v1, j1, w1 = cand[0], jnp.zeros_like(best_g), cand_score[0]
    for j in range(1, ge):
        upd = cand[j] > v1
        v1 = jnp.where(upd, cand[j], v1)
        j1 = jnp.where(upd, j, j1)
        w1 = jnp.where(upd, cand_score[j], w1)
    v2 = jnp.full_like(v1, -jnp.inf)
    j2, w2 = jnp.zeros_like(best_g), jnp.zeros_like(w1)
    for j in range(ge):
        upd = (cand[j] > v2) & (j1 != j)
        v2 = jnp.where(upd, cand[j], v2)
        j2 = jnp.where(upd, j, j2)
        w2 = jnp.where(upd, cand_score[j], w2)
    wsum = w1 + w2
    e1 = (best_g * ge + j1).astype(F32)
    e2 = (best_g * ge + j2).astype(F32)
    zero = jnp.zeros_like(w1)
    return jnp.concatenate([e1, e2, w1 / wsum, w2 / wsum, zero, zero, zero, zero], axis=0)


def _post_tail(m, x_ref, g_ref, b_ref, wr_ref, rb_ref, x1_ref, x1r_ref, route_ref):
    tm = m.shape[0]
    xn = _layer_norm(ALPHA * x_ref[...] + m, g_ref[...], b_ref[...])
    x1_ref[...] = xn
    for j in range(ROW_CHUNKS):
        x1r_ref[pl.ds(j, tm, stride=ROW_CHUNKS), :] = xn[:, j * LANES:(j + 1) * LANES]
    xh = xn.astype(BF16)
    xl = (xn - xh.astype(F32)).astype(BF16)
    wr = wr_ref[...]
    contract = (((1,), (1,)), ((), ()))
    a = lax.dot_general(wr, xh, contract, preferred_element_type=F32)
    c = lax.dot_general(wr[0:N_EXPERTS], xl, contract, preferred_element_type=F32)
    logits = a[0:N_EXPERTS] + a[N_EXPERTS:] + c
    route_ref[...] = _route(logits, rb_ref[...])


def _post_attn_kernel(o0_ref, l0_ref, o1_ref, l1_ref, o2_ref, l2_ref, ob_ref, w_ref,
                      x_ref, g_ref, b_ref, wr_ref, rb_ref, x1_ref, x1r_ref, route_ref):
    def wide(ref):
        return jnp.concatenate([ref[ch] for ch in range(GROUP_CHUNKS)], axis=-1)

    l0, l1, l2 = wide(l0_ref), wide(l1_ref), wide(l2_ref)
    mx = jnp.maximum(jnp.maximum(l0, l1), l2)
    e0, e1, e2 = jnp.exp(l0 - mx), jnp.exp(l1 - mx), jnp.exp(l2 - mx)
    inv = 1.0 / (e0 + e1 + e2)
    lhs = jnp.concatenate([wide(o0_ref) * (e0 * inv), wide(o1_ref) * (e1 * inv),
                           wide(o2_ref) * (e2 * inv), wide(ob_ref)], axis=-1).astype(BF16)
    m = jnp.dot(lhs, w_ref[...], preferred_element_type=F32)
    _post_tail(m, x_ref, g_ref, b_ref, wr_ref, rb_ref, x1_ref, x1r_ref, route_ref)


def _post_conv_kernel(u_ref, w_ref, wb_ref, x_ref, g_ref, b_ref, wr_ref, rb_ref,
                      x1_ref, x1r_ref, route_ref):
    m = jnp.dot(u_ref[...], w_ref[...], preferred_element_type=F32) + wb_ref[...]
    _post_tail(m, x_ref, g_ref, b_ref, wr_ref, rb_ref, x1_ref, x1r_ref, route_ref)


def _post_mixer(kernel, name, lhs_list, consts_front, x, g, b, wr, rb, tm=512):
    n, d = x.shape
    row = lambda i: (i, 0)
    const = lambda i: (0, 0)
    in_specs = [pl.BlockSpec((tm, a.shape[1]), row) if a.ndim == 2
                else pl.BlockSpec((a.shape[0], tm, LANES), lambda i: (0, i, 0)) for a in lhs_list]
    in_specs += [pl.BlockSpec(c.shape, const) for c in consts_front]
    in_specs += [pl.BlockSpec((tm, d), row)]
    in_specs += [pl.BlockSpec(c.shape, const) for c in (g, b, wr, rb)]
    return pl.pallas_call(
        kernel,
        grid=(n // tm,),
        in_specs=in_specs,
        out_specs=[pl.BlockSpec((tm, d), row),
                   pl.BlockSpec((tm * ROW_CHUNKS, LANES), row),
                   pl.BlockSpec((8, tm), lambda i: (0, i))],
        out_shape=[jax.ShapeDtypeStruct((n, d), F32),
                   jax.ShapeDtypeStruct((n * ROW_CHUNKS, LANES), F32),
                   jax.ShapeDtypeStruct((8, n), F32)],
        compiler_params=_params(("parallel",)),
        name=name,
    )(*lhs_list, *consts_front, x, g, b, wr, rb)


def _dwconv_kernel(up_ref, uc_ref, un_ref, w_ref, cb_ref, g_ref, b_ref, o_ref, win, acc, *, ts):
    t = pl.program_id(1)
    has_prev = (t > 0).astype(F32)
    has_next = (t < pl.num_programs(1) - 1).astype(F32)
    for j in range(ROW_CHUNKS):
        lanes = slice(j * LANES, (j + 1) * LANES)
        win[j, 0:CONV_HALO, :] = up_ref[:, lanes] * has_prev
        win[j, CONV_HALO:CONV_HALO + ts, :] = uc_ref[:, lanes]
        win[j, CONV_HALO + ts:, :] = un_ref[:, lanes] * has_next

    def body(j, carry):
        a = jnp.zeros((ts, LANES), F32)
        for tap in range(CONV_WIDTH):
            start = CONV_HALO - CONV_WIDTH // 2 + tap
            a = a + win[j, pl.ds(start, ts), :] * w_ref[j, tap:tap + 1, :]
        acc[j] = a
        return carry

    lax.fori_loop(0, ROW_CHUNKS, body, 0)
    u = jnp.concatenate([acc[j] for j in range(ROW_CHUNKS)], axis=-1) + cb_ref[...]
    y = _layer_norm(u, g_ref[...], b_ref[...])
    o_ref[...] = (y * jax.nn.sigmoid(y)).astype(o_ref.dtype)


def _dwconv_ln_silu(u, w_dw, b_dw, g, b, ts=256):
    bsz, seq, d = u.shape
    per_tile = ts // CONV_HALO
    last = seq // CONV_HALO - 1
    w3 = jnp.pad(w_dw, ((0, 32 - CONV_WIDTH), (0, 0))).reshape(32, ROW_CHUNKS, LANES).transpose(1, 0, 2)
    vec = pl.BlockSpec((1, d), lambda bb, t: (0, 0))
    return pl.pallas_call(
        functools.partial(_dwconv_kernel, ts=ts),
        grid=(bsz, seq // ts),
        in_specs=[pl.BlockSpec((None, CONV_HALO, d), lambda bb, t: (bb, jnp.maximum(t * per_tile - 1, 0), 0)),
                  pl.BlockSpec((None, ts, d), lambda bb, t: (bb, t, 0)),
                  pl.BlockSpec((None, CONV_HALO, d), lambda bb, t: (bb, jnp.minimum((t + 1) * per_tile, last), 0)),
                  pl.BlockSpec(w3.shape, lambda bb, t: (0, 0, 0)),
                  vec, vec, vec],
        out_specs=pl.BlockSpec((None, ts, d), lambda bb, t: (bb, t, 0)),
        out_shape=jax.ShapeDtypeStruct((bsz, seq, d), BF16),
        scratch_shapes=[pltpu.VMEM((ROW_CHUNKS, ts + 2 * CONV_HALO, LANES), F32),
                        pltpu.VMEM((ROW_CHUNKS, ts, LANES), F32)],
        compiler_params=_params(("parallel", "arbitrary")),
        name="dwconv_ln_silu",
    )(u, u, u, w3, b_dw, g, b)


def _row_gather_start(src_hbm, dst, sem, idx_ref, first, count):
    def body(r, carry):
        tok = idx_ref[first + r]
        pltpu.make_async_copy(src_hbm.at[pl.ds(pl.multiple_of(tok * ROW_CHUNKS, ROW_CHUNKS), ROW_CHUNKS)],
                              dst.at[pl.ds(pl.multiple_of(r * ROW_CHUNKS, ROW_CHUNKS), ROW_CHUNKS)],
                              sem).start()
        return carry
    lax.fori_loop(0, count, body, 0, unroll=8)


def _row_gather_wait(src_hbm, dst, sem):
    pltpu.make_async_copy(src_hbm.at[pl.ds(0, dst.shape[0])], dst, sem).wait()


def _dispatch_kernel(dest_ref, tail_ref, n_tiles_ref, x_ref, xs_hbm, zeros, stage, sem, zsem, *, n_tokens, tm):
    i = pl.program_id(0)
    n_steps = pl.num_programs(0)
    slot = i % 2
    tile_rows = EXPERT_TILE * ROW_CHUNKS
    all_tiles = xs_hbm.shape[0] // tile_rows

    def zero_tile(first_row):
        start = pl.multiple_of(first_row * ROW_CHUNKS, ROW_CHUNKS)
        return pltpu.make_async_copy(zeros, xs_hbm.at[pl.ds(start, tile_rows)], zsem)

    @pl.when(i == 0)
    def _():
        zeros[...] = jnp.zeros_like(zeros)
        for e in range(N_EXPERTS):
            zero_tile(tail_ref[e]).start()
        for e in range(N_EXPERTS):
            zero_tile(tail_ref[e]).wait()

        def unused(j, carry):
            cp = zero_tile(j * EXPERT_TILE)
            cp.start()
            cp.wait()
            return carry
        lax.fori_loop(n_tiles_ref[0], all_tiles, unused, 0)

    stage[slot] = x_ref[...]

    def body(r, carry):
        src = stage.at[slot, pl.ds(pl.multiple_of(r * ROW_CHUNKS, ROW_CHUNKS), ROW_CHUNKS)]
        for k in range(2):
            d = dest_ref[k * n_tokens + i * tm + r]
            pltpu.make_async_copy(src, xs_hbm.at[pl.ds(pl.multiple_of(d * ROW_CHUNKS, ROW_CHUNKS), ROW_CHUNKS)],
                                  sem.at[slot, k]).start()
        return carry
    lax.fori_loop(0, tm, body, 0, unroll=8)

    def wait_step(s):
        for k in range(2):
            pltpu.make_async_copy(stage.at[s], xs_hbm.at[pl.ds(0, tm * ROW_CHUNKS)], sem.at[s, k]).wait()

    @pl.when(i > 0)
    def _():
        wait_step(1 - slot)

    @pl.when(i == n_steps - 1)
    def _():
        wait_step(slot)


def _dispatch(x1r, dest, tail_start, n_tiles, n_rows, tm=512):
    n_tokens = x1r.shape[0] // ROW_CHUNKS
    grid_spec = pltpu.PrefetchScalarGridSpec(
        num_scalar_prefetch=3,
        grid=(n_tokens // tm,),
        in_specs=[pl.BlockSpec((tm * ROW_CHUNKS, LANES), lambda i, dst, tail, nt: (i, 0))],
        out_specs=pl.BlockSpec(memory_space=pl.ANY),
        scratch_shapes=[pltpu.VMEM((EXPERT_TILE * ROW_CHUNKS, LANES), F32),
                        pltpu.VMEM((2, tm * ROW_CHUNKS, LANES), F32),
                        pltpu.SemaphoreType.DMA((2, 2)),
                        pltpu.SemaphoreType.DMA(())],
    )
    return pl.pallas_call(
        functools.partial(_dispatch_kernel, n_tokens=n_tokens, tm=tm),
        grid_spec=grid_spec,
        out_shape=jax.ShapeDtypeStruct((n_rows * ROW_CHUNKS, LANES), F32),
        compiler_params=_params(("arbitrary",)),
        name="moe_dispatch",
    )(dest, tail_start, n_tiles, x1r)


def _expert_kernel(tile_expert_ref, n_tiles_ref, x_ref, wg_ref, wu_ref, wd_ref, y_ref,
                   wg_bf, wu_bf, wd_bf):
    i = pl.program_id(0)
    tm = EXPERT_TILE
    prev = tile_expert_ref[jnp.maximum(i - 1, 0)]

    @pl.when((i == 0) | (tile_expert_ref[i] != prev))
    def _():
        wg_bf[...] = wg_ref[...].astype(BF16)
        wu_bf[...] = wu_ref[...].astype(BF16)
        wd_bf[...] = wd_ref[...].astype(BF16)

    @pl.when(i < n_tiles_ref[0])
    def _():
        x = jnp.concatenate([x_ref[pl.ds(j, tm, stride=ROW_CHUNKS), :] for j in range(ROW_CHUNKS)],
                            axis=-1).astype(BF16)
        hg = jnp.dot(x, wg_bf[...], preferred_element_type=F32)
        hu = jnp.dot(x, wu_bf[...], preferred_element_type=F32)
        hid = (hg * jax.nn.sigmoid(hg) * hu).astype(BF16)
        y = jnp.dot(hid, wd_bf[...], preferred_element_type=F32)
        for j in range(ROW_CHUNKS):
            y_ref[pl.ds(j, tm, stride=ROW_CHUNKS), :] = y[:, j * LANES:(j + 1) * LANES]

    @pl.when(i >= n_tiles_ref[0])
    def _():
        y_ref[...] = jnp.zeros_like(y_ref)


def _experts(xs, w_gate, w_up, w_down, layer, tile_expert, n_tiles):
    tm = EXPERT_TILE
    n_rows = xs.shape[0] // ROW_CHUNKS
    d, de = w_gate.shape[2], w_gate.shape[3]
    row_map = lambda i, te, nt: (jnp.minimum(i, jnp.maximum(nt[0] - 1, 0)), 0)
    w_map = lambda i, te, nt: (layer, te[i], 0, 0)
    grid_spec = pltpu.PrefetchScalarGridSpec(
        num_scalar_prefetch=2,
        grid=(n_rows // tm,),
        in_specs=[pl.BlockSpec((tm * ROW_CHUNKS, LANES), row_map),
                  pl.BlockSpec((None, None, d, de), w_map),
                  pl.BlockSpec((None, None, d, de), w_map),
                  pl.BlockSpec((None, None, de, d), w_map)],
        out_specs=pl.BlockSpec((tm * ROW_CHUNKS, LANES), lambda i, te, nt: (i, 0)),
        scratch_shapes=[pltpu.VMEM((d, de), BF16), pltpu.VMEM((d, de), BF16), pltpu.VMEM((de, d), BF16)],
    )
    return pl.pallas_call(
        _expert_kernel,
        grid_spec=grid_spec,
        out_shape=jax.ShapeDtypeStruct((n_rows * ROW_CHUNKS, LANES), F32),
        compiler_params=_params(("arbitrary",)),
        name="moe_experts",
    )(tile_expert, n_tiles, xs, w_gate, w_up, w_down)


def _combine_kernel(dest_ref, y_hbm, x_ref, route_ref, g_ref, b_ref, o_ref, ybuf, sem, *, n_tokens):
    i = pl.program_id(0)
    n_steps = pl.num_programs(0)
    slot = i % 2
    tm = COMBINE_TILE

    def start(step, s):
        for k in range(2):
            _row_gather_start(y_hbm, ybuf.at[s, k], sem.at[s, k], dest_ref, k * n_tokens + step * tm, tm)

    @pl.when(i == 0)
    def _():
        start(0, 0)

    @pl.when(i + 1 < n_steps)
    def _():
        start(i + 1, 1 - slot)

    for k in range(2):
        _row_gather_wait(y_hbm, ybuf.at[slot, k], sem.at[slot, k])
    eye = (lax.broadcasted_iota(jnp.int32, (tm, tm), 0) == lax.broadcasted_iota(jnp.int32, (tm, tm), 1))
    route = route_ref[...]
    gate = [jnp.sum(jnp.where(eye, route[2 + k:3 + k, :], 0.0), axis=-1, keepdims=True) for k in range(2)]
    parts = []
    for j in range(ROW_CHUNKS):
        y0 = ybuf[slot, 0, pl.ds(j, tm, stride=ROW_CHUNKS), :]
        y1 = ybuf[slot, 1, pl.ds(j, tm, stride=ROW_CHUNKS), :]
        parts.append(gate[0] * y0 + gate[1] * y1)
    f = jnp.concatenate(parts, axis=-1)
    o_ref[...] = _layer_norm(ALPHA * x_ref[...] + f, g_ref[...], b_ref[...])


def _combine(y_rows, x1, route, dest, g, b):
    n, d = x1.shape
    tm = COMBINE_TILE
    grid_spec = pltpu.PrefetchScalarGridSpec(
        num_scalar_prefetch=1,
        grid=(n // tm,),
        in_specs=[pl.BlockSpec(memory_space=pl.ANY),
                  pl.BlockSpec((tm, d), lambda i, dst: (i, 0)),
                  pl.BlockSpec((8, tm), lambda i, dst: (0, i)),
                  pl.BlockSpec((1, d), lambda i, dst: (0, 0)),
                  pl.BlockSpec((1, d), lambda i, dst: (0, 0))],
        out_specs=pl.BlockSpec((tm, d), lambda i, dst: (i, 0)),
        scratch_shapes=[pltpu.VMEM((2, 2, tm * ROW_CHUNKS, LANES), F32),
                        pltpu.SemaphoreType.DMA((2, 2))],
    )
    return pl.pallas_call(
        functools.partial(_combine_kernel, n_tokens=n),
        grid_spec=grid_spec,
        out_shape=jax.ShapeDtypeStruct((n, d), F32),
        compiler_params=_params(("arbitrary",)),
        name="moe_combine",
    )(dest, y_rows, x1, route, g, b)


def _dispatch_plan(route, n_tokens):
    tm = EXPERT_TILE
    blk = LANES
    n_slots = 2 * n_tokens
    n_rows = n_slots + N_EXPERTS * tm
    hi = lax.Precision.HIGHEST
    expert = route[0:2].reshape(n_slots)
    onehot = (expert[:, None] == jnp.arange(N_EXPERTS, dtype=F32)[None, :]).astype(F32)
    blocks = onehot.reshape(n_slots // blk, blk, N_EXPERTS)
    tri = np.tril(np.ones((blk, blk), np.float32))
    local = jnp.einsum("ij,bjk->bik", tri, blocks, precision=hi)
    totals = local[:, -1, :]
    strict = np.tril(np.ones((n_slots // blk,) * 2, np.float32), -1)
    before = jnp.dot(strict, totals, precision=hi)
    rank = jnp.sum((local + before[:, None, :]) * blocks, axis=-1).reshape(n_slots) - 1.0
    counts = before[-1] + totals[-1]
    padded = jnp.ceil(counts / tm) * tm
    seg_end = jnp.dot(np.tril(np.ones((N_EXPERTS,) * 2, np.float32)), padded, precision=hi)
    seg_start = seg_end - padded
    dest = (jnp.sum(onehot * seg_start[None, :], axis=-1) + rank).astype(jnp.int32)
    tile_start = jnp.arange(n_rows // tm, dtype=F32) * tm
    tile_expert = jnp.sum((seg_end[None, :] <= tile_start[:, None]).astype(jnp.int32), axis=-1)
    tile_expert = jnp.minimum(tile_expert, N_EXPERTS - 1).astype(jnp.int32)
    n_tiles = (seg_end[-1] / tm).astype(jnp.int32).reshape(1)
    tail_start = jnp.maximum(seg_end - tm, 0.0).astype(jnp.int32)
    return tile_expert, n_tiles, tail_start, dest, n_rows


def _moe(x1, x1r, route, w_gate, w_up, w_down, layer, g, b):
    n = x1.shape[0]
    tile_expert, n_tiles, tail_start, dest, n_rows = _dispatch_plan(route, n)
    xs = _dispatch(x1r, dest, tail_start, n_tiles, n_rows)
    y_rows = _experts(xs, w_gate, w_up, w_down, layer, tile_expert, n_tiles)
    return _combine(y_rows, x1, route, dest, g, b)


def _vec(v):
    return v.reshape(1, -1).astype(F32)


def kernel(x, w_in_attn, w_out_attn, rel_bias, rpb_2d, w_pw1, b_pw1, w_dw, b_dw, conv_ln_g, conv_ln_b, w_pw2, b_pw2, ln_mix_g, ln_mix_b, ln_ffn_g, ln_ffn_b, w_router, router_bias, w_gate, w_up, w_down):
    bsz, seq, d = x.shape
    n = bsz * seq
    h = x.reshape(n, d)
    wr_t = w_router.T.astype(F32)
    wr_hi = wr_t.astype(BF16)
    wr_lo = (wr_t - wr_hi.astype(F32)).astype(BF16)
    wr = jnp.concatenate([wr_hi, wr_lo], axis=0)
    rb = router_bias.reshape(N_EXPERTS, 1).astype(F32)
    n_dil_cols = len(DILATIONS) * GROUP_COLS
    q_cols = np.zeros((3 * d,), np.float32) + 1.0
    q_cols[0:n_dil_cols] = HEAD_DIM ** -0.5
    q_cols[3 * n_dil_cols:3 * n_dil_cols + GROUP_COLS] = HEAD_DIM ** -0.5

    for layer in range(DEPTH):
        i = layer // 2
        g_mix, b_mix = _vec(ln_mix_g[layer]), _vec(ln_mix_b[layer])
        if layer % 2 == 0:
            w_in = (w_in_attn[i] * q_cols[None, :]).astype(BF16)
            proj = _matmul(h, w_in)
            lhs = []
            for grp in range(len(DILATIONS)):
                lhs += _dilated_attention(proj, _dilated_bias(rel_bias, grp), grp, bsz, seq)
            lhs.append(_neighborhood_attention(proj, _na_bias(rpb_2d[i]), bsz, seq))
            x1, x1r, route = _post_mixer(_post_attn_kernel, "post_attn", lhs,
                                         [w_out_attn[i].astype(BF16)], h, g_mix, b_mix, wr, rb)
        else:
            u = _pw1_glu(h, w_pw1[i].astype(BF16), _vec(b_pw1[i]))
            u = _dwconv_ln_silu(u.reshape(bsz, seq, d), w_dw[i].astype(F32), _vec(b_dw[i]),
                                _vec(conv_ln_g[i]), _vec(conv_ln_b[i]))
            x1, x1r, route = _post_mixer(_post_conv_kernel, "post_conv", [u.reshape(n, d)],
                                         [w_pw2[i].astype(BF16), _vec(b_pw2[i])], h, g_mix, b_mix, wr, rb)
        h = _moe(x1, x1r, route, w_gate, w_up, w_down, layer,
                 _vec(ln_ffn_g[layer]), _vec(ln_ffn_b[layer]))
    return h.reshape(bsz, seq, d)
```

```python
import functools
import math

import jax
import jax.numpy as jnp
import numpy as np
from jax import lax
from jax.experimental import pallas as pl
from jax.experimental.pallas import tpu as pltpu

D_MODEL = 1024
HEAD_DIM = 64
GROUP_HEADS = 4
GROUP_COLS = GROUP_HEADS * HEAD_DIM
DILATIONS = (1, 4, 16)
QUERY_BLOCK = 64
GRID_W = 64
NA_ROWS = 8
NA_COLS = 16
T5_BUCKETS = 32
T5_MAX_DIST = 1024
CONV_WIDTH = 31
CONV_HALO = 16
N_EXPERTS = 16
GROUP_EXPERTS = 4
D_EXPERT = 512
DEPTH = 4
ALPHA = (2 * DEPTH) ** 0.25
LN_EPS = 1e-5
NEG = -1e30

LANES = 128
ROW_CHUNKS = D_MODEL // LANES
GROUP_CHUNKS = GROUP_COLS // LANES
PAIR_ORDER = ((0, 1), (0, 2), (0, 3), (1, 3), (1, 2), (3, 2))
N_CLASSES = (N_EXPERTS // GROUP_EXPERTS) * len(PAIR_ORDER)
PACKED_SUBLANES = D_MODEL // (2 * LANES)
SEQ_TILE = 1024
EXPERT_TILE = 256
COMBINE_TILE = 256
VMEM_LIMIT = 56 * 1024 * 1024

F32 = jnp.float32
BF16 = jnp.bfloat16


def _params(semantics, vmem=VMEM_LIMIT):
    return pltpu.CompilerParams(dimension_semantics=semantics, vmem_limit_bytes=vmem)


def _matmul_kernel(x_ref, w_ref, o_ref):
    y = jnp.dot(x_ref[...].astype(BF16), w_ref[...], preferred_element_type=F32)
    for c in range(o_ref.shape[0]):
        o_ref[c] = y[:, c * LANES:(c + 1) * LANES]


def _matmul(x, w, tm=1024, tn=768):
    n, k = x.shape
    _, m = w.shape
    return pl.pallas_call(
        _matmul_kernel,
        grid=(n // tm, m // tn),
        in_specs=[pl.BlockSpec((tm, k), lambda i, j: (i, 0)),
                  pl.BlockSpec((k, tn), lambda i, j: (0, j))],
        out_specs=pl.BlockSpec((tn // LANES, tm, LANES), lambda i, j: (j, i, 0)),
        out_shape=jax.ShapeDtypeStruct((m // LANES, n, LANES), F32),
        compiler_params=_params(("parallel", "arbitrary")),
        name="qkv_proj",
    )(x, w)


def _glu_kernel(x_ref, wa_ref, wg_ref, ba_ref, bg_ref, o_ref):
    xb = x_ref[...].astype(BF16)
    a = jnp.dot(xb, wa_ref[...], preferred_element_type=F32) + ba_ref[...]
    g = jnp.dot(xb, wg_ref[...], preferred_element_type=F32) + bg_ref[...]
    o_ref[...] = a * jax.nn.sigmoid(g)


def _pw1_glu(x, w, b, tm=1024, tn=512):
    n, k = x.shape
    half = w.shape[1] // 2
    nj = half // tn
    return pl.pallas_call(
        _glu_kernel,
        grid=(n // tm, nj),
        in_specs=[pl.BlockSpec((tm, k), lambda i, j: (i, 0)),
                  pl.BlockSpec((k, tn), lambda i, j: (0, j)),
                  pl.BlockSpec((k, tn), lambda i, j: (0, j + nj)),
                  pl.BlockSpec((1, tn), lambda i, j: (0, j)),
                  pl.BlockSpec((1, tn), lambda i, j: (0, j + nj))],
        out_specs=pl.BlockSpec((tm, tn), lambda i, j: (i, j)),
        out_shape=jax.ShapeDtypeStruct((n, half), F32),
        compiler_params=_params(("parallel", "arbitrary")),
        name="pw1_glu",
    )(x, w, w, b, b)


def _head_masks(rows):
    lane = lax.broadcasted_iota(jnp.int32, (rows, GROUP_COLS), 1)
    return [(lane >= HEAD_DIM * h) & (lane < HEAD_DIM * (h + 1)) for h in range(GROUP_HEADS)]


def _stack_heads(q, masks):
    return jnp.concatenate([jnp.where(m, q, 0.0) for m in masks], axis=0)


def _unstack_heads(o4, masks, rows):
    out = jnp.where(masks[0], o4[0:rows], 0.0)
    for h in range(1, GROUP_HEADS):
        out = out + jnp.where(masks[h], o4[h * rows:(h + 1) * rows], 0.0)
    return out


def _softmax_pv(logits, v, with_lse):
    m = jnp.max(logits, axis=-1, keepdims=True)
    p = jnp.exp(logits - m)
    s = jnp.sum(p, axis=-1, keepdims=True)
    o4 = jnp.dot(p.astype(BF16), v, preferred_element_type=F32) * (1.0 / s)
    lse = (m + jnp.log(s)) if with_lse else None
    return o4, lse


def _dilated_kernel(q_ref, kp_ref, kc_ref, kn_ref, vp_ref, vc_ref, vn_ref, bias_ref,
                    o_ref, l_ref, *, dil):
    t = pl.program_id(1)
    chunk = QUERY_BLOCK * dil
    n_chunks = SEQ_TILE // chunk
    qb = QUERY_BLOCK
    masks = _head_masks(qb)
    col = lax.broadcasted_iota(jnp.int32, (GROUP_HEADS * qb, 3 * qb), 1)
    no_prev = jnp.where(col < qb, NEG, 0.0) * (t == 0).astype(F32)
    no_next = jnp.where(col >= 2 * qb, NEG, 0.0) * (t == pl.num_programs(1) - 1).astype(F32)
    bias = bias_ref[...]

    def sl(start):
        return pl.ds(start, qb) if dil == 1 else pl.ds(start, qb, stride=dil)

    def rows(ref, start):
        return jnp.concatenate([ref[c, sl(start), :] for c in range(GROUP_CHUNKS)], axis=-1)

    def window(p_ref, c_ref, n_ref, c, r):
        parts = []
        for cc in (c - 1, c, c + 1):
            if cc < 0:
                parts.append(rows(p_ref, r))
            elif cc >= n_chunks:
                parts.append(rows(n_ref, r))
            else:
                parts.append(rows(c_ref, cc * chunk + r))
        return jnp.concatenate(parts, axis=0).astype(BF16)

    for c in range(n_chunks):
        for r in range(dil):
            base = c * chunk + r
            q4 = _stack_heads(rows(q_ref, base), masks).astype(BF16)
            k = window(kp_ref, kc_ref, kn_ref, c, r)
            v = window(vp_ref, vc_ref, vn_ref, c, r)
            logits = lax.dot_general(q4, k, (((1,), (1,)), ((), ())),
                                     preferred_element_type=F32) + bias
            if c == 0:
                logits = logits + no_prev
            if c == n_chunks - 1:
                logits = logits + no_next
            o4, lse = _softmax_pv(logits, v, True)
            o = _unstack_heads(o4, masks, qb)
            lb = _unstack_heads(jnp.broadcast_to(lse, (GROUP_HEADS * qb, GROUP_COLS)), masks, qb)
            for ch in range(GROUP_CHUNKS):
                o_ref[ch, sl(base), :] = o[:, ch * LANES:(ch + 1) * LANES]
                l_ref[ch, sl(base), :] = lb[:, ch * LANES:(ch + 1) * LANES]


def _tile_spec(col_block, seq):
    tiles = seq // SEQ_TILE
    return pl.BlockSpec((GROUP_CHUNKS, SEQ_TILE, LANES), lambda b, t: (col_block, b * tiles + t, 0))


def _halo_specs(col_block, halo, seq):
    per_tile = SEQ_TILE // halo
    per_seq = seq // halo
    return [
        pl.BlockSpec((GROUP_CHUNKS, halo, LANES),
                     lambda b, t: (col_block, b * per_seq + jnp.maximum(t * per_tile - 1, 0), 0)),
        _tile_spec(col_block, seq),
        pl.BlockSpec((GROUP_CHUNKS, halo, LANES),
                     lambda b, t: (col_block, b * per_seq + jnp.minimum((t + 1) * per_tile, per_seq - 1), 0)),
    ]


def _dilated_attention(proj, bias, group, bsz, seq):
    dil = DILATIONS[group]
    n_groups = len(DILATIONS)
    q_col, k_col, v_col = group, n_groups + group, 2 * n_groups + group
    out_spec = _tile_spec(0, seq)
    out_sds = jax.ShapeDtypeStruct((GROUP_CHUNKS, bsz * seq, LANES), F32)
    return pl.pallas_call(
        functools.partial(_dilated_kernel, dil=dil),
        grid=(bsz, seq // SEQ_TILE),
        in_specs=[_tile_spec(q_col, seq)]
        + _halo_specs(k_col, QUERY_BLOCK * dil, seq) + _halo_specs(v_col, QUERY_BLOCK * dil, seq)
        + [pl.BlockSpec(bias.shape, lambda b, t: (0, 0))],
        out_specs=[out_spec, out_spec],
        out_shape=[out_sds, out_sds],
        compiler_params=_params(("parallel", "arbitrary")),
        name=f"dilated_attn_d{dil}",
    )(proj, proj, proj, proj, proj, proj, proj, bias)


NA_HALO = 256
NA_KEYS = NA_ROWS * GRID_W


def _na_kernel(q_ref, kp_ref, kc_ref, kn_ref, vp_ref, vc_ref, vn_ref, bias_ref, o_ref,
               kwin, vwin, *, grid_rows):
    t = pl.program_id(1)
    tile_rows = SEQ_TILE // GRID_W
    half = NA_ROWS // 2
    for win, p_ref, c_ref, n_ref in ((kwin, kp_ref, kc_ref, kn_ref), (vwin, vp_ref, vc_ref, vn_ref)):
        for ch in range(GROUP_CHUNKS):
            lanes = slice(ch * LANES, (ch + 1) * LANES)
            win[0:NA_HALO, lanes] = p_ref[ch].astype(BF16)
            win[NA_HALO:NA_HALO + SEQ_TILE, lanes] = c_ref[ch].astype(BF16)
            win[NA_HALO + SEQ_TILE:, lanes] = n_ref[ch].astype(BF16)
    masks = _head_masks(GRID_W)
    for i in range(tile_rows):
        row = t * tile_rows + i
        row0 = jnp.clip(row - half, 0, grid_rows - NA_ROWS)
        off = pl.multiple_of((row0 - (t * tile_rows - half)) * GRID_W, GRID_W)
        q = jnp.concatenate([q_ref[ch, i * GRID_W:(i + 1) * GRID_W, :] for ch in range(GROUP_CHUNKS)], axis=-1)
        q4 = _stack_heads(q, masks).astype(BF16)
        k = kwin[pl.ds(off, NA_KEYS), :]
        v = vwin[pl.ds(off, NA_KEYS), :]
        logits = lax.dot_general(q4, k, (((1,), (1,)), ((), ())),
                                 preferred_element_type=F32) + bias_ref[row - row0]
        o4, _ = _softmax_pv(logits, v, False)
        o = _unstack_heads(o4, masks, GRID_W)
        for ch in range(GROUP_CHUNKS):
            o_ref[ch, i * GRID_W:(i + 1) * GRID_W, :] = o[:, ch * LANES:(ch + 1) * LANES]


def _neighborhood_attention(proj, bias, bsz, seq):
    base = 3 * len(DILATIONS)
    win_rows = SEQ_TILE + 2 * NA_HALO
    return pl.pallas_call(
        functools.partial(_na_kernel, grid_rows=seq // GRID_W),
        grid=(bsz, seq // SEQ_TILE),
        in_specs=[_tile_spec(base, seq)]
        + _halo_specs(base + 1, NA_HALO, seq) + _halo_specs(base + 2, NA_HALO, seq)
        + [pl.BlockSpec(bias.shape, lambda b, t: (0, 0, 0))],
        out_specs=_tile_spec(0, seq),
        out_shape=jax.ShapeDtypeStruct((GROUP_CHUNKS, bsz * seq, LANES), F32),
        scratch_shapes=[pltpu.VMEM((win_rows, GROUP_COLS), BF16),
                        pltpu.VMEM((win_rows, GROUP_COLS), BF16)],
        compiler_params=_params(("parallel", "arbitrary")),
        name="neighborhood_attn",
    )(proj, proj, proj, proj, proj, proj, proj, bias)


def _t5_bucket_table(dil):
    qb = QUERY_BLOCK
    rel = (np.arange(3 * qb)[None, :] - qb - np.arange(qb)[:, None])
    nb = T5_BUCKETS // 2
    max_exact = nb // 2
    n = np.abs(rel * dil)
    nf = np.maximum(n, 1).astype(np.float32)
    large = max_exact + (np.log(nf / np.float32(max_exact)) / np.float32(math.log(T5_MAX_DIST / max_exact))
                         * np.float32(nb - max_exact)).astype(np.int32)
    large = np.minimum(large, nb - 1)
    bucket = np.where(rel * dil > 0, nb, 0) + np.where(n < max_exact, n, large)
    return bucket.astype(np.int32), np.abs(rel) <= qb


def _dilated_bias(rel_bias, group):
    bucket, band = _t5_bucket_table(DILATIONS[group])
    tab = rel_bias[:, group * GROUP_HEADS:(group + 1) * GROUP_HEADS].astype(F32)
    onehot = (bucket[None, :, :] == np.arange(T5_BUCKETS)[:, None, None]).astype(np.float32)
    bias = jnp.einsum("bh,bqk->hqk", tab, onehot, precision=lax.Precision.HIGHEST)
    bias = jnp.where(band[None], bias, NEG)
    return bias.reshape(GROUP_HEADS * QUERY_BLOCK, 3 * QUERY_BLOCK)


def _na_bias(rpb):
    qc = np.arange(GRID_W)[:, None]
    kc = np.arange(GRID_W)[None, :]
    win0 = np.clip(qc - NA_COLS // 2, 0, GRID_W - NA_COLS)
    col_ok = (kc >= win0) & (kc < win0 + NA_COLS)
    dc = np.clip(kc - qc + NA_COLS - 1, 0, 2 * NA_COLS - 2)
    shift = np.arange(NA_ROWS)[:, None]
    kr = np.arange(NA_ROWS)[None, :]
    dr = kr - shift + NA_ROWS - 1
    pick_row = (dr[:, :, None] == np.arange(2 * NA_ROWS - 1)[None, None, :]).astype(np.float32)
    pick_col = (dc[None, :, :] == np.arange(2 * NA_COLS - 1)[:, None, None]).astype(np.float32)
    hi = lax.Precision.HIGHEST
    rows = jnp.einsum("hac,dka->hdkc", rpb.astype(F32), pick_row, precision=hi)
    b = jnp.einsum("hdkc,cqx->dhqkx", rows, pick_col, precision=hi)
    b = jnp.where(col_ok[None, None, :, None, :], b, NEG)
    return b.reshape(NA_ROWS, GROUP_HEADS * GRID_W, NA_KEYS)


def _layer_norm(z, g, b):
    mu = jnp.mean(z, axis=-1, keepdims=True)
    zc = z - mu
    var = jnp.mean(zc * zc, axis=-1, keepdims=True)
    return zc * lax.rsqrt(var + LN_EPS) * g + b


def _row(x, i):
    return x[i:i + 1, :]


def _route(logits, rbias):
    scores = jax.nn.sigmoid(logits)
    sel = scores + rbias
    ge = GROUP_EXPERTS
    n_groups = N_EXPERTS // ge
    best_sum, best_g = None, None
    for g in range(n_groups):
        vals = [_row(sel, g * ge + j) for j in range(ge)]
        top2 = None
        for a in range(ge):
            for b in range(a + 1, ge):
                pair = vals[a] + vals[b]
                top2 = pair if top2 is None else jnp.maximum(top2, pair)
        if best_sum is None:
            best_sum, best_g = top2, jnp.zeros_like(top2, dtype=jnp.int32)
        else:
            upd = top2 > best_sum
            best_g = jnp.where(upd, g, best_g)
            best_sum = jnp.where(upd, top2, best_sum)

    def in_group(x, j):
        out = _row(x, j)
        for g in range(1, n_groups):
            out = jnp.where(best_g == g, _row(x, g * ge + j), out)
        return out

    cand = [in_group(sel, j) for j in range(ge)]
    cand_score = [in_group(scores, j) for j in range(ge)]
    v1, j1, w1 = cand[0], jnp.zeros_like(best_g), cand_score[0]
    for j in range(1, ge):
        upd = cand[j] > v1
        v1 = jnp.where(upd, cand[j], v1)
        j1 = jnp.where(upd, j, j1)
        w1 = jnp.where(upd, cand_score[j], w1)
    v2 = jnp.full_like(v1, -jnp.inf)
    j2, w2 = jnp.zeros_like(best_g), jnp.zeros_like(w1)
    for j in range(ge):
        upd = (cand[j] > v2) & (j1 != j)
        v2 = jnp.where(upd, cand[j], v2)
        j2 = jnp.where(upd, j, j2)
        w2 = jnp.where(upd, cand_score[j], w2)
    wsum = w1 + w2
    g1, g2 = w1 / wsum, w2 / wsum
    lo, hi = jnp.minimum(j1, j2), jnp.maximum(j1, j2)
    pair = jnp.zeros_like(best_g)
    first = jnp.zeros_like(best_g)
    for q, (a, b) in enumerate(PAIR_ORDER):
        hit = (lo == min(a, b)) & (hi == max(a, b))
        pair = jnp.where(hit, q, pair)
        first = jnp.where(hit, a, first)
    is_first = j1 == first
    g_first = jnp.where(is_first, g1, g2)
    g_second = jnp.where(is_first, g2, g1)
    cls = (best_g * len(PAIR_ORDER) + pair).astype(F32)
    zero = jnp.zeros_like(w1)
    return jnp.concatenate([cls, g_first, g_second, zero, zero, zero, zero, zero], axis=0)


def _to_column(row_vec, eye):
    return jnp.sum(jnp.where(eye, row_vec, 0.0), axis=-1, keepdims=True)


def _post_tail(m, x_ref, g_ref, b_ref, wr_ref, rb_ref, x1_ref, x1p_ref, route_ref):
    tm = m.shape[0]
    xn = _layer_norm(ALPHA * x_ref[...] + m, g_ref[...], b_ref[...])
    x1_ref[...] = xn
    xh = xn.astype(BF16)
    xhf = xh.astype(F32)
    xl = (xn - xhf).astype(BF16)
    wr = wr_ref[...]
    contract = (((1,), (1,)), ((), ()))
    a = lax.dot_general(wr, xh, contract, preferred_element_type=F32)
    c = lax.dot_general(wr[0:N_EXPERTS], xl, contract, preferred_element_type=F32)
    logits = a[0:N_EXPERTS] + a[N_EXPERTS:] + c
    route = _route(logits, rb_ref[...])
    route_ref[...] = route
    bits = pltpu.bitcast(xhf, jnp.uint32)
    for s in range(PACKED_SUBLANES):
        lo = bits[:, (2 * s) * LANES:(2 * s + 1) * LANES]
        hi = bits[:, (2 * s + 1) * LANES:(2 * s + 2) * LANES]
        x1p_ref[pl.ds(s, tm, stride=ROW_CHUNKS), :] = (lo >> 16) | (hi & jnp.uint32(0xFFFF0000))
    eye = (lax.broadcasted_iota(jnp.int32, (tm, tm), 0) == lax.broadcasted_iota(jnp.int32, (tm, tm), 1))
    g_first, g_second = _to_column(route[1:2, :], eye), _to_column(route[2:3, :], eye)
    lane = lax.broadcasted_iota(jnp.int32, (tm, LANES), 1)
    gates = jnp.where(lane == 0, g_first, jnp.where(lane == 1, g_second, 0.0))
    x1p_ref[pl.ds(PACKED_SUBLANES, tm, stride=ROW_CHUNKS), :] = pltpu.bitcast(gates, jnp.uint32)
    for s in range(PACKED_SUBLANES + 1, ROW_CHUNKS):
        x1p_ref[pl.ds(s, tm, stride=ROW_CHUNKS), :] = jnp.zeros((tm, LANES), jnp.uint32)


def _post_attn_kernel(o0_ref, l0_ref, o1_ref, l1_ref, o2_ref, l2_ref, ob_ref, w_ref,
                      x_ref, g_ref, b_ref, wr_ref, rb_ref, x1_ref, x1r_ref, route_ref):
    def wide(ref):
        return jnp.concatenate([ref[ch] for ch in range(GROUP_CHUNKS)], axis=-1)

    l0, l1, l2 = wide(l0_ref), wide(l1_ref), wide(l2_ref)
    mx = jnp.maximum(jnp.maximum(l0, l1), l2)
    e0, e1, e2 = jnp.exp(l0 - mx), jnp.exp(l1 - mx), jnp.exp(l2 - mx)
    inv = 1.0 / (e0 + e1 + e2)
    lhs = jnp.concatenate([wide(o0_ref) * (e0 * inv), wide(o1_ref) * (e1 * inv),
                           wide(o2_ref) * (e2 * inv), wide(ob_ref)], axis=-1).astype(BF16)
    m = jnp.dot(lhs, w_ref[...], preferred_element_type=F32)
    _post_tail(m, x_ref, g_ref, b_ref, wr_ref, rb_ref, x1_ref, x1r_ref, route_ref)


def _post_conv_kernel(u_ref, w_ref, wb_ref, x_ref, g_ref, b_ref, wr_ref, rb_ref,
                      x1_ref, x1r_ref, route_ref):
    m = jnp.dot(u_ref[...], w_ref[...], preferred_element_type=F32) + wb_ref[...]
    _post_tail(m, x_ref, g_ref, b_ref, wr_ref, rb_ref, x1_ref, x1r_ref, route_ref)


def _post_mixer(kernel, name, lhs_list, consts_front, x, g, b, wr, rb, tm=512):
    n, d = x.shape
    row = lambda i: (i, 0)
    const = lambda i: (0, 0)
    in_specs = [pl.BlockSpec((tm, a.shape[1]), row) if a.ndim == 2
                else pl.BlockSpec((a.shape[0], tm, LANES), lambda i: (0, i, 0)) for a in lhs_list]
    in_specs += [pl.BlockSpec(c.shape, const) for c in consts_front]
    in_specs += [pl.BlockSpec((tm, d), row)]
    in_specs += [pl.BlockSpec(c.shape, const) for c in (g, b, wr, rb)]
    return pl.pallas_call(
        kernel,
        grid=(n // tm,),
        in_specs=in_specs,
        out_specs=[pl.BlockSpec((tm, d), row),
                   pl.BlockSpec((tm * ROW_CHUNKS, LANES), row),
                   pl.BlockSpec((8, tm), lambda i: (0, i))],
        out_shape=[jax.ShapeDtypeStruct((n, d), F32),
                   jax.ShapeDtypeStruct((n * ROW_CHUNKS, LANES), jnp.uint32),
                   jax.ShapeDtypeStruct((8, n), F32)],
        compiler_params=_params(("parallel",)),
        name=name,
    )(*lhs_list, *consts_front, x, g, b, wr, rb)


def _dwconv_kernel(up_ref, uc_ref, un_ref, w_ref, cb_ref, g_ref, b_ref, o_ref, win, acc, *, ts):
    t = pl.program_id(1)
    has_prev = (t > 0).astype(F32)
    has_next = (t < pl.num_programs(1) - 1).astype(F32)
    for j in range(ROW_CHUNKS):
        lanes = slice(j * LANES, (j + 1) * LANES)
        win[j, 0:CONV_HALO, :] = up_ref[:, lanes] * has_prev
        win[j, CONV_HALO:CONV_HALO + ts, :] = uc_ref[:, lanes]
        win[j, CONV_HALO + ts:, :] = un_ref[:, lanes] * has_next

    def body(j, carry):
        a = jnp.zeros((ts, LANES), F32)
        for tap in range(CONV_WIDTH):
            start = CONV_HALO - CONV_WIDTH // 2 + tap
            a = a + win[j, pl.ds(start, ts), :] * w_ref[j, tap:tap + 1, :]
        acc[j] = a
        return carry

    lax.fori_loop(0, ROW_CHUNKS, body, 0)
    u = jnp.concatenate([acc[j] for j in range(ROW_CHUNKS)], axis=-1) + cb_ref[...]
    y = _layer_norm(u, g_ref[...], b_ref[...])
    o_ref[...] = (y * jax.nn.sigmoid(y)).astype(o_ref.dtype)


def _dwconv_ln_silu(u, w_dw, b_dw, g, b, ts=256):
    bsz, seq, d = u.shape
    per_tile = ts // CONV_HALO
    last = seq // CONV_HALO - 1
    w3 = jnp.pad(w_dw, ((0, 32 - CONV_WIDTH), (0, 0))).reshape(32, ROW_CHUNKS, LANES).transpose(1, 0, 2)
    vec = pl.BlockSpec((1, d), lambda bb, t: (0, 0))
    return pl.pallas_call(
        functools.partial(_dwconv_kernel, ts=ts),
        grid=(bsz, seq // ts),
        in_specs=[pl.BlockSpec((None, CONV_HALO, d), lambda bb, t: (bb, jnp.maximum(t * per_tile - 1, 0), 0)),
                  pl.BlockSpec((None, ts, d), lambda bb, t: (bb, t, 0)),
                  pl.BlockSpec((None, CONV_HALO, d), lambda bb, t: (bb, jnp.minimum((t + 1) * per_tile, last), 0)),
                  pl.BlockSpec(w3.shape, lambda bb, t: (0, 0, 0)),
                  vec, vec, vec],
        out_specs=pl.BlockSpec((None, ts, d), lambda bb, t: (bb, t, 0)),
        out_shape=jax.ShapeDtypeStruct((bsz, seq, d), BF16),
        scratch_shapes=[pltpu.VMEM((ROW_CHUNKS, ts + 2 * CONV_HALO, LANES), F32),
                        pltpu.VMEM((ROW_CHUNKS, ts, LANES), F32)],
        compiler_params=_params(("parallel", "arbitrary")),
        name="dwconv_ln_silu",
    )(u, u, u, w3, b_dw, g, b)


def _row_gather_start(src_hbm, dst, sem, idx_ref, first, count):
    def body(r, carry):
        tok = idx_ref[first + r]
        pltpu.make_async_copy(src_hbm.at[pl.ds(pl.multiple_of(tok * ROW_CHUNKS, ROW_CHUNKS), ROW_CHUNKS)],
                              dst.at[pl.ds(pl.multiple_of(r * ROW_CHUNKS, ROW_CHUNKS), ROW_CHUNKS)],
                              sem).start()
        return carry
    lax.fori_loop(0, count, body, 0, unroll=8)


def _row_gather_wait(src_hbm, dst, sem):
    pltpu.make_async_copy(src_hbm.at[pl.ds(0, dst.shape[0])], dst, sem).wait()


def _dispatch_kernel(dest_ref, tail_ref, n_tiles_ref, x_ref, xs_hbm, zeros, stage, sem, zsem, *, tm):
    i = pl.program_id(0)
    n_steps = pl.num_programs(0)
    slot = i % 2
    tile_rows = EXPERT_TILE * ROW_CHUNKS
    all_tiles = xs_hbm.shape[0] // tile_rows

    def zero_tile(first_row):
        start = pl.multiple_of(first_row * ROW_CHUNKS, ROW_CHUNKS)
        return pltpu.make_async_copy(zeros, xs_hbm.at[pl.ds(start, tile_rows)], zsem)

    @pl.when(i == 0)
    def _():
        zeros[...] = jnp.zeros_like(zeros)
        for c in range(N_CLASSES):
            zero_tile(tail_ref[c]).start()
        for c in range(N_CLASSES):
            zero_tile(tail_ref[c]).wait()

        def unused(j, carry):
            cp = zero_tile(j * EXPERT_TILE)
            cp.start()
            cp.wait()
            return carry
        lax.fori_loop(n_tiles_ref[0], all_tiles, unused, 0)

    stage[slot] = x_ref[...]

    def body(r, carry):
        d = dest_ref[i * tm + r]
        pltpu.make_async_copy(stage.at[slot, pl.ds(pl.multiple_of(r * ROW_CHUNKS, ROW_CHUNKS), ROW_CHUNKS)],
                              xs_hbm.at[pl.ds(pl.multiple_of(d * ROW_CHUNKS, ROW_CHUNKS), ROW_CHUNKS)],
                              sem.at[slot]).start()
        return carry
    lax.fori_loop(0, tm, body, 0, unroll=8)

    def wait_step(s):
        pltpu.make_async_copy(stage.at[s], xs_hbm.at[pl.ds(0, tm * ROW_CHUNKS)], sem.at[s]).wait()

    @pl.when(i > 0)
    def _():
        wait_step(1 - slot)

    @pl.when(i == n_steps - 1)
    def _():
        wait_step(slot)


def _dispatch(x1p, dest, tail_start, n_tiles, n_rows, tm=512):
    n_tokens = x1p.shape[0] // ROW_CHUNKS
    grid_spec = pltpu.PrefetchScalarGridSpec(
        num_scalar_prefetch=3,
        grid=(n_tokens // tm,),
        in_specs=[pl.BlockSpec((tm * ROW_CHUNKS, LANES), lambda i, dst, tail, nt: (i, 0))],
        out_specs=pl.BlockSpec(memory_space=pl.ANY),
        scratch_shapes=[pltpu.VMEM((EXPERT_TILE * ROW_CHUNKS, LANES), x1p.dtype),
                        pltpu.VMEM((2, tm * ROW_CHUNKS, LANES), x1p.dtype),
                        pltpu.SemaphoreType.DMA((2,)),
                        pltpu.SemaphoreType.DMA(())],
    )
    return pl.pallas_call(
        functools.partial(_dispatch_kernel, tm=tm),
        grid_spec=grid_spec,
        out_shape=jax.ShapeDtypeStruct((n_rows * ROW_CHUNKS, LANES), x1p.dtype),
        compiler_params=_params(("arbitrary",)),
        name="moe_dispatch",
    )(dest, tail_start, n_tiles, x1p)


def _expert_kernel(first_ref, second_ref, n_tiles_ref, x_ref,
                   wg1_ref, wu1_ref, wd1_ref, wg2_ref, wu2_ref, wd2_ref, y_ref,
                   wg1_bf, wu1_bf, wd1_bf, wg2_bf, wu2_bf, wd2_bf):
    i = pl.program_id(0)
    tm = EXPERT_TILE
    before = jnp.maximum(i - 1, 0)

    @pl.when((i == 0) | (first_ref[i] != first_ref[before]))
    def _():
        wg1_bf[...] = wg1_ref[...].astype(BF16)
        wu1_bf[...] = wu1_ref[...].astype(BF16)
        wd1_bf[...] = wd1_ref[...].astype(BF16)

    @pl.when((i == 0) | (second_ref[i] != second_ref[before]))
    def _():
        wg2_bf[...] = wg2_ref[...].astype(BF16)
        wu2_bf[...] = wu2_ref[...].astype(BF16)
        wd2_bf[...] = wd2_ref[...].astype(BF16)

    @pl.when(i < n_tiles_ref[0])
    def _():
        halves = []
        for s in range(PACKED_SUBLANES):
            word = x_ref[pl.ds(s, tm, stride=ROW_CHUNKS), :]
            halves.append(pltpu.bitcast(word << 16, F32))
            halves.append(pltpu.bitcast(word & jnp.uint32(0xFFFF0000), F32))
        x = jnp.concatenate(halves, axis=-1).astype(BF16)
        gates = pltpu.bitcast(x_ref[pl.ds(PACKED_SUBLANES, tm, stride=ROW_CHUNKS), :], F32)

        def ffn(wg, wu, wd):
            hg = jnp.dot(x, wg[...], preferred_element_type=F32)
            hu = jnp.dot(x, wu[...], preferred_element_type=F32)
            hid = (hg * jax.nn.sigmoid(hg) * hu).astype(BF16)
            return jnp.dot(hid, wd[...], preferred_element_type=F32)

        y = gates[:, 0:1] * ffn(wg1_bf, wu1_bf, wd1_bf) + gates[:, 1:2] * ffn(wg2_bf, wu2_bf, wd2_bf)
        for j in range(ROW_CHUNKS):
            y_ref[pl.ds(j, tm, stride=ROW_CHUNKS), :] = y[:, j * LANES:(j + 1) * LANES]

    @pl.when(i >= n_tiles_ref[0])
    def _():
        y_ref[...] = jnp.zeros_like(y_ref)


def _experts(xs, w_gate, w_up, w_down, layer, tile_first, tile_second, n_tiles):
    tm = EXPERT_TILE
    n_rows = xs.shape[0] // ROW_CHUNKS
    d, de = w_gate.shape[2], w_gate.shape[3]
    row_map = lambda i, t1, t2, nt: (jnp.minimum(i, jnp.maximum(nt[0] - 1, 0)), 0)
    map1 = lambda i, t1, t2, nt: (layer, t1[i], 0, 0)
    map2 = lambda i, t1, t2, nt: (layer, t2[i], 0, 0)
    up = lambda m: pl.BlockSpec((None, None, d, de), m)
    down = lambda m: pl.BlockSpec((None, None, de, d), m)
    grid_spec = pltpu.PrefetchScalarGridSpec(
        num_scalar_prefetch=3,
        grid=(n_rows // tm,),
        in_specs=[pl.BlockSpec((tm * ROW_CHUNKS, LANES), row_map),
                  up(map1), up(map1), down(map1), up(map2), up(map2), down(map2)],
        out_specs=pl.BlockSpec((tm * ROW_CHUNKS, LANES), lambda i, t1, t2, nt: (i, 0)),
        scratch_shapes=[pltpu.VMEM((d, de), BF16), pltpu.VMEM((d, de), BF16), pltpu.VMEM((de, d), BF16),
                        pltpu.VMEM((d, de), BF16), pltpu.VMEM((d, de), BF16), pltpu.VMEM((de, d), BF16)],
    )
    return pl.pallas_call(
        _expert_kernel,
        grid_spec=grid_spec,
        out_shape=jax.ShapeDtypeStruct((n_rows * ROW_CHUNKS, LANES), F32),
        compiler_params=_params(("arbitrary",)),
        name="moe_experts",
    )(tile_first, tile_second, n_tiles, xs, w_gate, w_up, w_down, w_gate, w_up, w_down)


def _combine_kernel(dest_ref, y_hbm, x_ref, g_ref, b_ref, o_ref, ybuf, sem):
    i = pl.program_id(0)
    n_steps = pl.num_programs(0)
    slot = i % 2
    tm = COMBINE_TILE

    @pl.when(i == 0)
    def _():
        _row_gather_start(y_hbm, ybuf.at[0], sem.at[0], dest_ref, 0, tm)

    @pl.when(i + 1 < n_steps)
    def _():
        _row_gather_start(y_hbm, ybuf.at[1 - slot], sem.at[1 - slot], dest_ref, (i + 1) * tm, tm)

    _row_gather_wait(y_hbm, ybuf.at[slot], sem.at[slot])
    f = jnp.concatenate([ybuf[slot, pl.ds(j, tm, stride=ROW_CHUNKS), :] for j in range(ROW_CHUNKS)], axis=-1)
    o_ref[...] = _layer_norm(ALPHA * x_ref[...] + f, g_ref[...], b_ref[...])


def _combine(y_rows, x1, dest, g, b):
    n, d = x1.shape
    tm = COMBINE_TILE
    grid_spec = pltpu.PrefetchScalarGridSpec(
        num_scalar_prefetch=1,
        grid=(n // tm,),
        in_specs=[pl.BlockSpec(memory_space=pl.ANY),
                  pl.BlockSpec((tm, d), lambda i, dst: (i, 0)),
                  pl.BlockSpec((1, d), lambda i, dst: (0, 0)),
                  pl.BlockSpec((1, d), lambda i, dst: (0, 0))],
        out_specs=pl.BlockSpec((tm, d), lambda i, dst: (i, 0)),
        scratch_shapes=[pltpu.VMEM((2, tm * ROW_CHUNKS, LANES), F32),
                        pltpu.SemaphoreType.DMA((2,))],
    )
    return pl.pallas_call(
        _combine_kernel,
        grid_spec=grid_spec,
        out_shape=jax.ShapeDtypeStruct((n, d), F32),
        compiler_params=_params(("arbitrary",)),
        name="moe_combine",
    )(dest, y_rows, x1, g, b)


def _dispatch_plan(route, n_tokens):
    tm = EXPERT_TILE
    blk = LANES
    n_rows = n_tokens + N_CLASSES * tm
    hi = lax.Precision.HIGHEST
    onehot = (route[0][:, None] == jnp.arange(N_CLASSES, dtype=F32)[None, :]).astype(F32)
    blocks = onehot.reshape(n_tokens // blk, blk, N_CLASSES)
    tri = np.tril(np.ones((blk, blk), np.float32))
    local = jnp.einsum("ij,bjk->bik", tri, blocks, precision=hi)
    totals = local[:, -1, :]
    strict = np.tril(np.ones((n_tokens // blk,) * 2, np.float32), -1)
    before = jnp.dot(strict, totals, precision=hi)
    rank = jnp.sum((local + before[:, None, :]) * blocks, axis=-1).reshape(n_tokens) - 1.0
    counts = before[-1] + totals[-1]
    padded = jnp.ceil(counts / tm) * tm
    seg_end = jnp.dot(np.tril(np.ones((N_CLASSES,) * 2, np.float32)), padded, precision=hi)
    seg_start = seg_end - padded
    dest = (jnp.sum(onehot * seg_start[None, :], axis=-1) + rank).astype(jnp.int32)
    tile_start = jnp.arange(n_rows // tm, dtype=F32) * tm
    tile_class = jnp.sum((seg_end[None, :] <= tile_start[:, None]).astype(jnp.int32), axis=-1)
    tile_class = jnp.minimum(tile_class, N_CLASSES - 1)
    tile_hot = (tile_class[:, None] == np.arange(N_CLASSES)[None, :]).astype(jnp.int32)
    n_pairs = len(PAIR_ORDER)
    first_of = np.array([(c // n_pairs) * GROUP_EXPERTS + PAIR_ORDER[c % n_pairs][0] for c in range(N_CLASSES)])
    second_of = np.array([(c // n_pairs) * GROUP_EXPERTS + PAIR_ORDER[c % n_pairs][1] for c in range(N_CLASSES)])
    tile_first = jnp.sum(tile_hot * first_of[None, :], axis=-1).astype(jnp.int32)
    tile_second = jnp.sum(tile_hot * second_of[None, :], axis=-1).astype(jnp.int32)
    n_tiles = (seg_end[-1] / tm).astype(jnp.int32).reshape(1)
    tail_start = jnp.maximum(seg_end - tm, 0.0).astype(jnp.int32)
    return tile_first, tile_second, n_tiles, tail_start, dest, n_rows


def _moe(x1, x1p, route, w_gate, w_up, w_down, layer, g, b):
    n = x1.shape[0]
    tile_first, tile_second, n_tiles, tail_start, dest, n_rows = _dispatch_plan(route, n)
    xs = _dispatch(x1p, dest, tail_start, n_tiles, n_rows)
    y_rows = _experts(xs, w_gate, w_up, w_down, layer, tile_first, tile_second, n_tiles)
    return _combine(y_rows, x1, dest, g, b)


def _vec(v):
    return v.reshape(1, -1).astype(F32)


def kernel(x, w_in_attn, w_out_attn, rel_bias, rpb_2d, w_pw1, b_pw1, w_dw, b_dw, conv_ln_g, conv_ln_b, w_pw2, b_pw2, ln_mix_g, ln_mix_b, ln_ffn_g, ln_ffn_b, w_router, router_bias, w_gate, w_up, w_down):
    bsz, seq, d = x.shape
    n = bsz * seq
    h = x.reshape(n, d)
    wr_t = w_router.T.astype(F32)
    wr_hi = wr_t.astype(BF16)
    wr_lo = (wr_t - wr_hi.astype(F32)).astype(BF16)
    wr = jnp.concatenate([wr_hi, wr_lo], axis=0)
    rb = router_bias.reshape(N_EXPERTS, 1).astype(F32)
    n_dil_cols = len(DILATIONS) * GROUP_COLS
    q_cols = np.zeros((3 * d,), np.float32) + 1.0
    q_cols[0:n_dil_cols] = HEAD_DIM ** -0.5
    q_cols[3 * n_dil_cols:3 * n_dil_cols + GROUP_COLS] = HEAD_DIM ** -0.5

    for layer in range(DEPTH):
        i = layer // 2
        g_mix, b_mix = _vec(ln_mix_g[layer]), _vec(ln_mix_b[layer])
        if layer % 2 == 0:
            w_in = (w_in_attn[i] * q_cols[None, :]).astype(BF16)
            proj = _matmul(h, w_in)
            lhs = []
            for grp in range(len(DILATIONS)):
                lhs += _dilated_attention(proj, _dilated_bias(rel_bias, grp), grp, bsz, seq)
            lhs.append(_neighborhood_attention(proj, _na_bias(rpb_2d[i]), bsz, seq))
            x1, x1r, route = _post_mixer(_post_attn_kernel, "post_attn", lhs,
                                         [w_out_attn[i].astype(BF16)], h, g_mix, b_mix, wr, rb)
        else:
            u = _pw1_glu(h, w_pw1[i].astype(BF16), _vec(b_pw1[i]))
            u = _dwconv_ln_silu(u.reshape(bsz, seq, d), w_dw[i].astype(F32), _vec(b_dw[i]),
                                _vec(conv_ln_g[i]), _vec(conv_ln_b[i]))
            x1, x1r, route = _post_mixer(_post_conv_kernel, "post_conv", [u.reshape(n, d)],
                                         [w_pw2[i].astype(BF16), _vec(b_pw2[i])], h, g_mix, b_mix, wr, rb)
        h = _moe(x1, x1r, route, w_gate, w_up, w_down, layer,
                 _vec(ln_ffn_g[layer]), _vec(ln_ffn_b[layer]))
    return h.reshape(bsz, seq, d)
```

```python
import functools
import math

import jax
import jax.numpy as jnp
import numpy as np
from jax import lax
from jax.experimental import pallas as pl
from jax.experimental.pallas import tpu as pltpu

D_MODEL = 1024
HEAD_DIM = 64
GROUP_HEADS = 4
GROUP_COLS = GROUP_HEADS * HEAD_DIM
DILATIONS = (1, 4, 16)
QUERY_BLOCK = 64
GRID_W = 64
NA_ROWS = 8
NA_COLS = 16
T5_BUCKETS = 32
T5_MAX_DIST = 1024
CONV_WIDTH = 31
CONV_HALO = 16
N_EXPERTS = 16
GROUP_EXPERTS = 4
D_EXPERT = 512
DEPTH = 4
ALPHA = (2 * DEPTH) ** 0.25
LN_EPS = 1e-5
NEG = -1e30

LANES = 128
ROW_CHUNKS = D_MODEL // LANES
GROUP_CHUNKS = GROUP_COLS // LANES
PAIR_ORDER = ((0, 1), (0, 2), (0, 3), (1, 3), (1, 2), (3, 2))
N_CLASSES = (N_EXPERTS // GROUP_EXPERTS) * len(PAIR_ORDER)
PACKED_SUBLANES = D_MODEL // (2 * LANES)
SEQ_TILE = 1024
EXPERT_TILE = 256
COMBINE_TILE = 512
VMEM_LIMIT = 56 * 1024 * 1024

F32 = jnp.float32
BF16 = jnp.bfloat16


def _params(semantics, vmem=VMEM_LIMIT):
    return pltpu.CompilerParams(dimension_semantics=semantics, vmem_limit_bytes=vmem)


MATMUL_SLAB = 512


def _matmul_kernel(x_ref, w_ref, o_ref):
    xb = x_ref[...].astype(BF16)
    per_slab = MATMUL_SLAB // LANES
    for s in range(w_ref.shape[1] // MATMUL_SLAB):
        y = jnp.dot(xb, w_ref[:, s * MATMUL_SLAB:(s + 1) * MATMUL_SLAB], preferred_element_type=F32)
        for c in range(per_slab):
            o_ref[s * per_slab + c] = y[:, c * LANES:(c + 1) * LANES]


def _matmul(x, w, tm=512):
    n, k = x.shape
    _, m = w.shape
    return pl.pallas_call(
        _matmul_kernel,
        grid=(n // tm,),
        in_specs=[pl.BlockSpec((tm, k), lambda i: (i, 0)),
                  pl.BlockSpec((k, m), lambda i: (0, 0))],
        out_specs=pl.BlockSpec((m // LANES, tm, LANES), lambda i: (0, i, 0)),
        out_shape=jax.ShapeDtypeStruct((m // LANES, n, LANES), F32),
        compiler_params=_params(("parallel",)),
        name="qkv_proj",
    )(x, w)


def _glu_kernel(x_ref, w_ref, b_ref, o_ref):
    xb = x_ref[...].astype(BF16)
    half = w_ref.shape[1] // 2
    for s in range(half // MATMUL_SLAB):
        cols = slice(s * MATMUL_SLAB, (s + 1) * MATMUL_SLAB)
        gate_cols = slice(half + s * MATMUL_SLAB, half + (s + 1) * MATMUL_SLAB)
        a = jnp.dot(xb, w_ref[:, cols], preferred_element_type=F32) + b_ref[:, cols]
        g = jnp.dot(xb, w_ref[:, gate_cols], preferred_element_type=F32) + b_ref[:, gate_cols]
        o_ref[:, cols] = a * jax.nn.sigmoid(g)


def _pw1_glu(x, w, b, tm=512):
    n, k = x.shape
    m = w.shape[1]
    return pl.pallas_call(
        _glu_kernel,
        grid=(n // tm,),
        in_specs=[pl.BlockSpec((tm, k), lambda i: (i, 0)),
                  pl.BlockSpec((k, m), lambda i: (0, 0)),
                  pl.BlockSpec((1, m), lambda i: (0, 0))],
        out_specs=pl.BlockSpec((tm, m // 2), lambda i: (i, 0)),
        out_shape=jax.ShapeDtypeStruct((n, m // 2), F32),
        compiler_params=_params(("parallel",)),
        name="pw1_glu",
    )(x, w, b)


def _head_masks(rows):
    lane = lax.broadcasted_iota(jnp.int32, (rows, GROUP_COLS), 1)
    return [(lane >= HEAD_DIM * h) & (lane < HEAD_DIM * (h + 1)) for h in range(GROUP_HEADS)]


def _stack_heads(q, masks):
    return jnp.concatenate([jnp.where(m, q, 0.0) for m in masks], axis=0)


def _unstack_heads(o4, masks, rows):
    out = jnp.where(masks[0], o4[0:rows], 0.0)
    for h in range(1, GROUP_HEADS):
        out = out + jnp.where(masks[h], o4[h * rows:(h + 1) * rows], 0.0)
    return out


def _softmax_pv(logits, v, with_lse):
    m = jnp.max(logits, axis=-1, keepdims=True)
    p = jnp.exp(logits - m)
    s = jnp.sum(p, axis=-1, keepdims=True)
    o4 = jnp.dot(p.astype(BF16), v, preferred_element_type=F32) * (1.0 / s)
    lse = (m + jnp.log(s)) if with_lse else None
    return o4, lse


def _dilated_kernel(q_ref, kp_ref, kc_ref, kn_ref, vp_ref, vc_ref, vn_ref, bias_ref,
                    o_ref, l_ref, *, dil):
    t = pl.program_id(1)
    chunk = QUERY_BLOCK * dil
    n_chunks = SEQ_TILE // chunk
    qb = QUERY_BLOCK
    masks = _head_masks(qb)
    col = lax.broadcasted_iota(jnp.int32, (GROUP_HEADS * qb, 3 * qb), 1)
    no_prev = jnp.where(col < qb, NEG, 0.0) * (t == 0).astype(F32)
    no_next = jnp.where(col >= 2 * qb, NEG, 0.0) * (t == pl.num_programs(1) - 1).astype(F32)
    bias = bias_ref[...]

    def sl(start):
        return pl.ds(start, qb) if dil == 1 else pl.ds(start, qb, stride=dil)

    def rows(ref, start):
        return jnp.concatenate([ref[c, sl(start), :] for c in range(GROUP_CHUNKS)], axis=-1)

    def window(p_ref, c_ref, n_ref, c, r):
        parts = []
        for cc in (c - 1, c, c + 1):
            if cc < 0:
                parts.append(rows(p_ref, r))
            elif cc >= n_chunks:
                parts.append(rows(n_ref, r))
            else:
                parts.append(rows(c_ref, cc * chunk + r))
        return jnp.concatenate(parts, axis=0).astype(BF16)

    for c in range(n_chunks):
        for r in range(dil):
            base = c * chunk + r
            q4 = _stack_heads(rows(q_ref, base), masks).astype(BF16)
            k = window(kp_ref, kc_ref, kn_ref, c, r)
            v = window(vp_ref, vc_ref, vn_ref, c, r)
            logits = lax.dot_general(q4, k, (((1,), (1,)), ((), ())),
                                     preferred_element_type=F32) + bias
            if c == 0:
                logits = logits + no_prev
            if c == n_chunks - 1:
                logits = logits + no_next
            o4, lse = _softmax_pv(logits, v, True)
            o = _unstack_heads(o4, masks, qb)
            lb = _unstack_heads(jnp.broadcast_to(lse, (GROUP_HEADS * qb, GROUP_COLS)), masks, qb)
            for ch in range(GROUP_CHUNKS):
                o_ref[ch, sl(base), :] = o[:, ch * LANES:(ch + 1) * LANES]
                l_ref[ch, sl(base), :] = lb[:, ch * LANES:(ch + 1) * LANES]


def _tile_spec(col_block, seq):
    tiles = seq // SEQ_TILE
    return pl.BlockSpec((GROUP_CHUNKS, SEQ_TILE, LANES), lambda b, t: (col_block, b * tiles + t, 0))


def _halo_specs(col_block, halo, seq):
    per_tile = SEQ_TILE // halo
    per_seq = seq // halo
    return [
        pl.BlockSpec((GROUP_CHUNKS, halo, LANES),
                     lambda b, t: (col_block, b * per_seq + jnp.maximum(t * per_tile - 1, 0), 0)),
        _tile_spec(col_block, seq),
        pl.BlockSpec((GROUP_CHUNKS, halo, LANES),
                     lambda b, t: (col_block, b * per_seq + jnp.minimum((t + 1) * per_tile, per_seq - 1), 0)),
    ]


def _dilated_attention(proj, bias, group, bsz, seq):
    dil = DILATIONS[group]
    n_groups = len(DILATIONS)
    q_col, k_col, v_col = group, n_groups + group, 2 * n_groups + group
    out_spec = _tile_spec(0, seq)
    out_sds = jax.ShapeDtypeStruct((GROUP_CHUNKS, bsz * seq, LANES), F32)
    return pl.pallas_call(
        functools.partial(_dilated_kernel, dil=dil),
        grid=(bsz, seq // SEQ_TILE),
        in_specs=[_tile_spec(q_col, seq)]
        + _halo_specs(k_col, QUERY_BLOCK * dil, seq) + _halo_specs(v_col, QUERY_BLOCK * dil, seq)
        + [pl.BlockSpec(bias.shape, lambda b, t: (0, 0))],
        out_specs=[out_spec, out_spec],
        out_shape=[out_sds, out_sds],
        compiler_params=_params(("parallel", "arbitrary")),
        name=f"dilated_attn_d{dil}",
    )(proj, proj, proj, proj, proj, proj, proj, bias)


NA_HALO = 256
NA_KEYS = NA_ROWS * GRID_W


def _na_kernel(q_ref, kp_ref, kc_ref, kn_ref, vp_ref, vc_ref, vn_ref, bias_ref, o_ref,
               kwin, vwin, *, grid_rows):
    t = pl.program_id(1)
    tile_rows = SEQ_TILE // GRID_W
    half = NA_ROWS // 2
    for win, p_ref, c_ref, n_ref in ((kwin, kp_ref, kc_ref, kn_ref), (vwin, vp_ref, vc_ref, vn_ref)):
        for ch in range(GROUP_CHUNKS):
            lanes = slice(ch * LANES, (ch + 1) * LANES)
            win[0:NA_HALO, lanes] = p_ref[ch].astype(BF16)
            win[NA_HALO:NA_HALO + SEQ_TILE, lanes] = c_ref[ch].astype(BF16)
            win[NA_HALO + SEQ_TILE:, lanes] = n_ref[ch].astype(BF16)
    masks = _head_masks(GRID_W)
    for i in range(tile_rows):
        row = t * tile_rows + i
        row0 = jnp.clip(row - half, 0, grid_rows - NA_ROWS)
        off = pl.multiple_of((row0 - (t * tile_rows - half)) * GRID_W, GRID_W)
        q = jnp.concatenate([q_ref[ch, i * GRID_W:(i + 1) * GRID_W, :] for ch in range(GROUP_CHUNKS)], axis=-1)
        q4 = _stack_heads(q, masks).astype(BF16)
        k = kwin[pl.ds(off, NA_KEYS), :]
        v = vwin[pl.ds(off, NA_KEYS), :]
        logits = lax.dot_general(q4, k, (((1,), (1,)), ((), ())),
                                 preferred_element_type=F32) + bias_ref[row - row0]
        o4, _ = _softmax_pv(logits, v, False)
        o = _unstack_heads(o4, masks, GRID_W)
        for ch in range(GROUP_CHUNKS):
            o_ref[ch, i * GRID_W:(i + 1) * GRID_W, :] = o[:, ch * LANES:(ch + 1) * LANES]


def _neighborhood_attention(proj, bias, bsz, seq):
    base = 3 * len(DILATIONS)
    win_rows = SEQ_TILE + 2 * NA_HALO
    return pl.pallas_call(
        functools.partial(_na_kernel, grid_rows=seq // GRID_W),
        grid=(bsz, seq // SEQ_TILE),
        in_specs=[_tile_spec(base, seq)]
        + _halo_specs(base + 1, NA_HALO, seq) + _halo_specs(base + 2, NA_HALO, seq)
        + [pl.BlockSpec(bias.shape, lambda b, t: (0, 0, 0))],
        out_specs=_tile_spec(0, seq),
        out_shape=jax.ShapeDtypeStruct((GROUP_CHUNKS, bsz * seq, LANES), F32),
        scratch_shapes=[pltpu.VMEM((win_rows, GROUP_COLS), BF16),
                        pltpu.VMEM((win_rows, GROUP_COLS), BF16)],
        compiler_params=_params(("parallel", "arbitrary")),
        name="neighborhood_attn",
    )(proj, proj, proj, proj, proj, proj, proj, bias)


def _t5_bucket_table(dil):
    qb = QUERY_BLOCK
    rel = (np.arange(3 * qb)[None, :] - qb - np.arange(qb)[:, None])
    nb = T5_BUCKETS // 2
    max_exact = nb // 2
    n = np.abs(rel * dil)
    nf = np.maximum(n, 1).astype(np.float32)
    large = max_exact + (np.log(nf / np.float32(max_exact)) / np.float32(math.log(T5_MAX_DIST / max_exact))
                         * np.float32(nb - max_exact)).astype(np.int32)
    large = np.minimum(large, nb - 1)
    bucket = np.where(rel * dil > 0, nb, 0) + np.where(n < max_exact, n, large)
    return bucket.astype(np.int32), np.abs(rel) <= qb


def _dilated_bias(rel_bias, group):
    bucket, band = _t5_bucket_table(DILATIONS[group])
    tab = rel_bias[:, group * GROUP_HEADS:(group + 1) * GROUP_HEADS].astype(F32)
    onehot = (bucket[None, :, :] == np.arange(T5_BUCKETS)[:, None, None]).astype(np.float32)
    bias = jnp.einsum("bh,bqk->hqk", tab, onehot, precision=lax.Precision.HIGHEST)
    bias = jnp.where(band[None], bias, NEG)
    return bias.reshape(GROUP_HEADS * QUERY_BLOCK, 3 * QUERY_BLOCK)


def _na_bias(rpb):
    qc = np.arange(GRID_W)[:, None]
    kc = np.arange(GRID_W)[None, :]
    win0 = np.clip(qc - NA_COLS // 2, 0, GRID_W - NA_COLS)
    col_ok = (kc >= win0) & (kc < win0 + NA_COLS)
    dc = np.clip(kc - qc + NA_COLS - 1, 0, 2 * NA_COLS - 2)
    shift = np.arange(NA_ROWS)[:, None]
    kr = np.arange(NA_ROWS)[None, :]
    dr = kr - shift + NA_ROWS - 1
    pick_row = (dr[:, :, None] == np.arange(2 * NA_ROWS - 1)[None, None, :]).astype(np.float32)
    pick_col = (dc[None, :, :] == np.arange(2 * NA_COLS - 1)[:, None, None]).astype(np.float32)
    hi = lax.Precision.HIGHEST
    rows = jnp.einsum("hac,dka->hdkc", rpb.astype(F32), pick_row, precision=hi)
    b = jnp.einsum("hdkc,cqx->dhqkx", rows, pick_col, precision=hi)
    b = jnp.where(col_ok[None, None, :, None, :], b, NEG)
    return b.reshape(NA_ROWS, GROUP_HEADS * GRID_W, NA_KEYS)


def _layer_norm(z, g, b):
    mu = jnp.mean(z, axis=-1, keepdims=True)
    zc = z - mu
    var = jnp.mean(zc * zc, axis=-1, keepdims=True)
    return zc * lax.rsqrt(var + LN_EPS) * g + b


def _row(x, i):
    return x[i:i + 1, :]


def _route(logits, rbias):
    scores = jax.nn.sigmoid(logits)
    sel = scores + rbias
    ge = GROUP_EXPERTS
    n_groups = N_EXPERTS // ge
    best_sum, best_g = None, None
    for g in range(n_groups):
        vals = [_row(sel, g * ge + j) for j in range(ge)]
        top2 = None
        for a in range(ge):
            for b in range(a + 1, ge):
                pair = vals[a] + vals[b]
                top2 = pair if top2 is None else jnp.maximum(top2, pair)
        if best_sum is None:
            best_sum, best_g = top2, jnp.zeros_like(top2, dtype=jnp.int32)
        else:
            upd = top2 > best_sum
            best_g = jnp.where(upd, g, best_g)
            best_sum = jnp.where(upd, top2, best_sum)

    def in_group(x, j):
        out = _row(x, j)
        for g in range(1, n_groups):
            out = jnp.where(best_g == g, _row(x, g * ge + j), out)
        return out

    cand = [in_group(sel, j) for j in range(ge)]
    cand_score = [in_group(scores, j) for j in range(ge)]
    v1, j1, w1 = cand[0], jnp.zeros_like(best_g), cand_score[0]
    for j in range(1, ge):
        upd = cand[j] > v1
        v1 = jnp.where(upd, cand[j], v1)
        j1 = jnp.where(upd, j, j1)
        w1 = jnp.where(upd, cand_score[j], w1)
    v2 = jnp.full_like(v1, -jnp.inf)
    j2, w2 = jnp.zeros_like(best_g), jnp.zeros_like(w1)
    for j in range(ge):
        upd = (cand[j] > v2) & (j1 != j)
        v2 = jnp.where(upd, cand[j], v2)
        j2 = jnp.where(upd, j, j2)
        w2 = jnp.where(upd, cand_score[j], w2)
    wsum = w1 + w2
    g1, g2 = w1 / wsum, w2 / wsum
    lo, hi = jnp.minimum(j1, j2), jnp.maximum(j1, j2)
    pair = jnp.zeros_like(best_g)
    first = jnp.zeros_like(best_g)
    for q, (a, b) in enumerate(PAIR_ORDER):
        hit = (lo == min(a, b)) & (hi == max(a, b))
        pair = jnp.where(hit, q, pair)
        first = jnp.where(hit, a, first)
    is_first = j1 == first
    g_first = jnp.where(is_first, g1, g2)
    g_second = jnp.where(is_first, g2, g1)
    cls = (best_g * len(PAIR_ORDER) + pair).astype(F32)
    zero = jnp.zeros_like(w1)
    return jnp.concatenate([cls, g_first, g_second, zero, zero, zero, zero, zero], axis=0)


def _to_column(row_vec, eye):
    return jnp.sum(jnp.where(eye, row_vec, 0.0), axis=-1, keepdims=True)


def _post_tail(m, x_ref, g_ref, b_ref, wr_ref, rb_ref, x1_ref, x1p_ref, route_ref):
    tm = m.shape[0]
    xn = _layer_norm(ALPHA * x_ref[...] + m, g_ref[...], b_ref[...])
    x1_ref[...] = xn
    xh = xn.astype(BF16)
    xhf = xh.astype(F32)
    xl = (xn - xhf).astype(BF16)
    wr = wr_ref[...]
    contract = (((1,), (1,)), ((), ()))
    a = lax.dot_general(wr, xh, contract, preferred_element_type=F32)
    c = lax.dot_general(wr[0:N_EXPERTS], xl, contract, preferred_element_type=F32)
    logits = a[0:N_EXPERTS] + a[N_EXPERTS:] + c
    route = _route(logits, rb_ref[...])
    route_ref[...] = route
    bits = pltpu.bitcast(xhf, jnp.uint32)
    for s in range(PACKED_SUBLANES):
        lo = bits[:, (2 * s) * LANES:(2 * s + 1) * LANES]
        hi = bits[:, (2 * s + 1) * LANES:(2 * s + 2) * LANES]
        x1p_ref[pl.ds(s, tm, stride=ROW_CHUNKS), :] = (lo >> 16) | (hi & jnp.uint32(0xFFFF0000))
    eye = (lax.broadcasted_iota(jnp.int32, (tm, tm), 0) == lax.broadcasted_iota(jnp.int32, (tm, tm), 1))
    g_first, g_second = _to_column(route[1:2, :], eye), _to_column(route[2:3, :], eye)
    lane = lax.broadcasted_iota(jnp.int32, (tm, LANES), 1)
    gates = jnp.where(lane == 0, g_first, jnp.where(lane == 1, g_second, 0.0))
    x1p_ref[pl.ds(PACKED_SUBLANES, tm, stride=ROW_CHUNKS), :] = pltpu.bitcast(gates, jnp.uint32)
    for s in range(PACKED_SUBLANES + 1, ROW_CHUNKS):
        x1p_ref[pl.ds(s, tm, stride=ROW_CHUNKS), :] = jnp.zeros((tm, LANES), jnp.uint32)


def _post_attn_kernel(o0_ref, l0_ref, o1_ref, l1_ref, o2_ref, l2_ref, ob_ref, w_ref,
                      x_ref, g_ref, b_ref, wr_ref, rb_ref, x1_ref, x1r_ref, route_ref):
    def wide(ref):
        return jnp.concatenate([ref[ch] for ch in range(GROUP_CHUNKS)], axis=-1)

    l0, l1, l2 = wide(l0_ref), wide(l1_ref), wide(l2_ref)
    mx = jnp.maximum(jnp.maximum(l0, l1), l2)
    e0, e1, e2 = jnp.exp(l0 - mx), jnp.exp(l1 - mx), jnp.exp(l2 - mx)
    inv = 1.0 / (e0 + e1 + e2)
    lhs = jnp.concatenate([wide(o0_ref) * (e0 * inv), wide(o1_ref) * (e1 * inv),
                           wide(o2_ref) * (e2 * inv), wide(ob_ref)], axis=-1).astype(BF16)
    m = jnp.dot(lhs, w_ref[...], preferred_element_type=F32)
    _post_tail(m, x_ref, g_ref, b_ref, wr_ref, rb_ref, x1_ref, x1r_ref, route_ref)


def _post_conv_kernel(u_ref, w_ref, wb_ref, x_ref, g_ref, b_ref, wr_ref, rb_ref,
                      x1_ref, x1r_ref, route_ref):
    m = jnp.dot(u_ref[...], w_ref[...], preferred_element_type=F32) + wb_ref[...]
    _post_tail(m, x_ref, g_ref, b_ref, wr_ref, rb_ref, x1_ref, x1r_ref, route_ref)


def _post_mixer(kernel, name, lhs_list, consts_front, x, g, b, wr, rb, tm=512):
    n, d = x.shape
    row = lambda i: (i, 0)
    const = lambda i: (0, 0)
    in_specs = [pl.BlockSpec((tm, a.shape[1]), row) if a.ndim == 2
                else pl.BlockSpec((a.shape[0], tm, LANES), lambda i: (0, i, 0)) for a in lhs_list]
    in_specs += [pl.BlockSpec(c.shape, const) for c in consts_front]
    in_specs += [pl.BlockSpec((tm, d), row)]
    in_specs += [pl.BlockSpec(c.shape, const) for c in (g, b, wr, rb)]
    return pl.pallas_call(
        kernel,
        grid=(n // tm,),
        in_specs=in_specs,
        out_specs=[pl.BlockSpec((tm, d), row),
                   pl.BlockSpec((tm * ROW_CHUNKS, LANES), row),
                   pl.BlockSpec((8, tm), lambda i: (0, i))],
        out_shape=[jax.ShapeDtypeStruct((n, d), F32),
                   jax.ShapeDtypeStruct((n * ROW_CHUNKS, LANES), jnp.uint32),
                   jax.ShapeDtypeStruct((8, n), F32)],
        compiler_params=_params(("parallel",)),
        name=name,
    )(*lhs_list, *consts_front, x, g, b, wr, rb)


def _dwconv_kernel(up_ref, uc_ref, un_ref, w_ref, cb_ref, g_ref, b_ref, o_ref, win, acc, *, ts):
    t = pl.program_id(1)
    has_prev = (t > 0).astype(F32)
    has_next = (t < pl.num_programs(1) - 1).astype(F32)
    for j in range(ROW_CHUNKS):
        lanes = slice(j * LANES, (j + 1) * LANES)
        win[j, 0:CONV_HALO, :] = up_ref[:, lanes] * has_prev
        win[j, CONV_HALO:CONV_HALO + ts, :] = uc_ref[:, lanes]
        win[j, CONV_HALO + ts:, :] = un_ref[:, lanes] * has_next

    def body(j, carry):
        a = jnp.zeros((ts, LANES), F32)
        for tap in range(CONV_WIDTH):
            start = CONV_HALO - CONV_WIDTH // 2 + tap
            a = a + win[j, pl.ds(start, ts), :] * w_ref[j, tap:tap + 1, :]
        acc[j] = a
        return carry

    lax.fori_loop(0, ROW_CHUNKS, body, 0)
    u = jnp.concatenate([acc[j] for j in range(ROW_CHUNKS)], axis=-1) + cb_ref[...]
    y = _layer_norm(u, g_ref[...], b_ref[...])
    o_ref[...] = (y * jax.nn.sigmoid(y)).astype(o_ref.dtype)


def _dwconv_ln_silu(u, w_dw, b_dw, g, b, ts=256):
    bsz, seq, d = u.shape
    per_tile = ts // CONV_HALO
    last = seq // CONV_HALO - 1
    w3 = jnp.pad(w_dw, ((0, 32 - CONV_WIDTH), (0, 0))).reshape(32, ROW_CHUNKS, LANES).transpose(1, 0, 2)
    vec = pl.BlockSpec((1, d), lambda bb, t: (0, 0))
    return pl.pallas_call(
        functools.partial(_dwconv_kernel, ts=ts),
        grid=(bsz, seq // ts),
        in_specs=[pl.BlockSpec((None, CONV_HALO, d), lambda bb, t: (bb, jnp.maximum(t * per_tile - 1, 0), 0)),
                  pl.BlockSpec((None, ts, d), lambda bb, t: (bb, t, 0)),
                  pl.BlockSpec((None, CONV_HALO, d), lambda bb, t: (bb, jnp.minimum((t + 1) * per_tile, last), 0)),
                  pl.BlockSpec(w3.shape, lambda bb, t: (0, 0, 0)),
                  vec, vec, vec],
        out_specs=pl.BlockSpec((None, ts, d), lambda bb, t: (bb, t, 0)),
        out_shape=jax.ShapeDtypeStruct((bsz, seq, d), BF16),
        scratch_shapes=[pltpu.VMEM((ROW_CHUNKS, ts + 2 * CONV_HALO, LANES), F32),
                        pltpu.VMEM((ROW_CHUNKS, ts, LANES), F32)],
        compiler_params=_params(("parallel", "arbitrary")),
        name="dwconv_ln_silu",
    )(u, u, u, w3, b_dw, g, b)


def _row_gather_start(src_hbm, dst, sem, idx_ref, first, count):
    def body(r, carry):
        tok = idx_ref[first + r]
        pltpu.make_async_copy(src_hbm.at[pl.ds(pl.multiple_of(tok * ROW_CHUNKS, ROW_CHUNKS), ROW_CHUNKS)],
                              dst.at[pl.ds(pl.multiple_of(r * ROW_CHUNKS, ROW_CHUNKS), ROW_CHUNKS)],
                              sem).start()
        return carry
    lax.fori_loop(0, count, body, 0, unroll=8)


def _row_gather_wait(src_hbm, dst, sem):
    pltpu.make_async_copy(src_hbm.at[pl.ds(0, dst.shape[0])], dst, sem).wait()


def _dispatch_kernel(dest_ref, tail_ref, n_tiles_ref, x_ref, xs_hbm, zeros, stage, sem, zsem, *, tm):
    i = pl.program_id(0)
    n_steps = pl.num_programs(0)
    slot = i % 2
    tile_rows = EXPERT_TILE * ROW_CHUNKS
    all_tiles = xs_hbm.shape[0] // tile_rows

    def zero_tile(first_row):
        start = pl.multiple_of(first_row * ROW_CHUNKS, ROW_CHUNKS)
        return pltpu.make_async_copy(zeros, xs_hbm.at[pl.ds(start, tile_rows)], zsem)

    @pl.when(i == 0)
    def _():
        zeros[...] = jnp.zeros_like(zeros)
        for c in range(N_CLASSES):
            zero_tile(tail_ref[c]).start()
        for c in range(N_CLASSES):
            zero_tile(tail_ref[c]).wait()

        def unused(j, carry):
            cp = zero_tile(j * EXPERT_TILE)
            cp.start()
            cp.wait()
            return carry
        lax.fori_loop(n_tiles_ref[0], all_tiles, unused, 0)

    stage[slot] = x_ref[...]

    def body(r, carry):
        d = dest_ref[i * tm + r]
        pltpu.make_async_copy(stage.at[slot, pl.ds(pl.multiple_of(r * ROW_CHUNKS, ROW_CHUNKS), ROW_CHUNKS)],
                              xs_hbm.at[pl.ds(pl.multiple_of(d * ROW_CHUNKS, ROW_CHUNKS), ROW_CHUNKS)],
                              sem.at[slot]).start()
        return carry
    lax.fori_loop(0, tm, body, 0, unroll=8)

    def wait_step(s):
        pltpu.make_async_copy(stage.at[s], xs_hbm.at[pl.ds(0, tm * ROW_CHUNKS)], sem.at[s]).wait()

    @pl.when(i > 0)
    def _():
        wait_step(1 - slot)

    @pl.when(i == n_steps - 1)
    def _():
        wait_step(slot)


def _dispatch(x1p, dest, tail_start, n_tiles, n_rows, tm=1024):
    n_tokens = x1p.shape[0] // ROW_CHUNKS
    grid_spec = pltpu.PrefetchScalarGridSpec(
        num_scalar_prefetch=3,
        grid=(n_tokens // tm,),
        in_specs=[pl.BlockSpec((tm * ROW_CHUNKS, LANES), lambda i, dst, tail, nt: (i, 0))],
        out_specs=pl.BlockSpec(memory_space=pl.ANY),
        scratch_shapes=[pltpu.VMEM((EXPERT_TILE * ROW_CHUNKS, LANES), x1p.dtype),
                        pltpu.VMEM((2, tm * ROW_CHUNKS, LANES), x1p.dtype),
                        pltpu.SemaphoreType.DMA((2,)),
                        pltpu.SemaphoreType.DMA(())],
    )
    return pl.pallas_call(
        functools.partial(_dispatch_kernel, tm=tm),
        grid_spec=grid_spec,
        out_shape=jax.ShapeDtypeStruct((n_rows * ROW_CHUNKS, LANES), x1p.dtype),
        compiler_params=_params(("arbitrary",)),
        name="moe_dispatch",
    )(dest, tail_start, n_tiles, x1p)


def _expert_kernel(run_ref, run_expert_ref, n_runs_ref, n_tiles_ref, x_ref,
                   wg_hbm, wu_hbm, wd_hbm, y_ref, stage_g, stage_u, stage_d,
                   wg1_bf, wu1_bf, wd1_bf, wg2_bf, wu2_bf, wd2_bf, sem, *, layer):
    i = pl.program_id(0)
    tm = EXPERT_TILE
    before = jnp.maximum(i - 1, 0)
    working = ((wg1_bf, wu1_bf, wd1_bf), (wg2_bf, wu2_bf, wd2_bf))

    def copies(role, run):
        slot = run % 2
        e = run_expert_ref[role, run]
        return [pltpu.make_async_copy(w.at[layer, e], st.at[role, slot], sem.at[role, slot])
                for w, st in ((wg_hbm, stage_g), (wu_hbm, stage_u), (wd_hbm, stage_d))]

    for role in range(2):
        run = run_ref[role, i]

        @pl.when(i == 0)
        def _():
            for cp in copies(role, 0):
                cp.start()

        @pl.when((i == 0) | (run != run_ref[role, before]))
        def _():
            for cp in copies(role, run):
                cp.wait()
            slot = run % 2
            for st, wk in zip((stage_g, stage_u, stage_d), working[role]):
                wk[...] = st[role, slot].astype(BF16)

            @pl.when(run + 1 < n_runs_ref[role])
            def _():
                for cp in copies(role, run + 1):
                    cp.start()

    @pl.when(i < n_tiles_ref[0])
    def _():
        halves = []
        for s in range(PACKED_SUBLANES):
            word = x_ref[pl.ds(s, tm, stride=ROW_CHUNKS), :]
            halves.append(pltpu.bitcast(word << 16, F32))
            halves.append(pltpu.bitcast(word & jnp.uint32(0xFFFF0000), F32))
        x = jnp.concatenate(halves, axis=-1).astype(BF16)
        gates = pltpu.bitcast(x_ref[pl.ds(PACKED_SUBLANES, tm, stride=ROW_CHUNKS), :], F32)

        def ffn(wg, wu, wd):
            hg = jnp.dot(x, wg[...], preferred_element_type=F32)
            hu = jnp.dot(x, wu[...], preferred_element_type=F32)
            hid = (hg * jax.nn.sigmoid(hg) * hu).astype(BF16)
            return jnp.dot(hid, wd[...], preferred_element_type=F32)

        y = gates[:, 0:1] * ffn(wg1_bf, wu1_bf, wd1_bf) + gates[:, 1:2] * ffn(wg2_bf, wu2_bf, wd2_bf)
        for j in range(ROW_CHUNKS):
            y_ref[pl.ds(j, tm, stride=ROW_CHUNKS), :] = y[:, j * LANES:(j + 1) * LANES]

    @pl.when(i >= n_tiles_ref[0])
    def _():
        y_ref[...] = jnp.zeros_like(y_ref)


def _experts(xs, w_gate, w_up, w_down, layer, tile_run, run_expert, n_runs, n_tiles):
    tm = EXPERT_TILE
    n_rows = xs.shape[0] // ROW_CHUNKS
    d, de = w_gate.shape[2], w_gate.shape[3]
    row_map = lambda i, run, rexp, nr, nt: (jnp.minimum(i, jnp.maximum(nt[0] - 1, 0)), 0)
    hbm = pl.BlockSpec(memory_space=pl.ANY)
    grid_spec = pltpu.PrefetchScalarGridSpec(
        num_scalar_prefetch=4,
        grid=(n_rows // tm,),
        in_specs=[pl.BlockSpec((tm * ROW_CHUNKS, LANES), row_map), hbm, hbm, hbm],
        out_specs=pl.BlockSpec((tm * ROW_CHUNKS, LANES), lambda i, run, rexp, nr, nt: (i, 0)),
        scratch_shapes=[pltpu.VMEM((2, 2, d, de), F32), pltpu.VMEM((2, 2, d, de), F32),
                        pltpu.VMEM((2, 2, de, d), F32),
                        pltpu.VMEM((d, de), BF16), pltpu.VMEM((d, de), BF16), pltpu.VMEM((de, d), BF16),
                        pltpu.VMEM((d, de), BF16), pltpu.VMEM((d, de), BF16), pltpu.VMEM((de, d), BF16),
                        pltpu.SemaphoreType.DMA((2, 2))],
    )
    return pl.pallas_call(
        functools.partial(_expert_kernel, layer=layer),
        grid_spec=grid_spec,
        out_shape=jax.ShapeDtypeStruct((n_rows * ROW_CHUNKS, LANES), F32),
        compiler_params=_params(("arbitrary",)),
        name="moe_experts",
    )(tile_run, run_expert, n_runs, n_tiles, xs, w_gate, w_up, w_down)


def _combine_kernel(dest_ref, y_hbm, x_ref, g_ref, b_ref, o_ref, ybuf, sem):
    i = pl.program_id(0)
    n_steps = pl.num_programs(0)
    slot = i % 2
    tm = COMBINE_TILE

    @pl.when(i == 0)
    def _():
        _row_gather_start(y_hbm, ybuf.at[0], sem.at[0], dest_ref, 0, tm)

    @pl.when(i + 1 < n_steps)
    def _():
        _row_gather_start(y_hbm, ybuf.at[1 - slot], sem.at[1 - slot], dest_ref, (i + 1) * tm, tm)

    _row_gather_wait(y_hbm, ybuf.at[slot], sem.at[slot])
    f = jnp.concatenate([ybuf[slot, pl.ds(j, tm, stride=ROW_CHUNKS), :] for j in range(ROW_CHUNKS)], axis=-1)
    o_ref[...] = _layer_norm(ALPHA * x_ref[...] + f, g_ref[...], b_ref[...])


def _combine(y_rows, x1, dest, g, b):
    n, d = x1.shape
    tm = COMBINE_TILE
    grid_spec = pltpu.PrefetchScalarGridSpec(
        num_scalar_prefetch=1,
        grid=(n // tm,),
        in_specs=[pl.BlockSpec(memory_space=pl.ANY),
                  pl.BlockSpec((tm, d), lambda i, dst: (i, 0)),
                  pl.BlockSpec((1, d), lambda i, dst: (0, 0)),
                  pl.BlockSpec((1, d), lambda i, dst: (0, 0))],
        out_specs=pl.BlockSpec((tm, d), lambda i, dst: (i, 0)),
        scratch_shapes=[pltpu.VMEM((2, tm * ROW_CHUNKS, LANES), F32),
                        pltpu.SemaphoreType.DMA((2,))],
    )
    return pl.pallas_call(
        _combine_kernel,
        grid_spec=grid_spec,
        out_shape=jax.ShapeDtypeStruct((n, d), F32),
        compiler_params=_params(("arbitrary",)),
        name="moe_combine",
    )(dest, y_rows, x1, g, b)


def _dispatch_plan(route, n_tokens):
    tm = EXPERT_TILE
    blk = LANES
    n_rows = n_tokens + N_CLASSES * tm
    hi = lax.Precision.HIGHEST
    onehot = (route[0][:, None] == jnp.arange(N_CLASSES, dtype=F32)[None, :]).astype(F32)
    blocks = onehot.reshape(n_tokens // blk, blk, N_CLASSES)
    tri = np.tril(np.ones((blk, blk), np.float32))
    local = jnp.einsum("ij,bjk->bik", tri, blocks, precision=hi)
    totals = local[:, -1, :]
    strict = np.tril(np.ones((n_tokens // blk,) * 2, np.float32), -1)
    before = jnp.dot(strict, totals, precision=hi)
    rank = jnp.sum((local + before[:, None, :]) * blocks, axis=-1).reshape(n_tokens) - 1.0
    counts = before[-1] + totals[-1]
    padded = jnp.ceil(counts / tm) * tm
    seg_end = jnp.dot(np.tril(np.ones((N_CLASSES,) * 2, np.float32)), padded, precision=hi)
    seg_start = seg_end - padded
    dest = (jnp.sum(onehot * seg_start[None, :], axis=-1) + rank).astype(jnp.int32)
    tile_start = jnp.arange(n_rows // tm, dtype=F32) * tm
    tile_class = jnp.sum((seg_end[None, :] <= tile_start[:, None]).astype(jnp.int32), axis=-1)
    last_class = jnp.max(jnp.where(padded > 0, jnp.arange(N_CLASSES), 0))
    tile_class = jnp.minimum(tile_class, last_class)
    tile_hot = (tile_class[:, None] == np.arange(N_CLASSES)[None, :]).astype(jnp.int32)
    n_pairs = len(PAIR_ORDER)
    expert_of = np.array([[(c // n_pairs) * GROUP_EXPERTS + PAIR_ORDER[c % n_pairs][role]
                           for c in range(N_CLASSES)] for role in range(2)])
    tile_expert = jnp.sum(tile_hot[None, :, :] * expert_of[:, None, :], axis=-1)
    n_all = n_rows // tm
    starts = jnp.concatenate([jnp.ones((2, 1), jnp.int32),
                              (tile_expert[:, 1:] != tile_expert[:, :-1]).astype(jnp.int32)], axis=1)
    tile_run = jnp.cumsum(starts, axis=1) - 1
    run_hot = (tile_run[:, :, None] == np.arange(n_all)[None, None, :]).astype(jnp.int32)
    run_expert = jnp.sum(run_hot * (starts * tile_expert)[:, :, None], axis=1)
    n_runs = tile_run[:, -1] + 1
    n_tiles = (seg_end[-1] / tm).astype(jnp.int32).reshape(1)
    tail_start = jnp.maximum(seg_end - tm, 0.0).astype(jnp.int32)
    plan = (tile_run.astype(jnp.int32), run_expert.astype(jnp.int32), n_runs.astype(jnp.int32), n_tiles)
    return plan, tail_start, dest, n_rows


def _moe(x1, x1p, route, w_gate, w_up, w_down, layer, g, b):
    n = x1.shape[0]
    plan, tail_start, dest, n_rows = _dispatch_plan(route, n)
    xs = _dispatch(x1p, dest, tail_start, plan[-1], n_rows)
    y_rows = _experts(xs, w_gate, w_up, w_down, layer, *plan)
    return _combine(y_rows, x1, dest, g, b)


def _vec(v):
    return v.reshape(1, -1).astype(F32)


def kernel(x, w_in_attn, w_out_attn, rel_bias, rpb_2d, w_pw1, b_pw1, w_dw, b_dw, conv_ln_g, conv_ln_b, w_pw2, b_pw2, ln_mix_g, ln_mix_b, ln_ffn_g, ln_ffn_b, w_router, router_bias, w_gate, w_up, w_down):
    bsz, seq, d = x.shape
    n = bsz * seq
    h = x.reshape(n, d)
    wr_t = w_router.T.astype(F32)
    wr_hi = wr_t.astype(BF16)
    wr_lo = (wr_t - wr_hi.astype(F32)).astype(BF16)
    wr = jnp.concatenate([wr_hi, wr_lo], axis=0)
    rb = router_bias.reshape(N_EXPERTS, 1).astype(F32)
    n_dil_cols = len(DILATIONS) * GROUP_COLS
    q_cols = np.zeros((3 * d,), np.float32) + 1.0
    q_cols[0:n_dil_cols] = HEAD_DIM ** -0.5
    q_cols[3 * n_dil_cols:3 * n_dil_cols + GROUP_COLS] = HEAD_DIM ** -0.5

    for layer in range(DEPTH):
        i = layer // 2
        g_mix, b_mix = _vec(ln_mix_g[layer]), _vec(ln_mix_b[layer])
        if layer % 2 == 0:
            w_in = (w_in_attn[i] * q_cols[None, :]).astype(BF16)
            proj = _matmul(h, w_in)
            lhs = []
            for grp in range(len(DILATIONS)):
                lhs += _dilated_attention(proj, _dilated_bias(rel_bias, grp), grp, bsz, seq)
            lhs.append(_neighborhood_attention(proj, _na_bias(rpb_2d[i]), bsz, seq))
            x1, x1r, route = _post_mixer(_post_attn_kernel, "post_attn", lhs,
                                         [w_out_attn[i].astype(BF16)], h, g_mix, b_mix, wr, rb)
        else:
            u = _pw1_glu(h, w_pw1[i].astype(BF16), _vec(b_pw1[i]))
            u = _dwconv_ln_silu(u.reshape(bsz, seq, d), w_dw[i].astype(F32), _vec(b_dw[i]),
                                _vec(conv_ln_g[i]), _vec(conv_ln_b[i]))
            x1, x1r, route = _post_mixer(_post_conv_kernel, "post_conv", [u.reshape(n, d)],
                                         [w_pw2[i].astype(BF16), _vec(b_pw2[i])], h, g_mix, b_mix, wr, rb)
        h = _moe(x1, x1r, route, w_gate, w_up, w_down, layer,
                 _vec(ln_ffn_g[layer]), _vec(ln_ffn_b[layer]))
    return h.reshape(bsz, seq, d)
```

```python
import functools
import math

import jax
import jax.numpy as jnp
import numpy as np
from jax import lax
from jax.experimental import pallas as pl
from jax.experimental.pallas import tpu as pltpu

D_MODEL = 1024
HEAD_DIM = 64
GROUP_HEADS = 4
GROUP_COLS = GROUP_HEADS * HEAD_DIM
DILATIONS = (1, 4, 16)
QUERY_BLOCK = 64
GRID_W = 64
NA_ROWS = 8
NA_COLS = 16
T5_BUCKETS = 32
T5_MAX_DIST = 1024
CONV_WIDTH = 31
CONV_HALO = 16
N_EXPERTS = 16
GROUP_EXPERTS = 4
D_EXPERT = 512
DEPTH = 4
ALPHA = (2 * DEPTH) ** 0.25
LN_EPS = 1e-5
NEG = -1e30

LANES = 128
ROW_CHUNKS = D_MODEL // LANES
GROUP_CHUNKS = GROUP_COLS // LANES
PAIR_ORDER = ((0, 1), (0, 2), (0, 3), (1, 3), (1, 2), (3, 2))
N_CLASSES = (N_EXPERTS // GROUP_EXPERTS) * len(PAIR_ORDER)
PACKED_SUBLANES = D_MODEL // (2 * LANES)
SEQ_TILE = 1024
EXPERT_TILE = 256
COMBINE_TILE = 512
ROW_COPY_UNROLL = 8
VMEM_LIMIT = 56 * 1024 * 1024

F32 = jnp.float32
BF16 = jnp.bfloat16


def _params(semantics, vmem=VMEM_LIMIT):
    return pltpu.CompilerParams(dimension_semantics=semantics, vmem_limit_bytes=vmem)


MATMUL_SLAB = 512


def _matmul_kernel(x_ref, w_ref, o_ref):
    xb = x_ref[...].astype(BF16)
    per_slab = MATMUL_SLAB // LANES
    for s in range(w_ref.shape[1] // MATMUL_SLAB):
        y = jnp.dot(xb, w_ref[:, s * MATMUL_SLAB:(s + 1) * MATMUL_SLAB], preferred_element_type=F32)
        for c in range(per_slab):
            o_ref[s * per_slab + c] = y[:, c * LANES:(c + 1) * LANES]


def _matmul(x, w, tm=512):
    n, k = x.shape
    _, m = w.shape
    return pl.pallas_call(
        _matmul_kernel,
        grid=(n // tm,),
        in_specs=[pl.BlockSpec((tm, k), lambda i: (i, 0)),
                  pl.BlockSpec((k, m), lambda i: (0, 0))],
        out_specs=pl.BlockSpec((m // LANES, tm, LANES), lambda i: (0, i, 0)),
        out_shape=jax.ShapeDtypeStruct((m // LANES, n, LANES), F32),
        compiler_params=_params(("parallel",)),
        name="qkv_proj",
    )(x, w)


def _glu_kernel(x_ref, w_ref, b_ref, o_ref):
    xb = x_ref[...].astype(BF16)
    half = w_ref.shape[1] // 2
    for s in range(half // MATMUL_SLAB):
        cols = slice(s * MATMUL_SLAB, (s + 1) * MATMUL_SLAB)
        gate_cols = slice(half + s * MATMUL_SLAB, half + (s + 1) * MATMUL_SLAB)
        a = jnp.dot(xb, w_ref[:, cols], preferred_element_type=F32) + b_ref[:, cols]
        g = jnp.dot(xb, w_ref[:, gate_cols], preferred_element_type=F32) + b_ref[:, gate_cols]
        o_ref[:, cols] = a * jax.nn.sigmoid(g)


def _pw1_glu(x, w, b, tm=512):
    n, k = x.shape
    m = w.shape[1]
    return pl.pallas_call(
        _glu_kernel,
        grid=(n // tm,),
        in_specs=[pl.BlockSpec((tm, k), lambda i: (i, 0)),
                  pl.BlockSpec((k, m), lambda i: (0, 0)),
                  pl.BlockSpec((1, m), lambda i: (0, 0))],
        out_specs=pl.BlockSpec((tm, m // 2), lambda i: (i, 0)),
        out_shape=jax.ShapeDtypeStruct((n, m // 2), F32),
        compiler_params=_params(("parallel",)),
        name="pw1_glu",
    )(x, w, b)


def _head_masks(rows):
    lane = lax.broadcasted_iota(jnp.int32, (rows, GROUP_COLS), 1)
    return [(lane >= HEAD_DIM * h) & (lane < HEAD_DIM * (h + 1)) for h in range(GROUP_HEADS)]


def _stack_heads(q, masks):
    return jnp.concatenate([jnp.where(m, q, 0.0) for m in masks], axis=0)


def _unstack_heads(o4, masks, rows):
    out = jnp.where(masks[0], o4[0:rows], 0.0)
    for h in range(1, GROUP_HEADS):
        out = out + jnp.where(masks[h], o4[h * rows:(h + 1) * rows], 0.0)
    return out


def _softmax_pv(logits, v, with_lse):
    m = jnp.max(logits, axis=-1, keepdims=True)
    p = jnp.exp(logits - m)
    s = jnp.sum(p, axis=-1, keepdims=True)
    o4 = jnp.dot(p.astype(BF16), v, preferred_element_type=F32) * (1.0 / s)
    lse = (m + jnp.log(s)) if with_lse else None
    return o4, lse


def _dilated_kernel(q_ref, kp_ref, kc_ref, kn_ref, vp_ref, vc_ref, vn_ref, bias_ref,
                    o_ref, l_ref, *, dil):
    t = pl.program_id(1)
    chunk = QUERY_BLOCK * dil
    n_chunks = SEQ_TILE // chunk
    qb = QUERY_BLOCK
    masks = _head_masks(qb)
    col = lax.broadcasted_iota(jnp.int32, (GROUP_HEADS * qb, 3 * qb), 1)
    no_prev = jnp.where(col < qb, NEG, 0.0) * (t == 0).astype(F32)
    no_next = jnp.where(col >= 2 * qb, NEG, 0.0) * (t == pl.num_programs(1) - 1).astype(F32)
    bias = bias_ref[...]

    def sl(start):
        return pl.ds(start, qb) if dil == 1 else pl.ds(start, qb, stride=dil)

    def rows(ref, start):
        return jnp.concatenate([ref[c, sl(start), :] for c in range(GROUP_CHUNKS)], axis=-1)

    def window(p_ref, c_ref, n_ref, c, r):
        parts = []
        for cc in (c - 1, c, c + 1):
            if cc < 0:
                parts.append(rows(p_ref, r))
            elif cc >= n_chunks:
                parts.append(rows(n_ref, r))
            else:
                parts.append(rows(c_ref, cc * chunk + r))
        return jnp.concatenate(parts, axis=0).astype(BF16)

    for c in range(n_chunks):
        for r in range(dil):
            base = c * chunk + r
            q4 = _stack_heads(rows(q_ref, base), masks).astype(BF16)
            k = window(kp_ref, kc_ref, kn_ref, c, r)
            v = window(vp_ref, vc_ref, vn_ref, c, r)
            logits = lax.dot_general(q4, k, (((1,), (1,)), ((), ())),
                                     preferred_element_type=F32) + bias
            if c == 0:
                logits = logits + no_prev
            if c == n_chunks - 1:
                logits = logits + no_next
            o4, lse = _softmax_pv(logits, v, True)
            o = _unstack_heads(o4, masks, qb)
            lb = _unstack_heads(jnp.broadcast_to(lse, (GROUP_HEADS * qb, GROUP_COLS)), masks, qb)
            for ch in range(GROUP_CHUNKS):
                o_ref[ch, sl(base), :] = o[:, ch * LANES:(ch + 1) * LANES]
                l_ref[ch, sl(base), :] = lb[:, ch * LANES:(ch + 1) * LANES]


def _tile_spec(col_block, seq):
    tiles = seq // SEQ_TILE
    return pl.BlockSpec((GROUP_CHUNKS, SEQ_TILE, LANES), lambda b, t: (col_block, b * tiles + t, 0))


def _halo_specs(col_block, halo, seq):
    per_tile = SEQ_TILE // halo
    per_seq = seq // halo
    return [
        pl.BlockSpec((GROUP_CHUNKS, halo, LANES),
                     lambda b, t: (col_block, b * per_seq + jnp.maximum(t * per_tile - 1, 0), 0)),
        _tile_spec(col_block, seq),
        pl.BlockSpec((GROUP_CHUNKS, halo, LANES),
                     lambda b, t: (col_block, b * per_seq + jnp.minimum((t + 1) * per_tile, per_seq - 1), 0)),
    ]


def _dilated_attention(proj, bias, group, bsz, seq):
    dil = DILATIONS[group]
    n_groups = len(DILATIONS)
    q_col, k_col, v_col = group, n_groups + group, 2 * n_groups + group
    out_spec = _tile_spec(0, seq)
    out_sds = jax.ShapeDtypeStruct((GROUP_CHUNKS, bsz * seq, LANES), F32)
    return pl.pallas_call(
        functools.partial(_dilated_kernel, dil=dil),
        grid=(bsz, seq // SEQ_TILE),
        in_specs=[_tile_spec(q_col, seq)]
        + _halo_specs(k_col, QUERY_BLOCK * dil, seq) + _halo_specs(v_col, QUERY_BLOCK * dil, seq)
        + [pl.BlockSpec(bias.shape, lambda b, t: (0, 0))],
        out_specs=[out_spec, out_spec],
        out_shape=[out_sds, out_sds],
        compiler_params=_params(("parallel", "arbitrary")),
        name=f"dilated_attn_d{dil}",
    )(proj, proj, proj, proj, proj, proj, proj, bias)


NA_HALO = 256
NA_KEYS = NA_ROWS * GRID_W


def _na_kernel(q_ref, kp_ref, kc_ref, kn_ref, vp_ref, vc_ref, vn_ref, bias_ref, o_ref,
               kwin, vwin, *, grid_rows):
    t = pl.program_id(1)
    tile_rows = SEQ_TILE // GRID_W
    half = NA_ROWS // 2
    for win, p_ref, c_ref, n_ref in ((kwin, kp_ref, kc_ref, kn_ref), (vwin, vp_ref, vc_ref, vn_ref)):
        for ch in range(GROUP_CHUNKS):
            lanes = slice(ch * LANES, (ch + 1) * LANES)
            win[0:NA_HALO, lanes] = p_ref[ch].astype(BF16)
            win[NA_HALO:NA_HALO + SEQ_TILE, lanes] = c_ref[ch].astype(BF16)
            win[NA_HALO + SEQ_TILE:, lanes] = n_ref[ch].astype(BF16)
    masks = _head_masks(GRID_W)
    for i in range(tile_rows):
        row = t * tile_rows + i
        row0 = jnp.clip(row - half, 0, grid_rows - NA_ROWS)
        off = pl.multiple_of((row0 - (t * tile_rows - half)) * GRID_W, GRID_W)
        q = jnp.concatenate([q_ref[ch, i * GRID_W:(i + 1) * GRID_W, :] for ch in range(GROUP_CHUNKS)], axis=-1)
        q4 = _stack_heads(q, masks).astype(BF16)
        k = kwin[pl.ds(off, NA_KEYS), :]
        v = vwin[pl.ds(off, NA_KEYS), :]
        logits = lax.dot_general(q4, k, (((1,), (1,)), ((), ())),
                                 preferred_element_type=F32) + bias_ref[row - row0]
        o4, _ = _softmax_pv(logits, v, False)
        o = _unstack_heads(o4, masks, GRID_W)
        for ch in range(GROUP_CHUNKS):
            o_ref[ch, i * GRID_W:(i + 1) * GRID_W, :] = o[:, ch * LANES:(ch + 1) * LANES]


def _neighborhood_attention(proj, bias, bsz, seq):
    base = 3 * len(DILATIONS)
    win_rows = SEQ_TILE + 2 * NA_HALO
    return pl.pallas_call(
        functools.partial(_na_kernel, grid_rows=seq // GRID_W),
        grid=(bsz, seq // SEQ_TILE),
        in_specs=[_tile_spec(base, seq)]
        + _halo_specs(base + 1, NA_HALO, seq) + _halo_specs(base + 2, NA_HALO, seq)
        + [pl.BlockSpec(bias.shape, lambda b, t: (0, 0, 0))],
        out_specs=_tile_spec(0, seq),
        out_shape=jax.ShapeDtypeStruct((GROUP_CHUNKS, bsz * seq, LANES), F32),
        scratch_shapes=[pltpu.VMEM((win_rows, GROUP_COLS), BF16),
                        pltpu.VMEM((win_rows, GROUP_COLS), BF16)],
        compiler_params=_params(("parallel", "arbitrary")),
        name="neighborhood_attn",
    )(proj, proj, proj, proj, proj, proj, proj, bias)


def _t5_bucket_table(dil):
    qb = QUERY_BLOCK
    rel = (np.arange(3 * qb)[None, :] - qb - np.arange(qb)[:, None])
    nb = T5_BUCKETS // 2
    max_exact = nb // 2
    n = np.abs(rel * dil)
    nf = np.maximum(n, 1).astype(np.float32)
    large = max_exact + (np.log(nf / np.float32(max_exact)) / np.float32(math.log(T5_MAX_DIST / max_exact))
                         * np.float32(nb - max_exact)).astype(np.int32)
    large = np.minimum(large, nb - 1)
    bucket = np.where(rel * dil > 0, nb, 0) + np.where(n < max_exact, n, large)
    return bucket.astype(np.int32), np.abs(rel) <= qb


def _dilated_bias(rel_bias, group):
    bucket, band = _t5_bucket_table(DILATIONS[group])
    tab = rel_bias[:, group * GROUP_HEADS:(group + 1) * GROUP_HEADS].astype(F32)
    onehot = (bucket[None, :, :] == np.arange(T5_BUCKETS)[:, None, None]).astype(np.float32)
    bias = jnp.einsum("bh,bqk->hqk", tab, onehot, precision=lax.Precision.HIGHEST)
    bias = jnp.where(band[None], bias, NEG)
    return bias.reshape(GROUP_HEADS * QUERY_BLOCK, 3 * QUERY_BLOCK)


def _na_bias(rpb):
    qc = np.arange(GRID_W)[:, None]
    kc = np.arange(GRID_W)[None, :]
    win0 = np.clip(qc - NA_COLS // 2, 0, GRID_W - NA_COLS)
    col_ok = (kc >= win0) & (kc < win0 + NA_COLS)
    dc = np.clip(kc - qc + NA_COLS - 1, 0, 2 * NA_COLS - 2)
    shift = np.arange(NA_ROWS)[:, None]
    kr = np.arange(NA_ROWS)[None, :]
    dr = kr - shift + NA_ROWS - 1
    pick_row = (dr[:, :, None] == np.arange(2 * NA_ROWS - 1)[None, None, :]).astype(np.float32)
    pick_col = (dc[None, :, :] == np.arange(2 * NA_COLS - 1)[:, None, None]).astype(np.float32)
    hi = lax.Precision.HIGHEST
    rows = jnp.einsum("hac,dka->hdkc", rpb.astype(F32), pick_row, precision=hi)
    b = jnp.einsum("hdkc,cqx->dhqkx", rows, pick_col, precision=hi)
    b = jnp.where(col_ok[None, None, :, None, :], b, NEG)
    return b.reshape(NA_ROWS, GROUP_HEADS * GRID_W, NA_KEYS)


def _layer_norm(z, g, b):
    mu = jnp.mean(z, axis=-1, keepdims=True)
    zc = z - mu
    var = jnp.mean(zc * zc, axis=-1, keepdims=True)
    return zc * lax.rsqrt(var + LN_EPS) * g + b


def _row(x, i):
    return x[i:i + 1, :]


def _route(logits, rbias):
    scores = jax.nn.sigmoid(logits)
    sel = scores + rbias
    ge = GROUP_EXPERTS
    n_groups = N_EXPERTS // ge
    best_sum, best_g = None, None
    for g in range(n_groups):
        vals = [_row(sel, g * ge + j) for j in range(ge)]
        top2 = None
        for a in range(ge):
            for b in range(a + 1, ge):
                pair = vals[a] + vals[b]
                top2 = pair if top2 is None else jnp.maximum(top2, pair)
        if best_sum is None:
            best_sum, best_g = top2, jnp.zeros_like(top2, dtype=jnp.int32)
        else:
            upd = top2 > best_sum
            best_g = jnp.where(upd, g, best_g)
            best_sum = jnp.where(upd, top2, best_sum)

    def in_group(x, j):
        out = _row(x, j)
        for g in range(1, n_groups):
            out = jnp.where(best_g == g, _row(x, g * ge + j), out)
        return out

    cand = [in_group(sel, j) for j in range(ge)]
    cand_score = [in_group(scores, j) for j in range(ge)]
    v1, j1, w1 = cand[0], jnp.zeros_like(best_g), cand_score[0]
    for j in range(1, ge):
        upd = cand[j] > v1
        v1 = jnp.where(upd, cand[j], v1)
        j1 = jnp.where(upd, j, j1)
        w1 = jnp.where(upd, cand_score[j], w1)
    v2 = jnp.full_like(v1, -jnp.inf)
    j2, w2 = jnp.zeros_like(best_g), jnp.zeros_like(w1)
    for j in range(ge):
        upd = (cand[j] > v2) & (j1 != j)
        v2 = jnp.where(upd, cand[j], v2)
        j2 = jnp.where(upd, j, j2)
        w2 = jnp.where(upd, cand_score[j], w2)
    wsum = w1 + w2
    g1, g2 = w1 / wsum, w2 / wsum
    lo, hi = jnp.minimum(j1, j2), jnp.maximum(j1, j2)
    pair = jnp.zeros_like(best_g)
    first = jnp.zeros_like(best_g)
    for q, (a, b) in enumerate(PAIR_ORDER):
        hit = (lo == min(a, b)) & (hi == max(a, b))
        pair = jnp.where(hit, q, pair)
        first = jnp.where(hit, a, first)
    is_first = j1 == first
    g_first = jnp.where(is_first, g1, g2)
    g_second = jnp.where(is_first, g2, g1)
    cls = (best_g * len(PAIR_ORDER) + pair).astype(F32)
    zero = jnp.zeros_like(w1)
    return jnp.concatenate([cls, g_first, g_second, zero, zero, zero, zero, zero], axis=0)


def _to_column(row_vec, eye):
    return jnp.sum(jnp.where(eye, row_vec, 0.0), axis=-1, keepdims=True)


def _post_tail(m, x_ref, g_ref, b_ref, wr_ref, rb_ref, x1_ref, x1p_ref, route_ref):
    tm = m.shape[0]
    xn = _layer_norm(ALPHA * x_ref[...] + m, g_ref[...], b_ref[...])
    x1_ref[...] = xn
    xh = xn.astype(BF16)
    xhf = xh.astype(F32)
    xl = (xn - xhf).astype(BF16)
    wr = wr_ref[...]
    contract = (((1,), (1,)), ((), ()))
    a = lax.dot_general(wr, xh, contract, preferred_element_type=F32)
    c = lax.dot_general(wr[0:N_EXPERTS], xl, contract, preferred_element_type=F32)
    logits = a[0:N_EXPERTS] + a[N_EXPERTS:] + c
    route = _route(logits, rb_ref[...])
    route_ref[...] = route
    bits = pltpu.bitcast(xhf, jnp.uint32)
    for s in range(PACKED_SUBLANES):
        lo = bits[:, (2 * s) * LANES:(2 * s + 1) * LANES]
        hi = bits[:, (2 * s + 1) * LANES:(2 * s + 2) * LANES]
        x1p_ref[pl.ds(s, tm, stride=ROW_CHUNKS), :] = (lo >> 16) | (hi & jnp.uint32(0xFFFF0000))
    eye = (lax.broadcasted_iota(jnp.int32, (tm, tm), 0) == lax.broadcasted_iota(jnp.int32, (tm, tm), 1))
    g_first, g_second = _to_column(route[1:2, :], eye), _to_column(route[2:3, :], eye)
    lane = lax.broadcasted_iota(jnp.int32, (tm, LANES), 1)
    gates = jnp.where(lane == 0, g_first, jnp.where(lane == 1, g_second, 0.0))
    x1p_ref[pl.ds(PACKED_SUBLANES, tm, stride=ROW_CHUNKS), :] = pltpu.bitcast(gates, jnp.uint32)
    for s in range(PACKED_SUBLANES + 1, ROW_CHUNKS):
        x1p_ref[pl.ds(s, tm, stride=ROW_CHUNKS), :] = jnp.zeros((tm, LANES), jnp.uint32)


def _post_attn_kernel(o0_ref, l0_ref, o1_ref, l1_ref, o2_ref, l2_ref, ob_ref, w_ref,
                      x_ref, g_ref, b_ref, wr_ref, rb_ref, x1_ref, x1r_ref, route_ref):
    def wide(ref):
        return jnp.concatenate([ref[ch] for ch in range(GROUP_CHUNKS)], axis=-1)

    l0, l1, l2 = wide(l0_ref), wide(l1_ref), wide(l2_ref)
    mx = jnp.maximum(jnp.maximum(l0, l1), l2)
    e0, e1, e2 = jnp.exp(l0 - mx), jnp.exp(l1 - mx), jnp.exp(l2 - mx)
    inv = 1.0 / (e0 + e1 + e2)
    lhs = jnp.concatenate([wide(o0_ref) * (e0 * inv), wide(o1_ref) * (e1 * inv),
                           wide(o2_ref) * (e2 * inv), wide(ob_ref)], axis=-1).astype(BF16)
    m = jnp.dot(lhs, w_ref[...], preferred_element_type=F32)
    _post_tail(m, x_ref, g_ref, b_ref, wr_ref, rb_ref, x1_ref, x1r_ref, route_ref)


def _post_conv_kernel(u_ref, w_ref, wb_ref, x_ref, g_ref, b_ref, wr_ref, rb_ref,
                      x1_ref, x1r_ref, route_ref):
    m = jnp.dot(u_ref[...], w_ref[...], preferred_element_type=F32) + wb_ref[...]
    _post_tail(m, x_ref, g_ref, b_ref, wr_ref, rb_ref, x1_ref, x1r_ref, route_ref)


def _post_mixer(kernel, name, lhs_list, consts_front, x, g, b, wr, rb, tm=512):
    n, d = x.shape
    row = lambda i: (i, 0)
    const = lambda i: (0, 0)
    in_specs = [pl.BlockSpec((tm, a.shape[1]), row) if a.ndim == 2
                else pl.BlockSpec((a.shape[0], tm, LANES), lambda i: (0, i, 0)) for a in lhs_list]
    in_specs += [pl.BlockSpec(c.shape, const) for c in consts_front]
    in_specs += [pl.BlockSpec((tm, d), row)]
    in_specs += [pl.BlockSpec(c.shape, const) for c in (g, b, wr, rb)]
    return pl.pallas_call(
        kernel,
        grid=(n // tm,),
        in_specs=in_specs,
        out_specs=[pl.BlockSpec((tm, d), row),
                   pl.BlockSpec((tm * ROW_CHUNKS, LANES), row),
                   pl.BlockSpec((8, tm), lambda i: (0, i))],
        out_shape=[jax.ShapeDtypeStruct((n, d), F32),
                   jax.ShapeDtypeStruct((n * ROW_CHUNKS, LANES), jnp.uint32),
                   jax.ShapeDtypeStruct((8, n), F32)],
        compiler_params=_params(("parallel",)),
        name=name,
    )(*lhs_list, *consts_front, x, g, b, wr, rb)


def _dwconv_kernel(up_ref, uc_ref, un_ref, w_ref, cb_ref, g_ref, b_ref, o_ref, win, acc, *, ts):
    t = pl.program_id(1)
    has_prev = (t > 0).astype(F32)
    has_next = (t < pl.num_programs(1) - 1).astype(F32)
    for j in range(ROW_CHUNKS):
        lanes = slice(j * LANES, (j + 1) * LANES)
        win[j, 0:CONV_HALO, :] = up_ref[:, lanes] * has_prev
        win[j, CONV_HALO:CONV_HALO + ts, :] = uc_ref[:, lanes]
        win[j, CONV_HALO + ts:, :] = un_ref[:, lanes] * has_next

    def body(j, carry):
        a = jnp.zeros((ts, LANES), F32)
        for tap in range(CONV_WIDTH):
            start = CONV_HALO - CONV_WIDTH // 2 + tap
            a = a + win[j, pl.ds(start, ts), :] * w_ref[j, tap:tap + 1, :]
        acc[j] = a
        return carry

    lax.fori_loop(0, ROW_CHUNKS, body, 0)
    u = jnp.concatenate([acc[j] for j in range(ROW_CHUNKS)], axis=-1) + cb_ref[...]
    y = _layer_norm(u, g_ref[...], b_ref[...])
    o_ref[...] = (y * jax.nn.sigmoid(y)).astype(o_ref.dtype)


def _dwconv_ln_silu(u, w_dw, b_dw, g, b, ts=256):
    bsz, seq, d = u.shape
    per_tile = ts // CONV_HALO
    last = seq // CONV_HALO - 1
    w3 = jnp.pad(w_dw, ((0, 32 - CONV_WIDTH), (0, 0))).reshape(32, ROW_CHUNKS, LANES).transpose(1, 0, 2)
    vec = pl.BlockSpec((1, d), lambda bb, t: (0, 0))
    return pl.pallas_call(
        functools.partial(_dwconv_kernel, ts=ts),
        grid=(bsz, seq // ts),
        in_specs=[pl.BlockSpec((None, CONV_HALO, d), lambda bb, t: (bb, jnp.maximum(t * per_tile - 1, 0), 0)),
                  pl.BlockSpec((None, ts, d), lambda bb, t: (bb, t, 0)),
                  pl.BlockSpec((None, CONV_HALO, d), lambda bb, t: (bb, jnp.minimum((t + 1) * per_tile, last), 0)),
                  pl.BlockSpec(w3.shape, lambda bb, t: (0, 0, 0)),
                  vec, vec, vec],
        out_specs=pl.BlockSpec((None, ts, d), lambda bb, t: (bb, t, 0)),
        out_shape=jax.ShapeDtypeStruct((bsz, seq, d), BF16),
        scratch_shapes=[pltpu.VMEM((ROW_CHUNKS, ts + 2 * CONV_HALO, LANES), F32),
                        pltpu.VMEM((ROW_CHUNKS, ts, LANES), F32)],
        compiler_params=_params(("parallel", "arbitrary")),
        name="dwconv_ln_silu",
    )(u, u, u, w3, b_dw, g, b)


def _row_gather_start(src_hbm, dst, sem, idx_ref, first, count):
    def body(r8, carry):
        for u in range(ROW_COPY_UNROLL):
            r = r8 * ROW_COPY_UNROLL + u
            tok = idx_ref[first + r]
            pltpu.make_async_copy(src_hbm.at[pl.ds(pl.multiple_of(tok * ROW_CHUNKS, ROW_CHUNKS), ROW_CHUNKS)],
                                  dst.at[pl.ds(pl.multiple_of(r * ROW_CHUNKS, ROW_CHUNKS), ROW_CHUNKS)],
                                  sem).start(priority=u % 2)
        return carry
    lax.fori_loop(0, count // ROW_COPY_UNROLL, body, 0)


def _row_gather_wait(src_hbm, dst, sem):
    pltpu.make_async_copy(src_hbm.at[pl.ds(0, dst.shape[0])], dst, sem).wait()


def _dispatch_kernel(dest_ref, tail_ref, n_tiles_ref, x_ref, xs_hbm, zeros, stage, sem, zsem, *, tm):
    i = pl.program_id(0)
    n_steps = pl.num_programs(0)
    slot = i % 2
    tile_rows = EXPERT_TILE * ROW_CHUNKS
    all_tiles = xs_hbm.shape[0] // tile_rows

    def zero_tile(first_row):
        start = pl.multiple_of(first_row * ROW_CHUNKS, ROW_CHUNKS)
        return pltpu.make_async_copy(zeros, xs_hbm.at[pl.ds(start, tile_rows)], zsem)

    @pl.when(i == 0)
    def _():
        zeros[...] = jnp.zeros_like(zeros)
        for c in range(N_CLASSES):
            zero_tile(tail_ref[c]).start()
        for c in range(N_CLASSES):
            zero_tile(tail_ref[c]).wait()

        def unused(j, carry):
            cp = zero_tile(j * EXPERT_TILE)
            cp.start()
            cp.wait()
            return carry
        lax.fori_loop(n_tiles_ref[0], all_tiles, unused, 0)

    stage[slot] = x_ref[...]

    def body(r8, carry):
        for u in range(ROW_COPY_UNROLL):
            r = r8 * ROW_COPY_UNROLL + u
            d = dest_ref[i * tm + r]
            pltpu.make_async_copy(stage.at[slot, pl.ds(pl.multiple_of(r * ROW_CHUNKS, ROW_CHUNKS), ROW_CHUNKS)],
                                  xs_hbm.at[pl.ds(pl.multiple_of(d * ROW_CHUNKS, ROW_CHUNKS), ROW_CHUNKS)],
                                  sem.at[slot]).start(priority=u % 2)
        return carry
    lax.fori_loop(0, tm // ROW_COPY_UNROLL, body, 0)

    def wait_step(s):
        pltpu.make_async_copy(stage.at[s], xs_hbm.at[pl.ds(0, tm * ROW_CHUNKS)], sem.at[s]).wait()

    @pl.when(i > 0)
    def _():
        wait_step(1 - slot)

    @pl.when(i == n_steps - 1)
    def _():
        wait_step(slot)


def _dispatch(x1p, dest, tail_start, n_tiles, n_rows, tm=1024):
    n_tokens = x1p.shape[0] // ROW_CHUNKS
    grid_spec = pltpu.PrefetchScalarGridSpec(
        num_scalar_prefetch=3,
        grid=(n_tokens // tm,),
        in_specs=[pl.BlockSpec((tm * ROW_CHUNKS, LANES), lambda i, dst, tail, nt: (i, 0))],
        out_specs=pl.BlockSpec(memory_space=pl.ANY),
        scratch_shapes=[pltpu.VMEM((EXPERT_TILE * ROW_CHUNKS, LANES), x1p.dtype),
                        pltpu.VMEM((2, tm * ROW_CHUNKS, LANES), x1p.dtype),
                        pltpu.SemaphoreType.DMA((2,)),
                        pltpu.SemaphoreType.DMA(())],
    )
    return pl.pallas_call(
        functools.partial(_dispatch_kernel, tm=tm),
        grid_spec=grid_spec,
        out_shape=jax.ShapeDtypeStruct((n_rows * ROW_CHUNKS, LANES), x1p.dtype),
        compiler_params=_params(("arbitrary",)),
        name="moe_dispatch",
    )(dest, tail_start, n_tiles, x1p)


def _expert_kernel(run_ref, run_expert_ref, n_runs_ref, n_tiles_ref, x_ref,
                   wg_hbm, wu_hbm, wd_hbm, y_ref, stage_g, stage_u, stage_d,
                   wg1_bf, wu1_bf, wd1_bf, wg2_bf, wu2_bf, wd2_bf, sem, *, layer):
    i = pl.program_id(0)
    tm = EXPERT_TILE
    before = jnp.maximum(i - 1, 0)
    working = ((wg1_bf, wu1_bf, wd1_bf), (wg2_bf, wu2_bf, wd2_bf))

    def copies(role, run):
        slot = run % 2
        e = run_expert_ref[role, run]
        return [pltpu.make_async_copy(w.at[layer, e], st.at[role, slot], sem.at[role, slot])
                for w, st in ((wg_hbm, stage_g), (wu_hbm, stage_u), (wd_hbm, stage_d))]

    for role in range(2):
        run = run_ref[role, i]

        @pl.when(i == 0)
        def _():
            for cp in copies(role, 0):
                cp.start(priority=1)

        @pl.when((i == 0) | (run != run_ref[role, before]))
        def _():
            for cp in copies(role, run):
                cp.wait()
            slot = run % 2
            for st, wk in zip((stage_g, stage_u, stage_d), working[role]):
                wk[...] = st[role, slot].astype(BF16)

            @pl.when(run + 1 < n_runs_ref[role])
            def _():
                for cp in copies(role, run + 1):
                    cp.start(priority=1)

    @pl.when(i < n_tiles_ref[0])
    def _():
        halves = []
        for s in range(PACKED_SUBLANES):
            word = x_ref[pl.ds(s, tm, stride=ROW_CHUNKS), :]
            halves.append(pltpu.bitcast(word << 16, F32))
            halves.append(pltpu.bitcast(word & jnp.uint32(0xFFFF0000), F32))
        x = jnp.concatenate(halves, axis=-1).astype(BF16)
        gates = pltpu.bitcast(x_ref[pl.ds(PACKED_SUBLANES, tm, stride=ROW_CHUNKS), :], F32)

        def ffn(wg, wu, wd):
            hg = jnp.dot(x, wg[...], preferred_element_type=F32)
            hu = jnp.dot(x, wu[...], preferred_element_type=F32)
            hid = (hg * jax.nn.sigmoid(hg) * hu).astype(BF16)
            return jnp.dot(hid, wd[...], preferred_element_type=F32)

        y = gates[:, 0:1] * ffn(wg1_bf, wu1_bf, wd1_bf) + gates[:, 1:2] * ffn(wg2_bf, wu2_bf, wd2_bf)
        for j in range(ROW_CHUNKS):
            y_ref[pl.ds(j, tm, stride=ROW_CHUNKS), :] = y[:, j * LANES:(j + 1) * LANES]

    @pl.when(i >= n_tiles_ref[0])
    def _():
        y_ref[...] = jnp.zeros_like(y_ref)


def _experts(xs, w_gate, w_up, w_down, layer, tile_run, run_expert, n_runs, n_tiles):
    tm = EXPERT_TILE
    n_rows = xs.shape[0] // ROW_CHUNKS
    d, de = w_gate.shape[2], w_gate.shape[3]
    row_map = lambda i, run, rexp, nr, nt: (jnp.minimum(i, jnp.maximum(nt[0] - 1, 0)), 0)
    hbm = pl.BlockSpec(memory_space=pl.ANY)
    grid_spec = pltpu.PrefetchScalarGridSpec(
        num_scalar_prefetch=4,
        grid=(n_rows // tm,),
        in_specs=[pl.BlockSpec((tm * ROW_CHUNKS, LANES), row_map), hbm, hbm, hbm],
        out_specs=pl.BlockSpec((tm * ROW_CHUNKS, LANES), lambda i, run, rexp, nr, nt: (i, 0)),
        scratch_shapes=[pltpu.VMEM((2, 2, d, de), F32), pltpu.VMEM((2, 2, d, de), F32),
                        pltpu.VMEM((2, 2, de, d), F32),
                        pltpu.VMEM((d, de), BF16), pltpu.VMEM((d, de), BF16), pltpu.VMEM((de, d), BF16),
                        pltpu.VMEM((d, de), BF16), pltpu.VMEM((d, de), BF16), pltpu.VMEM((de, d), BF16),
                        pltpu.SemaphoreType.DMA((2, 2))],
    )
    return pl.pallas_call(
        functools.partial(_expert_kernel, layer=layer),
        grid_spec=grid_spec,
        out_shape=jax.ShapeDtypeStruct((n_rows * ROW_CHUNKS, LANES), F32),
        compiler_params=_params(("arbitrary",)),
        name="moe_experts",
    )(tile_run, run_expert, n_runs, n_tiles, xs, w_gate, w_up, w_down)


def _combine_kernel(dest_ref, y_hbm, x_ref, g_ref, b_ref, o_ref, ybuf, sem):
    i = pl.program_id(0)
    n_steps = pl.num_programs(0)
    slot = i % 2
    tm = COMBINE_TILE

    @pl.when(i == 0)
    def _():
        _row_gather_start(y_hbm, ybuf.at[0], sem.at[0], dest_ref, 0, tm)

    @pl.when(i + 1 < n_steps)
    def _():
        _row_gather_start(y_hbm, ybuf.at[1 - slot], sem.at[1 - slot], dest_ref, (i + 1) * tm, tm)

    _row_gather_wait(y_hbm, ybuf.at[slot], sem.at[slot])
    f = jnp.concatenate([ybuf[slot, pl.ds(j, tm, stride=ROW_CHUNKS), :] for j in range(ROW_CHUNKS)], axis=-1)
    o_ref[...] = _layer_norm(ALPHA * x_ref[...] + f, g_ref[...], b_ref[...])


def _combine(y_rows, x1, dest, g, b):
    n, d = x1.shape
    tm = COMBINE_TILE
    grid_spec = pltpu.PrefetchScalarGridSpec(
        num_scalar_prefetch=1,
        grid=(n // tm,),
        in_specs=[pl.BlockSpec(memory_space=pl.ANY),
                  pl.BlockSpec((tm, d), lambda i, dst: (i, 0)),
                  pl.BlockSpec((1, d), lambda i, dst: (0, 0)),
                  pl.BlockSpec((1, d), lambda i, dst: (0, 0))],
        out_specs=pl.BlockSpec((tm, d), lambda i, dst: (i, 0)),
        scratch_shapes=[pltpu.VMEM((2, tm * ROW_CHUNKS, LANES), F32),
                        pltpu.SemaphoreType.DMA((2,))],
    )
    return pl.pallas_call(
        _combine_kernel,
        grid_spec=grid_spec,
        out_shape=jax.ShapeDtypeStruct((n, d), F32),
        compiler_params=_params(("arbitrary",)),
        name="moe_combine",
    )(dest, y_rows, x1, g, b)


def _dispatch_plan(route, n_tokens):
    tm = EXPERT_TILE
    blk = LANES
    n_rows = n_tokens + N_CLASSES * tm
    hi = lax.Precision.HIGHEST
    onehot = (route[0][:, None] == jnp.arange(N_CLASSES, dtype=F32)[None, :]).astype(F32)
    blocks = onehot.reshape(n_tokens // blk, blk, N_CLASSES)
    tri = np.tril(np.ones((blk, blk), np.float32))
    local = jnp.einsum("ij,bjk->bik", tri, blocks, precision=hi)
    totals = local[:, -1, :]
    strict = np.tril(np.ones((n_tokens // blk,) * 2, np.float32), -1)
    before = jnp.dot(strict, totals, precision=hi)
    rank = jnp.sum((local + before[:, None, :]) * blocks, axis=-1).reshape(n_tokens) - 1.0
    counts = before[-1] + totals[-1]
    padded = jnp.ceil(counts / tm) * tm
    seg_end = jnp.dot(np.tril(np.ones((N_CLASSES,) * 2, np.float32)), padded, precision=hi)
    seg_start = seg_end - padded
    dest = (jnp.sum(onehot * seg_start[None, :], axis=-1) + rank).astype(jnp.int32)
    tile_start = jnp.arange(n_rows // tm, dtype=F32) * tm
    tile_class = jnp.sum((seg_end[None, :] <= tile_start[:, None]).astype(jnp.int32), axis=-1)
    last_class = jnp.max(jnp.where(padded > 0, jnp.arange(N_CLASSES), 0))
    tile_class = jnp.minimum(tile_class, last_class)
    tile_hot = (tile_class[:, None] == np.arange(N_CLASSES)[None, :]).astype(jnp.int32)
    n_pairs = len(PAIR_ORDER)
    expert_of = np.array([[(c // n_pairs) * GROUP_EXPERTS + PAIR_ORDER[c % n_pairs][role]
                           for c in range(N_CLASSES)] for role in range(2)])
    tile_expert = jnp.sum(tile_hot[None, :, :] * expert_of[:, None, :], axis=-1)
    n_all = n_rows // tm
    starts = jnp.concatenate([jnp.ones((2, 1), jnp.int32),
                              (tile_expert[:, 1:] != tile_expert[:, :-1]).astype(jnp.int32)], axis=1)
    tile_run = jnp.cumsum(starts, axis=1) - 1
    run_hot = (tile_run[:, :, None] == np.arange(n_all)[None, None, :]).astype(jnp.int32)
    run_expert = jnp.sum(run_hot * (starts * tile_expert)[:, :, None], axis=1)
    n_runs = tile_run[:, -1] + 1
    n_tiles = (seg_end[-1] / tm).astype(jnp.int32).reshape(1)
    tail_start = jnp.maximum(seg_end - tm, 0.0).astype(jnp.int32)
    plan = (tile_run.astype(jnp.int32), run_expert.astype(jnp.int32), n_runs.astype(jnp.int32), n_tiles)
    return plan, tail_start, dest, n_rows


def _moe(x1, x1p, route, w_gate, w_up, w_down, layer, g, b):
    n = x1.shape[0]
    plan, tail_start, dest, n_rows = _dispatch_plan(route, n)
    xs = _dispatch(x1p, dest, tail_start, plan[-1], n_rows)
    y_rows = _experts(xs, w_gate, w_up, w_down, layer, *plan)
    return _combine(y_rows, x1, dest, g, b)


def _vec(v):
    return v.reshape(1, -1).astype(F32)


def kernel(x, w_in_attn, w_out_attn, rel_bias, rpb_2d, w_pw1, b_pw1, w_dw, b_dw, conv_ln_g, conv_ln_b, w_pw2, b_pw2, ln_mix_g, ln_mix_b, ln_ffn_g, ln_ffn_b, w_router, router_bias, w_gate, w_up, w_down):
    bsz, seq, d = x.shape
    n = bsz * seq
    h = x.reshape(n, d)
    wr_t = w_router.T.astype(F32)
    wr_hi = wr_t.astype(BF16)
    wr_lo = (wr_t - wr_hi.astype(F32)).astype(BF16)
    wr = jnp.concatenate([wr_hi, wr_lo], axis=0)
    rb = router_bias.reshape(N_EXPERTS, 1).astype(F32)
    n_dil_cols = len(DILATIONS) * GROUP_COLS
    q_cols = np.zeros((3 * d,), np.float32) + 1.0
    q_cols[0:n_dil_cols] = HEAD_DIM ** -0.5
    q_cols[3 * n_dil_cols:3 * n_dil_cols + GROUP_COLS] = HEAD_DIM ** -0.5

    for layer in range(DEPTH):
        i = layer // 2
        g_mix, b_mix = _vec(ln_mix_g[layer]), _vec(ln_mix_b[layer])
        if layer % 2 == 0:
            w_in = (w_in_attn[i] * q_cols[None, :]).astype(BF16)
            proj = _matmul(h, w_in)
            lhs = []
            for grp in range(len(DILATIONS)):
                lhs += _dilated_attention(proj, _dilated_bias(rel_bias, grp), grp, bsz, seq)
            lhs.append(_neighborhood_attention(proj, _na_bias(rpb_2d[i]), bsz, seq))
            x1, x1r, route = _post_mixer(_post_attn_kernel, "post_attn", lhs,
                                         [w_out_attn[i].astype(BF16)], h, g_mix, b_mix, wr, rb)
        else:
            u = _pw1_glu(h, w_pw1[i].astype(BF16), _vec(b_pw1[i]))
            u = _dwconv_ln_silu(u.reshape(bsz, seq, d), w_dw[i].astype(F32), _vec(b_dw[i]),
                                _vec(conv_ln_g[i]), _vec(conv_ln_b[i]))
            x1, x1r, route = _post_mixer(_post_conv_kernel, "post_conv", [u.reshape(n, d)],
                                         [w_pw2[i].astype(BF16), _vec(b_pw2[i])], h, g_mix, b_mix, wr, rb)
        h = _moe(x1, x1r, route, w_gate, w_up, w_down, layer,
                 _vec(ln_ffn_g[layer]), _vec(ln_ffn_b[layer]))
    return h.reshape(bsz, seq, d)
```

```python
import functools
import math

import jax
import jax.numpy as jnp
import numpy as np
from jax import lax
from jax.experimental import pallas as pl
from jax.experimental.pallas import tpu as pltpu

D_MODEL = 1024
HEAD_DIM = 64
GROUP_HEADS = 4
GROUP_COLS = GROUP_HEADS * HEAD_DIM
DILATIONS = (1, 4, 16)
QUERY_BLOCK = 64
GRID_W = 64
NA_ROWS = 8
NA_COLS = 16
T5_BUCKETS = 32
T5_MAX_DIST = 1024
CONV_WIDTH = 31
CONV_HALO = 16
N_EXPERTS = 16
GROUP_EXPERTS = 4
D_EXPERT = 512
DEPTH = 4
ALPHA = (2 * DEPTH) ** 0.25
LN_EPS = 1e-5
NEG = -1e30

LANES = 128
ROW_CHUNKS = D_MODEL // LANES
GROUP_CHUNKS = GROUP_COLS // LANES
PAIR_ORDER = ((0, 1), (0, 2), (0, 3), (1, 3), (1, 2), (3, 2))
N_CLASSES = (N_EXPERTS // GROUP_EXPERTS) * len(PAIR_ORDER)
PACKED_SUBLANES = D_MODEL // (2 * LANES)
SEQ_TILE = 1024
EXPERT_TILE = 256
COMBINE_TILE = 512
ROW_COPY_UNROLL = 8
VMEM_LIMIT = 56 * 1024 * 1024

F32 = jnp.float32
BF16 = jnp.bfloat16


def _params(semantics, vmem=VMEM_LIMIT):
    return pltpu.CompilerParams(dimension_semantics=semantics, vmem_limit_bytes=vmem)


MATMUL_SLAB = 512


def _matmul_kernel(x_ref, w_ref, o_ref):
    xb = x_ref[...].astype(BF16)
    per_slab = MATMUL_SLAB // LANES
    for s in range(w_ref.shape[1] // MATMUL_SLAB):
        y = jnp.dot(xb, w_ref[:, s * MATMUL_SLAB:(s + 1) * MATMUL_SLAB], preferred_element_type=F32)
        for c in range(per_slab):
            o_ref[s * per_slab + c] = y[:, c * LANES:(c + 1) * LANES]


def _matmul(x, w, tm=512):
    n, k = x.shape
    _, m = w.shape
    return pl.pallas_call(
        _matmul_kernel,
        grid=(n // tm,),
        in_specs=[pl.BlockSpec((tm, k), lambda i: (i, 0)),
                  pl.BlockSpec((k, m), lambda i: (0, 0))],
        out_specs=pl.BlockSpec((m // LANES, tm, LANES), lambda i: (0, i, 0)),
        out_shape=jax.ShapeDtypeStruct((m // LANES, n, LANES), F32),
        compiler_params=_params(("parallel",)),
        name="qkv_proj",
    )(x, w)


def _glu_kernel(x_ref, w_ref, b_ref, o_ref):
    xb = x_ref[...].astype(BF16)
    half = w_ref.shape[1] // 2
    for s in range(half // MATMUL_SLAB):
        cols = slice(s * MATMUL_SLAB, (s + 1) * MATMUL_SLAB)
        gate_cols = slice(half + s * MATMUL_SLAB, half + (s + 1) * MATMUL_SLAB)
        a = jnp.dot(xb, w_ref[:, cols], preferred_element_type=F32) + b_ref[:, cols]
        g = jnp.dot(xb, w_ref[:, gate_cols], preferred_element_type=F32) + b_ref[:, gate_cols]
        o_ref[:, cols] = a * jax.nn.sigmoid(g)


def _pw1_glu(x, w, b, tm=512):
    n, k = x.shape
    m = w.shape[1]
    return pl.pallas_call(
        _glu_kernel,
        grid=(n // tm,),
        in_specs=[pl.BlockSpec((tm, k), lambda i: (i, 0)),
                  pl.BlockSpec((k, m), lambda i: (0, 0)),
                  pl.BlockSpec((1, m), lambda i: (0, 0))],
        out_specs=pl.BlockSpec((tm, m // 2), lambda i: (i, 0)),
        out_shape=jax.ShapeDtypeStruct((n, m // 2), F32),
        compiler_params=_params(("parallel",)),
        name="pw1_glu",
    )(x, w, b)


def _head_masks(rows):
    lane = lax.broadcasted_iota(jnp.int32, (rows, GROUP_COLS), 1)
    return [(lane >= HEAD_DIM * h) & (lane < HEAD_DIM * (h + 1)) for h in range(GROUP_HEADS)]


def _stack_heads(q, masks):
    return jnp.concatenate([jnp.where(m, q, 0.0) for m in masks], axis=0)


def _unstack_heads(o4, masks, rows):
    out = jnp.where(masks[0], o4[0:rows], 0.0)
    for h in range(1, GROUP_HEADS):
        out = out + jnp.where(masks[h], o4[h * rows:(h + 1) * rows], 0.0)
    return out


def _softmax_pv(logits, v, with_lse):
    m = jnp.max(logits, axis=-1, keepdims=True)
    p = jnp.exp(logits - m)
    s = jnp.sum(p, axis=-1, keepdims=True)
    o4 = jnp.dot(p.astype(BF16), v, preferred_element_type=F32) * (1.0 / s)
    lse = (m + jnp.log(s)) if with_lse else None
    return o4, lse


def _dilated_kernel(q_ref, kp_ref, kc_ref, kn_ref, vp_ref, vc_ref, vn_ref, bias_ref,
                    o_ref, l_ref, *, dil):
    t = pl.program_id(1)
    chunk = QUERY_BLOCK * dil
    n_chunks = SEQ_TILE // chunk
    qb = QUERY_BLOCK
    masks = _head_masks(qb)
    col = lax.broadcasted_iota(jnp.int32, (GROUP_HEADS * qb, 3 * qb), 1)
    no_prev = jnp.where(col < qb, NEG, 0.0) * (t == 0).astype(F32)
    no_next = jnp.where(col >= 2 * qb, NEG, 0.0) * (t == pl.num_programs(1) - 1).astype(F32)
    bias = bias_ref[...]

    def sl(start):
        return pl.ds(start, qb) if dil == 1 else pl.ds(start, qb, stride=dil)

    def rows(ref, start):
        return jnp.concatenate([ref[c, sl(start), :] for c in range(GROUP_CHUNKS)], axis=-1)

    def window(p_ref, c_ref, n_ref, c, r):
        parts = []
        for cc in (c - 1, c, c + 1):
            if cc < 0:
                parts.append(rows(p_ref, r))
            elif cc >= n_chunks:
                parts.append(rows(n_ref, r))
            else:
                parts.append(rows(c_ref, cc * chunk + r))
        return jnp.concatenate(parts, axis=0).astype(BF16)

    for c in range(n_chunks):
        for r in range(dil):
            base = c * chunk + r
            q4 = _stack_heads(rows(q_ref, base), masks).astype(BF16)
            k = window(kp_ref, kc_ref, kn_ref, c, r)
            v = window(vp_ref, vc_ref, vn_ref, c, r)
            logits = lax.dot_general(q4, k, (((1,), (1,)), ((), ())),
                                     preferred_element_type=F32) + bias
            if c == 0:
                logits = logits + no_prev
            if c == n_chunks - 1:
                logits = logits + no_next
            o4, lse = _softmax_pv(logits, v, True)
            o = _unstack_heads(o4, masks, qb)
            lb = _unstack_heads(jnp.broadcast_to(lse, (GROUP_HEADS * qb, GROUP_COLS)), masks, qb)
            for ch in range(GROUP_CHUNKS):
                o_ref[ch, sl(base), :] = o[:, ch * LANES:(ch + 1) * LANES]
                l_ref[ch, sl(base), :] = lb[:, ch * LANES:(ch + 1) * LANES]


def _tile_spec(col_block, seq):
    tiles = seq // SEQ_TILE
    return pl.BlockSpec((GROUP_CHUNKS, SEQ_TILE, LANES), lambda b, t: (col_block, b * tiles + t, 0))


def _halo_specs(col_block, halo, seq):
    per_tile = SEQ_TILE // halo
    per_seq = seq // halo
    return [
        pl.BlockSpec((GROUP_CHUNKS, halo, LANES),
                     lambda b, t: (col_block, b * per_seq + jnp.maximum(t * per_tile - 1, 0), 0)),
        _tile_spec(col_block, seq),
        pl.BlockSpec((GROUP_CHUNKS, halo, LANES),
                     lambda b, t: (col_block, b * per_seq + jnp.minimum((t + 1) * per_tile, per_seq - 1), 0)),
    ]


def _dilated_attention(proj, bias, group, bsz, seq):
    dil = DILATIONS[group]
    n_groups = len(DILATIONS)
    q_col, k_col, v_col = group, n_groups + group, 2 * n_groups + group
    out_spec = _tile_spec(0, seq)
    out_sds = jax.ShapeDtypeStruct((GROUP_CHUNKS, bsz * seq, LANES), F32)
    return pl.pallas_call(
        functools.partial(_dilated_kernel, dil=dil),
        grid=(bsz, seq // SEQ_TILE),
        in_specs=[_tile_spec(q_col, seq)]
        + _halo_specs(k_col, QUERY_BLOCK * dil, seq) + _halo_specs(v_col, QUERY_BLOCK * dil, seq)
        + [pl.BlockSpec(bias.shape, lambda b, t: (0, 0))],
        out_specs=[out_spec, out_spec],
        out_shape=[out_sds, out_sds],
        compiler_params=_params(("parallel", "arbitrary")),
        name=f"dilated_attn_d{dil}",
    )(proj, proj, proj, proj, proj, proj, proj, bias)


NA_HALO = 256
NA_KEYS = NA_ROWS * GRID_W


def _na_kernel(q_ref, kp_ref, kc_ref, kn_ref, vp_ref, vc_ref, vn_ref, bias_ref, o_ref,
               kwin, vwin, *, grid_rows):
    t = pl.program_id(1)
    tile_rows = SEQ_TILE // GRID_W
    half = NA_ROWS // 2
    for win, p_ref, c_ref, n_ref in ((kwin, kp_ref, kc_ref, kn_ref), (vwin, vp_ref, vc_ref, vn_ref)):
        for ch in range(GROUP_CHUNKS):
            lanes = slice(ch * LANES, (ch + 1) * LANES)
            win[0:NA_HALO, lanes] = p_ref[ch].astype(BF16)
            win[NA_HALO:NA_HALO + SEQ_TILE, lanes] = c_ref[ch].astype(BF16)
            win[NA_HALO + SEQ_TILE:, lanes] = n_ref[ch].astype(BF16)
    masks = _head_masks(GRID_W)
    for i in range(tile_rows):
        row = t * tile_rows + i
        row0 = jnp.clip(row - half, 0, grid_rows - NA_ROWS)
        off = pl.multiple_of((row0 - (t * tile_rows - half)) * GRID_W, GRID_W)
        q = jnp.concatenate([q_ref[ch, i * GRID_W:(i + 1) * GRID_W, :] for ch in range(GROUP_CHUNKS)], axis=-1)
        q4 = _stack_heads(q, masks).astype(BF16)
        k = kwin[pl.ds(off, NA_KEYS), :]
        v = vwin[pl.ds(off, NA_KEYS), :]
        logits = lax.dot_general(q4, k, (((1,), (1,)), ((), ())),
                                 preferred_element_type=F32) + bias_ref[row - row0]
        o4, _ = _softmax_pv(logits, v, False)
        o = _unstack_heads(o4, masks, GRID_W)
        for ch in range(GROUP_CHUNKS):
            o_ref[ch, i * GRID_W:(i + 1) * GRID_W, :] = o[:, ch * LANES:(ch + 1) * LANES]


def _neighborhood_attention(proj, bias, bsz, seq):
    base = 3 * len(DILATIONS)
    win_rows = SEQ_TILE + 2 * NA_HALO
    return pl.pallas_call(
        functools.partial(_na_kernel, grid_rows=seq // GRID_W),
        grid=(bsz, seq // SEQ_TILE),
        in_specs=[_tile_spec(base, seq)]
        + _halo_specs(base + 1, NA_HALO, seq) + _halo_specs(base + 2, NA_HALO, seq)
        + [pl.BlockSpec(bias.shape, lambda b, t: (0, 0, 0))],
        out_specs=_tile_spec(0, seq),
        out_shape=jax.ShapeDtypeStruct((GROUP_CHUNKS, bsz * seq, LANES), F32),
        scratch_shapes=[pltpu.VMEM((win_rows, GROUP_COLS), BF16),
                        pltpu.VMEM((win_rows, GROUP_COLS), BF16)],
        compiler_params=_params(("parallel", "arbitrary")),
        name="neighborhood_attn",
    )(proj, proj, proj, proj, proj, proj, proj, bias)


def _t5_bucket_table(dil):
    qb = QUERY_BLOCK
    rel = (np.arange(3 * qb)[None, :] - qb - np.arange(qb)[:, None])
    nb = T5_BUCKETS // 2
    max_exact = nb // 2
    n = np.abs(rel * dil)
    nf = np.maximum(n, 1).astype(np.float32)
    large = max_exact + (np.log(nf / np.float32(max_exact)) / np.float32(math.log(T5_MAX_DIST / max_exact))
                         * np.float32(nb - max_exact)).astype(np.int32)
    large = np.minimum(large, nb - 1)
    bucket = np.where(rel * dil > 0, nb, 0) + np.where(n < max_exact, n, large)
    return bucket.astype(np.int32), np.abs(rel) <= qb


def _dilated_bias(rel_bias, group):
    bucket, band = _t5_bucket_table(DILATIONS[group])
    tab = rel_bias[:, group * GROUP_HEADS:(group + 1) * GROUP_HEADS].astype(F32)
    onehot = (bucket[None, :, :] == np.arange(T5_BUCKETS)[:, None, None]).astype(np.float32)
    bias = jnp.einsum("bh,bqk->hqk", tab, onehot, precision=lax.Precision.HIGHEST)
    bias = jnp.where(band[None], bias, NEG)
    return bias.reshape(GROUP_HEADS * QUERY_BLOCK, 3 * QUERY_BLOCK)


def _na_bias(rpb):
    qc = np.arange(GRID_W)[:, None]
    kc = np.arange(GRID_W)[None, :]
    win0 = np.clip(qc - NA_COLS // 2, 0, GRID_W - NA_COLS)
    col_ok = (kc >= win0) & (kc < win0 + NA_COLS)
    dc = np.clip(kc - qc + NA_COLS - 1, 0, 2 * NA_COLS - 2)
    shift = np.arange(NA_ROWS)[:, None]
    kr = np.arange(NA_ROWS)[None, :]
    dr = kr - shift + NA_ROWS - 1
    pick_row = (dr[:, :, None] == np.arange(2 * NA_ROWS - 1)[None, None, :]).astype(np.float32)
    pick_col = (dc[None, :, :] == np.arange(2 * NA_COLS - 1)[:, None, None]).astype(np.float32)
    hi = lax.Precision.HIGHEST
    rows = jnp.einsum("hac,dka->hdkc", rpb.astype(F32), pick_row, precision=hi)
    b = jnp.einsum("hdkc,cqx->dhqkx", rows, pick_col, precision=hi)
    b = jnp.where(col_ok[None, None, :, None, :], b, NEG)
    return b.reshape(NA_ROWS, GROUP_HEADS * GRID_W, NA_KEYS)


def _layer_norm(z, g, b):
    mu = jnp.mean(z, axis=-1, keepdims=True)
    zc = z - mu
    var = jnp.mean(zc * zc, axis=-1, keepdims=True)
    return zc * lax.rsqrt(var + LN_EPS) * g + b


def _row(x, i):
    return x[i:i + 1, :]


def _route(logits, rbias):
    scores = jax.nn.sigmoid(logits)
    sel = scores + rbias
    ge = GROUP_EXPERTS
    n_groups = N_EXPERTS // ge
    best_sum, best_g = None, None
    for g in range(n_groups):
        vals = [_row(sel, g * ge + j) for j in range(ge)]
        top2 = None
        for a in range(ge):
            for b in range(a + 1, ge):
                pair = vals[a] + vals[b]
                top2 = pair if top2 is None else jnp.maximum(top2, pair)
        if best_sum is None:
            best_sum, best_g = top2, jnp.zeros_like(top2, dtype=jnp.int32)
        else:
            upd = top2 > best_sum
            best_g = jnp.where(upd, g, best_g)
            best_sum = jnp.where(upd, top2, best_sum)

    def in_group(x, j):
        out = _row(x, j)
        for g in range(1, n_groups):
            out = jnp.where(best_g == g, _row(x, g * ge + j), out)
        return out

    cand = [in_group(sel, j) for j in range(ge)]
    cand_score = [in_group(scores, j) for j in range(ge)]
    v1, j1, w1 = cand[0], jnp.zeros_like(best_g), cand_score[0]
    for j in range(1, ge):
        upd = cand[j] > v1
        v1 = jnp.where(upd, cand[j], v1)
        j1 = jnp.where(upd, j, j1)
        w1 = jnp.where(upd, cand_score[j], w1)
    v2 = jnp.full_like(v1, -jnp.inf)
    j2, w2 = jnp.zeros_like(best_g), jnp.zeros_like(w1)
    for j in range(ge):
        upd = (cand[j] > v2) & (j1 != j)
        v2 = jnp.where(upd, cand[j], v2)
        j2 = jnp.where(upd, j, j2)
        w2 = jnp.where(upd, cand_score[j], w2)
    wsum = w1 + w2
    g1, g2 = w1 / wsum, w2 / wsum
    lo, hi = jnp.minimum(j1, j2), jnp.maximum(j1, j2)
    pair = jnp.zeros_like(best_g)
    first = jnp.zeros_like(best_g)
    for q, (a, b) in enumerate(PAIR_ORDER):
        hit = (lo == min(a, b)) & (hi == max(a, b))
        pair = jnp.where(hit, q, pair)
        first = jnp.where(hit, a, first)
    is_first = j1 == first
    g_first = jnp.where(is_first, g1, g2)
    g_second = jnp.where(is_first, g2, g1)
    cls = (best_g * len(PAIR_ORDER) + pair).astype(F32)
    zero = jnp.zeros_like(w1)
    return jnp.concatenate([cls, g_first, g_second, zero, zero, zero, zero, zero], axis=0)


def _to_column(row_vec, eye):
    return jnp.sum(jnp.where(eye, row_vec, 0.0), axis=-1, keepdims=True)


def _post_tail(m, x_ref, g_ref, b_ref, wr_ref, rb_ref, x1_ref, x1p_ref, route_ref):
    tm = m.shape[0]
    xn = _layer_norm(ALPHA * x_ref[...] + m, g_ref[...], b_ref[...])
    x1_ref[...] = xn
    xh = xn.astype(BF16)
    xhf = xh.astype(F32)
    xl = (xn - xhf).astype(BF16)
    wr = wr_ref[...]
    contract = (((1,), (1,)), ((), ()))
    a = lax.dot_general(wr, xh, contract, preferred_element_type=F32)
    c = lax.dot_general(wr[0:N_EXPERTS], xl, contract, preferred_element_type=F32)
    logits = a[0:N_EXPERTS] + a[N_EXPERTS:] + c
    route = _route(logits, rb_ref[...])
    route_ref[...] = route
    bits = pltpu.bitcast(xhf, jnp.uint32)
    for s in range(PACKED_SUBLANES):
        lo = bits[:, (2 * s) * LANES:(2 * s + 1) * LANES]
        hi = bits[:, (2 * s + 1) * LANES:(2 * s + 2) * LANES]
        x1p_ref[pl.ds(s, tm, stride=ROW_CHUNKS), :] = (lo >> 16) | (hi & jnp.uint32(0xFFFF0000))
    eye = (lax.broadcasted_iota(jnp.int32, (tm, tm), 0) == lax.broadcasted_iota(jnp.int32, (tm, tm), 1))
    g_first, g_second = _to_column(route[1:2, :], eye), _to_column(route[2:3, :], eye)
    lane = lax.broadcasted_iota(jnp.int32, (tm, LANES), 1)
    gates = jnp.where(lane == 0, g_first, jnp.where(lane == 1, g_second, 0.0))
    x1p_ref[pl.ds(PACKED_SUBLANES, tm, stride=ROW_CHUNKS), :] = pltpu.bitcast(gates, jnp.uint32)
    for s in range(PACKED_SUBLANES + 1, ROW_CHUNKS):
        x1p_ref[pl.ds(s, tm, stride=ROW_CHUNKS), :] = jnp.zeros((tm, LANES), jnp.uint32)


def _post_attn_kernel(o0_ref, l0_ref, o1_ref, l1_ref, o2_ref, l2_ref, ob_ref, w_ref,
                      x_ref, g_ref, b_ref, wr_ref, rb_ref, x1_ref, x1r_ref, route_ref):
    def wide(ref):
        return jnp.concatenate([ref[ch] for ch in range(GROUP_CHUNKS)], axis=-1)

    l0, l1, l2 = wide(l0_ref), wide(l1_ref), wide(l2_ref)
    mx = jnp.maximum(jnp.maximum(l0, l1), l2)
    e0, e1, e2 = jnp.exp(l0 - mx), jnp.exp(l1 - mx), jnp.exp(l2 - mx)
    inv = 1.0 / (e0 + e1 + e2)
    lhs = jnp.concatenate([wide(o0_ref) * (e0 * inv), wide(o1_ref) * (e1 * inv),
                           wide(o2_ref) * (e2 * inv), wide(ob_ref)], axis=-1).astype(BF16)
    m = jnp.dot(lhs, w_ref[...], preferred_element_type=F32)
    _post_tail(m, x_ref, g_ref, b_ref, wr_ref, rb_ref, x1_ref, x1r_ref, route_ref)


def _post_conv_kernel(u_ref, w_ref, wb_ref, x_ref, g_ref, b_ref, wr_ref, rb_ref,
                      x1_ref, x1r_ref, route_ref):
    m = jnp.dot(u_ref[...], w_ref[...], preferred_element_type=F32) + wb_ref[...]
    _post_tail(m, x_ref, g_ref, b_ref, wr_ref, rb_ref, x1_ref, x1r_ref, route_ref)


def _post_mixer(kernel, name, lhs_list, consts_front, x, g, b, wr, rb, tm=512):
    n, d = x.shape
    row = lambda i: (i, 0)
    const = lambda i: (0, 0)
    in_specs = [pl.BlockSpec((tm, a.shape[1]), row) if a.ndim == 2
                else pl.BlockSpec((a.shape[0], tm, LANES), lambda i: (0, i, 0)) for a in lhs_list]
    in_specs += [pl.BlockSpec(c.shape, const) for c in consts_front]
    in_specs += [pl.BlockSpec((tm, d), row)]
    in_specs += [pl.BlockSpec(c.shape, const) for c in (g, b, wr, rb)]
    return pl.pallas_call(
        kernel,
        grid=(n // tm,),
        in_specs=in_specs,
        out_specs=[pl.BlockSpec((tm, d), row),
                   pl.BlockSpec((tm * ROW_CHUNKS, LANES), row),
                   pl.BlockSpec((8, tm), lambda i: (0, i))],
        out_shape=[jax.ShapeDtypeStruct((n, d), F32),
                   jax.ShapeDtypeStruct((n * ROW_CHUNKS, LANES), jnp.uint32),
                   jax.ShapeDtypeStruct((8, n), F32)],
        compiler_params=_params(("parallel",)),
        name=name,
    )(*lhs_list, *consts_front, x, g, b, wr, rb)


CONV_SLAB = 256


def _conv_front_kernel(xp_ref, xc_ref, xn_ref, w_ref, wb_ref, taps_ref, cb_ref, g_ref, b_ref,
                       o_ref, win, acc, *, ts):
    t = pl.program_id(1)
    has_prev = (t > 0).astype(F32)
    has_next = (t < pl.num_programs(1) - 1).astype(F32)
    rows = ts + 2 * CONV_HALO
    row = lax.broadcasted_iota(jnp.int32, (rows, 1), 0)
    keep = jnp.where(row < CONV_HALO, has_prev, jnp.where(row >= CONV_HALO + ts, has_next, 1.0))
    xw = jnp.concatenate([xp_ref[...], xc_ref[...], xn_ref[...]], axis=0).astype(BF16)
    half = w_ref.shape[1] // 2
    chunks_per_slab = CONV_SLAB // LANES
    for s in range(half // CONV_SLAB):
        cols = slice(s * CONV_SLAB, (s + 1) * CONV_SLAB)
        gate_cols = slice(half + s * CONV_SLAB, half + (s + 1) * CONV_SLAB)
        a = jnp.dot(xw, w_ref[:, cols], preferred_element_type=F32) + wb_ref[:, cols]
        gt = jnp.dot(xw, w_ref[:, gate_cols], preferred_element_type=F32) + wb_ref[:, gate_cols]
        u = a * jax.nn.sigmoid(gt) * keep
        for c in range(chunks_per_slab):
            j = s * chunks_per_slab + c
            win[j] = u[:, c * LANES:(c + 1) * LANES]
            conv = jnp.zeros((ts, LANES), F32)
            for tap in range(CONV_WIDTH):
                start = CONV_HALO - CONV_WIDTH // 2 + tap
                conv = conv + win[j, pl.ds(start, ts), :] * taps_ref[j, tap:tap + 1, :]
            acc[j] = conv
    u = jnp.concatenate([acc[j] for j in range(ROW_CHUNKS)], axis=-1) + cb_ref[...]
    y = _layer_norm(u, g_ref[...], b_ref[...])
    o_ref[...] = (y * jax.nn.sigmoid(y)).astype(o_ref.dtype)


def _conv_front(x, w_pw1, b_pw1, w_dw, b_dw, g, b, ts=256):
    bsz, seq, d = x.shape
    per_tile = ts // CONV_HALO
    last = seq // CONV_HALO - 1
    taps = jnp.pad(w_dw, ((0, 32 - CONV_WIDTH), (0, 0))).reshape(32, ROW_CHUNKS, LANES).transpose(1, 0, 2)
    vec = pl.BlockSpec((1, d), lambda bb, t: (0, 0))
    return pl.pallas_call(
        functools.partial(_conv_front_kernel, ts=ts),
        grid=(bsz, seq // ts),
        in_specs=[pl.BlockSpec((None, CONV_HALO, d), lambda bb, t: (bb, jnp.maximum(t * per_tile - 1, 0), 0)),
                  pl.BlockSpec((None, ts, d), lambda bb, t: (bb, t, 0)),
                  pl.BlockSpec((None, CONV_HALO, d), lambda bb, t: (bb, jnp.minimum((t + 1) * per_tile, last), 0)),
                  pl.BlockSpec(w_pw1.shape, lambda bb, t: (0, 0)),
                  pl.BlockSpec(b_pw1.shape, lambda bb, t: (0, 0)),
                  pl.BlockSpec(taps.shape, lambda bb, t: (0, 0, 0)),
                  vec, vec, vec],
        out_specs=pl.BlockSpec((None, ts, d), lambda bb, t: (bb, t, 0)),
        out_shape=jax.ShapeDtypeStruct((bsz, seq, d), BF16),
        scratch_shapes=[pltpu.VMEM((ROW_CHUNKS, ts + 2 * CONV_HALO, LANES), F32),
                        pltpu.VMEM((ROW_CHUNKS, ts, LANES), F32)],
        compiler_params=_params(("parallel", "arbitrary")),
        name="conv_front",
    )(x, x, x, w_pw1, b_pw1, taps, b_dw, g, b)


def _row_gather_start(src_hbm, dst, sem, idx_ref, first, count):
    def body(r8, carry):
        for u in range(ROW_COPY_UNROLL):
            r = r8 * ROW_COPY_UNROLL + u
            tok = idx_ref[first + r]
            pltpu.make_async_copy(src_hbm.at[pl.ds(pl.multiple_of(tok * ROW_CHUNKS, ROW_CHUNKS), ROW_CHUNKS)],
                                  dst.at[pl.ds(pl.multiple_of(r * ROW_CHUNKS, ROW_CHUNKS), ROW_CHUNKS)],
                                  sem).start(priority=u % 2)
        return carry
    lax.fori_loop(0, count // ROW_COPY_UNROLL, body, 0)


def _row_gather_wait(src_hbm, dst, sem):
    pltpu.make_async_copy(src_hbm.at[pl.ds(0, dst.shape[0])], dst, sem).wait()


def _dispatch_kernel(dest_ref, tail_ref, n_tiles_ref, x_ref, xs_hbm, zeros, stage, sem, zsem, *, tm):
    i = pl.program_id(0)
    n_steps = pl.num_programs(0)
    slot = i % 2
    tile_rows = EXPERT_TILE * ROW_CHUNKS
    all_tiles = xs_hbm.shape[0] // tile_rows

    def zero_tile(first_row):
        start = pl.multiple_of(first_row * ROW_CHUNKS, ROW_CHUNKS)
        return pltpu.make_async_copy(zeros, xs_hbm.at[pl.ds(start, tile_rows)], zsem)

    @pl.when(i == 0)
    def _():
        zeros[...] = jnp.zeros_like(zeros)
        for c in range(N_CLASSES):
            zero_tile(tail_ref[c]).start()
        for c in range(N_CLASSES):
            zero_tile(tail_ref[c]).wait()

        def unused(j, carry):
            cp = zero_tile(j * EXPERT_TILE)
            cp.start()
            cp.wait()
            return carry
        lax.fori_loop(n_tiles_ref[0], all_tiles, unused, 0)

    stage[slot] = x_ref[...]

    def body(r8, carry):
        for u in range(ROW_COPY_UNROLL):
            r = r8 * ROW_COPY_UNROLL + u
            d = dest_ref[i * tm + r]
            pltpu.make_async_copy(stage.at[slot, pl.ds(pl.multiple_of(r * ROW_CHUNKS, ROW_CHUNKS), ROW_CHUNKS)],
                                  xs_hbm.at[pl.ds(pl.multiple_of(d * ROW_CHUNKS, ROW_CHUNKS), ROW_CHUNKS)],
                                  sem.at[slot]).start(priority=u % 2)
        return carry
    lax.fori_loop(0, tm // ROW_COPY_UNROLL, body, 0)

    def wait_step(s):
        pltpu.make_async_copy(stage.at[s], xs_hbm.at[pl.ds(0, tm * ROW_CHUNKS)], sem.at[s]).wait()

    @pl.when(i > 0)
    def _():
        wait_step(1 - slot)

    @pl.when(i == n_steps - 1)
    def _():
        wait_step(slot)


def _dispatch(x1p, dest, tail_start, n_tiles, n_rows, tm=1024):
    n_tokens = x1p.shape[0] // ROW_CHUNKS
    grid_spec = pltpu.PrefetchScalarGridSpec(
        num_scalar_prefetch=3,
        grid=(n_tokens // tm,),
        in_specs=[pl.BlockSpec((tm * ROW_CHUNKS, LANES), lambda i, dst, tail, nt: (i, 0))],
        out_specs=pl.BlockSpec(memory_space=pl.ANY),
        scratch_shapes=[pltpu.VMEM((EXPERT_TILE * ROW_CHUNKS, LANES), x1p.dtype),
                        pltpu.VMEM((2, tm * ROW_CHUNKS, LANES), x1p.dtype),
                        pltpu.SemaphoreType.DMA((2,)),
                        pltpu.SemaphoreType.DMA(())],
    )
    return pl.pallas_call(
        functools.partial(_dispatch_kernel, tm=tm),
        grid_spec=grid_spec,
        out_shape=jax.ShapeDtypeStruct((n_rows * ROW_CHUNKS, LANES), x1p.dtype),
        compiler_params=_params(("arbitrary",)),
        name="moe_dispatch",
    )(dest, tail_start, n_tiles, x1p)


def _expert_kernel(run_ref, run_expert_ref, n_runs_ref, n_tiles_ref, x_ref,
                   wg_hbm, wu_hbm, wd_hbm, y_ref, stage_g, stage_u, stage_d,
                   wg1_bf, wu1_bf, wd1_bf, wg2_bf, wu2_bf, wd2_bf, sem, *, layer):
    i = pl.program_id(0)
    tm = EXPERT_TILE
    before = jnp.maximum(i - 1, 0)
    working = ((wg1_bf, wu1_bf, wd1_bf), (wg2_bf, wu2_bf, wd2_bf))

    def copies(role, run):
        slot = run % 2
        e = run_expert_ref[role, run]
        return [pltpu.make_async_copy(w.at[layer, e], st.at[role, slot], sem.at[role, slot])
                for w, st in ((wg_hbm, stage_g), (wu_hbm, stage_u), (wd_hbm, stage_d))]

    for role in range(2):
        run = run_ref[role, i]

        @pl.when(i == 0)
        def _():
            for cp in copies(role, 0):
                cp.start(priority=1)

        @pl.when((i == 0) | (run != run_ref[role, before]))
        def _():
            for cp in copies(role, run):
                cp.wait()
            slot = run % 2
            for st, wk in zip((stage_g, stage_u, stage_d), working[role]):
                wk[...] = st[role, slot].astype(BF16)

            @pl.when(run + 1 < n_runs_ref[role])
            def _():
                for cp in copies(role, run + 1):
                    cp.start(priority=1)

    @pl.when(i < n_tiles_ref[0])
    def _():
        halves = []
        for s in range(PACKED_SUBLANES):
            word = x_ref[pl.ds(s, tm, stride=ROW_CHUNKS), :]
            halves.append(pltpu.bitcast(word << 16, F32))
            halves.append(pltpu.bitcast(word & jnp.uint32(0xFFFF0000), F32))
        x = jnp.concatenate(halves, axis=-1).astype(BF16)
        gates = pltpu.bitcast(x_ref[pl.ds(PACKED_SUBLANES, tm, stride=ROW_CHUNKS), :], F32)

        def ffn(wg, wu, wd):
            hg = jnp.dot(x, wg[...], preferred_element_type=F32)
            hu = jnp.dot(x, wu[...], preferred_element_type=F32)
            hid = (hg * jax.nn.sigmoid(hg) * hu).astype(BF16)
            return jnp.dot(hid, wd[...], preferred_element_type=F32)

        y = gates[:, 0:1] * ffn(wg1_bf, wu1_bf, wd1_bf) + gates[:, 1:2] * ffn(wg2_bf, wu2_bf, wd2_bf)
        for j in range(ROW_CHUNKS):
            y_ref[pl.ds(j, tm, stride=ROW_CHUNKS), :] = y[:, j * LANES:(j + 1) * LANES]

    @pl.when(i >= n_tiles_ref[0])
    def _():
        y_ref[...] = jnp.zeros_like(y_ref)


def _experts(xs, w_gate, w_up, w_down, layer, tile_run, run_expert, n_runs, n_tiles):
    tm = EXPERT_TILE
    n_rows = xs.shape[0] // ROW_CHUNKS
    d, de = w_gate.shape[2], w_gate.shape[3]
    row_map = lambda i, run, rexp, nr, nt: (jnp.minimum(i, jnp.maximum(nt[0] - 1, 0)), 0)
    hbm = pl.BlockSpec(memory_space=pl.ANY)
    grid_spec = pltpu.PrefetchScalarGridSpec(
        num_scalar_prefetch=4,
        grid=(n_rows // tm,),
        in_specs=[pl.BlockSpec((tm * ROW_CHUNKS, LANES), row_map), hbm, hbm, hbm],
        out_specs=pl.BlockSpec((tm * ROW_CHUNKS, LANES), lambda i, run, rexp, nr, nt: (i, 0)),
        scratch_shapes=[pltpu.VMEM((2, 2, d, de), F32), pltpu.VMEM((2, 2, d, de), F32),
                        pltpu.VMEM((2, 2, de, d), F32),
                        pltpu.VMEM((d, de), BF16), pltpu.VMEM((d, de), BF16), pltpu.VMEM((de, d), BF16),
                        pltpu.VMEM((d, de), BF16), pltpu.VMEM((d, de), BF16), pltpu.VMEM((de, d), BF16),
                        pltpu.SemaphoreType.DMA((2, 2))],
    )
    return pl.pallas_call(
        functools.partial(_expert_kernel, layer=layer),
        grid_spec=grid_spec,
        out_shape=jax.ShapeDtypeStruct((n_rows * ROW_CHUNKS, LANES), F32),
        compiler_params=_params(("arbitrary",)),
        name="moe_experts",
    )(tile_run, run_expert, n_runs, n_tiles, xs, w_gate, w_up, w_down)


def _combine_kernel(dest_ref, y_hbm, x_ref, g_ref, b_ref, o_ref, ybuf, sem):
    i = pl.program_id(0)
    n_steps = pl.num_programs(0)
    slot = i % 2
    tm = COMBINE_TILE

    @pl.when(i == 0)
    def _():
        _row_gather_start(y_hbm, ybuf.at[0], sem.at[0], dest_ref, 0, tm)

    @pl.when(i + 1 < n_steps)
    def _():
        _row_gather_start(y_hbm, ybuf.at[1 - slot], sem.at[1 - slot], dest_ref, (i + 1) * tm, tm)

    _row_gather_wait(y_hbm, ybuf.at[slot], sem.at[slot])
    f = jnp.concatenate([ybuf[slot, pl.ds(j, tm, stride=ROW_CHUNKS), :] for j in range(ROW_CHUNKS)], axis=-1)
    o_ref[...] = _layer_norm(ALPHA * x_ref[...] + f, g_ref[...], b_ref[...])


def _combine(y_rows, x1, dest, g, b):
    n, d = x1.shape
    tm = COMBINE_TILE
    grid_spec = pltpu.PrefetchScalarGridSpec(
        num_scalar_prefetch=1,
        grid=(n // tm,),
        in_specs=[pl.BlockSpec(memory_space=pl.ANY),
                  pl.BlockSpec((tm, d), lambda i, dst: (i, 0)),
                  pl.BlockSpec((1, d), lambda i, dst: (0, 0)),
                  pl.BlockSpec((1, d), lambda i, dst: (0, 0))],
        out_specs=pl.BlockSpec((tm, d), lambda i, dst: (i, 0)),
        scratch_shapes=[pltpu.VMEM((2, tm * ROW_CHUNKS, LANES), F32),
                        pltpu.SemaphoreType.DMA((2,))],
    )
    return pl.pallas_call(
        _combine_kernel,
        grid_spec=grid_spec,
        out_shape=jax.ShapeDtypeStruct((n, d), F32),
        compiler_params=_params(("arbitrary",)),
        name="moe_combine",
    )(dest, y_rows, x1, g, b)


def _dispatch_plan(route, n_tokens):
    tm = EXPERT_TILE
    blk = LANES
    n_rows = n_tokens + N_CLASSES * tm
    hi = lax.Precision.HIGHEST
    onehot = (route[0][:, None] == jnp.arange(N_CLASSES, dtype=F32)[None, :]).astype(F32)
    blocks = onehot.reshape(n_tokens // blk, blk, N_CLASSES)
    tri = np.tril(np.ones((blk, blk), np.float32))
    local = jnp.einsum("ij,bjk->bik", tri, blocks, precision=hi)
    totals = local[:, -1, :]
    strict = np.tril(np.ones((n_tokens // blk,) * 2, np.float32), -1)
    before = jnp.dot(strict, totals, precision=hi)
    rank = jnp.sum((local + before[:, None, :]) * blocks, axis=-1).reshape(n_tokens) - 1.0
    counts = before[-1] + totals[-1]
    padded = jnp.ceil(counts / tm) * tm
    seg_end = jnp.dot(np.tril(np.ones((N_CLASSES,) * 2, np.float32)), padded, precision=hi)
    seg_start = seg_end - padded
    dest = (jnp.sum(onehot * seg_start[None, :], axis=-1) + rank).astype(jnp.int32)
    tile_start = jnp.arange(n_rows // tm, dtype=F32) * tm
    tile_class = jnp.sum((seg_end[None, :] <= tile_start[:, None]).astype(jnp.int32), axis=-1)
    last_class = jnp.max(jnp.where(padded > 0, jnp.arange(N_CLASSES), 0))
    tile_class = jnp.minimum(tile_class, last_class)
    tile_hot = (tile_class[:, None] == np.arange(N_CLASSES)[None, :]).astype(jnp.int32)
    n_pairs = len(PAIR_ORDER)
    expert_of = np.array([[(c // n_pairs) * GROUP_EXPERTS + PAIR_ORDER[c % n_pairs][role]
                           for c in range(N_CLASSES)] for role in range(2)])
    tile_expert = jnp.sum(tile_hot[None, :, :] * expert_of[:, None, :], axis=-1)
    n_all = n_rows // tm
    starts = jnp.concatenate([jnp.ones((2, 1), jnp.int32),
                              (tile_expert[:, 1:] != tile_expert[:, :-1]).astype(jnp.int32)], axis=1)
    tile_run = jnp.cumsum(starts, axis=1) - 1
    run_hot = (tile_run[:, :, None] == np.arange(n_all)[None, None, :]).astype(jnp.int32)
    run_expert = jnp.sum(run_hot * (starts * tile_expert)[:, :, None], axis=1)
    n_runs = tile_run[:, -1] + 1
    n_tiles = (seg_end[-1] / tm).astype(jnp.int32).reshape(1)
    tail_start = jnp.maximum(seg_end - tm, 0.0).astype(jnp.int32)
    plan = (tile_run.astype(jnp.int32), run_expert.astype(jnp.int32), n_runs.astype(jnp.int32), n_tiles)
    return plan, tail_start, dest, n_rows


def _moe(x1, x1p, route, w_gate, w_up, w_down, layer, g, b):
    n = x1.shape[0]
    plan, tail_start, dest, n_rows = _dispatch_plan(route, n)
    xs = _dispatch(x1p, dest, tail_start, plan[-1], n_rows)
    y_rows = _experts(xs, w_gate, w_up, w_down, layer, *plan)
    return _combine(y_rows, x1, dest, g, b)


def _vec(v):
    return v.reshape(1, -1).astype(F32)


def kernel(x, w_in_attn, w_out_attn, rel_bias, rpb_2d, w_pw1, b_pw1, w_dw, b_dw, conv_ln_g, conv_ln_b, w_pw2, b_pw2, ln_mix_g, ln_mix_b, ln_ffn_g, ln_ffn_b, w_router, router_bias, w_gate, w_up, w_down):
    bsz, seq, d = x.shape
    n = bsz * seq
    h = x.reshape(n, d)
    wr_t = w_router.T.astype(F32)
    wr_hi = wr_t.astype(BF16)
    wr_lo = (wr_t - wr_hi.astype(F32)).astype(BF16)
    wr = jnp.concatenate([wr_hi, wr_lo], axis=0)
    rb = router_bias.reshape(N_EXPERTS, 1).astype(F32)
    n_dil_cols = len(DILATIONS) * GROUP_COLS
    q_cols = np.zeros((3 * d,), np.float32) + 1.0
    q_cols[0:n_dil_cols] = HEAD_DIM ** -0.5
    q_cols[3 * n_dil_cols:3 * n_dil_cols + GROUP_COLS] = HEAD_DIM ** -0.5

    for layer in range(DEPTH):
        i = layer // 2
        g_mix, b_mix = _vec(ln_mix_g[layer]), _vec(ln_mix_b[layer])
        if layer % 2 == 0:
            w_in = (w_in_attn[i] * q_cols[None, :]).astype(BF16)
            proj = _matmul(h, w_in)
            lhs = []
            for grp in range(len(DILATIONS)):
                lhs += _dilated_attention(proj, _dilated_bias(rel_bias, grp), grp, bsz, seq)
            lhs.append(_neighborhood_attention(proj, _na_bias(rpb_2d[i]), bsz, seq))
            x1, x1r, route = _post_mixer(_post_attn_kernel, "post_attn", lhs,
                                         [w_out_attn[i].astype(BF16)], h, g_mix, b_mix, wr, rb)
        else:
            u = _conv_front(h.reshape(bsz, seq, d), w_pw1[i].astype(BF16), _vec(b_pw1[i]),
                            w_dw[i].astype(F32), _vec(b_dw[i]), _vec(conv_ln_g[i]), _vec(conv_ln_b[i]))
            x1, x1r, route = _post_mixer(_post_conv_kernel, "post_conv", [u.reshape(n, d)],
                                         [w_pw2[i].astype(BF16), _vec(b_pw2[i])], h, g_mix, b_mix, wr, rb)
        h = _moe(x1, x1r, route, w_gate, w_up, w_down, layer,
                 _vec(ln_ffn_g[layer]), _vec(ln_ffn_b[layer]))
    return h.reshape(bsz, seq, d)
```

```python
import functools
import math

import jax
import jax.numpy as jnp
import numpy as np
from jax import lax
from jax.experimental import pallas as pl
from jax.experimental.pallas import tpu as pltpu

D_MODEL = 1024
HEAD_DIM = 64
GROUP_HEADS = 4
GROUP_COLS = GROUP_HEADS * HEAD_DIM
DILATIONS = (1, 4, 16)
QUERY_BLOCK = 64
GRID_W = 64
NA_ROWS = 8
NA_COLS = 16
T5_BUCKETS = 32
T5_MAX_DIST = 1024
CONV_WIDTH = 31
CONV_HALO = 16
N_EXPERTS = 16
GROUP_EXPERTS = 4
D_EXPERT = 512
DEPTH = 4
ALPHA = (2 * DEPTH) ** 0.25
LN_EPS = 1e-5
NEG = -1e30

LANES = 128
ROW_CHUNKS = D_MODEL // LANES
GROUP_CHUNKS = GROUP_COLS // LANES
PAIR_ORDER = ((0, 1), (0, 2), (0, 3), (1, 3), (1, 2), (3, 2))
N_CLASSES = (N_EXPERTS // GROUP_EXPERTS) * len(PAIR_ORDER)
PACKED_SUBLANES = D_MODEL // (2 * LANES)
SEQ_TILE = 1024
EXPERT_TILE = 256
COMBINE_TILE = 512
ROW_COPY_UNROLL = 8
VMEM_LIMIT = 56 * 1024 * 1024

F32 = jnp.float32
BF16 = jnp.bfloat16


def _params(semantics, vmem=VMEM_LIMIT):
    return pltpu.CompilerParams(dimension_semantics=semantics, vmem_limit_bytes=vmem)


MATMUL_SLAB = 512


def _pack_bf16_pair(lo, hi):
    lo_bits = pltpu.bitcast(lo.astype(BF16).astype(F32), jnp.uint32)
    hi_bits = pltpu.bitcast(hi.astype(BF16).astype(F32), jnp.uint32)
    return (lo_bits >> 16) | (hi_bits & jnp.uint32(0xFFFF0000))


def _unpack_bf16_pair(word):
    return pltpu.bitcast(word << 16, F32), pltpu.bitcast(word & jnp.uint32(0xFFFF0000), F32)


def _matmul_kernel(x_ref, w_ref, o_ref):
    xb = x_ref[...].astype(BF16)
    per_slab = MATMUL_SLAB // GROUP_COLS
    for s in range(w_ref.shape[1] // MATMUL_SLAB):
        y = jnp.dot(xb, w_ref[:, s * MATMUL_SLAB:(s + 1) * MATMUL_SLAB], preferred_element_type=F32)
        for c in range(per_slab):
            lo = y[:, (2 * c) * LANES:(2 * c + 1) * LANES]
            hi = y[:, (2 * c + 1) * LANES:(2 * c + 2) * LANES]
            o_ref[s * per_slab + c] = _pack_bf16_pair(lo, hi)


def _matmul(x, w, tm=512):
    n, k = x.shape
    _, m = w.shape
    return pl.pallas_call(
        _matmul_kernel,
        grid=(n // tm,),
        in_specs=[pl.BlockSpec((tm, k), lambda i: (i, 0)),
                  pl.BlockSpec((k, m), lambda i: (0, 0))],
        out_specs=pl.BlockSpec((m // GROUP_COLS, tm, LANES), lambda i: (0, i, 0)),
        out_shape=jax.ShapeDtypeStruct((m // GROUP_COLS, n, LANES), jnp.uint32),
        compiler_params=_params(("parallel",)),
        name="qkv_proj",
    )(x, w)


def _glu_kernel(x_ref, w_ref, b_ref, o_ref):
    xb = x_ref[...].astype(BF16)
    half = w_ref.shape[1] // 2
    for s in range(half // MATMUL_SLAB):
        cols = slice(s * MATMUL_SLAB, (s + 1) * MATMUL_SLAB)
        gate_cols = slice(half + s * MATMUL_SLAB, half + (s + 1) * MATMUL_SLAB)
        a = jnp.dot(xb, w_ref[:, cols], preferred_element_type=F32) + b_ref[:, cols]
        g = jnp.dot(xb, w_ref[:, gate_cols], preferred_element_type=F32) + b_ref[:, gate_cols]
        o_ref[:, cols] = a * jax.nn.sigmoid(g)


def _pw1_glu(x, w, b, tm=512):
    n, k = x.shape
    m = w.shape[1]
    return pl.pallas_call(
        _glu_kernel,
        grid=(n // tm,),
        in_specs=[pl.BlockSpec((tm, k), lambda i: (i, 0)),
                  pl.BlockSpec((k, m), lambda i: (0, 0)),
                  pl.BlockSpec((1, m), lambda i: (0, 0))],
        out_specs=pl.BlockSpec((tm, m // 2), lambda i: (i, 0)),
        out_shape=jax.ShapeDtypeStruct((n, m // 2), F32),
        compiler_params=_params(("parallel",)),
        name="pw1_glu",
    )(x, w, b)


def _head_masks(rows):
    lane = lax.broadcasted_iota(jnp.int32, (rows, GROUP_COLS), 1)
    return [(lane >= HEAD_DIM * h) & (lane < HEAD_DIM * (h + 1)) for h in range(GROUP_HEADS)]


def _stack_heads(q, masks):
    return jnp.concatenate([jnp.where(m, q, 0.0) for m in masks], axis=0)


def _unstack_heads(o4, masks, rows):
    out = jnp.where(masks[0], o4[0:rows], 0.0)
    for h in range(1, GROUP_HEADS):
        out = out + jnp.where(masks[h], o4[h * rows:(h + 1) * rows], 0.0)
    return out


def _softmax_pv(logits, v, with_lse):
    m = jnp.max(logits, axis=-1, keepdims=True)
    p = jnp.exp(logits - m)
    s = jnp.sum(p, axis=-1, keepdims=True)
    o4 = jnp.dot(p.astype(BF16), v, preferred_element_type=F32) * (1.0 / s)
    lse = (m + jnp.log(s)) if with_lse else None
    return o4, lse


def _dilated_kernel(q_ref, kp_ref, kc_ref, kn_ref, vp_ref, vc_ref, vn_ref, bias_ref,
                    o_ref, l_ref, *, dil):
    t = pl.program_id(1)
    chunk = QUERY_BLOCK * dil
    n_chunks = SEQ_TILE // chunk
    qb = QUERY_BLOCK
    masks = _head_masks(qb)
    col = lax.broadcasted_iota(jnp.int32, (GROUP_HEADS * qb, 3 * qb), 1)
    no_prev = jnp.where(col < qb, NEG, 0.0) * (t == 0).astype(F32)
    no_next = jnp.where(col >= 2 * qb, NEG, 0.0) * (t == pl.num_programs(1) - 1).astype(F32)
    bias = bias_ref[...]

    def sl(start):
        return pl.ds(start, qb) if dil == 1 else pl.ds(start, qb, stride=dil)

    def rows(ref, start):
        return jnp.concatenate(_unpack_bf16_pair(ref[0, sl(start), :]), axis=-1)

    def window(p_ref, c_ref, n_ref, c, r):
        parts = []
        for cc in (c - 1, c, c + 1):
            if cc < 0:
                parts.append(rows(p_ref, r))
            elif cc >= n_chunks:
                parts.append(rows(n_ref, r))
            else:
                parts.append(rows(c_ref, cc * chunk + r))
        return jnp.concatenate(parts, axis=0).astype(BF16)

    for c in range(n_chunks):
        for r in range(dil):
            base = c * chunk + r
            q4 = _stack_heads(rows(q_ref, base), masks).astype(BF16)
            k = window(kp_ref, kc_ref, kn_ref, c, r)
            v = window(vp_ref, vc_ref, vn_ref, c, r)
            logits = lax.dot_general(q4, k, (((1,), (1,)), ((), ())),
                                     preferred_element_type=F32) + bias
            if c == 0:
                logits = logits + no_prev
            if c == n_chunks - 1:
                logits = logits + no_next
            o4, lse = _softmax_pv(logits, v, True)
            o = _unstack_heads(o4, masks, qb)
            lb = _unstack_heads(jnp.broadcast_to(lse, (GROUP_HEADS * qb, GROUP_COLS)), masks, qb)
            o_ref[0, sl(base), :] = _pack_bf16_pair(o[:, 0:LANES], o[:, LANES:])
            for ch in range(GROUP_CHUNKS):
                l_ref[ch, sl(base), :] = lb[:, ch * LANES:(ch + 1) * LANES]


def _tile_spec(col_block, seq, chunks=1):
    tiles = seq // SEQ_TILE
    return pl.BlockSpec((chunks, SEQ_TILE, LANES), lambda b, t: (col_block, b * tiles + t, 0))


def _halo_specs(col_block, halo, seq):
    per_tile = SEQ_TILE // halo
    per_seq = seq // halo
    return [
        pl.BlockSpec((1, halo, LANES),
                     lambda b, t: (col_block, b * per_seq + jnp.maximum(t * per_tile - 1, 0), 0)),
        _tile_spec(col_block, seq),
        pl.BlockSpec((1, halo, LANES),
                     lambda b, t: (col_block, b * per_seq + jnp.minimum((t + 1) * per_tile, per_seq - 1), 0)),
    ]


def _dilated_attention(proj, bias, group, bsz, seq):
    dil = DILATIONS[group]
    n_groups = len(DILATIONS)
    q_col, k_col, v_col = group, n_groups + group, 2 * n_groups + group
    return pl.pallas_call(
        functools.partial(_dilated_kernel, dil=dil),
        grid=(bsz, seq // SEQ_TILE),
        in_specs=[_tile_spec(q_col, seq)]
        + _halo_specs(k_col, QUERY_BLOCK * dil, seq) + _halo_specs(v_col, QUERY_BLOCK * dil, seq)
        + [pl.BlockSpec(bias.shape, lambda b, t: (0, 0))],
        out_specs=[_tile_spec(0, seq), _tile_spec(0, seq, GROUP_CHUNKS)],
        out_shape=[jax.ShapeDtypeStruct((1, bsz * seq, LANES), jnp.uint32),
                   jax.ShapeDtypeStruct((GROUP_CHUNKS, bsz * seq, LANES), F32)],
        compiler_params=_params(("parallel", "arbitrary")),
        name=f"dilated_attn_d{dil}",
    )(proj, proj, proj, proj, proj, proj, proj, bias)


NA_HALO = 256
NA_KEYS = NA_ROWS * GRID_W


def _na_kernel(q_ref, kp_ref, kc_ref, kn_ref, vp_ref, vc_ref, vn_ref, bias_ref, o_ref,
               kwin, vwin, *, grid_rows):
    t = pl.program_id(1)
    tile_rows = SEQ_TILE // GRID_W
    half = NA_ROWS // 2
    for win, p_ref, c_ref, n_ref in ((kwin, kp_ref, kc_ref, kn_ref), (vwin, vp_ref, vc_ref, vn_ref)):
        for first, src in ((0, p_ref), (NA_HALO, c_ref), (NA_HALO + SEQ_TILE, n_ref)):
            n_rows = src.shape[1]
            for ch, part in enumerate(_unpack_bf16_pair(src[0])):
                win[first:first + n_rows, ch * LANES:(ch + 1) * LANES] = part.astype(BF16)
    masks = _head_masks(GRID_W)
    for i in range(tile_rows):
        row = t * tile_rows + i
        row0 = jnp.clip(row - half, 0, grid_rows - NA_ROWS)
        off = pl.multiple_of((row0 - (t * tile_rows - half)) * GRID_W, GRID_W)
        q = jnp.concatenate(_unpack_bf16_pair(q_ref[0, i * GRID_W:(i + 1) * GRID_W, :]), axis=-1)
        q4 = _stack_heads(q, masks).astype(BF16)
        k = kwin[pl.ds(off, NA_KEYS), :]
        v = vwin[pl.ds(off, NA_KEYS), :]
        logits = lax.dot_general(q4, k, (((1,), (1,)), ((), ())),
                                 preferred_element_type=F32) + bias_ref[row - row0]
        o4, _ = _softmax_pv(logits, v, False)
        o = _unstack_heads(o4, masks, GRID_W)
        o_ref[0, i * GRID_W:(i + 1) * GRID_W, :] = _pack_bf16_pair(o[:, 0:LANES], o[:, LANES:])


def _neighborhood_attention(proj, bias, bsz, seq):
    base = 3 * len(DILATIONS)
    win_rows = SEQ_TILE + 2 * NA_HALO
    return pl.pallas_call(
        functools.partial(_na_kernel, grid_rows=seq // GRID_W),
        grid=(bsz, seq // SEQ_TILE),
        in_specs=[_tile_spec(base, seq)]
        + _halo_specs(base + 1, NA_HALO, seq) + _halo_specs(base + 2, NA_HALO, seq)
        + [pl.BlockSpec(bias.shape, lambda b, t: (0, 0, 0))],
        out_specs=_tile_spec(0, seq),
        out_shape=jax.ShapeDtypeStruct((1, bsz * seq, LANES), jnp.uint32),
        scratch_shapes=[pltpu.VMEM((win_rows, GROUP_COLS), BF16),
                        pltpu.VMEM((win_rows, GROUP_COLS), BF16)],
        compiler_params=_params(("parallel", "arbitrary")),
        name="neighborhood_attn",
    )(proj, proj, proj, proj, proj, proj, proj, bias)


def _t5_bucket_table(dil):
    qb = QUERY_BLOCK
    rel = (np.arange(3 * qb)[None, :] - qb - np.arange(qb)[:, None])
    nb = T5_BUCKETS // 2
    max_exact = nb // 2
    n = np.abs(rel * dil)
    nf = np.maximum(n, 1).astype(np.float32)
    large = max_exact + (np.log(nf / np.float32(max_exact)) / np.float32(math.log(T5_MAX_DIST / max_exact))
                         * np.float32(nb - max_exact)).astype(np.int32)
    large = np.minimum(large, nb - 1)
    bucket = np.where(rel * dil > 0, nb, 0) + np.where(n < max_exact, n, large)
    return bucket.astype(np.int32), np.abs(rel) <= qb


def _dilated_bias(rel_bias, group):
    bucket, band = _t5_bucket_table(DILATIONS[group])
    tab = rel_bias[:, group * GROUP_HEADS:(group + 1) * GROUP_HEADS].astype(F32)
    onehot = (bucket[None, :, :] == np.arange(T5_BUCKETS)[:, None, None]).astype(np.float32)
    bias = jnp.einsum("bh,bqk->hqk", tab, onehot, precision=lax.Precision.HIGHEST)
    bias = jnp.where(band[None], bias, NEG)
    return bias.reshape(GROUP_HEADS * QUERY_BLOCK, 3 * QUERY_BLOCK)


def _na_bias(rpb):
    qc = np.arange(GRID_W)[:, None]
    kc = np.arange(GRID_W)[None, :]
    win0 = np.clip(qc - NA_COLS // 2, 0, GRID_W - NA_COLS)
    col_ok = (kc >= win0) & (kc < win0 + NA_COLS)
    dc = np.clip(kc - qc + NA_COLS - 1, 0, 2 * NA_COLS - 2)
    shift = np.arange(NA_ROWS)[:, None]
    kr = np.arange(NA_ROWS)[None, :]
    dr = kr - shift + NA_ROWS - 1
    pick_row = (dr[:, :, None] == np.arange(2 * NA_ROWS - 1)[None, None, :]).astype(np.float32)
    pick_col = (dc[None, :, :] == np.arange(2 * NA_COLS - 1)[:, None, None]).astype(np.float32)
    hi = lax.Precision.HIGHEST
    rows = jnp.einsum("hac,dka->hdkc", rpb.astype(F32), pick_row, precision=hi)
    b = jnp.einsum("hdkc,cqx->dhqkx", rows, pick_col, precision=hi)
    b = jnp.where(col_ok[None, None, :, None, :], b, NEG)
    return b.reshape(NA_ROWS, GROUP_HEADS * GRID_W, NA_KEYS)


def _layer_norm(z, g, b):
    mu = jnp.mean(z, axis=-1, keepdims=True)
    zc = z - mu
    var = jnp.mean(zc * zc, axis=-1, keepdims=True)
    return zc * lax.rsqrt(var + LN_EPS) * g + b


def _row(x, i):
    return x[i:i + 1, :]


def _route(logits, rbias):
    scores = jax.nn.sigmoid(logits)
    sel = scores + rbias
    ge = GROUP_EXPERTS
    n_groups = N_EXPERTS // ge
    best_sum, best_g = None, None
    for g in range(n_groups):
        vals = [_row(sel, g * ge + j) for j in range(ge)]
        top2 = None
        for a in range(ge):
            for b in range(a + 1, ge):
                pair = vals[a] + vals[b]
                top2 = pair if top2 is None else jnp.maximum(top2, pair)
        if best_sum is None:
            best_sum, best_g = top2, jnp.zeros_like(top2, dtype=jnp.int32)
        else:
            upd = top2 > best_sum
            best_g = jnp.where(upd, g, best_g)
            best_sum = jnp.where(upd, top2, best_sum)

    def in_group(x, j):
        out = _row(x, j)
        for g in range(1, n_groups):
            out = jnp.where(best_g == g, _row(x, g * ge + j), out)
        return out

    cand = [in_group(sel, j) for j in range(ge)]
    cand_score = [in_group(scores, j) for j in range(ge)]
    v1, j1, w1 = cand[0], jnp.zeros_like(best_g), cand_score[0]
    for j in range(1, ge):
        upd = cand[j] > v1
        v1 = jnp.where(upd, cand[j], v1)
        j1 = jnp.where(upd, j, j1)
        w1 = jnp.where(upd, cand_score[j], w1)
    v2 = jnp.full_like(v1, -jnp.inf)
    j2, w2 = jnp.zeros_like(best_g), jnp.zeros_like(w1)
    for j in range(ge):
        upd = (cand[j] > v2) & (j1 != j)
        v2 = jnp.where(upd, cand[j], v2)
        j2 = jnp.where(upd, j, j2)
        w2 = jnp.where(upd, cand_score[j], w2)
    wsum = w1 + w2
    g1, g2 = w1 / wsum, w2 / wsum
    lo, hi = jnp.minimum(j1, j2), jnp.maximum(j1, j2)
    pair = jnp.zeros_like(best_g)
    first = jnp.zeros_like(best_g)
    for q, (a, b) in enumerate(PAIR_ORDER):
        hit = (lo == min(a, b)) & (hi == max(a, b))
        pair = jnp.where(hit, q, pair)
        first = jnp.where(hit, a, first)
    is_first = j1 == first
    g_first = jnp.where(is_first, g1, g2)
    g_second = jnp.where(is_first, g2, g1)
    cls = (best_g * len(PAIR_ORDER) + pair).astype(F32)
    zero = jnp.zeros_like(w1)
    return jnp.concatenate([cls, g_first, g_second, zero, zero, zero, zero, zero], axis=0)


def _to_column(row_vec, eye):
    return jnp.sum(jnp.where(eye, row_vec, 0.0), axis=-1, keepdims=True)


def _post_tail(m, x_ref, g_ref, b_ref, wr_ref, rb_ref, x1_ref, x1p_ref, route_ref):
    tm = m.shape[0]
    xn = _layer_norm(ALPHA * x_ref[...] + m, g_ref[...], b_ref[...])
    x1_ref[...] = xn
    xh = xn.astype(BF16)
    xhf = xh.astype(F32)
    xl = (xn - xhf).astype(BF16)
    wr = wr_ref[...]
    contract = (((1,), (1,)), ((), ()))
    a = lax.dot_general(wr, xh, contract, preferred_element_type=F32)
    c = lax.dot_general(wr[0:N_EXPERTS], xl, contract, preferred_element_type=F32)
    logits = a[0:N_EXPERTS] + a[N_EXPERTS:] + c
    route = _route(logits, rb_ref[...])
    route_ref[...] = route
    bits = pltpu.bitcast(xhf, jnp.uint32)
    for s in range(PACKED_SUBLANES):
        lo = bits[:, (2 * s) * LANES:(2 * s + 1) * LANES]
        hi = bits[:, (2 * s + 1) * LANES:(2 * s + 2) * LANES]
        x1p_ref[pl.ds(s, tm, stride=ROW_CHUNKS), :] = (lo >> 16) | (hi & jnp.uint32(0xFFFF0000))
    eye = (lax.broadcasted_iota(jnp.int32, (tm, tm), 0) == lax.broadcasted_iota(jnp.int32, (tm, tm), 1))
    g_first, g_second = _to_column(route[1:2, :], eye), _to_column(route[2:3, :], eye)
    lane = lax.broadcasted_iota(jnp.int32, (tm, LANES), 1)
    gates = jnp.where(lane == 0, g_first, jnp.where(lane == 1, g_second, 0.0))
    x1p_ref[pl.ds(PACKED_SUBLANES, tm, stride=ROW_CHUNKS), :] = pltpu.bitcast(gates, jnp.uint32)
    for s in range(PACKED_SUBLANES + 1, ROW_CHUNKS):
        x1p_ref[pl.ds(s, tm, stride=ROW_CHUNKS), :] = jnp.zeros((tm, LANES), jnp.uint32)


def _post_attn_kernel(o0_ref, l0_ref, o1_ref, l1_ref, o2_ref, l2_ref, ob_ref, w_ref,
                      x_ref, g_ref, b_ref, wr_ref, rb_ref, x1_ref, x1r_ref, route_ref):
    def wide(ref):
        return jnp.concatenate([ref[ch] for ch in range(GROUP_CHUNKS)], axis=-1)

    def out(ref):
        return jnp.concatenate(_unpack_bf16_pair(ref[0]), axis=-1)

    l0, l1, l2 = wide(l0_ref), wide(l1_ref), wide(l2_ref)
    mx = jnp.maximum(jnp.maximum(l0, l1), l2)
    e0, e1, e2 = jnp.exp(l0 - mx), jnp.exp(l1 - mx), jnp.exp(l2 - mx)
    inv = 1.0 / (e0 + e1 + e2)
    lhs = jnp.concatenate([out(o0_ref) * (e0 * inv), out(o1_ref) * (e1 * inv),
                           out(o2_ref) * (e2 * inv), out(ob_ref)], axis=-1).astype(BF16)
    m = jnp.dot(lhs, w_ref[...], preferred_element_type=F32)
    _post_tail(m, x_ref, g_ref, b_ref, wr_ref, rb_ref, x1_ref, x1r_ref, route_ref)


def _post_conv_kernel(u_ref, w_ref, wb_ref, x_ref, g_ref, b_ref, wr_ref, rb_ref,
                      x1_ref, x1r_ref, route_ref):
    m = jnp.dot(u_ref[...], w_ref[...], preferred_element_type=F32) + wb_ref[...]
    _post_tail(m, x_ref, g_ref, b_ref, wr_ref, rb_ref, x1_ref, x1r_ref, route_ref)


def _post_mixer(kernel, name, lhs_list, consts_front, x, g, b, wr, rb, tm=512):
    n, d = x.shape
    row = lambda i: (i, 0)
    const = lambda i: (0, 0)
    in_specs = [pl.BlockSpec((tm, a.shape[1]), row) if a.ndim == 2
                else pl.BlockSpec((a.shape[0], tm, LANES), lambda i: (0, i, 0)) for a in lhs_list]
    in_specs += [pl.BlockSpec(c.shape, const) for c in consts_front]
    in_specs += [pl.BlockSpec((tm, d), row)]
    in_specs += [pl.BlockSpec(c.shape, const) for c in (g, b, wr, rb)]
    return pl.pallas_call(
        kernel,
        grid=(n // tm,),
        in_specs=in_specs,
        out_specs=[pl.BlockSpec((tm, d), row),
                   pl.BlockSpec((tm * ROW_CHUNKS, LANES), row),
                   pl.BlockSpec((8, tm), lambda i: (0, i))],
        out_shape=[jax.ShapeDtypeStruct((n, d), F32),
                   jax.ShapeDtypeStruct((n * ROW_CHUNKS, LANES), jnp.uint32),
                   jax.ShapeDtypeStruct((8, n), F32)],
        compiler_params=_params(("parallel",)),
        name=name,
    )(*lhs_list, *consts_front, x, g, b, wr, rb)


CONV_SLAB = 256


def _conv_front_kernel(xp_ref, xc_ref, xn_ref, w_ref, wb_ref, taps_ref, cb_ref, g_ref, b_ref,
                       o_ref, win, acc, *, ts):
    t = pl.program_id(1)
    has_prev = (t > 0).astype(F32)
    has_next = (t < pl.num_programs(1) - 1).astype(F32)
    rows = ts + 2 * CONV_HALO
    row = lax.broadcasted_iota(jnp.int32, (rows, 1), 0)
    keep = jnp.where(row < CONV_HALO, has_prev, jnp.where(row >= CONV_HALO + ts, has_next, 1.0))
    xw = jnp.concatenate([xp_ref[...], xc_ref[...], xn_ref[...]], axis=0).astype(BF16)
    half = w_ref.shape[1] // 2
    chunks_per_slab = CONV_SLAB // LANES
    for s in range(half // CONV_SLAB):
        cols = slice(s * CONV_SLAB, (s + 1) * CONV_SLAB)
        gate_cols = slice(half + s * CONV_SLAB, half + (s + 1) * CONV_SLAB)
        a = jnp.dot(xw, w_ref[:, cols], preferred_element_type=F32) + wb_ref[:, cols]
        gt = jnp.dot(xw, w_ref[:, gate_cols], preferred_element_type=F32) + wb_ref[:, gate_cols]
        u = a * jax.nn.sigmoid(gt) * keep
        for c in range(chunks_per_slab):
            j = s * chunks_per_slab + c
            win[j] = u[:, c * LANES:(c + 1) * LANES]
            conv = jnp.zeros((ts, LANES), F32)
            for tap in range(CONV_WIDTH):
                start = CONV_HALO - CONV_WIDTH // 2 + tap
                conv = conv + win[j, pl.ds(start, ts), :] * taps_ref[j, tap:tap + 1, :]
            acc[j] = conv
    u = jnp.concatenate([acc[j] for j in range(ROW_CHUNKS)], axis=-1) + cb_ref[...]
    y = _layer_norm(u, g_ref[...], b_ref[...])
    o_ref[...] = (y * jax.nn.sigmoid(y)).astype(o_ref.dtype)


def _conv_front(x, w_pw1, b_pw1, w_dw, b_dw, g, b, ts=256):
    bsz, seq, d = x.shape
    per_tile = ts // CONV_HALO
    last = seq // CONV_HALO - 1
    taps = jnp.pad(w_dw, ((0, 32 - CONV_WIDTH), (0, 0))).reshape(32, ROW_CHUNKS, LANES).transpose(1, 0, 2)
    vec = pl.BlockSpec((1, d), lambda bb, t: (0, 0))
    return pl.pallas_call(
        functools.partial(_conv_front_kernel, ts=ts),
        grid=(bsz, seq // ts),
        in_specs=[pl.BlockSpec((None, CONV_HALO, d), lambda bb, t: (bb, jnp.maximum(t * per_tile - 1, 0), 0)),
                  pl.BlockSpec((None, ts, d), lambda bb, t: (bb, t, 0)),
                  pl.BlockSpec((None, CONV_HALO, d), lambda bb, t: (bb, jnp.minimum((t + 1) * per_tile, last), 0)),
                  pl.BlockSpec(w_pw1.shape, lambda bb, t: (0, 0)),
                  pl.BlockSpec(b_pw1.shape, lambda bb, t: (0, 0)),
                  pl.BlockSpec(taps.shape, lambda bb, t: (0, 0, 0)),
                  vec, vec, vec],
        out_specs=pl.BlockSpec((None, ts, d), lambda bb, t: (bb, t, 0)),
        out_shape=jax.ShapeDtypeStruct((bsz, seq, d), BF16),
        scratch_shapes=[pltpu.VMEM((ROW_CHUNKS, ts + 2 * CONV_HALO, LANES), F32),
                        pltpu.VMEM((ROW_CHUNKS, ts, LANES), F32)],
        compiler_params=_params(("parallel", "arbitrary")),
        name="conv_front",
    )(x, x, x, w_pw1, b_pw1, taps, b_dw, g, b)


def _row_gather_start(src_hbm, dst, sem, idx_ref, first, count):
    def body(r8, carry):
        for u in range(ROW_COPY_UNROLL):
            r = r8 * ROW_COPY_UNROLL + u
            tok = idx_ref[first + r]
            pltpu.make_async_copy(src_hbm.at[pl.ds(pl.multiple_of(tok * ROW_CHUNKS, ROW_CHUNKS), ROW_CHUNKS)],
                                  dst.at[pl.ds(pl.multiple_of(r * ROW_CHUNKS, ROW_CHUNKS), ROW_CHUNKS)],
                                  sem).start(priority=u % 2)
        return carry
    lax.fori_loop(0, count // ROW_COPY_UNROLL, body, 0)


def _row_gather_wait(src_hbm, dst, sem):
    pltpu.make_async_copy(src_hbm.at[pl.ds(0, dst.shape[0])], dst, sem).wait()


def _dispatch_kernel(dest_ref, tail_ref, n_tiles_ref, x_ref, xs_hbm, zeros, stage, sem, zsem, *, tm):
    i = pl.program_id(0)
    n_steps = pl.num_programs(0)
    slot = i % 2
    tile_rows = EXPERT_TILE * ROW_CHUNKS
    all_tiles = xs_hbm.shape[0] // tile_rows

    def zero_tile(first_row):
        start = pl.multiple_of(first_row * ROW_CHUNKS, ROW_CHUNKS)
        return pltpu.make_async_copy(zeros, xs_hbm.at[pl.ds(start, tile_rows)], zsem)

    @pl.when(i == 0)
    def _():
        zeros[...] = jnp.zeros_like(zeros)
        for c in range(N_CLASSES):
            zero_tile(tail_ref[c]).start()
        for c in range(N_CLASSES):
            zero_tile(tail_ref[c]).wait()

        def unused(j, carry):
            cp = zero_tile(j * EXPERT_TILE)
            cp.start()
            cp.wait()
            return carry
        lax.fori_loop(n_tiles_ref[0], all_tiles, unused, 0)

    stage[slot] = x_ref[...]

    def body(r8, carry):
        for u in range(ROW_COPY_UNROLL):
            r = r8 * ROW_COPY_UNROLL + u
            d = dest_ref[i * tm + r]
            pltpu.make_async_copy(stage.at[slot, pl.ds(pl.multiple_of(r * ROW_CHUNKS, ROW_CHUNKS), ROW_CHUNKS)],
                                  xs_hbm.at[pl.ds(pl.multiple_of(d * ROW_CHUNKS, ROW_CHUNKS), ROW_CHUNKS)],
                                  sem.at[slot]).start(priority=u % 2)
        return carry
    lax.fori_loop(0, tm // ROW_COPY_UNROLL, body, 0)

    def wait_step(s):
        pltpu.make_async_copy(stage.at[s], xs_hbm.at[pl.ds(0, tm * ROW_CHUNKS)], sem.at[s]).wait()

    @pl.when(i > 0)
    def _():
        wait_step(1 - slot)

    @pl.when(i == n_steps - 1)
    def _():
        wait_step(slot)


def _dispatch(x1p, dest, tail_start, n_tiles, n_rows, tm=1024):
    n_tokens = x1p.shape[0] // ROW_CHUNKS
    grid_spec = pltpu.PrefetchScalarGridSpec(
        num_scalar_prefetch=3,
        grid=(n_tokens // tm,),
        in_specs=[pl.BlockSpec((tm * ROW_CHUNKS, LANES), lambda i, dst, tail, nt: (i, 0))],
        out_specs=pl.BlockSpec(memory_space=pl.ANY),
        scratch_shapes=[pltpu.VMEM((EXPERT_TILE * ROW_CHUNKS, LANES), x1p.dtype),
                        pltpu.VMEM((2, tm * ROW_CHUNKS, LANES), x1p.dtype),
                        pltpu.SemaphoreType.DMA((2,)),
                        pltpu.SemaphoreType.DMA(())],
    )
    return pl.pallas_call(
        functools.partial(_dispatch_kernel, tm=tm),
        grid_spec=grid_spec,
        out_shape=jax.ShapeDtypeStruct((n_rows * ROW_CHUNKS, LANES), x1p.dtype),
        compiler_params=_params(("arbitrary",)),
        name="moe_dispatch",
    )(dest, tail_start, n_tiles, x1p)


def _expert_kernel(run_ref, run_expert_ref, n_runs_ref, n_tiles_ref, x_ref,
                   wg_hbm, wu_hbm, wd_hbm, y_ref, stage_g, stage_u, stage_d,
                   wg1_bf, wu1_bf, wd1_bf, wg2_bf, wu2_bf, wd2_bf, sem, *, layer):
    i = pl.program_id(0)
    tm = EXPERT_TILE
    before = jnp.maximum(i - 1, 0)
    working = ((wg1_bf, wu1_bf, wd1_bf), (wg2_bf, wu2_bf, wd2_bf))

    def copies(role, run):
        slot = run % 2
        e = run_expert_ref[role, run]
        return [pltpu.make_async_copy(w.at[layer, e], st.at[role, slot], sem.at[role, slot])
                for w, st in ((wg_hbm, stage_g), (wu_hbm, stage_u), (wd_hbm, stage_d))]

    for role in range(2):
        run = run_ref[role, i]

        @pl.when(i == 0)
        def _():
            for cp in copies(role, 0):
                cp.start(priority=1)

        @pl.when((i == 0) | (run != run_ref[role, before]))
        def _():
            for cp in copies(role, run):
                cp.wait()
            slot = run % 2
            for st, wk in zip((stage_g, stage_u, stage_d), working[role]):
                wk[...] = st[role, slot].astype(BF16)

            @pl.when(run + 1 < n_runs_ref[role])
            def _():
                for cp in copies(role, run + 1):
                    cp.start(priority=1)

    @pl.when(i < n_tiles_ref[0])
    def _():
        halves = []
        for s in range(PACKED_SUBLANES):
            word = x_ref[pl.ds(s, tm, stride=ROW_CHUNKS), :]
            halves.append(pltpu.bitcast(word << 16, F32))
            halves.append(pltpu.bitcast(word & jnp.uint32(0xFFFF0000), F32))
        x = jnp.concatenate(halves, axis=-1).astype(BF16)
        gates = pltpu.bitcast(x_ref[pl.ds(PACKED_SUBLANES, tm, stride=ROW_CHUNKS), :], F32)

        def ffn(wg, wu, wd):
            hg = jnp.dot(x, wg[...], preferred_element_type=F32)
            hu = jnp.dot(x, wu[...], preferred_element_type=F32)
            hid = (hg * jax.nn.sigmoid(hg) * hu).astype(BF16)
            return jnp.dot(hid, wd[...], preferred_element_type=F32)

        y = gates[:, 0:1] * ffn(wg1_bf, wu1_bf, wd1_bf) + gates[:, 1:2] * ffn(wg2_bf, wu2_bf, wd2_bf)
        for j in range(ROW_CHUNKS):
            y_ref[pl.ds(j, tm, stride=ROW_CHUNKS), :] = y[:, j * LANES:(j + 1) * LANES]

    @pl.when(i >= n_tiles_ref[0])
    def _():
        y_ref[...] = jnp.zeros_like(y_ref)


def _experts(xs, w_gate, w_up, w_down, layer, tile_run, run_expert, n_runs, n_tiles):
    tm = EXPERT_TILE
    n_rows = xs.shape[0] // ROW_CHUNKS
    d, de = w_gate.shape[2], w_gate.shape[3]
    row_map = lambda i, run, rexp, nr, nt: (jnp.minimum(i, jnp.maximum(nt[0] - 1, 0)), 0)
    hbm = pl.BlockSpec(memory_space=pl.ANY)
    grid_spec = pltpu.PrefetchScalarGridSpec(
        num_scalar_prefetch=4,
        grid=(n_rows // tm,),
        in_specs=[pl.BlockSpec((tm * ROW_CHUNKS, LANES), row_map), hbm, hbm, hbm],
        out_specs=pl.BlockSpec((tm * ROW_CHUNKS, LANES), lambda i, run, rexp, nr, nt: (i, 0)),
        scratch_shapes=[pltpu.VMEM((2, 2, d, de), F32), pltpu.VMEM((2, 2, d, de), F32),
                        pltpu.VMEM((2, 2, de, d), F32),
                        pltpu.VMEM((d, de), BF16), pltpu.VMEM((d, de), BF16), pltpu.VMEM((de, d), BF16),
                        pltpu.VMEM((d, de), BF16), pltpu.VMEM((d, de), BF16), pltpu.VMEM((de, d), BF16),
                        pltpu.SemaphoreType.DMA((2, 2))],
    )
    return pl.pallas_call(
        functools.partial(_expert_kernel, layer=layer),
        grid_spec=grid_spec,
        out_shape=jax.ShapeDtypeStruct((n_rows * ROW_CHUNKS, LANES), F32),
        compiler_params=_params(("arbitrary",)),
        name="moe_experts",
    )(tile_run, run_expert, n_runs, n_tiles, xs, w_gate, w_up, w_down)


def _combine_kernel(dest_ref, y_hbm, x_ref, g_ref, b_ref, o_ref, ybuf, sem):
    i = pl.program_id(0)
    n_steps = pl.num_programs(0)
    slot = i % 2
    tm = COMBINE_TILE

    @pl.when(i == 0)
    def _():
        _row_gather_start(y_hbm, ybuf.at[0], sem.at[0], dest_ref, 0, tm)

    @pl.when(i + 1 < n_steps)
    def _():
        _row_gather_start(y_hbm, ybuf.at[1 - slot], sem.at[1 - slot], dest_ref, (i + 1) * tm, tm)

    _row_gather_wait(y_hbm, ybuf.at[slot], sem.at[slot])
    f = jnp.concatenate([ybuf[slot, pl.ds(j, tm, stride=ROW_CHUNKS), :] for j in range(ROW_CHUNKS)], axis=-1)
    o_ref[...] = _layer_norm(ALPHA * x_ref[...] + f, g_ref[...], b_ref[...])


def _combine(y_rows, x1, dest, g, b):
    n, d = x1.shape
    tm = COMBINE_TILE
    grid_spec = pltpu.PrefetchScalarGridSpec(
        num_scalar_prefetch=1,
        grid=(n // tm,),
        in_specs=[pl.BlockSpec(memory_space=pl.ANY),
                  pl.BlockSpec((tm, d), lambda i, dst: (i, 0)),
                  pl.BlockSpec((1, d), lambda i, dst: (0, 0)),
                  pl.BlockSpec((1, d), lambda i, dst: (0, 0))],
        out_specs=pl.BlockSpec((tm, d), lambda i, dst: (i, 0)),
        scratch_shapes=[pltpu.VMEM((2, tm * ROW_CHUNKS, LANES), F32),
                        pltpu.SemaphoreType.DMA((2,))],
    )
    return pl.pallas_call(
        _combine_kernel,
        grid_spec=grid_spec,
        out_shape=jax.ShapeDtypeStruct((n, d), F32),
        compiler_params=_params(("arbitrary",)),
        name="moe_combine",
    )(dest, y_rows, x1, g, b)


def _dispatch_plan(route, n_tokens):
    tm = EXPERT_TILE
    blk = LANES
    n_rows = n_tokens + N_CLASSES * tm
    hi = lax.Precision.HIGHEST
    onehot = (route[0][:, None] == jnp.arange(N_CLASSES, dtype=F32)[None, :]).astype(F32)
    blocks = onehot.reshape(n_tokens // blk, blk, N_CLASSES)
    tri = np.tril(np.ones((blk, blk), np.float32))
    local = jnp.einsum("ij,bjk->bik", tri, blocks, precision=hi)
    totals = local[:, -1, :]
    strict = np.tril(np.ones((n_tokens // blk,) * 2, np.float32), -1)
    before = jnp.dot(strict, totals, precision=hi)
    rank = jnp.sum((local + before[:, None, :]) * blocks, axis=-1).reshape(n_tokens) - 1.0
    counts = before[-1] + totals[-1]
    padded = jnp.ceil(counts / tm) * tm
    seg_end = jnp.dot(np.tril(np.ones((N_CLASSES,) * 2, np.float32)), padded, precision=hi)
    seg_start = seg_end - padded
    dest = (jnp.sum(onehot * seg_start[None, :], axis=-1) + rank).astype(jnp.int32)
    tile_start = jnp.arange(n_rows // tm, dtype=F32) * tm
    tile_class = jnp.sum((seg_end[None, :] <= tile_start[:, None]).astype(jnp.int32), axis=-1)
    last_class = jnp.max(jnp.where(padded > 0, jnp.arange(N_CLASSES), 0))
    tile_class = jnp.minimum(tile_class, last_class)
    tile_hot = (tile_class[:, None] == np.arange(N_CLASSES)[None, :]).astype(jnp.int32)
    n_pairs = len(PAIR_ORDER)
    expert_of = np.array([[(c // n_pairs) * GROUP_EXPERTS + PAIR_ORDER[c % n_pairs][role]
                           for c in range(N_CLASSES)] for role in range(2)])
    tile_expert = jnp.sum(tile_hot[None, :, :] * expert_of[:, None, :], axis=-1)
    n_all = n_rows // tm
    starts = jnp.concatenate([jnp.ones((2, 1), jnp.int32),
                              (tile_expert[:, 1:] != tile_expert[:, :-1]).astype(jnp.int32)], axis=1)
    tile_run = jnp.cumsum(starts, axis=1) - 1
    run_hot = (tile_run[:, :, None] == np.arange(n_all)[None, None, :]).astype(jnp.int32)
    run_expert = jnp.sum(run_hot * (starts * tile_expert)[:, :, None], axis=1)
    n_runs = tile_run[:, -1] + 1
    n_tiles = (seg_end[-1] / tm).astype(jnp.int32).reshape(1)
    tail_start = jnp.maximum(seg_end - tm, 0.0).astype(jnp.int32)
    plan = (tile_run.astype(jnp.int32), run_expert.astype(jnp.int32), n_runs.astype(jnp.int32), n_tiles)
    return plan, tail_start, dest, n_rows


def _moe(x1, x1p, route, w_gate, w_up, w_down, layer, g, b):
    n = x1.shape[0]
    plan, tail_start, dest, n_rows = _dispatch_plan(route, n)
    xs = _dispatch(x1p, dest, tail_start, plan[-1], n_rows)
    y_rows = _experts(xs, w_gate, w_up, w_down, layer, *plan)
    return _combine(y_rows, x1, dest, g, b)


def _vec(v):
    return v.reshape(1, -1).astype(F32)


def kernel(x, w_in_attn, w_out_attn, rel_bias, rpb_2d, w_pw1, b_pw1, w_dw, b_dw, conv_ln_g, conv_ln_b, w_pw2, b_pw2, ln_mix_g, ln_mix_b, ln_ffn_g, ln_ffn_b, w_router, router_bias, w_gate, w_up, w_down):
    bsz, seq, d = x.shape
    n = bsz * seq
    h = x.reshape(n, d)
    wr_t = w_router.T.astype(F32)
    wr_hi = wr_t.astype(BF16)
    wr_lo = (wr_t - wr_hi.astype(F32)).astype(BF16)
    wr = jnp.concatenate([wr_hi, wr_lo], axis=0)
    rb = router_bias.reshape(N_EXPERTS, 1).astype(F32)
    n_dil_cols = len(DILATIONS) * GROUP_COLS
    q_cols = np.zeros((3 * d,), np.float32) + 1.0
    q_cols[0:n_dil_cols] = HEAD_DIM ** -0.5
    q_cols[3 * n_dil_cols:3 * n_dil_cols + GROUP_COLS] = HEAD_DIM ** -0.5

    for layer in range(DEPTH):
        i = layer // 2
        g_mix, b_mix = _vec(ln_mix_g[layer]), _vec(ln_mix_b[layer])
        if layer % 2 == 0:
            w_in = (w_in_attn[i] * q_cols[None, :]).astype(BF16)
            proj = _matmul(h, w_in)
            lhs = []
            for grp in range(len(DILATIONS)):
                lhs += _dilated_attention(proj, _dilated_bias(rel_bias, grp), grp, bsz, seq)
            lhs.append(_neighborhood_attention(proj, _na_bias(rpb_2d[i]), bsz, seq))
            x1, x1r, route = _post_mixer(_post_attn_kernel, "post_attn", lhs,
                                         [w_out_attn[i].astype(BF16)], h, g_mix, b_mix, wr, rb)
        else:
            u = _conv_front(h.reshape(bsz, seq, d), w_pw1[i].astype(BF16), _vec(b_pw1[i]),
                            w_dw[i].astype(F32), _vec(b_dw[i]), _vec(conv_ln_g[i]), _vec(conv_ln_b[i]))
            x1, x1r, route = _post_mixer(_post_conv_kernel, "post_conv", [u.reshape(n, d)],
                                         [w_pw2[i].astype(BF16), _vec(b_pw2[i])], h, g_mix, b_mix, wr, rb)
        h = _moe(x1, x1r, route, w_gate, w_up, w_down, layer,
                 _vec(ln_ffn_g[layer]), _vec(ln_ffn_b[layer]))
    return h.reshape(bsz, seq, d)
```

```python
import functools
import math

import jax
import jax.numpy as jnp
import numpy as np
from jax import lax
from jax.experimental import pallas as pl
from jax.experimental.pallas import tpu as pltpu

D_MODEL = 1024
HEAD_DIM = 64
GROUP_HEADS = 4
GROUP_COLS = GROUP_HEADS * HEAD_DIM
DILATIONS = (1, 4, 16)
QUERY_BLOCK = 64
GRID_W = 64
NA_ROWS = 8
NA_COLS = 16
T5_BUCKETS = 32
T5_MAX_DIST = 1024
CONV_WIDTH = 31
CONV_HALO = 16
N_EXPERTS = 16
GROUP_EXPERTS = 4
D_EXPERT = 512
DEPTH = 4
ALPHA = (2 * DEPTH) ** 0.25
LN_EPS = 1e-5
NEG = -1e30

LANES = 128
ROW_CHUNKS = D_MODEL // LANES
GROUP_CHUNKS = GROUP_COLS // LANES
PAIR_ORDER = ((0, 1), (0, 2), (0, 3), (1, 3), (1, 2), (3, 2))
N_CLASSES = (N_EXPERTS // GROUP_EXPERTS) * len(PAIR_ORDER)
PACKED_SUBLANES = D_MODEL // (2 * LANES)
SEQ_TILE = 1024
EXPERT_TILE = 256
COMBINE_TILE = 512
ROW_COPY_UNROLL = 8
VMEM_LIMIT = 56 * 1024 * 1024

F32 = jnp.float32
BF16 = jnp.bfloat16


def _params(semantics, vmem=VMEM_LIMIT):
    return pltpu.CompilerParams(dimension_semantics=semantics, vmem_limit_bytes=vmem)


MATMUL_SLAB = 512


def _pack_bf16_pair(lo, hi):
    lo_bits = pltpu.bitcast(lo.astype(BF16).astype(F32), jnp.uint32)
    hi_bits = pltpu.bitcast(hi.astype(BF16).astype(F32), jnp.uint32)
    return (lo_bits >> 16) | (hi_bits & jnp.uint32(0xFFFF0000))


def _unpack_bf16_pair(word):
    return pltpu.bitcast(word << 16, F32), pltpu.bitcast(word & jnp.uint32(0xFFFF0000), F32)


def _matmul_kernel(x_ref, w_ref, o_ref):
    xb = x_ref[...].astype(BF16)
    per_slab = MATMUL_SLAB // GROUP_COLS
    for s in range(w_ref.shape[1] // MATMUL_SLAB):
        y = jnp.dot(xb, w_ref[:, s * MATMUL_SLAB:(s + 1) * MATMUL_SLAB], preferred_element_type=F32)
        for c in range(per_slab):
            lo = y[:, (2 * c) * LANES:(2 * c + 1) * LANES]
            hi = y[:, (2 * c + 1) * LANES:(2 * c + 2) * LANES]
            o_ref[s * per_slab + c] = _pack_bf16_pair(lo, hi)


def _matmul(x, w, tm=512):
    n, k = x.shape
    _, m = w.shape
    return pl.pallas_call(
        _matmul_kernel,
        grid=(n // tm,),
        in_specs=[pl.BlockSpec((tm, k), lambda i: (i, 0)),
                  pl.BlockSpec((k, m), lambda i: (0, 0))],
        out_specs=pl.BlockSpec((m // GROUP_COLS, tm, LANES), lambda i: (0, i, 0)),
        out_shape=jax.ShapeDtypeStruct((m // GROUP_COLS, n, LANES), jnp.uint32),
        compiler_params=_params(("parallel",)),
        name="qkv_proj",
    )(x, w)


def _glu_kernel(x_ref, w_ref, b_ref, o_ref):
    xb = x_ref[...].astype(BF16)
    half = w_ref.shape[1] // 2
    for s in range(half // MATMUL_SLAB):
        cols = slice(s * MATMUL_SLAB, (s + 1) * MATMUL_SLAB)
        gate_cols = slice(half + s * MATMUL_SLAB, half + (s + 1) * MATMUL_SLAB)
        a = jnp.dot(xb, w_ref[:, cols], preferred_element_type=F32) + b_ref[:, cols]
        g = jnp.dot(xb, w_ref[:, gate_cols], preferred_element_type=F32) + b_ref[:, gate_cols]
        o_ref[:, cols] = a * jax.nn.sigmoid(g)


def _pw1_glu(x, w, b, tm=512):
    n, k = x.shape
    m = w.shape[1]
    return pl.pallas_call(
        _glu_kernel,
        grid=(n // tm,),
        in_specs=[pl.BlockSpec((tm, k), lambda i: (i, 0)),
                  pl.BlockSpec((k, m), lambda i: (0, 0)),
                  pl.BlockSpec((1, m), lambda i: (0, 0))],
        out_specs=pl.BlockSpec((tm, m // 2), lambda i: (i, 0)),
        out_shape=jax.ShapeDtypeStruct((n, m // 2), F32),
        compiler_params=_params(("parallel",)),
        name="pw1_glu",
    )(x, w, b)


def _head_masks(rows):
    lane = lax.broadcasted_iota(jnp.int32, (rows, GROUP_COLS), 1)
    return [(lane >= HEAD_DIM * h) & (lane < HEAD_DIM * (h + 1)) for h in range(GROUP_HEADS)]


def _stack_heads(q, masks):
    return jnp.concatenate([jnp.where(m, q, 0.0) for m in masks], axis=0)


def _unstack_heads(o4, masks, rows):
    out = jnp.where(masks[0], o4[0:rows], 0.0)
    for h in range(1, GROUP_HEADS):
        out = out + jnp.where(masks[h], o4[h * rows:(h + 1) * rows], 0.0)
    return out


def _softmax_pv(logits, v, with_lse):
    m = jnp.max(logits, axis=-1, keepdims=True)
    p = jnp.exp(logits - m)
    s = jnp.sum(p, axis=-1, keepdims=True)
    o4 = jnp.dot(p.astype(BF16), v, preferred_element_type=F32) * (1.0 / s)
    lse = (m + jnp.log(s)) if with_lse else None
    return o4, lse


def _dilated_kernel(q_ref, kp_ref, kc_ref, kn_ref, vp_ref, vc_ref, vn_ref, bias_ref,
                    o_ref, l_ref, *, dil):
    t = pl.program_id(1)
    chunk = QUERY_BLOCK * dil
    n_chunks = SEQ_TILE // chunk
    qb = QUERY_BLOCK
    masks = _head_masks(qb)
    col = lax.broadcasted_iota(jnp.int32, (GROUP_HEADS * qb, 3 * qb), 1)
    no_prev = jnp.where(col < qb, NEG, 0.0) * (t == 0).astype(F32)
    no_next = jnp.where(col >= 2 * qb, NEG, 0.0) * (t == pl.num_programs(1) - 1).astype(F32)
    bias = bias_ref[...]

    def sl(start):
        return pl.ds(start, qb) if dil == 1 else pl.ds(start, qb, stride=dil)

    def rows(ref, start):
        return jnp.concatenate(_unpack_bf16_pair(ref[0, sl(start), :]), axis=-1)

    def window(p_ref, c_ref, n_ref, c, r):
        parts = []
        for cc in (c - 1, c, c + 1):
            if cc < 0:
                parts.append(rows(p_ref, r))
            elif cc >= n_chunks:
                parts.append(rows(n_ref, r))
            else:
                parts.append(rows(c_ref, cc * chunk + r))
        return jnp.concatenate(parts, axis=0).astype(BF16)

    for c in range(n_chunks):
        for r in range(dil):
            base = c * chunk + r
            q4 = _stack_heads(rows(q_ref, base), masks).astype(BF16)
            k = window(kp_ref, kc_ref, kn_ref, c, r)
            v = window(vp_ref, vc_ref, vn_ref, c, r)
            logits = lax.dot_general(q4, k, (((1,), (1,)), ((), ())),
                                     preferred_element_type=F32) + bias
            if c == 0:
                logits = logits + no_prev
            if c == n_chunks - 1:
                logits = logits + no_next
            o4, lse = _softmax_pv(logits, v, True)
            o = _unstack_heads(o4, masks, qb)
            lb = _unstack_heads(jnp.broadcast_to(lse, (GROUP_HEADS * qb, GROUP_COLS)), masks, qb)
            o_ref[0, sl(base), :] = _pack_bf16_pair(o[:, 0:LANES], o[:, LANES:])
            for ch in range(GROUP_CHUNKS):
                l_ref[ch, sl(base), :] = lb[:, ch * LANES:(ch + 1) * LANES]


def _tile_spec(col_block, seq, chunks=1):
    tiles = seq // SEQ_TILE
    return pl.BlockSpec((chunks, SEQ_TILE, LANES), lambda b, t: (col_block, b * tiles + t, 0))


def _halo_specs(col_block, halo, seq):
    per_tile = SEQ_TILE // halo
    per_seq = seq // halo
    return [
        pl.BlockSpec((1, halo, LANES),
                     lambda b, t: (col_block, b * per_seq + jnp.maximum(t * per_tile - 1, 0), 0)),
        _tile_spec(col_block, seq),
        pl.BlockSpec((1, halo, LANES),
                     lambda b, t: (col_block, b * per_seq + jnp.minimum((t + 1) * per_tile, per_seq - 1), 0)),
    ]


def _dilated_attention(proj, bias, group, bsz, seq):
    dil = DILATIONS[group]
    n_groups = len(DILATIONS)
    q_col, k_col, v_col = group, n_groups + group, 2 * n_groups + group
    return pl.pallas_call(
        functools.partial(_dilated_kernel, dil=dil),
        grid=(bsz, seq // SEQ_TILE),
        in_specs=[_tile_spec(q_col, seq)]
        + _halo_specs(k_col, QUERY_BLOCK * dil, seq) + _halo_specs(v_col, QUERY_BLOCK * dil, seq)
        + [pl.BlockSpec(bias.shape, lambda b, t: (0, 0))],
        out_specs=[_tile_spec(0, seq), _tile_spec(0, seq, GROUP_CHUNKS)],
        out_shape=[jax.ShapeDtypeStruct((1, bsz * seq, LANES), jnp.uint32),
                   jax.ShapeDtypeStruct((GROUP_CHUNKS, bsz * seq, LANES), F32)],
        compiler_params=_params(("parallel", "arbitrary")),
        name=f"dilated_attn_d{dil}",
    )(proj, proj, proj, proj, proj, proj, proj, bias)


NA_HALO = 256
NA_KEYS = NA_ROWS * GRID_W


def _na_kernel(q_ref, kp_ref, kc_ref, kn_ref, vp_ref, vc_ref, vn_ref, bias_ref, o_ref,
               kwin, vwin, *, grid_rows):
    t = pl.program_id(1)
    tile_rows = SEQ_TILE // GRID_W
    half = NA_ROWS // 2
    for win, p_ref, c_ref, n_ref in ((kwin, kp_ref, kc_ref, kn_ref), (vwin, vp_ref, vc_ref, vn_ref)):
        for first, src in ((0, p_ref), (NA_HALO, c_ref), (NA_HALO + SEQ_TILE, n_ref)):
            n_rows = src.shape[1]
            for ch, part in enumerate(_unpack_bf16_pair(src[0])):
                win[first:first + n_rows, ch * LANES:(ch + 1) * LANES] = part.astype(BF16)
    masks = _head_masks(GRID_W)
    for i in range(tile_rows):
        row = t * tile_rows + i
        row0 = jnp.clip(row - half, 0, grid_rows - NA_ROWS)
        off = pl.multiple_of((row0 - (t * tile_rows - half)) * GRID_W, GRID_W)
        q = jnp.concatenate(_unpack_bf16_pair(q_ref[0, i * GRID_W:(i + 1) * GRID_W, :]), axis=-1)
        q4 = _stack_heads(q, masks).astype(BF16)
        k = kwin[pl.ds(off, NA_KEYS), :]
        v = vwin[pl.ds(off, NA_KEYS), :]
        logits = lax.dot_general(q4, k, (((1,), (1,)), ((), ())),
                                 preferred_element_type=F32) + bias_ref[row - row0]
        o4, _ = _softmax_pv(logits, v, False)
        o = _unstack_heads(o4, masks, GRID_W)
        o_ref[0, i * GRID_W:(i + 1) * GRID_W, :] = _pack_bf16_pair(o[:, 0:LANES], o[:, LANES:])


def _neighborhood_attention(proj, bias, bsz, seq):
    base = 3 * len(DILATIONS)
    win_rows = SEQ_TILE + 2 * NA_HALO
    return pl.pallas_call(
        functools.partial(_na_kernel, grid_rows=seq // GRID_W),
        grid=(bsz, seq // SEQ_TILE),
        in_specs=[_tile_spec(base, seq)]
        + _halo_specs(base + 1, NA_HALO, seq) + _halo_specs(base + 2, NA_HALO, seq)
        + [pl.BlockSpec(bias.shape, lambda b, t: (0, 0, 0))],
        out_specs=_tile_spec(0, seq),
        out_shape=jax.ShapeDtypeStruct((1, bsz * seq, LANES), jnp.uint32),
        scratch_shapes=[pltpu.VMEM((win_rows, GROUP_COLS), BF16),
                        pltpu.VMEM((win_rows, GROUP_COLS), BF16)],
        compiler_params=_params(("parallel", "arbitrary")),
        name="neighborhood_attn",
    )(proj, proj, proj, proj, proj, proj, proj, bias)


def _t5_bucket_table(dil):
    qb = QUERY_BLOCK
    rel = (np.arange(3 * qb)[None, :] - qb - np.arange(qb)[:, None])
    nb = T5_BUCKETS // 2
    max_exact = nb // 2
    n = np.abs(rel * dil)
    nf = np.maximum(n, 1).astype(np.float32)
    large = max_exact + (np.log(nf / np.float32(max_exact)) / np.float32(math.log(T5_MAX_DIST / max_exact))
                         * np.float32(nb - max_exact)).astype(np.int32)
    large = np.minimum(large, nb - 1)
    bucket = np.where(rel * dil > 0, nb, 0) + np.where(n < max_exact, n, large)
    return bucket.astype(np.int32), np.abs(rel) <= qb


def _dilated_bias(rel_bias, group):
    bucket, band = _t5_bucket_table(DILATIONS[group])
    tab = rel_bias[:, group * GROUP_HEADS:(group + 1) * GROUP_HEADS].astype(F32)
    onehot = (bucket[None, :, :] == np.arange(T5_BUCKETS)[:, None, None]).astype(np.float32)
    bias = jnp.einsum("bh,bqk->hqk", tab, onehot, precision=lax.Precision.HIGHEST)
    bias = jnp.where(band[None], bias, NEG)
    return bias.reshape(GROUP_HEADS * QUERY_BLOCK, 3 * QUERY_BLOCK)


def _na_bias(rpb):
    qc = np.arange(GRID_W)[:, None]
    kc = np.arange(GRID_W)[None, :]
    win0 = np.clip(qc - NA_COLS // 2, 0, GRID_W - NA_COLS)
    col_ok = (kc >= win0) & (kc < win0 + NA_COLS)
    dc = np.clip(kc - qc + NA_COLS - 1, 0, 2 * NA_COLS - 2)
    shift = np.arange(NA_ROWS)[:, None]
    kr = np.arange(NA_ROWS)[None, :]
    dr = kr - shift + NA_ROWS - 1
    pick_row = (dr[:, :, None] == np.arange(2 * NA_ROWS - 1)[None, None, :]).astype(np.float32)
    pick_col = (dc[None, :, :] == np.arange(2 * NA_COLS - 1)[:, None, None]).astype(np.float32)
    hi = lax.Precision.HIGHEST
    rows = jnp.einsum("hac,dka->hdkc", rpb.astype(F32), pick_row, precision=hi)
    b = jnp.einsum("hdkc,cqx->dhqkx", rows, pick_col, precision=hi)
    b = jnp.where(col_ok[None, None, :, None, :], b, NEG)
    return b.reshape(NA_ROWS, GROUP_HEADS * GRID_W, NA_KEYS)


def _layer_norm(z, g, b):
    mu = jnp.mean(z, axis=-1, keepdims=True)
    zc = z - mu
    var = jnp.mean(zc * zc, axis=-1, keepdims=True)
    return zc * lax.rsqrt(var + LN_EPS) * g + b


def _row(x, i):
    return x[i:i + 1, :]


def _route(logits, rbias):
    scores = jax.nn.sigmoid(logits)
    sel = scores + rbias
    ge = GROUP_EXPERTS
    n_groups = N_EXPERTS // ge
    best_sum, best_g = None, None
    for g in range(n_groups):
        vals = [_row(sel, g * ge + j) for j in range(ge)]
        top2 = None
        for a in range(ge):
            for b in range(a + 1, ge):
                pair = vals[a] + vals[b]
                top2 = pair if top2 is None else jnp.maximum(top2, pair)
        if best_sum is None:
            best_sum, best_g = top2, jnp.zeros_like(top2, dtype=jnp.int32)
        else:
            upd = top2 > best_sum
            best_g = jnp.where(upd, g, best_g)
            best_sum = jnp.where(upd, top2, best_sum)

    def in_group(x, j):
        out = _row(x, j)
        for g in range(1, n_groups):
            out = jnp.where(best_g == g, _row(x, g * ge + j), out)
        return out

    cand = [in_group(sel, j) for j in range(ge)]
    cand_score = [in_group(scores, j) for j in range(ge)]
    v1, j1, w1 = cand[0], jnp.zeros_like(best_g), cand_score[0]
    for j in range(1, ge):
        upd = cand[j] > v1
        v1 = jnp.where(upd, cand[j], v1)
        j1 = jnp.where(upd, j, j1)
        w1 = jnp.where(upd, cand_score[j], w1)
    v2 = jnp.full_like(v1, -jnp.inf)
    j2, w2 = jnp.zeros_like(best_g), jnp.zeros_like(w1)
    for j in range(ge):
        upd = (cand[j] > v2) & (j1 != j)
        v2 = jnp.where(upd, cand[j], v2)
        j2 = jnp.where(upd, j, j2)
        w2 = jnp.where(upd, cand_score[j], w2)
    wsum = w1 + w2
    g1, g2 = w1 / wsum, w2 / wsum
    lo, hi = jnp.minimum(j1, j2), jnp.maximum(j1, j2)
    pair = jnp.zeros_like(best_g)
    first = jnp.zeros_like(best_g)
    for q, (a, b) in enumerate(PAIR_ORDER):
        hit = (lo == min(a, b)) & (hi == max(a, b))
        pair = jnp.where(hit, q, pair)
        first = jnp.where(hit, a, first)
    is_first = j1 == first
    g_first = jnp.where(is_first, g1, g2)
    g_second = jnp.where(is_first, g2, g1)
    cls = (best_g * len(PAIR_ORDER) + pair).astype(F32)
    zero = jnp.zeros_like(w1)
    return jnp.concatenate([cls, g_first, g_second, zero, zero, zero, zero, zero], axis=0)


def _to_column(row_vec, eye):
    return jnp.sum(jnp.where(eye, row_vec, 0.0), axis=-1, keepdims=True)


def _post_tail(m, x_ref, g_ref, b_ref, wr_ref, rb_ref, x1_ref, route_ref):
    xn = _layer_norm(ALPHA * x_ref[...] + m, g_ref[...], b_ref[...])
    x1_ref[...] = xn
    xh = xn.astype(BF16)
    xl = (xn - xh.astype(F32)).astype(BF16)
    wr = wr_ref[...]
    contract = (((1,), (1,)), ((), ()))
    a = lax.dot_general(wr, xh, contract, preferred_element_type=F32)
    c = lax.dot_general(wr[0:N_EXPERTS], xl, contract, preferred_element_type=F32)
    logits = a[0:N_EXPERTS] + a[N_EXPERTS:] + c
    route_ref[...] = _route(logits, rb_ref[...])


def _pack_token_rows(x, route, dst):
    tm = x.shape[0]
    for s in range(PACKED_SUBLANES):
        lo = x[:, (2 * s) * LANES:(2 * s + 1) * LANES]
        hi = x[:, (2 * s + 1) * LANES:(2 * s + 2) * LANES]
        dst[pl.ds(s, tm, stride=ROW_CHUNKS), :] = _pack_bf16_pair(lo, hi)
    eye = (lax.broadcasted_iota(jnp.int32, (tm, tm), 0) == lax.broadcasted_iota(jnp.int32, (tm, tm), 1))
    g_first, g_second = _to_column(route[1:2, :], eye), _to_column(route[2:3, :], eye)
    lane = lax.broadcasted_iota(jnp.int32, (tm, LANES), 1)
    gates = jnp.where(lane == 0, g_first, jnp.where(lane == 1, g_second, 0.0))
    dst[pl.ds(PACKED_SUBLANES, tm, stride=ROW_CHUNKS), :] = pltpu.bitcast(gates, jnp.uint32)
    for s in range(PACKED_SUBLANES + 1, ROW_CHUNKS):
        dst[pl.ds(s, tm, stride=ROW_CHUNKS), :] = jnp.zeros((tm, LANES), jnp.uint32)


def _post_attn_kernel(o0_ref, l0_ref, o1_ref, l1_ref, o2_ref, l2_ref, ob_ref, w_ref,
                      x_ref, g_ref, b_ref, wr_ref, rb_ref, x1_ref, route_ref):
    def wide(ref):
        return jnp.concatenate([ref[ch] for ch in range(GROUP_CHUNKS)], axis=-1)

    def out(ref):
        return jnp.concatenate(_unpack_bf16_pair(ref[0]), axis=-1)

    l0, l1, l2 = wide(l0_ref), wide(l1_ref), wide(l2_ref)
    mx = jnp.maximum(jnp.maximum(l0, l1), l2)
    e0, e1, e2 = jnp.exp(l0 - mx), jnp.exp(l1 - mx), jnp.exp(l2 - mx)
    inv = 1.0 / (e0 + e1 + e2)
    lhs = jnp.concatenate([out(o0_ref) * (e0 * inv), out(o1_ref) * (e1 * inv),
                           out(o2_ref) * (e2 * inv), out(ob_ref)], axis=-1).astype(BF16)
    m = jnp.dot(lhs, w_ref[...], preferred_element_type=F32)
    _post_tail(m, x_ref, g_ref, b_ref, wr_ref, rb_ref, x1_ref, route_ref)


def _post_conv_kernel(u_ref, w_ref, wb_ref, x_ref, g_ref, b_ref, wr_ref, rb_ref,
                      x1_ref, route_ref):
    m = jnp.dot(u_ref[...], w_ref[...], preferred_element_type=F32) + wb_ref[...]
    _post_tail(m, x_ref, g_ref, b_ref, wr_ref, rb_ref, x1_ref, route_ref)


def _post_mixer(kernel, name, lhs_list, consts_front, x, g, b, wr, rb, tm=512):
    n, d = x.shape
    row = lambda i: (i, 0)
    const = lambda i: (0, 0)
    in_specs = [pl.BlockSpec((tm, a.shape[1]), row) if a.ndim == 2
                else pl.BlockSpec((a.shape[0], tm, LANES), lambda i: (0, i, 0)) for a in lhs_list]
    in_specs += [pl.BlockSpec(c.shape, const) for c in consts_front]
    in_specs += [pl.BlockSpec((tm, d), row)]
    in_specs += [pl.BlockSpec(c.shape, const) for c in (g, b, wr, rb)]
    return pl.pallas_call(
        kernel,
        grid=(n // tm,),
        in_specs=in_specs,
        out_specs=[pl.BlockSpec((tm, d), row),
                   pl.BlockSpec((8, tm), lambda i: (0, i))],
        out_shape=[jax.ShapeDtypeStruct((n, d), F32),
                   jax.ShapeDtypeStruct((8, n), F32)],
        compiler_params=_params(("parallel",)),
        name=name,
    )(*lhs_list, *consts_front, x, g, b, wr, rb)


CONV_SLAB = 256


def _conv_front_kernel(xp_ref, xc_ref, xn_ref, w_ref, wb_ref, taps_ref, cb_ref, g_ref, b_ref,
                       o_ref, win, acc, *, ts):
    t = pl.program_id(1)
    has_prev = (t > 0).astype(F32)
    has_next = (t < pl.num_programs(1) - 1).astype(F32)
    rows = ts + 2 * CONV_HALO
    row = lax.broadcasted_iota(jnp.int32, (rows, 1), 0)
    keep = jnp.where(row < CONV_HALO, has_prev, jnp.where(row >= CONV_HALO + ts, has_next, 1.0))
    xw = jnp.concatenate([xp_ref[...], xc_ref[...], xn_ref[...]], axis=0).astype(BF16)
    half = w_ref.shape[1] // 2
    chunks_per_slab = CONV_SLAB // LANES
    for s in range(half // CONV_SLAB):
        cols = slice(s * CONV_SLAB, (s + 1) * CONV_SLAB)
        gate_cols = slice(half + s * CONV_SLAB, half + (s + 1) * CONV_SLAB)
        a = jnp.dot(xw, w_ref[:, cols], preferred_element_type=F32) + wb_ref[:, cols]
        gt = jnp.dot(xw, w_ref[:, gate_cols], preferred_element_type=F32) + wb_ref[:, gate_cols]
        u = a * jax.nn.sigmoid(gt) * keep
        for c in range(chunks_per_slab):
            j = s * chunks_per_slab + c
            win[j] = u[:, c * LANES:(c + 1) * LANES]
            conv = jnp.zeros((ts, LANES), F32)
            for tap in range(CONV_WIDTH):
                start = CONV_HALO - CONV_WIDTH // 2 + tap
                conv = conv + win[j, pl.ds(start, ts), :] * taps_ref[j, tap:tap + 1, :]
            acc[j] = conv
    u = jnp.concatenate([acc[j] for j in range(ROW_CHUNKS)], axis=-1) + cb_ref[...]
    y = _layer_norm(u, g_ref[...], b_ref[...])
    o_ref[...] = (y * jax.nn.sigmoid(y)).astype(o_ref.dtype)


def _conv_front(x, w_pw1, b_pw1, w_dw, b_dw, g, b, ts=256):
    bsz, seq, d = x.shape
    per_tile = ts // CONV_HALO
    last = seq // CONV_HALO - 1
    taps = jnp.pad(w_dw, ((0, 32 - CONV_WIDTH), (0, 0))).reshape(32, ROW_CHUNKS, LANES).transpose(1, 0, 2)
    vec = pl.BlockSpec((1, d), lambda bb, t: (0, 0))
    return pl.pallas_call(
        functools.partial(_conv_front_kernel, ts=ts),
        grid=(bsz, seq // ts),
        in_specs=[pl.BlockSpec((None, CONV_HALO, d), lambda bb, t: (bb, jnp.maximum(t * per_tile - 1, 0), 0)),
                  pl.BlockSpec((None, ts, d), lambda bb, t: (bb, t, 0)),
                  pl.BlockSpec((None, CONV_HALO, d), lambda bb, t: (bb, jnp.minimum((t + 1) * per_tile, last), 0)),
                  pl.BlockSpec(w_pw1.shape, lambda bb, t: (0, 0)),
                  pl.BlockSpec(b_pw1.shape, lambda bb, t: (0, 0)),
                  pl.BlockSpec(taps.shape, lambda bb, t: (0, 0, 0)),
                  vec, vec, vec],
        out_specs=pl.BlockSpec((None, ts, d), lambda bb, t: (bb, t, 0)),
        out_shape=jax.ShapeDtypeStruct((bsz, seq, d), BF16),
        scratch_shapes=[pltpu.VMEM((ROW_CHUNKS, ts + 2 * CONV_HALO, LANES), F32),
                        pltpu.VMEM((ROW_CHUNKS, ts, LANES), F32)],
        compiler_params=_params(("parallel", "arbitrary")),
        name="conv_front",
    )(x, x, x, w_pw1, b_pw1, taps, b_dw, g, b)


def _row_gather_start(src_hbm, dst, sem, idx_ref, first, count):
    def body(r8, carry):
        for u in range(ROW_COPY_UNROLL):
            r = r8 * ROW_COPY_UNROLL + u
            tok = idx_ref[first + r]
            pltpu.make_async_copy(src_hbm.at[pl.ds(pl.multiple_of(tok * ROW_CHUNKS, ROW_CHUNKS), ROW_CHUNKS)],
                                  dst.at[pl.ds(pl.multiple_of(r * ROW_CHUNKS, ROW_CHUNKS), ROW_CHUNKS)],
                                  sem).start(priority=u % 2)
        return carry
    lax.fori_loop(0, count // ROW_COPY_UNROLL, body, 0)


def _row_gather_wait(src_hbm, dst, sem):
    pltpu.make_async_copy(src_hbm.at[pl.ds(0, dst.shape[0])], dst, sem).wait()


def _dispatch_kernel(dest_ref, tail_ref, n_tiles_ref, x_ref, route_ref, xs_hbm, zeros, stage, sem, zsem, *, tm):
    i = pl.program_id(0)
    n_steps = pl.num_programs(0)
    slot = i % 2
    tile_rows = EXPERT_TILE * ROW_CHUNKS
    all_tiles = xs_hbm.shape[0] // tile_rows

    def zero_tile(first_row):
        start = pl.multiple_of(first_row * ROW_CHUNKS, ROW_CHUNKS)
        return pltpu.make_async_copy(zeros, xs_hbm.at[pl.ds(start, tile_rows)], zsem)

    @pl.when(i == 0)
    def _():
        zeros[...] = jnp.zeros_like(zeros)
        for c in range(N_CLASSES):
            zero_tile(tail_ref[c]).start()
        for c in range(N_CLASSES):
            zero_tile(tail_ref[c]).wait()

        def unused(j, carry):
            cp = zero_tile(j * EXPERT_TILE)
            cp.start()
            cp.wait()
            return carry
        lax.fori_loop(n_tiles_ref[0], all_tiles, unused, 0)

    _pack_token_rows(x_ref[...], route_ref[...], stage.at[slot])

    def body(r8, carry):
        for u in range(ROW_COPY_UNROLL):
            r = r8 * ROW_COPY_UNROLL + u
            d = dest_ref[i * tm + r]
            pltpu.make_async_copy(stage.at[slot, pl.ds(pl.multiple_of(r * ROW_CHUNKS, ROW_CHUNKS), ROW_CHUNKS)],
                                  xs_hbm.at[pl.ds(pl.multiple_of(d * ROW_CHUNKS, ROW_CHUNKS), ROW_CHUNKS)],
                                  sem.at[slot]).start(priority=u % 2)
        return carry
    lax.fori_loop(0, tm // ROW_COPY_UNROLL, body, 0)

    def wait_step(s):
        pltpu.make_async_copy(stage.at[s], xs_hbm.at[pl.ds(0, tm * ROW_CHUNKS)], sem.at[s]).wait()

    @pl.when(i > 0)
    def _():
        wait_step(1 - slot)

    @pl.when(i == n_steps - 1)
    def _():
        wait_step(slot)


def _dispatch(x1, route, dest, tail_start, n_tiles, n_rows, tm=512):
    n_tokens, d = x1.shape
    grid_spec = pltpu.PrefetchScalarGridSpec(
        num_scalar_prefetch=3,
        grid=(n_tokens // tm,),
        in_specs=[pl.BlockSpec((tm, d), lambda i, dst, tail, nt: (i, 0)),
                  pl.BlockSpec((8, tm), lambda i, dst, tail, nt: (0, i))],
        out_specs=pl.BlockSpec(memory_space=pl.ANY),
        scratch_shapes=[pltpu.VMEM((EXPERT_TILE * ROW_CHUNKS, LANES), jnp.uint32),
                        pltpu.VMEM((2, tm * ROW_CHUNKS, LANES), jnp.uint32),
                        pltpu.SemaphoreType.DMA((2,)),
                        pltpu.SemaphoreType.DMA(())],
    )
    return pl.pallas_call(
        functools.partial(_dispatch_kernel, tm=tm),
        grid_spec=grid_spec,
        out_shape=jax.ShapeDtypeStruct((n_rows * ROW_CHUNKS, LANES), jnp.uint32),
        compiler_params=_params(("arbitrary",)),
        name="moe_dispatch",
    )(dest, tail_start, n_tiles, x1, route)


def _expert_kernel(run_ref, run_expert_ref, n_runs_ref, n_tiles_ref, x_ref,
                   wg_hbm, wu_hbm, wd_hbm, y_ref, stage_g, stage_u, stage_d,
                   wg1_bf, wu1_bf, wd1_bf, wg2_bf, wu2_bf, wd2_bf, sem, *, layer):
    i = pl.program_id(0)
    tm = EXPERT_TILE
    before = jnp.maximum(i - 1, 0)
    working = ((wg1_bf, wu1_bf, wd1_bf), (wg2_bf, wu2_bf, wd2_bf))

    def copies(role, run):
        slot = run % 2
        e = run_expert_ref[role, run]
        return [pltpu.make_async_copy(w.at[layer, e], st.at[role, slot], sem.at[role, slot])
                for w, st in ((wg_hbm, stage_g), (wu_hbm, stage_u), (wd_hbm, stage_d))]

    for role in range(2):
        run = run_ref[role, i]

        @pl.when(i == 0)
        def _():
            for cp in copies(role, 0):
                cp.start(priority=1)

        @pl.when((i == 0) | (run != run_ref[role, before]))
        def _():
            for cp in copies(role, run):
                cp.wait()
            slot = run % 2
            for st, wk in zip((stage_g, stage_u, stage_d), working[role]):
                wk[...] = st[role, slot].astype(BF16)

            @pl.when(run + 1 < n_runs_ref[role])
            def _():
                for cp in copies(role, run + 1):
                    cp.start(priority=1)

    @pl.when(i < n_tiles_ref[0])
    def _():
        halves = []
        for s in range(PACKED_SUBLANES):
            word = x_ref[pl.ds(s, tm, stride=ROW_CHUNKS), :]
            halves.append(pltpu.bitcast(word << 16, F32))
            halves.append(pltpu.bitcast(word & jnp.uint32(0xFFFF0000), F32))
        x = jnp.concatenate(halves, axis=-1).astype(BF16)
        gates = pltpu.bitcast(x_ref[pl.ds(PACKED_SUBLANES, tm, stride=ROW_CHUNKS), :], F32)

        def ffn(wg, wu, wd):
            hg = jnp.dot(x, wg[...], preferred_element_type=F32)
            hu = jnp.dot(x, wu[...], preferred_element_type=F32)
            hid = (hg * jax.nn.sigmoid(hg) * hu).astype(BF16)
            return jnp.dot(hid, wd[...], preferred_element_type=F32)

        y = gates[:, 0:1] * ffn(wg1_bf, wu1_bf, wd1_bf) + gates[:, 1:2] * ffn(wg2_bf, wu2_bf, wd2_bf)
        for j in range(ROW_CHUNKS):
            y_ref[pl.ds(j, tm, stride=ROW_CHUNKS), :] = y[:, j * LANES:(j + 1) * LANES]

    @pl.when(i >= n_tiles_ref[0])
    def _():
        y_ref[...] = jnp.zeros_like(y_ref)


def _experts(xs, w_gate, w_up, w_down, layer, tile_run, run_expert, n_runs, n_tiles):
    tm = EXPERT_TILE
    n_rows = xs.shape[0] // ROW_CHUNKS
    d, de = w_gate.shape[2], w_gate.shape[3]
    row_map = lambda i, run, rexp, nr, nt: (jnp.minimum(i, jnp.maximum(nt[0] - 1, 0)), 0)
    hbm = pl.BlockSpec(memory_space=pl.ANY)
    grid_spec = pltpu.PrefetchScalarGridSpec(
        num_scalar_prefetch=4,
        grid=(n_rows // tm,),
        in_specs=[pl.BlockSpec((tm * ROW_CHUNKS, LANES), row_map), hbm, hbm, hbm],
        out_specs=pl.BlockSpec((tm * ROW_CHUNKS, LANES), lambda i, run, rexp, nr, nt: (i, 0)),
        scratch_shapes=[pltpu.VMEM((2, 2, d, de), F32), pltpu.VMEM((2, 2, d, de), F32),
                        pltpu.VMEM((2, 2, de, d), F32),
                        pltpu.VMEM((d, de), BF16), pltpu.VMEM((d, de), BF16), pltpu.VMEM((de, d), BF16),
                        pltpu.VMEM((d, de), BF16), pltpu.VMEM((d, de), BF16), pltpu.VMEM((de, d), BF16),
                        pltpu.SemaphoreType.DMA((2, 2))],
    )
    return pl.pallas_call(
        functools.partial(_expert_kernel, layer=layer),
        grid_spec=grid_spec,
        out_shape=jax.ShapeDtypeStruct((n_rows * ROW_CHUNKS, LANES), F32),
        compiler_params=_params(("arbitrary",)),
        name="moe_experts",
    )(tile_run, run_expert, n_runs, n_tiles, xs, w_gate, w_up, w_down)


def _combine_kernel(dest_ref, y_hbm, x_ref, g_ref, b_ref, o_ref, ybuf, sem):
    i = pl.program_id(0)
    n_steps = pl.num_programs(0)
    slot = i % 2
    tm = COMBINE_TILE

    @pl.when(i == 0)
    def _():
        _row_gather_start(y_hbm, ybuf.at[0], sem.at[0], dest_ref, 0, tm)

    @pl.when(i + 1 < n_steps)
    def _():
        _row_gather_start(y_hbm, ybuf.at[1 - slot], sem.at[1 - slot], dest_ref, (i + 1) * tm, tm)

    _row_gather_wait(y_hbm, ybuf.at[slot], sem.at[slot])
    f = jnp.concatenate([ybuf[slot, pl.ds(j, tm, stride=ROW_CHUNKS), :] for j in range(ROW_CHUNKS)], axis=-1)
    o_ref[...] = _layer_norm(ALPHA * x_ref[...] + f, g_ref[...], b_ref[...])


def _combine(y_rows, x1, dest, g, b):
    n, d = x1.shape
    tm = COMBINE_TILE
    grid_spec = pltpu.PrefetchScalarGridSpec(
        num_scalar_prefetch=1,
        grid=(n // tm,),
        in_specs=[pl.BlockSpec(memory_space=pl.ANY),
                  pl.BlockSpec((tm, d), lambda i, dst: (i, 0)),
                  pl.BlockSpec((1, d), lambda i, dst: (0, 0)),
                  pl.BlockSpec((1, d), lambda i, dst: (0, 0))],
        out_specs=pl.BlockSpec((tm, d), lambda i, dst: (i, 0)),
        scratch_shapes=[pltpu.VMEM((2, tm * ROW_CHUNKS, LANES), F32),
                        pltpu.SemaphoreType.DMA((2,))],
    )
    return pl.pallas_call(
        _combine_kernel,
        grid_spec=grid_spec,
        out_shape=jax.ShapeDtypeStruct((n, d), F32),
        compiler_params=_params(("arbitrary",)),
        name="moe_combine",
    )(dest, y_rows, x1, g, b)


def _dispatch_plan(route, n_tokens):
    tm = EXPERT_TILE
    blk = LANES
    n_rows = n_tokens + N_CLASSES * tm
    hi = lax.Precision.HIGHEST
    onehot = (route[0][:, None] == jnp.arange(N_CLASSES, dtype=F32)[None, :]).astype(F32)
    blocks = onehot.reshape(n_tokens // blk, blk, N_CLASSES)
    tri = np.tril(np.ones((blk, blk), np.float32))
    local = jnp.einsum("ij,bjk->bik", tri, blocks, precision=hi)
    totals = local[:, -1, :]
    strict = np.tril(np.ones((n_tokens // blk,) * 2, np.float32), -1)
    before = jnp.dot(strict, totals, precision=hi)
    rank = jnp.sum((local + before[:, None, :]) * blocks, axis=-1).reshape(n_tokens) - 1.0
    counts = before[-1] + totals[-1]
    padded = jnp.ceil(counts / tm) * tm
    seg_end = jnp.dot(np.tril(np.ones((N_CLASSES,) * 2, np.float32)), padded, precision=hi)
    seg_start = seg_end - padded
    dest = (jnp.sum(onehot * seg_start[None, :], axis=-1) + rank).astype(jnp.int32)
    tile_start = jnp.arange(n_rows // tm, dtype=F32) * tm
    tile_class = jnp.sum((seg_end[None, :] <= tile_start[:, None]).astype(jnp.int32), axis=-1)
    last_class = jnp.max(jnp.where(padded > 0, jnp.arange(N_CLASSES), 0))
    tile_class = jnp.minimum(tile_class, last_class)
    tile_hot = (tile_class[:, None] == np.arange(N_CLASSES)[None, :]).astype(jnp.int32)
    n_pairs = len(PAIR_ORDER)
    expert_of = np.array([[(c // n_pairs) * GROUP_EXPERTS + PAIR_ORDER[c % n_pairs][role]
                           for c in range(N_CLASSES)] for role in range(2)])
    tile_expert = jnp.sum(tile_hot[None, :, :] * expert_of[:, None, :], axis=-1)
    n_all = n_rows // tm
    starts = jnp.concatenate([jnp.ones((2, 1), jnp.int32),
                              (tile_expert[:, 1:] != tile_expert[:, :-1]).astype(jnp.int32)], axis=1)
    tile_run = jnp.cumsum(starts, axis=1) - 1
    run_hot = (tile_run[:, :, None] == np.arange(n_all)[None, None, :]).astype(jnp.int32)
    run_expert = jnp.sum(run_hot * (starts * tile_expert)[:, :, None], axis=1)
    n_runs = tile_run[:, -1] + 1
    n_tiles = (seg_end[-1] / tm).astype(jnp.int32).reshape(1)
    tail_start = jnp.maximum(seg_end - tm, 0.0).astype(jnp.int32)
    plan = (tile_run.astype(jnp.int32), run_expert.astype(jnp.int32), n_runs.astype(jnp.int32), n_tiles)
    return plan, tail_start, dest, n_rows


def _moe(x1, route, w_gate, w_up, w_down, layer, g, b):
    n = x1.shape[0]
    plan, tail_start, dest, n_rows = _dispatch_plan(route, n)
    xs = _dispatch(x1, route, dest, tail_start, plan[-1], n_rows)
    y_rows = _experts(xs, w_gate, w_up, w_down, layer, *plan)
    return _combine(y_rows, x1, dest, g, b)


def _vec(v):
    return v.reshape(1, -1).astype(F32)


def kernel(x, w_in_attn, w_out_attn, rel_bias, rpb_2d, w_pw1, b_pw1, w_dw, b_dw, conv_ln_g, conv_ln_b, w_pw2, b_pw2, ln_mix_g, ln_mix_b, ln_ffn_g, ln_ffn_b, w_router, router_bias, w_gate, w_up, w_down):
    bsz, seq, d = x.shape
    n = bsz * seq
    h = x.reshape(n, d)
    wr_t = w_router.T.astype(F32)
    wr_hi = wr_t.astype(BF16)
    wr_lo = (wr_t - wr_hi.astype(F32)).astype(BF16)
    wr = jnp.concatenate([wr_hi, wr_lo], axis=0)
    rb = router_bias.reshape(N_EXPERTS, 1).astype(F32)
    n_dil_cols = len(DILATIONS) * GROUP_COLS
    q_cols = np.zeros((3 * d,), np.float32) + 1.0
    q_cols[0:n_dil_cols] = HEAD_DIM ** -0.5
    q_cols[3 * n_dil_cols:3 * n_dil_cols + GROUP_COLS] = HEAD_DIM ** -0.5

    for layer in range(DEPTH):
        i = layer // 2
        g_mix, b_mix = _vec(ln_mix_g[layer]), _vec(ln_mix_b[layer])
        if layer % 2 == 0:
            w_in = (w_in_attn[i] * q_cols[None, :]).astype(BF16)
            proj = _matmul(h, w_in)
            lhs = []
            for grp in range(len(DILATIONS)):
                lhs += _dilated_attention(proj, _dilated_bias(rel_bias, grp), grp, bsz, seq)
            lhs.append(_neighborhood_attention(proj, _na_bias(rpb_2d[i]), bsz, seq))
            x1, route = _post_mixer(_post_attn_kernel, "post_attn", lhs,
                                    [w_out_attn[i].astype(BF16)], h, g_mix, b_mix, wr, rb)
        else:
            u = _conv_front(h.reshape(bsz, seq, d), w_pw1[i].astype(BF16), _vec(b_pw1[i]),
                            w_dw[i].astype(F32), _vec(b_dw[i]), _vec(conv_ln_g[i]), _vec(conv_ln_b[i]))
            x1, route = _post_mixer(_post_conv_kernel, "post_conv", [u.reshape(n, d)],
                                    [w_pw2[i].astype(BF16), _vec(b_pw2[i])], h, g_mix, b_mix, wr, rb)
        h = _moe(x1, route, w_gate, w_up, w_down, layer,
                 _vec(ln_ffn_g[layer]), _vec(ln_ffn_b[layer]))
    return h.reshape(bsz, seq, d)
```

```python
import functools
import math

import jax
import jax.numpy as jnp
import numpy as np
from jax import lax
from jax.experimental import pallas as pl
from jax.experimental.pallas import tpu as pltpu

D_MODEL = 1024
HEAD_DIM = 64
GROUP_HEADS = 4
GROUP_COLS = GROUP_HEADS * HEAD_DIM
DILATIONS = (1, 4, 16)
QUERY_BLOCK = 64
GRID_W = 64
NA_ROWS = 8
NA_COLS = 16
T5_BUCKETS = 32
T5_MAX_DIST = 1024
CONV_WIDTH = 31
CONV_HALO = 16
N_EXPERTS = 16
GROUP_EXPERTS = 4
D_EXPERT = 512
DEPTH = 4
ALPHA = (2 * DEPTH) ** 0.25
LN_EPS = 1e-5
NEG = -1e30

LANES = 128
ROW_CHUNKS = D_MODEL // LANES
GROUP_CHUNKS = GROUP_COLS // LANES
PAIR_ORDER = ((0, 1), (0, 2), (0, 3), (1, 3), (1, 2), (3, 2))
N_CLASSES = (N_EXPERTS // GROUP_EXPERTS) * len(PAIR_ORDER)
PACKED_SUBLANES = D_MODEL // (2 * LANES)
GATE_LANE = 1
SEQ_TILE = 1024
EXPERT_TILE = 256
COMBINE_TILE = 512
ROW_COPY_UNROLL = 8
VMEM_LIMIT = 56 * 1024 * 1024

F32 = jnp.float32
BF16 = jnp.bfloat16


def _params(semantics, vmem=VMEM_LIMIT):
    return pltpu.CompilerParams(dimension_semantics=semantics, vmem_limit_bytes=vmem)


MATMUL_SLAB = 512


def _pack_bf16_pair(lo, hi):
    lo_bits = pltpu.bitcast(lo.astype(BF16).astype(F32), jnp.uint32)
    hi_bits = pltpu.bitcast(hi.astype(BF16).astype(F32), jnp.uint32)
    return (lo_bits >> 16) | (hi_bits & jnp.uint32(0xFFFF0000))


def _unpack_bf16_pair(word):
    return pltpu.bitcast(word << 16, F32), pltpu.bitcast(word & jnp.uint32(0xFFFF0000), F32)


def _matmul_kernel(x_ref, w_ref, o_ref):
    xb = x_ref[...].astype(BF16)
    per_slab = MATMUL_SLAB // GROUP_COLS
    for s in range(w_ref.shape[1] // MATMUL_SLAB):
        y = jnp.dot(xb, w_ref[:, s * MATMUL_SLAB:(s + 1) * MATMUL_SLAB], preferred_element_type=F32)
        for c in range(per_slab):
            lo = y[:, (2 * c) * LANES:(2 * c + 1) * LANES]
            hi = y[:, (2 * c + 1) * LANES:(2 * c + 2) * LANES]
            o_ref[s * per_slab + c] = _pack_bf16_pair(lo, hi)


def _matmul(x, w, tm=512):
    n, k = x.shape
    _, m = w.shape
    return pl.pallas_call(
        _matmul_kernel,
        grid=(n // tm,),
        in_specs=[pl.BlockSpec((tm, k), lambda i: (i, 0)),
                  pl.BlockSpec((k, m), lambda i: (0, 0))],
        out_specs=pl.BlockSpec((m // GROUP_COLS, tm, LANES), lambda i: (0, i, 0)),
        out_shape=jax.ShapeDtypeStruct((m // GROUP_COLS, n, LANES), jnp.uint32),
        compiler_params=_params(("parallel",)),
        name="qkv_proj",
    )(x, w)


def _glu_kernel(x_ref, w_ref, b_ref, o_ref):
    xb = x_ref[...].astype(BF16)
    half = w_ref.shape[1] // 2
    for s in range(half // MATMUL_SLAB):
        cols = slice(s * MATMUL_SLAB, (s + 1) * MATMUL_SLAB)
        gate_cols = slice(half + s * MATMUL_SLAB, half + (s + 1) * MATMUL_SLAB)
        a = jnp.dot(xb, w_ref[:, cols], preferred_element_type=F32) + b_ref[:, cols]
        g = jnp.dot(xb, w_ref[:, gate_cols], preferred_element_type=F32) + b_ref[:, gate_cols]
        o_ref[:, cols] = a * jax.nn.sigmoid(g)


def _pw1_glu(x, w, b, tm=512):
    n, k = x.shape
    m = w.shape[1]
    return pl.pallas_call(
        _glu_kernel,
        grid=(n // tm,),
        in_specs=[pl.BlockSpec((tm, k), lambda i: (i, 0)),
                  pl.BlockSpec((k, m), lambda i: (0, 0)),
                  pl.BlockSpec((1, m), lambda i: (0, 0))],
        out_specs=pl.BlockSpec((tm, m // 2), lambda i: (i, 0)),
        out_shape=jax.ShapeDtypeStruct((n, m // 2), F32),
        compiler_params=_params(("parallel",)),
        name="pw1_glu",
    )(x, w, b)


def _head_masks(rows):
    lane = lax.broadcasted_iota(jnp.int32, (rows, GROUP_COLS), 1)
    return [(lane >= HEAD_DIM * h) & (lane < HEAD_DIM * (h + 1)) for h in range(GROUP_HEADS)]


def _stack_heads(q, masks):
    return jnp.concatenate([jnp.where(m, q, 0.0) for m in masks], axis=0)


def _unstack_heads(o4, masks, rows):
    out = jnp.where(masks[0], o4[0:rows], 0.0)
    for h in range(1, GROUP_HEADS):
        out = out + jnp.where(masks[h], o4[h * rows:(h + 1) * rows], 0.0)
    return out


def _softmax_pv(logits, v, with_lse):
    m = jnp.max(logits, axis=-1, keepdims=True)
    p = jnp.exp(logits - m)
    s = jnp.sum(p, axis=-1, keepdims=True)
    o4 = jnp.dot(p.astype(BF16), v, preferred_element_type=F32) * (1.0 / s)
    lse = (m + jnp.log(s)) if with_lse else None
    return o4, lse


def _dilated_kernel(q_ref, kp_ref, kc_ref, kn_ref, vp_ref, vc_ref, vn_ref, bias_ref,
                    o_ref, l_ref, *, dil):
    t = pl.program_id(1)
    chunk = QUERY_BLOCK * dil
    n_chunks = SEQ_TILE // chunk
    qb = QUERY_BLOCK
    masks = _head_masks(qb)
    col = lax.broadcasted_iota(jnp.int32, (GROUP_HEADS * qb, 3 * qb), 1)
    no_prev = jnp.where(col < qb, NEG, 0.0) * (t == 0).astype(F32)
    no_next = jnp.where(col >= 2 * qb, NEG, 0.0) * (t == pl.num_programs(1) - 1).astype(F32)
    bias = bias_ref[...]

    def sl(start):
        return pl.ds(start, qb) if dil == 1 else pl.ds(start, qb, stride=dil)

    def rows(ref, start):
        return jnp.concatenate(_unpack_bf16_pair(ref[0, sl(start), :]), axis=-1)

    def window(p_ref, c_ref, n_ref, c, r):
        parts = []
        for cc in (c - 1, c, c + 1):
            if cc < 0:
                parts.append(rows(p_ref, r))
            elif cc >= n_chunks:
                parts.append(rows(n_ref, r))
            else:
                parts.append(rows(c_ref, cc * chunk + r))
        return jnp.concatenate(parts, axis=0).astype(BF16)

    for c in range(n_chunks):
        for r in range(dil):
            base = c * chunk + r
            q4 = _stack_heads(rows(q_ref, base), masks).astype(BF16)
            k = window(kp_ref, kc_ref, kn_ref, c, r)
            v = window(vp_ref, vc_ref, vn_ref, c, r)
            logits = lax.dot_general(q4, k, (((1,), (1,)), ((), ())),
                                     preferred_element_type=F32) + bias
            if c == 0:
                logits = logits + no_prev
            if c == n_chunks - 1:
                logits = logits + no_next
            o4, lse = _softmax_pv(logits, v, True)
            o = _unstack_heads(o4, masks, qb)
            lb = _unstack_heads(jnp.broadcast_to(lse, (GROUP_HEADS * qb, GROUP_COLS)), masks, qb)
            o_ref[0, sl(base), :] = _pack_bf16_pair(o[:, 0:LANES], o[:, LANES:])
            for ch in range(GROUP_CHUNKS):
                l_ref[ch, sl(base), :] = lb[:, ch * LANES:(ch + 1) * LANES]


def _tile_spec(col_block, seq, chunks=1):
    tiles = seq // SEQ_TILE
    return pl.BlockSpec((chunks, SEQ_TILE, LANES), lambda b, t: (col_block, b * tiles + t, 0))


def _halo_specs(col_block, halo, seq):
    per_tile = SEQ_TILE // halo
    per_seq = seq // halo
    return [
        pl.BlockSpec((1, halo, LANES),
                     lambda b, t: (col_block, b * per_seq + jnp.maximum(t * per_tile - 1, 0), 0)),
        _tile_spec(col_block, seq),
        pl.BlockSpec((1, halo, LANES),
                     lambda b, t: (col_block, b * per_seq + jnp.minimum((t + 1) * per_tile, per_seq - 1), 0)),
    ]


def _dilated_attention(proj, bias, group, bsz, seq):
    dil = DILATIONS[group]
    n_groups = len(DILATIONS)
    q_col, k_col, v_col = group, n_groups + group, 2 * n_groups + group
    return pl.pallas_call(
        functools.partial(_dilated_kernel, dil=dil),
        grid=(bsz, seq // SEQ_TILE),
        in_specs=[_tile_spec(q_col, seq)]
        + _halo_specs(k_col, QUERY_BLOCK * dil, seq) + _halo_specs(v_col, QUERY_BLOCK * dil, seq)
        + [pl.BlockSpec(bias.shape, lambda b, t: (0, 0))],
        out_specs=[_tile_spec(0, seq), _tile_spec(0, seq, GROUP_CHUNKS)],
        out_shape=[jax.ShapeDtypeStruct((1, bsz * seq, LANES), jnp.uint32),
                   jax.ShapeDtypeStruct((GROUP_CHUNKS, bsz * seq, LANES), F32)],
        compiler_params=_params(("parallel", "arbitrary")),
        name=f"dilated_attn_d{dil}",
    )(proj, proj, proj, proj, proj, proj, proj, bias)


NA_HALO = 256
NA_KEYS = NA_ROWS * GRID_W


def _na_kernel(q_ref, kp_ref, kc_ref, kn_ref, vp_ref, vc_ref, vn_ref, bias_ref, o_ref,
               kwin, vwin, *, grid_rows):
    t = pl.program_id(1)
    tile_rows = SEQ_TILE // GRID_W
    half = NA_ROWS // 2
    for win, p_ref, c_ref, n_ref in ((kwin, kp_ref, kc_ref, kn_ref), (vwin, vp_ref, vc_ref, vn_ref)):
        for first, src in ((0, p_ref), (NA_HALO, c_ref), (NA_HALO + SEQ_TILE, n_ref)):
            n_rows = src.shape[1]
            for ch, part in enumerate(_unpack_bf16_pair(src[0])):
                win[first:first + n_rows, ch * LANES:(ch + 1) * LANES] = part.astype(BF16)
    masks = _head_masks(GRID_W)
    for i in range(tile_rows):
        row = t * tile_rows + i
        row0 = jnp.clip(row - half, 0, grid_rows - NA_ROWS)
        off = pl.multiple_of((row0 - (t * tile_rows - half)) * GRID_W, GRID_W)
        q = jnp.concatenate(_unpack_bf16_pair(q_ref[0, i * GRID_W:(i + 1) * GRID_W, :]), axis=-1)
        q4 = _stack_heads(q, masks).astype(BF16)
        k = kwin[pl.ds(off, NA_KEYS), :]
        v = vwin[pl.ds(off, NA_KEYS), :]
        logits = lax.dot_general(q4, k, (((1,), (1,)), ((), ())),
                                 preferred_element_type=F32) + bias_ref[row - row0]
        o4, _ = _softmax_pv(logits, v, False)
        o = _unstack_heads(o4, masks, GRID_W)
        o_ref[0, i * GRID_W:(i + 1) * GRID_W, :] = _pack_bf16_pair(o[:, 0:LANES], o[:, LANES:])


def _neighborhood_attention(proj, bias, bsz, seq):
    base = 3 * len(DILATIONS)
    win_rows = SEQ_TILE + 2 * NA_HALO
    return pl.pallas_call(
        functools.partial(_na_kernel, grid_rows=seq // GRID_W),
        grid=(bsz, seq // SEQ_TILE),
        in_specs=[_tile_spec(base, seq)]
        + _halo_specs(base + 1, NA_HALO, seq) + _halo_specs(base + 2, NA_HALO, seq)
        + [pl.BlockSpec(bias.shape, lambda b, t: (0, 0, 0))],
        out_specs=_tile_spec(0, seq),
        out_shape=jax.ShapeDtypeStruct((1, bsz * seq, LANES), jnp.uint32),
        scratch_shapes=[pltpu.VMEM((win_rows, GROUP_COLS), BF16),
                        pltpu.VMEM((win_rows, GROUP_COLS), BF16)],
        compiler_params=_params(("parallel", "arbitrary")),
        name="neighborhood_attn",
    )(proj, proj, proj, proj, proj, proj, proj, bias)


def _t5_bucket_table(dil):
    qb = QUERY_BLOCK
    rel = (np.arange(3 * qb)[None, :] - qb - np.arange(qb)[:, None])
    nb = T5_BUCKETS // 2
    max_exact = nb // 2
    n = np.abs(rel * dil)
    nf = np.maximum(n, 1).astype(np.float32)
    large = max_exact + (np.log(nf / np.float32(max_exact)) / np.float32(math.log(T5_MAX_DIST / max_exact))
                         * np.float32(nb - max_exact)).astype(np.int32)
    large = np.minimum(large, nb - 1)
    bucket = np.where(rel * dil > 0, nb, 0) + np.where(n < max_exact, n, large)
    return bucket.astype(np.int32), np.abs(rel) <= qb


def _dilated_bias(rel_bias, group):
    bucket, band = _t5_bucket_table(DILATIONS[group])
    tab = rel_bias[:, group * GROUP_HEADS:(group + 1) * GROUP_HEADS].astype(F32)
    onehot = (bucket[None, :, :] == np.arange(T5_BUCKETS)[:, None, None]).astype(np.float32)
    bias = jnp.einsum("bh,bqk->hqk", tab, onehot, precision=lax.Precision.HIGHEST)
    bias = jnp.where(band[None], bias, NEG)
    return bias.reshape(GROUP_HEADS * QUERY_BLOCK, 3 * QUERY_BLOCK)


def _na_bias(rpb):
    qc = np.arange(GRID_W)[:, None]
    kc = np.arange(GRID_W)[None, :]
    win0 = np.clip(qc - NA_COLS // 2, 0, GRID_W - NA_COLS)
    col_ok = (kc >= win0) & (kc < win0 + NA_COLS)
    dc = np.clip(kc - qc + NA_COLS - 1, 0, 2 * NA_COLS - 2)
    shift = np.arange(NA_ROWS)[:, None]
    kr = np.arange(NA_ROWS)[None, :]
    dr = kr - shift + NA_ROWS - 1
    pick_row = (dr[:, :, None] == np.arange(2 * NA_ROWS - 1)[None, None, :]).astype(np.float32)
    pick_col = (dc[None, :, :] == np.arange(2 * NA_COLS - 1)[:, None, None]).astype(np.float32)
    hi = lax.Precision.HIGHEST
    rows = jnp.einsum("hac,dka->hdkc", rpb.astype(F32), pick_row, precision=hi)
    b = jnp.einsum("hdkc,cqx->dhqkx", rows, pick_col, precision=hi)
    b = jnp.where(col_ok[None, None, :, None, :], b, NEG)
    return b.reshape(NA_ROWS, GROUP_HEADS * GRID_W, NA_KEYS)


def _layer_norm(z, g, b):
    mu = jnp.mean(z, axis=-1, keepdims=True)
    zc = z - mu
    var = jnp.mean(zc * zc, axis=-1, keepdims=True)
    return zc * lax.rsqrt(var + LN_EPS) * g + b


def _row(x, i):
    return x[i:i + 1, :]


def _route(logits, rbias):
    scores = jax.nn.sigmoid(logits)
    sel = scores + rbias
    ge = GROUP_EXPERTS
    n_groups = N_EXPERTS // ge
    best_sum, best_g = None, None
    for g in range(n_groups):
        vals = [_row(sel, g * ge + j) for j in range(ge)]
        top2 = None
        for a in range(ge):
            for b in range(a + 1, ge):
                pair = vals[a] + vals[b]
                top2 = pair if top2 is None else jnp.maximum(top2, pair)
        if best_sum is None:
            best_sum, best_g = top2, jnp.zeros_like(top2, dtype=jnp.int32)
        else:
            upd = top2 > best_sum
            best_g = jnp.where(upd, g, best_g)
            best_sum = jnp.where(upd, top2, best_sum)

    def in_group(x, j):
        out = _row(x, j)
        for g in range(1, n_groups):
            out = jnp.where(best_g == g, _row(x, g * ge + j), out)
        return out

    cand = [in_group(sel, j) for j in range(ge)]
    cand_score = [in_group(scores, j) for j in range(ge)]
    v1, j1, w1 = cand[0], jnp.zeros_like(best_g), cand_score[0]
    for j in range(1, ge):
        upd = cand[j] > v1
        v1 = jnp.where(upd, cand[j], v1)
        j1 = jnp.where(upd, j, j1)
        w1 = jnp.where(upd, cand_score[j], w1)
    v2 = jnp.full_like(v1, -jnp.inf)
    j2, w2 = jnp.zeros_like(best_g), jnp.zeros_like(w1)
    for j in range(ge):
        upd = (cand[j] > v2) & (j1 != j)
        v2 = jnp.where(upd, cand[j], v2)
        j2 = jnp.where(upd, j, j2)
        w2 = jnp.where(upd, cand_score[j], w2)
    wsum = w1 + w2
    g1, g2 = w1 / wsum, w2 / wsum
    lo, hi = jnp.minimum(j1, j2), jnp.maximum(j1, j2)
    pair = jnp.zeros_like(best_g)
    first = jnp.zeros_like(best_g)
    for q, (a, b) in enumerate(PAIR_ORDER):
        hit = (lo == min(a, b)) & (hi == max(a, b))
        pair = jnp.where(hit, q, pair)
        first = jnp.where(hit, a, first)
    is_first = j1 == first
    g_first = jnp.where(is_first, g1, g2)
    g_second = jnp.where(is_first, g2, g1)
    cls = (best_g * len(PAIR_ORDER) + pair).astype(F32)
    zero = jnp.zeros_like(w1)
    return jnp.concatenate([cls, g_first, g_second, zero, zero, zero, zero, zero], axis=0)


def _post_tail(m, x_ref, g_ref, b_ref, wr_ref, rb_ref, x1_ref, route_ref):
    xn = _layer_norm(ALPHA * x_ref[...] + m, g_ref[...], b_ref[...])
    x1_ref[...] = xn
    xh = xn.astype(BF16)
    xl = (xn - xh.astype(F32)).astype(BF16)
    wr = wr_ref[...]
    contract = (((1,), (1,)), ((), ()))
    a = lax.dot_general(wr, xh, contract, preferred_element_type=F32)
    c = lax.dot_general(wr[0:N_EXPERTS], xl, contract, preferred_element_type=F32)
    logits = a[0:N_EXPERTS] + a[N_EXPERTS:] + c
    route_ref[...] = _route(logits, rb_ref[...])


def _pack_token_rows(x, route, dst):
    tm = x.shape[0]
    for s in range(PACKED_SUBLANES):
        lo = x[:, (2 * s) * LANES:(2 * s + 1) * LANES]
        hi = x[:, (2 * s + 1) * LANES:(2 * s + 2) * LANES]
        dst[pl.ds(s, tm, stride=ROW_CHUNKS), :] = _pack_bf16_pair(lo, hi)
    padded = jnp.concatenate([route, jnp.zeros((LANES - route.shape[0], tm), F32)], axis=0)
    dst[pl.ds(PACKED_SUBLANES, tm, stride=ROW_CHUNKS), :] = pltpu.bitcast(padded.T, jnp.uint32)
    for s in range(PACKED_SUBLANES + 1, ROW_CHUNKS):
        dst[pl.ds(s, tm, stride=ROW_CHUNKS), :] = jnp.zeros((tm, LANES), jnp.uint32)


def _post_attn_kernel(o0_ref, l0_ref, o1_ref, l1_ref, o2_ref, l2_ref, ob_ref, w_ref,
                      x_ref, g_ref, b_ref, wr_ref, rb_ref, x1_ref, route_ref):
    def wide(ref):
        return jnp.concatenate([ref[ch] for ch in range(GROUP_CHUNKS)], axis=-1)

    def out(ref):
        return jnp.concatenate(_unpack_bf16_pair(ref[0]), axis=-1)

    l0, l1, l2 = wide(l0_ref), wide(l1_ref), wide(l2_ref)
    mx = jnp.maximum(jnp.maximum(l0, l1), l2)
    e0, e1, e2 = jnp.exp(l0 - mx), jnp.exp(l1 - mx), jnp.exp(l2 - mx)
    inv = 1.0 / (e0 + e1 + e2)
    lhs = jnp.concatenate([out(o0_ref) * (e0 * inv), out(o1_ref) * (e1 * inv),
                           out(o2_ref) * (e2 * inv), out(ob_ref)], axis=-1).astype(BF16)
    m = jnp.dot(lhs, w_ref[...], preferred_element_type=F32)
    _post_tail(m, x_ref, g_ref, b_ref, wr_ref, rb_ref, x1_ref, route_ref)


def _post_conv_kernel(u_ref, w_ref, wb_ref, x_ref, g_ref, b_ref, wr_ref, rb_ref,
                      x1_ref, route_ref):
    m = jnp.dot(u_ref[...], w_ref[...], preferred_element_type=F32) + wb_ref[...]
    _post_tail(m, x_ref, g_ref, b_ref, wr_ref, rb_ref, x1_ref, route_ref)


def _post_mixer(kernel, name, lhs_list, consts_front, x, g, b, wr, rb, tm=512):
    n, d = x.shape
    row = lambda i: (i, 0)
    const = lambda i: (0, 0)
    in_specs = [pl.BlockSpec((tm, a.shape[1]), row) if a.ndim == 2
                else pl.BlockSpec((a.shape[0], tm, LANES), lambda i: (0, i, 0)) for a in lhs_list]
    in_specs += [pl.BlockSpec(c.shape, const) for c in consts_front]
    in_specs += [pl.BlockSpec((tm, d), row)]
    in_specs += [pl.BlockSpec(c.shape, const) for c in (g, b, wr, rb)]
    return pl.pallas_call(
        kernel,
        grid=(n // tm,),
        in_specs=in_specs,
        out_specs=[pl.BlockSpec((tm, d), row),
                   pl.BlockSpec((8, tm), lambda i: (0, i))],
        out_shape=[jax.ShapeDtypeStruct((n, d), F32),
                   jax.ShapeDtypeStruct((8, n), F32)],
        compiler_params=_params(("parallel",)),
        name=name,
    )(*lhs_list, *consts_front, x, g, b, wr, rb)


CONV_SLAB = 256


def _conv_front_kernel(xp_ref, xc_ref, xn_ref, w_ref, wb_ref, taps_ref, cb_ref, g_ref, b_ref,
                       o_ref, win, acc, *, ts):
    t = pl.program_id(1)
    has_prev = (t > 0).astype(F32)
    has_next = (t < pl.num_programs(1) - 1).astype(F32)
    rows = ts + 2 * CONV_HALO
    row = lax.broadcasted_iota(jnp.int32, (rows, 1), 0)
    keep = jnp.where(row < CONV_HALO, has_prev, jnp.where(row >= CONV_HALO + ts, has_next, 1.0))
    xw = jnp.concatenate([xp_ref[...], xc_ref[...], xn_ref[...]], axis=0).astype(BF16)
    half = w_ref.shape[1] // 2
    chunks_per_slab = CONV_SLAB // LANES
    for s in range(half // CONV_SLAB):
        cols = slice(s * CONV_SLAB, (s + 1) * CONV_SLAB)
        gate_cols = slice(half + s * CONV_SLAB, half + (s + 1) * CONV_SLAB)
        a = jnp.dot(xw, w_ref[:, cols], preferred_element_type=F32) + wb_ref[:, cols]
        gt = jnp.dot(xw, w_ref[:, gate_cols], preferred_element_type=F32) + wb_ref[:, gate_cols]
        u = a * jax.nn.sigmoid(gt) * keep
        for c in range(chunks_per_slab):
            j = s * chunks_per_slab + c
            win[j] = u[:, c * LANES:(c + 1) * LANES]
            conv = jnp.zeros((ts, LANES), F32)
            for tap in range(CONV_WIDTH):
                start = CONV_HALO - CONV_WIDTH // 2 + tap
                conv = conv + win[j, pl.ds(start, ts), :] * taps_ref[j, tap:tap + 1, :]
            acc[j] = conv
    u = jnp.concatenate([acc[j] for j in range(ROW_CHUNKS)], axis=-1) + cb_ref[...]
    y = _layer_norm(u, g_ref[...], b_ref[...])
    o_ref[...] = (y * jax.nn.sigmoid(y)).astype(o_ref.dtype)


def _conv_front(x, w_pw1, b_pw1, w_dw, b_dw, g, b, ts=256):
    bsz, seq, d = x.shape
    per_tile = ts // CONV_HALO
    last = seq // CONV_HALO - 1
    taps = jnp.pad(w_dw, ((0, 32 - CONV_WIDTH), (0, 0))).reshape(32, ROW_CHUNKS, LANES).transpose(1, 0, 2)
    vec = pl.BlockSpec((1, d), lambda bb, t: (0, 0))
    return pl.pallas_call(
        functools.partial(_conv_front_kernel, ts=ts),
        grid=(bsz, seq // ts),
        in_specs=[pl.BlockSpec((None, CONV_HALO, d), lambda bb, t: (bb, jnp.maximum(t * per_tile - 1, 0), 0)),
                  pl.BlockSpec((None, ts, d), lambda bb, t: (bb, t, 0)),
                  pl.BlockSpec((None, CONV_HALO, d), lambda bb, t: (bb, jnp.minimum((t + 1) * per_tile, last), 0)),
                  pl.BlockSpec(w_pw1.shape, lambda bb, t: (0, 0)),
                  pl.BlockSpec(b_pw1.shape, lambda bb, t: (0, 0)),
                  pl.BlockSpec(taps.shape, lambda bb, t: (0, 0, 0)),
                  vec, vec, vec],
        out_specs=pl.BlockSpec((None, ts, d), lambda bb, t: (bb, t, 0)),
        out_shape=jax.ShapeDtypeStruct((bsz, seq, d), BF16),
        scratch_shapes=[pltpu.VMEM((ROW_CHUNKS, ts + 2 * CONV_HALO, LANES), F32),
                        pltpu.VMEM((ROW_CHUNKS, ts, LANES), F32)],
        compiler_params=_params(("parallel", "arbitrary")),
        name="conv_front",
    )(x, x, x, w_pw1, b_pw1, taps, b_dw, g, b)


def _row_gather_start(src_hbm, dst, sem, idx_ref, first, count):
    def body(r8, carry):
        for u in range(ROW_COPY_UNROLL):
            r = r8 * ROW_COPY_UNROLL + u
            tok = idx_ref[first + r]
            pltpu.make_async_copy(src_hbm.at[pl.ds(pl.multiple_of(tok * ROW_CHUNKS, ROW_CHUNKS), ROW_CHUNKS)],
                                  dst.at[pl.ds(pl.multiple_of(r * ROW_CHUNKS, ROW_CHUNKS), ROW_CHUNKS)],
                                  sem).start(priority=u % 2)
        return carry
    lax.fori_loop(0, count // ROW_COPY_UNROLL, body, 0)


def _row_gather_wait(src_hbm, dst, sem):
    pltpu.make_async_copy(src_hbm.at[pl.ds(0, dst.shape[0])], dst, sem).wait()


def _dispatch_kernel(dest_ref, tail_ref, n_tiles_ref, x_ref, route_ref, xs_hbm, zeros, stage, sem, zsem, *, tm):
    i = pl.program_id(0)
    n_steps = pl.num_programs(0)
    slot = i % 2
    tile_rows = EXPERT_TILE * ROW_CHUNKS
    all_tiles = xs_hbm.shape[0] // tile_rows

    def zero_tile(first_row):
        start = pl.multiple_of(first_row * ROW_CHUNKS, ROW_CHUNKS)
        return pltpu.make_async_copy(zeros, xs_hbm.at[pl.ds(start, tile_rows)], zsem)

    @pl.when(i == 0)
    def _():
        zeros[...] = jnp.zeros_like(zeros)
        for c in range(N_CLASSES):
            zero_tile(tail_ref[c]).start()
        for c in range(N_CLASSES):
            zero_tile(tail_ref[c]).wait()

        def unused(j, carry):
            cp = zero_tile(j * EXPERT_TILE)
            cp.start()
            cp.wait()
            return carry
        lax.fori_loop(n_tiles_ref[0], all_tiles, unused, 0)

    _pack_token_rows(x_ref[...], route_ref[...], stage.at[slot])

    def body(r8, carry):
        for u in range(ROW_COPY_UNROLL):
            r = r8 * ROW_COPY_UNROLL + u
            d = dest_ref[i * tm + r]
            pltpu.make_async_copy(stage.at[slot, pl.ds(pl.multiple_of(r * ROW_CHUNKS, ROW_CHUNKS), ROW_CHUNKS)],
                                  xs_hbm.at[pl.ds(pl.multiple_of(d * ROW_CHUNKS, ROW_CHUNKS), ROW_CHUNKS)],
                                  sem.at[slot]).start(priority=u % 2)
        return carry
    lax.fori_loop(0, tm // ROW_COPY_UNROLL, body, 0)

    def wait_step(s):
        pltpu.make_async_copy(stage.at[s], xs_hbm.at[pl.ds(0, tm * ROW_CHUNKS)], sem.at[s]).wait()

    @pl.when(i > 0)
    def _():
        wait_step(1 - slot)

    @pl.when(i == n_steps - 1)
    def _():
        wait_step(slot)


def _dispatch(x1, route, dest, tail_start, n_tiles, n_rows, tm=512):
    n_tokens, d = x1.shape
    grid_spec = pltpu.PrefetchScalarGridSpec(
        num_scalar_prefetch=3,
        grid=(n_tokens // tm,),
        in_specs=[pl.BlockSpec((tm, d), lambda i, dst, tail, nt: (i, 0)),
                  pl.BlockSpec((8, tm), lambda i, dst, tail, nt: (0, i))],
        out_specs=pl.BlockSpec(memory_space=pl.ANY),
        scratch_shapes=[pltpu.VMEM((EXPERT_TILE * ROW_CHUNKS, LANES), jnp.uint32),
                        pltpu.VMEM((2, tm * ROW_CHUNKS, LANES), jnp.uint32),
                        pltpu.SemaphoreType.DMA((2,)),
                        pltpu.SemaphoreType.DMA(())],
    )
    return pl.pallas_call(
        functools.partial(_dispatch_kernel, tm=tm),
        grid_spec=grid_spec,
        out_shape=jax.ShapeDtypeStruct((n_rows * ROW_CHUNKS, LANES), jnp.uint32),
        compiler_params=_params(("arbitrary",)),
        name="moe_dispatch",
    )(dest, tail_start, n_tiles, x1, route)


def _expert_kernel(run_ref, run_expert_ref, n_runs_ref, n_tiles_ref, x_ref,
                   wg_hbm, wu_hbm, wd_hbm, y_ref, stage_g, stage_u, stage_d,
                   wg1_bf, wu1_bf, wd1_bf, wg2_bf, wu2_bf, wd2_bf, sem, *, layer):
    i = pl.program_id(0)
    tm = EXPERT_TILE
    before = jnp.maximum(i - 1, 0)
    working = ((wg1_bf, wu1_bf, wd1_bf), (wg2_bf, wu2_bf, wd2_bf))

    def copies(role, run):
        slot = run % 2
        e = run_expert_ref[role, run]
        return [pltpu.make_async_copy(w.at[layer, e], st.at[role, slot], sem.at[role, slot])
                for w, st in ((wg_hbm, stage_g), (wu_hbm, stage_u), (wd_hbm, stage_d))]

    for role in range(2):
        run = run_ref[role, i]

        @pl.when(i == 0)
        def _():
            for cp in copies(role, 0):
                cp.start(priority=1)

        @pl.when((i == 0) | (run != run_ref[role, before]))
        def _():
            for cp in copies(role, run):
                cp.wait()
            slot = run % 2
            for st, wk in zip((stage_g, stage_u, stage_d), working[role]):
                wk[...] = st[role, slot].astype(BF16)

            @pl.when(run + 1 < n_runs_ref[role])
            def _():
                for cp in copies(role, run + 1):
                    cp.start(priority=1)

    @pl.when(i < n_tiles_ref[0])
    def _():
        halves = []
        for s in range(PACKED_SUBLANES):
            word = x_ref[pl.ds(s, tm, stride=ROW_CHUNKS), :]
            halves.append(pltpu.bitcast(word << 16, F32))
            halves.append(pltpu.bitcast(word & jnp.uint32(0xFFFF0000), F32))
        x = jnp.concatenate(halves, axis=-1).astype(BF16)
        gates = pltpu.bitcast(x_ref[pl.ds(PACKED_SUBLANES, tm, stride=ROW_CHUNKS), :], F32)

        def ffn(wg, wu, wd):
            hg = jnp.dot(x, wg[...], preferred_element_type=F32)
            hu = jnp.dot(x, wu[...], preferred_element_type=F32)
            hid = (hg * jax.nn.sigmoid(hg) * hu).astype(BF16)
            return jnp.dot(hid, wd[...], preferred_element_type=F32)

        y = (gates[:, GATE_LANE:GATE_LANE + 1] * ffn(wg1_bf, wu1_bf, wd1_bf)
             + gates[:, GATE_LANE + 1:GATE_LANE + 2] * ffn(wg2_bf, wu2_bf, wd2_bf))
        for j in range(ROW_CHUNKS):
            y_ref[pl.ds(j, tm, stride=ROW_CHUNKS), :] = y[:, j * LANES:(j + 1) * LANES]

    @pl.when(i >= n_tiles_ref[0])
    def _():
        y_ref[...] = jnp.zeros_like(y_ref)


def _experts(xs, w_gate, w_up, w_down, layer, tile_run, run_expert, n_runs, n_tiles):
    tm = EXPERT_TILE
    n_rows = xs.shape[0] // ROW_CHUNKS
    d, de = w_gate.shape[2], w_gate.shape[3]
    row_map = lambda i, run, rexp, nr, nt: (jnp.minimum(i, jnp.maximum(nt[0] - 1, 0)), 0)
    hbm = pl.BlockSpec(memory_space=pl.ANY)
    grid_spec = pltpu.PrefetchScalarGridSpec(
        num_scalar_prefetch=4,
        grid=(n_rows // tm,),
        in_specs=[pl.BlockSpec((tm * ROW_CHUNKS, LANES), row_map), hbm, hbm, hbm],
        out_specs=pl.BlockSpec((tm * ROW_CHUNKS, LANES), lambda i, run, rexp, nr, nt: (i, 0)),
        scratch_shapes=[pltpu.VMEM((2, 2, d, de), F32), pltpu.VMEM((2, 2, d, de), F32),
                        pltpu.VMEM((2, 2, de, d), F32),
                        pltpu.VMEM((d, de), BF16), pltpu.VMEM((d, de), BF16), pltpu.VMEM((de, d), BF16),
                        pltpu.VMEM((d, de), BF16), pltpu.VMEM((d, de), BF16), pltpu.VMEM((de, d), BF16),
                        pltpu.SemaphoreType.DMA((2, 2))],
    )
    return pl.pallas_call(
        functools.partial(_expert_kernel, layer=layer),
        grid_spec=grid_spec,
        out_shape=jax.ShapeDtypeStruct((n_rows * ROW_CHUNKS, LANES), F32),
        compiler_params=_params(("arbitrary",)),
        name="moe_experts",
    )(tile_run, run_expert, n_runs, n_tiles, xs, w_gate, w_up, w_down)


def _combine_kernel(dest_ref, y_hbm, x_ref, g_ref, b_ref, o_ref, ybuf, sem):
    i = pl.program_id(0)
    n_steps = pl.num_programs(0)
    slot = i % 2
    tm = COMBINE_TILE

    @pl.when(i == 0)
    def _():
        _row_gather_start(y_hbm, ybuf.at[0], sem.at[0], dest_ref, 0, tm)

    @pl.when(i + 1 < n_steps)
    def _():
        _row_gather_start(y_hbm, ybuf.at[1 - slot], sem.at[1 - slot], dest_ref, (i + 1) * tm, tm)

    _row_gather_wait(y_hbm, ybuf.at[slot], sem.at[slot])
    f = jnp.concatenate([ybuf[slot, pl.ds(j, tm, stride=ROW_CHUNKS), :] for j in range(ROW_CHUNKS)], axis=-1)
    o_ref[...] = _layer_norm(ALPHA * x_ref[...] + f, g_ref[...], b_ref[...])


def _combine(y_rows, x1, dest, g, b):
    n, d = x1.shape
    tm = COMBINE_TILE
    grid_spec = pltpu.PrefetchScalarGridSpec(
        num_scalar_prefetch=1,
        grid=(n // tm,),
        in_specs=[pl.BlockSpec(memory_space=pl.ANY),
                  pl.BlockSpec((tm, d), lambda i, dst: (i, 0)),
                  pl.BlockSpec((1, d), lambda i, dst: (0, 0)),
                  pl.BlockSpec((1, d), lambda i, dst: (0, 0))],
        out_specs=pl.BlockSpec((tm, d), lambda i, dst: (i, 0)),
        scratch_shapes=[pltpu.VMEM((2, tm * ROW_CHUNKS, LANES), F32),
                        pltpu.SemaphoreType.DMA((2,))],
    )
    return pl.pallas_call(
        _combine_kernel,
        grid_spec=grid_spec,
        out_shape=jax.ShapeDtypeStruct((n, d), F32),
        compiler_params=_params(("arbitrary",)),
        name="moe_combine",
    )(dest, y_rows, x1, g, b)


def _dispatch_plan(route, n_tokens):
    tm = EXPERT_TILE
    blk = LANES
    n_rows = n_tokens + N_CLASSES * tm
    hi = lax.Precision.HIGHEST
    onehot = (route[0][:, None] == jnp.arange(N_CLASSES, dtype=F32)[None, :]).astype(F32)
    blocks = onehot.reshape(n_tokens // blk, blk, N_CLASSES)
    tri = np.tril(np.ones((blk, blk), np.float32))
    local = jnp.einsum("ij,bjk->bik", tri, blocks, precision=hi)
    totals = local[:, -1, :]
    strict = np.tril(np.ones((n_tokens // blk,) * 2, np.float32), -1)
    before = jnp.dot(strict, totals, precision=hi)
    rank = jnp.sum((local + before[:, None, :]) * blocks, axis=-1).reshape(n_tokens) - 1.0
    counts = before[-1] + totals[-1]
    padded = jnp.ceil(counts / tm) * tm
    seg_end = jnp.dot(np.tril(np.ones((N_CLASSES,) * 2, np.float32)), padded, precision=hi)
    seg_start = seg_end - padded
    dest = (jnp.sum(onehot * seg_start[None, :], axis=-1) + rank).astype(jnp.int32)
    tile_start = jnp.arange(n_rows // tm, dtype=F32) * tm
    tile_class = jnp.sum((seg_end[None, :] <= tile_start[:, None]).astype(jnp.int32), axis=-1)
    last_class = jnp.max(jnp.where(padded > 0, jnp.arange(N_CLASSES), 0))
    tile_class = jnp.minimum(tile_class, last_class)
    tile_hot = (tile_class[:, None] == np.arange(N_CLASSES)[None, :]).astype(jnp.int32)
    n_pairs = len(PAIR_ORDER)
    expert_of = np.array([[(c // n_pairs) * GROUP_EXPERTS + PAIR_ORDER[c % n_pairs][role]
                           for c in range(N_CLASSES)] for role in range(2)])
    tile_expert = jnp.sum(tile_hot[None, :, :] * expert_of[:, None, :], axis=-1)
    n_all = n_rows // tm
    starts = jnp.concatenate([jnp.ones((2, 1), jnp.int32),
                              (tile_expert[:, 1:] != tile_expert[:, :-1]).astype(jnp.int32)], axis=1)
    tile_run = jnp.cumsum(starts, axis=1) - 1
    run_hot = (tile_run[:, :, None] == np.arange(n_all)[None, None, :]).astype(jnp.int32)
    run_expert = jnp.sum(run_hot * (starts * tile_expert)[:, :, None], axis=1)
    n_runs = tile_run[:, -1] + 1
    n_tiles = (seg_end[-1] / tm).astype(jnp.int32).reshape(1)
    tail_start = jnp.maximum(seg_end - tm, 0.0).astype(jnp.int32)
    plan = (tile_run.astype(jnp.int32), run_expert.astype(jnp.int32), n_runs.astype(jnp.int32), n_tiles)
    return plan, tail_start, dest, n_rows


def _moe(x1, route, w_gate, w_up, w_down, layer, g, b):
    n = x1.shape[0]
    plan, tail_start, dest, n_rows = _dispatch_plan(route, n)
    xs = _dispatch(x1, route, dest, tail_start, plan[-1], n_rows)
    y_rows = _experts(xs, w_gate, w_up, w_down, layer, *plan)
    return _combine(y_rows, x1, dest, g, b)


def _vec(v):
    return v.reshape(1, -1).astype(F32)


def kernel(x, w_in_attn, w_out_attn, rel_bias, rpb_2d, w_pw1, b_pw1, w_dw, b_dw, conv_ln_g, conv_ln_b, w_pw2, b_pw2, ln_mix_g, ln_mix_b, ln_ffn_g, ln_ffn_b, w_router, router_bias, w_gate, w_up, w_down):
    bsz, seq, d = x.shape
    n = bsz * seq
    h = x.reshape(n, d)
    wr_t = w_router.T.astype(F32)
    wr_hi = wr_t.astype(BF16)
    wr_lo = (wr_t - wr_hi.astype(F32)).astype(BF16)
    wr = jnp.concatenate([wr_hi, wr_lo], axis=0)
    rb = router_bias.reshape(N_EXPERTS, 1).astype(F32)
    n_dil_cols = len(DILATIONS) * GROUP_COLS
    q_cols = np.zeros((3 * d,), np.float32) + 1.0
    q_cols[0:n_dil_cols] = HEAD_DIM ** -0.5
    q_cols[3 * n_dil_cols:3 * n_dil_cols + GROUP_COLS] = HEAD_DIM ** -0.5

    for layer in range(DEPTH):
        i = layer // 2
        g_mix, b_mix = _vec(ln_mix_g[layer]), _vec(ln_mix_b[layer])
        if layer % 2 == 0:
            w_in = (w_in_attn[i] * q_cols[None, :]).astype(BF16)
            proj = _matmul(h, w_in)
            lhs = []
            for grp in range(len(DILATIONS)):
                lhs += _dilated_attention(proj, _dilated_bias(rel_bias, grp), grp, bsz, seq)
            lhs.append(_neighborhood_attention(proj, _na_bias(rpb_2d[i]), bsz, seq))
            x1, route = _post_mixer(_post_attn_kernel, "post_attn", lhs,
                                    [w_out_attn[i].astype(BF16)], h, g_mix, b_mix, wr, rb)
        else:
            u = _conv_front(h.reshape(bsz, seq, d), w_pw1[i].astype(BF16), _vec(b_pw1[i]),
                            w_dw[i].astype(F32), _vec(b_dw[i]), _vec(conv_ln_g[i]), _vec(conv_ln_b[i]))
            x1, route = _post_mixer(_post_conv_kernel, "post_conv", [u.reshape(n, d)],
                                    [w_pw2[i].astype(BF16), _vec(b_pw2[i])], h, g_mix, b_mix, wr, rb)
        h = _moe(x1, route, w_gate, w_up, w_down, layer,
                 _vec(ln_ffn_g[layer]), _vec(ln_ffn_b[layer]))
    return h.reshape(bsz, seq, d)
```

```python
import functools
import math

import jax
import jax.numpy as jnp
import numpy as np
from jax import lax
from jax.experimental import pallas as pl
from jax.experimental.pallas import tpu as pltpu

D_MODEL = 1024
HEAD_DIM = 64
GROUP_HEADS = 4
GROUP_COLS = GROUP_HEADS * HEAD_DIM
DILATIONS = (1, 4, 16)
QUERY_BLOCK = 64
GRID_W = 64
NA_ROWS = 8
NA_COLS = 16
T5_BUCKETS = 32
T5_MAX_DIST = 1024
CONV_WIDTH = 31
CONV_HALO = 16
N_EXPERTS = 16
GROUP_EXPERTS = 4
D_EXPERT = 512
DEPTH = 4
ALPHA = (2 * DEPTH) ** 0.25
LN_EPS = 1e-5
NEG = -1e30

LANES = 128
ROW_CHUNKS = D_MODEL // LANES
GROUP_CHUNKS = GROUP_COLS // LANES
PAIR_ORDER = ((0, 1), (0, 2), (0, 3), (1, 3), (1, 2), (3, 2))
N_CLASSES = (N_EXPERTS // GROUP_EXPERTS) * len(PAIR_ORDER)
PACKED_SUBLANES = D_MODEL // (2 * LANES)
GATE_LANE = 1
SEQ_TILE = 1024
EXPERT_TILE = 256
COMBINE_TILE = 512
ROW_COPY_UNROLL = 8
VMEM_LIMIT = 56 * 1024 * 1024

F32 = jnp.float32
BF16 = jnp.bfloat16


def _params(semantics, vmem=VMEM_LIMIT):
    return pltpu.CompilerParams(dimension_semantics=semantics, vmem_limit_bytes=vmem)


MATMUL_SLAB = 512


def _pack_bf16_pair(lo, hi):
    lo_bits = pltpu.bitcast(lo.astype(BF16).astype(F32), jnp.uint32)
    hi_bits = pltpu.bitcast(hi.astype(BF16).astype(F32), jnp.uint32)
    return (lo_bits >> 16) | (hi_bits & jnp.uint32(0xFFFF0000))


def _unpack_bf16_pair(word):
    return pltpu.bitcast(word << 16, F32), pltpu.bitcast(word & jnp.uint32(0xFFFF0000), F32)


def _matmul_kernel(x_ref, w_ref, o_ref):
    xb = x_ref[...].astype(BF16)
    per_slab = MATMUL_SLAB // GROUP_COLS
    for s in range(w_ref.shape[1] // MATMUL_SLAB):
        y = jnp.dot(xb, w_ref[:, s * MATMUL_SLAB:(s + 1) * MATMUL_SLAB], preferred_element_type=F32)
        for c in range(per_slab):
            lo = y[:, (2 * c) * LANES:(2 * c + 1) * LANES]
            hi = y[:, (2 * c + 1) * LANES:(2 * c + 2) * LANES]
            o_ref[s * per_slab + c] = _pack_bf16_pair(lo, hi)


def _matmul(x, w, tm=1024):
    n, k = x.shape
    _, m = w.shape
    return pl.pallas_call(
        _matmul_kernel,
        grid=(n // tm,),
        in_specs=[pl.BlockSpec((tm, k), lambda i: (i, 0)),
                  pl.BlockSpec((k, m), lambda i: (0, 0))],
        out_specs=pl.BlockSpec((m // GROUP_COLS, tm, LANES), lambda i: (0, i, 0)),
        out_shape=jax.ShapeDtypeStruct((m // GROUP_COLS, n, LANES), jnp.uint32),
        compiler_params=_params(("parallel",)),
        name="qkv_proj",
    )(x, w)


def _head_masks(rows):
    lane = lax.broadcasted_iota(jnp.int32, (rows, GROUP_COLS), 1)
    return [(lane >= HEAD_DIM * h) & (lane < HEAD_DIM * (h + 1)) for h in range(GROUP_HEADS)]


def _stack_heads(q, masks):
    return jnp.concatenate([jnp.where(m, q, 0.0) for m in masks], axis=0)


def _unstack_heads(o4, masks, rows):
    out = jnp.where(masks[0], o4[0:rows], 0.0)
    for h in range(1, GROUP_HEADS):
        out = out + jnp.where(masks[h], o4[h * rows:(h + 1) * rows], 0.0)
    return out


def _softmax_pv(logits, v, with_lse):
    m = jnp.max(logits, axis=-1, keepdims=True)
    p = jnp.exp(logits - m)
    s = jnp.sum(p, axis=-1, keepdims=True)
    o4 = jnp.dot(p.astype(BF16), v, preferred_element_type=F32) * (1.0 / s)
    lse = (m + jnp.log(s)) if with_lse else None
    return o4, lse


def _dilated_kernel(q_ref, kp_ref, kc_ref, kn_ref, vp_ref, vc_ref, vn_ref, bias_ref,
                    o_ref, l_ref, *, dil):
    t = pl.program_id(1)
    chunk = QUERY_BLOCK * dil
    n_chunks = SEQ_TILE // chunk
    qb = QUERY_BLOCK
    masks = _head_masks(qb)
    col = lax.broadcasted_iota(jnp.int32, (GROUP_HEADS * qb, 3 * qb), 1)
    no_prev = jnp.where(col < qb, NEG, 0.0) * (t == 0).astype(F32)
    no_next = jnp.where(col >= 2 * qb, NEG, 0.0) * (t == pl.num_programs(1) - 1).astype(F32)
    bias = bias_ref[...]

    def sl(start):
        return pl.ds(start, qb) if dil == 1 else pl.ds(start, qb, stride=dil)

    def rows(ref, start):
        return jnp.concatenate(_unpack_bf16_pair(ref[0, sl(start), :]), axis=-1)

    def window(p_ref, c_ref, n_ref, c, r):
        parts = []
        for cc in (c - 1, c, c + 1):
            if cc < 0:
                parts.append(rows(p_ref, r))
            elif cc >= n_chunks:
                parts.append(rows(n_ref, r))
            else:
                parts.append(rows(c_ref, cc * chunk + r))
        return jnp.concatenate(parts, axis=0).astype(BF16)

    for c in range(n_chunks):
        for r in range(dil):
            base = c * chunk + r
            q4 = _stack_heads(rows(q_ref, base), masks).astype(BF16)
            k = window(kp_ref, kc_ref, kn_ref, c, r)
            v = window(vp_ref, vc_ref, vn_ref, c, r)
            logits = lax.dot_general(q4, k, (((1,), (1,)), ((), ())),
                                     preferred_element_type=F32) + bias
            if c == 0:
                logits = logits + no_prev
            if c == n_chunks - 1:
                logits = logits + no_next
            o4, lse = _softmax_pv(logits, v, True)
            o = _unstack_heads(o4, masks, qb)
            lb = _unstack_heads(jnp.broadcast_to(lse, (GROUP_HEADS * qb, GROUP_COLS)), masks, qb)
            o_ref[0, sl(base), :] = _pack_bf16_pair(o[:, 0:LANES], o[:, LANES:])
            for ch in range(GROUP_CHUNKS):
                l_ref[ch, sl(base), :] = lb[:, ch * LANES:(ch + 1) * LANES]


def _tile_spec(col_block, seq, chunks=1):
    tiles = seq // SEQ_TILE
    return pl.BlockSpec((chunks, SEQ_TILE, LANES), lambda b, t: (col_block, b * tiles + t, 0))


def _halo_specs(col_block, halo, seq):
    per_tile = SEQ_TILE // halo
    per_seq = seq // halo
    return [
        pl.BlockSpec((1, halo, LANES),
                     lambda b, t: (col_block, b * per_seq + jnp.maximum(t * per_tile - 1, 0), 0)),
        _tile_spec(col_block, seq),
        pl.BlockSpec((1, halo, LANES),
                     lambda b, t: (col_block, b * per_seq + jnp.minimum((t + 1) * per_tile, per_seq - 1), 0)),
    ]


def _dilated_attention(proj, bias, group, bsz, seq):
    dil = DILATIONS[group]
    n_groups = len(DILATIONS)
    q_col, k_col, v_col = group, n_groups + group, 2 * n_groups + group
    return pl.pallas_call(
        functools.partial(_dilated_kernel, dil=dil),
        grid=(bsz, seq // SEQ_TILE),
        in_specs=[_tile_spec(q_col, seq)]
        + _halo_specs(k_col, QUERY_BLOCK * dil, seq) + _halo_specs(v_col, QUERY_BLOCK * dil, seq)
        + [pl.BlockSpec(bias.shape, lambda b, t: (0, 0))],
        out_specs=[_tile_spec(0, seq), _tile_spec(0, seq, GROUP_CHUNKS)],
        out_shape=[jax.ShapeDtypeStruct((1, bsz * seq, LANES), jnp.uint32),
                   jax.ShapeDtypeStruct((GROUP_CHUNKS, bsz * seq, LANES), F32)],
        compiler_params=_params(("parallel", "arbitrary")),
        name=f"dilated_attn_d{dil}",
    )(proj, proj, proj, proj, proj, proj, proj, bias)


NA_HALO = 256
NA_KEYS = NA_ROWS * GRID_W


def _na_kernel(q_ref, kp_ref, kc_ref, kn_ref, vp_ref, vc_ref, vn_ref, bias_ref, o_ref,
               kwin, vwin, *, grid_rows):
    t = pl.program_id(1)
    tile_rows = SEQ_TILE // GRID_W
    half = NA_ROWS // 2
    for win, p_ref, c_ref, n_ref in ((kwin, kp_ref, kc_ref, kn_ref), (vwin, vp_ref, vc_ref, vn_ref)):
        for first, src in ((0, p_ref), (NA_HALO, c_ref), (NA_HALO + SEQ_TILE, n_ref)):
            n_rows = src.shape[1]
            for ch, part in enumerate(_unpack_bf16_pair(src[0])):
                win[first:first + n_rows, ch * LANES:(ch + 1) * LANES] = part.astype(BF16)
    masks = _head_masks(GRID_W)
    for i in range(tile_rows):
        row = t * tile_rows + i
        row0 = jnp.clip(row - half, 0, grid_rows - NA_ROWS)
        off = pl.multiple_of((row0 - (t * tile_rows - half)) * GRID_W, GRID_W)
        q = jnp.concatenate(_unpack_bf16_pair(q_ref[0, i * GRID_W:(i + 1) * GRID_W, :]), axis=-1)
        q4 = _stack_heads(q, masks).astype(BF16)
        k = kwin[pl.ds(off, NA_KEYS), :]
        v = vwin[pl.ds(off, NA_KEYS), :]
        logits = lax.dot_general(q4, k, (((1,), (1,)), ((), ())),
                                 preferred_element_type=F32) + bias_ref[row - row0]
        o4, _ = _softmax_pv(logits, v, False)
        o = _unstack_heads(o4, masks, GRID_W)
        o_ref[0, i * GRID_W:(i + 1) * GRID_W, :] = _pack_bf16_pair(o[:, 0:LANES], o[:, LANES:])


def _neighborhood_attention(proj, bias, bsz, seq):
    base = 3 * len(DILATIONS)
    win_rows = SEQ_TILE + 2 * NA_HALO
    return pl.pallas_call(
        functools.partial(_na_kernel, grid_rows=seq // GRID_W),
        grid=(bsz, seq // SEQ_TILE),
        in_specs=[_tile_spec(base, seq)]
        + _halo_specs(base + 1, NA_HALO, seq) + _halo_specs(base + 2, NA_HALO, seq)
        + [pl.BlockSpec(bias.shape, lambda b, t: (0, 0, 0))],
        out_specs=_tile_spec(0, seq),
        out_shape=jax.ShapeDtypeStruct((1, bsz * seq, LANES), jnp.uint32),
        scratch_shapes=[pltpu.VMEM((win_rows, GROUP_COLS), BF16),
                        pltpu.VMEM((win_rows, GROUP_COLS), BF16)],
        compiler_params=_params(("parallel", "arbitrary")),
        name="neighborhood_attn",
    )(proj, proj, proj, proj, proj, proj, proj, bias)


def _t5_bucket_table(dil):
    qb = QUERY_BLOCK
    rel = (np.arange(3 * qb)[None, :] - qb - np.arange(qb)[:, None])
    nb = T5_BUCKETS // 2
    max_exact = nb // 2
    n = np.abs(rel * dil)
    nf = np.maximum(n, 1).astype(np.float32)
    large = max_exact + (np.log(nf / np.float32(max_exact)) / np.float32(math.log(T5_MAX_DIST / max_exact))
                         * np.float32(nb - max_exact)).astype(np.int32)
    large = np.minimum(large, nb - 1)
    bucket = np.where(rel * dil > 0, nb, 0) + np.where(n < max_exact, n, large)
    return bucket.astype(np.int32), np.abs(rel) <= qb


def _dilated_bias(rel_bias, group):
    bucket, band = _t5_bucket_table(DILATIONS[group])
    tab = rel_bias[:, group * GROUP_HEADS:(group + 1) * GROUP_HEADS].astype(F32)
    onehot = (bucket[None, :, :] == np.arange(T5_BUCKETS)[:, None, None]).astype(np.float32)
    bias = jnp.einsum("bh,bqk->hqk", tab, onehot, precision=lax.Precision.HIGHEST)
    bias = jnp.where(band[None], bias, NEG)
    return bias.reshape(GROUP_HEADS * QUERY_BLOCK, 3 * QUERY_BLOCK)


def _na_bias(rpb):
    qc = np.arange(GRID_W)[:, None]
    kc = np.arange(GRID_W)[None, :]
    win0 = np.clip(qc - NA_COLS // 2, 0, GRID_W - NA_COLS)
    col_ok = (kc >= win0) & (kc < win0 + NA_COLS)
    dc = np.clip(kc - qc + NA_COLS - 1, 0, 2 * NA_COLS - 2)
    shift = np.arange(NA_ROWS)[:, None]
    kr = np.arange(NA_ROWS)[None, :]
    dr = kr - shift + NA_ROWS - 1
    pick_row = (dr[:, :, None] == np.arange(2 * NA_ROWS - 1)[None, None, :]).astype(np.float32)
    pick_col = (dc[None, :, :] == np.arange(2 * NA_COLS - 1)[:, None, None]).astype(np.float32)
    hi = lax.Precision.HIGHEST
    rows = jnp.einsum("hac,dka->hdkc", rpb.astype(F32), pick_row, precision=hi)
    b = jnp.einsum("hdkc,cqx->dhqkx", rows, pick_col, precision=hi)
    b = jnp.where(col_ok[None, None, :, None, :], b, NEG)
    return b.reshape(NA_ROWS, GROUP_HEADS * GRID_W, NA_KEYS)


def _layer_norm(z, g, b):
    mu = jnp.mean(z, axis=-1, keepdims=True)
    zc = z - mu
    var = jnp.mean(zc * zc, axis=-1, keepdims=True)
    return zc * lax.rsqrt(var + LN_EPS) * g + b


def _row(x, i):
    return x[i:i + 1, :]


def _route(logits, rbias):
    scores = jax.nn.sigmoid(logits)
    sel = scores + rbias
    ge = GROUP_EXPERTS
    n_groups = N_EXPERTS // ge
    best_sum, best_g = None, None
    for g in range(n_groups):
        vals = [_row(sel, g * ge + j) for j in range(ge)]
        top2 = None
        for a in range(ge):
            for b in range(a + 1, ge):
                pair = vals[a] + vals[b]
                top2 = pair if top2 is None else jnp.maximum(top2, pair)
        if best_sum is None:
            best_sum, best_g = top2, jnp.zeros_like(top2, dtype=jnp.int32)
        else:
            upd = top2 > best_sum
            best_g = jnp.where(upd, g, best_g)
            best_sum = jnp.where(upd, top2, best_sum)

    def in_group(x, j):
        out = _row(x, j)
        for g in range(1, n_groups):
            out = jnp.where(best_g == g, _row(x, g * ge + j), out)
        return out

    cand = [in_group(sel, j) for j in range(ge)]
    cand_score = [in_group(scores, j) for j in range(ge)]
    v1, j1, w1 = cand[0], jnp.zeros_like(best_g), cand_score[0]
    for j in range(1, ge):
        upd = cand[j] > v1
        v1 = jnp.where(upd, cand[j], v1)
        j1 = jnp.where(upd, j, j1)
        w1 = jnp.where(upd, cand_score[j], w1)
    v2 = jnp.full_like(v1, -jnp.inf)
    j2, w2 = jnp.zeros_like(best_g), jnp.zeros_like(w1)
    for j in range(ge):
        upd = (cand[j] > v2) & (j1 != j)
        v2 = jnp.where(upd, cand[j], v2)
        j2 = jnp.where(upd, j, j2)
        w2 = jnp.where(upd, cand_score[j], w2)
    wsum = w1 + w2
    g1, g2 = w1 / wsum, w2 / wsum
    lo, hi = jnp.minimum(j1, j2), jnp.maximum(j1, j2)
    pair = jnp.zeros_like(best_g)
    first = jnp.zeros_like(best_g)
    for q, (a, b) in enumerate(PAIR_ORDER):
        hit = (lo == min(a, b)) & (hi == max(a, b))
        pair = jnp.where(hit, q, pair)
        first = jnp.where(hit, a, first)
    is_first = j1 == first
    g_first = jnp.where(is_first, g1, g2)
    g_second = jnp.where(is_first, g2, g1)
    cls = (best_g * len(PAIR_ORDER) + pair).astype(F32)
    zero = jnp.zeros_like(w1)
    return jnp.concatenate([cls, g_first, g_second, zero, zero, zero, zero, zero], axis=0)


def _post_tail(m, x_ref, g_ref, b_ref, wr_ref, rb_ref, x1_ref, route_ref):
    xn = _layer_norm(ALPHA * x_ref[...] + m, g_ref[...], b_ref[...])
    x1_ref[...] = xn
    xh = xn.astype(BF16)
    xl = (xn - xh.astype(F32)).astype(BF16)
    wr = wr_ref[...]
    contract = (((1,), (1,)), ((), ()))
    a = lax.dot_general(wr, xh, contract, preferred_element_type=F32)
    c = lax.dot_general(wr[0:N_EXPERTS], xl, contract, preferred_element_type=F32)
    logits = a[0:N_EXPERTS] + a[N_EXPERTS:] + c
    route_ref[...] = _route(logits, rb_ref[...])


def _pack_token_rows(x, route, dst):
    tm = x.shape[0]
    for s in range(PACKED_SUBLANES):
        lo = x[:, (2 * s) * LANES:(2 * s + 1) * LANES]
        hi = x[:, (2 * s + 1) * LANES:(2 * s + 2) * LANES]
        dst[pl.ds(s, tm, stride=ROW_CHUNKS), :] = _pack_bf16_pair(lo, hi)
    padded = jnp.concatenate([route, jnp.zeros((LANES - route.shape[0], tm), F32)], axis=0)
    dst[pl.ds(PACKED_SUBLANES, tm, stride=ROW_CHUNKS), :] = pltpu.bitcast(padded.T, jnp.uint32)


def _zero_unused_sublanes(dst, tm):
    for s in range(PACKED_SUBLANES + 1, ROW_CHUNKS):
        dst[pl.ds(s, tm, stride=ROW_CHUNKS), :] = jnp.zeros((tm, LANES), jnp.uint32)


def _post_attn_kernel(o0_ref, l0_ref, o1_ref, l1_ref, o2_ref, l2_ref, ob_ref, w_ref,
                      x_ref, g_ref, b_ref, wr_ref, rb_ref, x1_ref, route_ref):
    def wide(ref):
        return jnp.concatenate([ref[ch] for ch in range(GROUP_CHUNKS)], axis=-1)

    def out(ref):
        return jnp.concatenate(_unpack_bf16_pair(ref[0]), axis=-1)

    l0, l1, l2 = wide(l0_ref), wide(l1_ref), wide(l2_ref)
    mx = jnp.maximum(jnp.maximum(l0, l1), l2)
    e0, e1, e2 = jnp.exp(l0 - mx), jnp.exp(l1 - mx), jnp.exp(l2 - mx)
    inv = 1.0 / (e0 + e1 + e2)
    lhs = jnp.concatenate([out(o0_ref) * (e0 * inv), out(o1_ref) * (e1 * inv),
                           out(o2_ref) * (e2 * inv), out(ob_ref)], axis=-1).astype(BF16)
    m = jnp.dot(lhs, w_ref[...], preferred_element_type=F32)
    _post_tail(m, x_ref, g_ref, b_ref, wr_ref, rb_ref, x1_ref, route_ref)


def _post_conv_kernel(u_ref, w_ref, wb_ref, x_ref, g_ref, b_ref, wr_ref, rb_ref,
                      x1_ref, route_ref):
    m = jnp.dot(u_ref[...], w_ref[...], preferred_element_type=F32) + wb_ref[...]
    _post_tail(m, x_ref, g_ref, b_ref, wr_ref, rb_ref, x1_ref, route_ref)


def _post_mixer(kernel, name, lhs_list, consts_front, x, g, b, wr, rb, tm=1024):
    n, d = x.shape
    row = lambda i: (i, 0)
    const = lambda i: (0, 0)
    in_specs = [pl.BlockSpec((tm, a.shape[1]), row) if a.ndim == 2
                else pl.BlockSpec((a.shape[0], tm, LANES), lambda i: (0, i, 0)) for a in lhs_list]
    in_specs += [pl.BlockSpec(c.shape, const) for c in consts_front]
    in_specs += [pl.BlockSpec((tm, d), row)]
    in_specs += [pl.BlockSpec(c.shape, const) for c in (g, b, wr, rb)]
    return pl.pallas_call(
        kernel,
        grid=(n // tm,),
        in_specs=in_specs,
        out_specs=[pl.BlockSpec((tm, d), row),
                   pl.BlockSpec((8, tm), lambda i: (0, i))],
        out_shape=[jax.ShapeDtypeStruct((n, d), F32),
                   jax.ShapeDtypeStruct((8, n), F32)],
        compiler_params=_params(("parallel",)),
        name=name,
    )(*lhs_list, *consts_front, x, g, b, wr, rb)


CONV_SLAB = 256


def _conv_front_kernel(xp_ref, xc_ref, xn_ref, w_ref, wb_ref, taps_ref, cb_ref, g_ref, b_ref,
                       o_ref, win, acc, *, ts):
    t = pl.program_id(1)
    has_prev = (t > 0).astype(F32)
    has_next = (t < pl.num_programs(1) - 1).astype(F32)
    rows = ts + 2 * CONV_HALO
    row = lax.broadcasted_iota(jnp.int32, (rows, 1), 0)
    keep = jnp.where(row < CONV_HALO, has_prev, jnp.where(row >= CONV_HALO + ts, has_next, 1.0))
    xw = jnp.concatenate([xp_ref[...], xc_ref[...], xn_ref[...]], axis=0).astype(BF16)
    half = w_ref.shape[1] // 2
    chunks_per_slab = CONV_SLAB // LANES
    for s in range(half // CONV_SLAB):
        cols = slice(s * CONV_SLAB, (s + 1) * CONV_SLAB)
        gate_cols = slice(half + s * CONV_SLAB, half + (s + 1) * CONV_SLAB)
        a = jnp.dot(xw, w_ref[:, cols], preferred_element_type=F32) + wb_ref[:, cols]
        gt = jnp.dot(xw, w_ref[:, gate_cols], preferred_element_type=F32) + wb_ref[:, gate_cols]
        u = a * jax.nn.sigmoid(gt) * keep
        for c in range(chunks_per_slab):
            j = s * chunks_per_slab + c
            win[j] = u[:, c * LANES:(c + 1) * LANES]
            conv = jnp.zeros((ts, LANES), F32)
            for tap in range(CONV_WIDTH):
                start = CONV_HALO - CONV_WIDTH // 2 + tap
                conv = conv + win[j, pl.ds(start, ts), :] * taps_ref[j, tap:tap + 1, :]
            acc[j] = conv
    u = jnp.concatenate([acc[j] for j in range(ROW_CHUNKS)], axis=-1) + cb_ref[...]
    y = _layer_norm(u, g_ref[...], b_ref[...])
    o_ref[...] = (y * jax.nn.sigmoid(y)).astype(o_ref.dtype)


def _conv_front(x, w_pw1, b_pw1, w_dw, b_dw, g, b, ts=256):
    bsz, seq, d = x.shape
    per_tile = ts // CONV_HALO
    last = seq // CONV_HALO - 1
    taps = jnp.pad(w_dw, ((0, 32 - CONV_WIDTH), (0, 0))).reshape(32, ROW_CHUNKS, LANES).transpose(1, 0, 2)
    vec = pl.BlockSpec((1, d), lambda bb, t: (0, 0))
    return pl.pallas_call(
        functools.partial(_conv_front_kernel, ts=ts),
        grid=(bsz, seq // ts),
        in_specs=[pl.BlockSpec((None, CONV_HALO, d), lambda bb, t: (bb, jnp.maximum(t * per_tile - 1, 0), 0)),
                  pl.BlockSpec((None, ts, d), lambda bb, t: (bb, t, 0)),
                  pl.BlockSpec((None, CONV_HALO, d), lambda bb, t: (bb, jnp.minimum((t + 1) * per_tile, last), 0)),
                  pl.BlockSpec(w_pw1.shape, lambda bb, t: (0, 0)),
                  pl.BlockSpec(b_pw1.shape, lambda bb, t: (0, 0)),
                  pl.BlockSpec(taps.shape, lambda bb, t: (0, 0, 0)),
                  vec, vec, vec],
        out_specs=pl.BlockSpec((None, ts, d), lambda bb, t: (bb, t, 0)),
        out_shape=jax.ShapeDtypeStruct((bsz, seq, d), BF16),
        scratch_shapes=[pltpu.VMEM((ROW_CHUNKS, ts + 2 * CONV_HALO, LANES), F32),
                        pltpu.VMEM((ROW_CHUNKS, ts, LANES), F32)],
        compiler_params=_params(("parallel", "arbitrary")),
        name="conv_front",
    )(x, x, x, w_pw1, b_pw1, taps, b_dw, g, b)


def _row_gather_start(src_hbm, dst, sem, idx_ref, first, count):
    def body(r8, carry):
        for u in range(ROW_COPY_UNROLL):
            r = r8 * ROW_COPY_UNROLL + u
            tok = idx_ref[first + r]
            pltpu.make_async_copy(src_hbm.at[pl.ds(pl.multiple_of(tok * ROW_CHUNKS, ROW_CHUNKS), ROW_CHUNKS)],
                                  dst.at[pl.ds(pl.multiple_of(r * ROW_CHUNKS, ROW_CHUNKS), ROW_CHUNKS)],
                                  sem).start(priority=u % 2)
        return carry
    lax.fori_loop(0, count // ROW_COPY_UNROLL, body, 0)


def _row_gather_wait(src_hbm, dst, sem):
    pltpu.make_async_copy(src_hbm.at[pl.ds(0, dst.shape[0])], dst, sem).wait()


def _dispatch_kernel(dest_ref, tail_ref, n_tiles_ref, x_ref, route_ref, xs_hbm, zeros, stage, sem, zsem, *, tm):
    i = pl.program_id(0)
    n_steps = pl.num_programs(0)
    slot = i % 2
    tile_rows = EXPERT_TILE * ROW_CHUNKS
    all_tiles = xs_hbm.shape[0] // tile_rows

    def zero_tile(first_row):
        start = pl.multiple_of(first_row * ROW_CHUNKS, ROW_CHUNKS)
        return pltpu.make_async_copy(zeros, xs_hbm.at[pl.ds(start, tile_rows)], zsem)

    @pl.when(i == 0)
    def _():
        zeros[...] = jnp.zeros_like(zeros)
        for c in range(N_CLASSES):
            zero_tile(tail_ref[c]).start()
        for c in range(N_CLASSES):
            zero_tile(tail_ref[c]).wait()

        def unused(j, carry):
            cp = zero_tile(j * EXPERT_TILE)
            cp.start()
            cp.wait()
            return carry
        lax.fori_loop(n_tiles_ref[0], all_tiles, unused, 0)

    @pl.when(i < 2)
    def _():
        _zero_unused_sublanes(stage.at[slot], tm)

    _pack_token_rows(x_ref[...], route_ref[...], stage.at[slot])

    def body(r8, carry):
        for u in range(ROW_COPY_UNROLL):
            r = r8 * ROW_COPY_UNROLL + u
            d = dest_ref[i * tm + r]
            pltpu.make_async_copy(stage.at[slot, pl.ds(pl.multiple_of(r * ROW_CHUNKS, ROW_CHUNKS), ROW_CHUNKS)],
                                  xs_hbm.at[pl.ds(pl.multiple_of(d * ROW_CHUNKS, ROW_CHUNKS), ROW_CHUNKS)],
                                  sem.at[slot]).start(priority=u % 2)
        return carry
    lax.fori_loop(0, tm // ROW_COPY_UNROLL, body, 0)

    def wait_step(s):
        pltpu.make_async_copy(stage.at[s], xs_hbm.at[pl.ds(0, tm * ROW_CHUNKS)], sem.at[s]).wait()

    @pl.when(i > 0)
    def _():
        wait_step(1 - slot)

    @pl.when(i == n_steps - 1)
    def _():
        wait_step(slot)


def _dispatch(x1, route, dest, tail_start, n_tiles, n_rows, tm=512):
    n_tokens, d = x1.shape
    grid_spec = pltpu.PrefetchScalarGridSpec(
        num_scalar_prefetch=3,
        grid=(n_tokens // tm,),
        in_specs=[pl.BlockSpec((tm, d), lambda i, dst, tail, nt: (i, 0)),
                  pl.BlockSpec((8, tm), lambda i, dst, tail, nt: (0, i))],
        out_specs=pl.BlockSpec(memory_space=pl.ANY),
        scratch_shapes=[pltpu.VMEM((EXPERT_TILE * ROW_CHUNKS, LANES), jnp.uint32),
                        pltpu.VMEM((2, tm * ROW_CHUNKS, LANES), jnp.uint32),
                        pltpu.SemaphoreType.DMA((2,)),
                        pltpu.SemaphoreType.DMA(())],
    )
    return pl.pallas_call(
        functools.partial(_dispatch_kernel, tm=tm),
        grid_spec=grid_spec,
        out_shape=jax.ShapeDtypeStruct((n_rows * ROW_CHUNKS, LANES), jnp.uint32),
        compiler_params=_params(("arbitrary",)),
        name="moe_dispatch",
    )(dest, tail_start, n_tiles, x1, route)


def _expert_kernel(run_ref, run_expert_ref, n_runs_ref, n_tiles_ref, x_ref,
                   wg_hbm, wu_hbm, wd_hbm, y_ref, stage_g, stage_u, stage_d,
                   wg1_bf, wu1_bf, wd1_bf, wg2_bf, wu2_bf, wd2_bf, sem, *, layer):
    i = pl.program_id(0)
    tm = EXPERT_TILE
    before = jnp.maximum(i - 1, 0)
    working = ((wg1_bf, wu1_bf, wd1_bf), (wg2_bf, wu2_bf, wd2_bf))

    def copies(role, run):
        slot = run % 2
        e = run_expert_ref[role, run]
        return [pltpu.make_async_copy(w.at[layer, e], st.at[role, slot], sem.at[role, slot])
                for w, st in ((wg_hbm, stage_g), (wu_hbm, stage_u), (wd_hbm, stage_d))]

    for role in range(2):
        run = run_ref[role, i]

        @pl.when(i == 0)
        def _():
            for cp in copies(role, 0):
                cp.start(priority=1)

        @pl.when((i == 0) | (run != run_ref[role, before]))
        def _():
            for cp in copies(role, run):
                cp.wait()
            slot = run % 2
            for st, wk in zip((stage_g, stage_u, stage_d), working[role]):
                wk[...] = st[role, slot].astype(BF16)

            @pl.when(run + 1 < n_runs_ref[role])
            def _():
                for cp in copies(role, run + 1):
                    cp.start(priority=1)

    @pl.when(i < n_tiles_ref[0])
    def _():
        halves = []
        for s in range(PACKED_SUBLANES):
            word = x_ref[pl.ds(s, tm, stride=ROW_CHUNKS), :]
            halves.append(pltpu.bitcast(word << 16, F32))
            halves.append(pltpu.bitcast(word & jnp.uint32(0xFFFF0000), F32))
        x = jnp.concatenate(halves, axis=-1).astype(BF16)
        gates = pltpu.bitcast(x_ref[pl.ds(PACKED_SUBLANES, tm, stride=ROW_CHUNKS), :], F32)

        def ffn(wg, wu, wd):
            hg = jnp.dot(x, wg[...], preferred_element_type=F32)
            hu = jnp.dot(x, wu[...], preferred_element_type=F32)
            hid = (hg * jax.nn.sigmoid(hg) * hu).astype(BF16)
            return jnp.dot(hid, wd[...], preferred_element_type=F32)

        y = (gates[:, GATE_LANE:GATE_LANE + 1] * ffn(wg1_bf, wu1_bf, wd1_bf)
             + gates[:, GATE_LANE + 1:GATE_LANE + 2] * ffn(wg2_bf, wu2_bf, wd2_bf))
        for j in range(ROW_CHUNKS):
            y_ref[pl.ds(j, tm, stride=ROW_CHUNKS), :] = y[:, j * LANES:(j + 1) * LANES]

    @pl.when(i >= n_tiles_ref[0])
    def _():
        y_ref[...] = jnp.zeros_like(y_ref)


def _experts(xs, w_gate, w_up, w_down, layer, tile_run, run_expert, n_runs, n_tiles):
    tm = EXPERT_TILE
    n_rows = xs.shape[0] // ROW_CHUNKS
    d, de = w_gate.shape[2], w_gate.shape[3]
    row_map = lambda i, run, rexp, nr, nt: (jnp.minimum(i, jnp.maximum(nt[0] - 1, 0)), 0)
    hbm = pl.BlockSpec(memory_space=pl.ANY)
    grid_spec = pltpu.PrefetchScalarGridSpec(
        num_scalar_prefetch=4,
        grid=(n_rows // tm,),
        in_specs=[pl.BlockSpec((tm * ROW_CHUNKS, LANES), row_map), hbm, hbm, hbm],
        out_specs=pl.BlockSpec((tm * ROW_CHUNKS, LANES), lambda i, run, rexp, nr, nt: (i, 0)),
        scratch_shapes=[pltpu.VMEM((2, 2, d, de), F32), pltpu.VMEM((2, 2, d, de), F32),
                        pltpu.VMEM((2, 2, de, d), F32),
                        pltpu.VMEM((d, de), BF16), pltpu.VMEM((d, de), BF16), pltpu.VMEM((de, d), BF16),
                        pltpu.VMEM((d, de), BF16), pltpu.VMEM((d, de), BF16), pltpu.VMEM((de, d), BF16),
                        pltpu.SemaphoreType.DMA((2, 2))],
    )
    return pl.pallas_call(
        functools.partial(_expert_kernel, layer=layer),
        grid_spec=grid_spec,
        out_shape=jax.ShapeDtypeStruct((n_rows * ROW_CHUNKS, LANES), F32),
        compiler_params=_params(("arbitrary",)),
        name="moe_experts",
    )(tile_run, run_expert, n_runs, n_tiles, xs, w_gate, w_up, w_down)


def _combine_kernel(dest_ref, y_hbm, x_ref, g_ref, b_ref, o_ref, ybuf, sem):
    i = pl.program_id(0)
    n_steps = pl.num_programs(0)
    slot = i % 2
    tm = COMBINE_TILE

    @pl.when(i == 0)
    def _():
        _row_gather_start(y_hbm, ybuf.at[0], sem.at[0], dest_ref, 0, tm)

    @pl.when(i + 1 < n_steps)
    def _():
        _row_gather_start(y_hbm, ybuf.at[1 - slot], sem.at[1 - slot], dest_ref, (i + 1) * tm, tm)

    _row_gather_wait(y_hbm, ybuf.at[slot], sem.at[slot])
    f = jnp.concatenate([ybuf[slot, pl.ds(j, tm, stride=ROW_CHUNKS), :] for j in range(ROW_CHUNKS)], axis=-1)
    o_ref[...] = _layer_norm(ALPHA * x_ref[...] + f, g_ref[...], b_ref[...])


def _combine(y_rows, x1, dest, g, b):
    n, d = x1.shape
    tm = COMBINE_TILE
    grid_spec = pltpu.PrefetchScalarGridSpec(
        num_scalar_prefetch=1,
        grid=(n // tm,),
        in_specs=[pl.BlockSpec(memory_space=pl.ANY),
                  pl.BlockSpec((tm, d), lambda i, dst: (i, 0)),
                  pl.BlockSpec((1, d), lambda i, dst: (0, 0)),
                  pl.BlockSpec((1, d), lambda i, dst: (0, 0))],
        out_specs=pl.BlockSpec((tm, d), lambda i, dst: (i, 0)),
        scratch_shapes=[pltpu.VMEM((2, tm * ROW_CHUNKS, LANES), F32),
                        pltpu.SemaphoreType.DMA((2,))],
    )
    return pl.pallas_call(
        _combine_kernel,
        grid_spec=grid_spec,
        out_shape=jax.ShapeDtypeStruct((n, d), F32),
        compiler_params=_params(("arbitrary",)),
        name="moe_combine",
    )(dest, y_rows, x1, g, b)


def _dispatch_plan(route, n_tokens):
    tm = EXPERT_TILE
    blk = LANES
    n_rows = n_tokens + N_CLASSES * tm
    hi = lax.Precision.HIGHEST
    onehot = (route[0][:, None] == jnp.arange(N_CLASSES, dtype=F32)[None, :]).astype(F32)
    blocks = onehot.reshape(n_tokens // blk, blk, N_CLASSES)
    tri = np.tril(np.ones((blk, blk), np.float32))
    local = jnp.einsum("ij,bjk->bik", tri, blocks, precision=hi)
    totals = local[:, -1, :]
    strict = np.tril(np.ones((n_tokens // blk,) * 2, np.float32), -1)
    before = jnp.dot(strict, totals, precision=hi)
    rank = jnp.sum((local + before[:, None, :]) * blocks, axis=-1).reshape(n_tokens) - 1.0
    counts = before[-1] + totals[-1]
    padded = jnp.ceil(counts / tm) * tm
    seg_end = jnp.dot(np.tril(np.ones((N_CLASSES,) * 2, np.float32)), padded, precision=hi)
    seg_start = seg_end - padded
    dest = (jnp.sum(onehot * seg_start[None, :], axis=-1) + rank).astype(jnp.int32)
    tile_start = jnp.arange(n_rows // tm, dtype=F32) * tm
    tile_class = jnp.sum((seg_end[None, :] <= tile_start[:, None]).astype(jnp.int32), axis=-1)
    last_class = jnp.max(jnp.where(padded > 0, jnp.arange(N_CLASSES), 0))
    tile_class = jnp.minimum(tile_class, last_class)
    tile_hot = (tile_class[:, None] == np.arange(N_CLASSES)[None, :]).astype(jnp.int32)
    n_pairs = len(PAIR_ORDER)
    expert_of = np.array([[(c // n_pairs) * GROUP_EXPERTS + PAIR_ORDER[c % n_pairs][role]
                           for c in range(N_CLASSES)] for role in range(2)])
    tile_expert = jnp.sum(tile_hot[None, :, :] * expert_of[:, None, :], axis=-1)
    n_all = n_rows // tm
    starts = jnp.concatenate([jnp.ones((2, 1), jnp.int32),
                              (tile_expert[:, 1:] != tile_expert[:, :-1]).astype(jnp.int32)], axis=1)
    tile_run = jnp.cumsum(starts, axis=1) - 1
    run_hot = (tile_run[:, :, None] == np.arange(n_all)[None, None, :]).astype(jnp.int32)
    run_expert = jnp.sum(run_hot * (starts * tile_expert)[:, :, None], axis=1)
    n_runs = tile_run[:, -1] + 1
    n_tiles = (seg_end[-1] / tm).astype(jnp.int32).reshape(1)
    tail_start = jnp.maximum(seg_end - tm, 0.0).astype(jnp.int32)
    plan = (tile_run.astype(jnp.int32), run_expert.astype(jnp.int32), n_runs.astype(jnp.int32), n_tiles)
    return plan, tail_start, dest, n_rows


def _moe(x1, route, w_gate, w_up, w_down, layer, g, b):
    n = x1.shape[0]
    plan, tail_start, dest, n_rows = _dispatch_plan(route, n)
    xs = _dispatch(x1, route, dest, tail_start, plan[-1], n_rows)
    y_rows = _experts(xs, w_gate, w_up, w_down, layer, *plan)
    return _combine(y_rows, x1, dest, g, b)


def _vec(v):
    return v.reshape(1, -1).astype(F32)


def kernel(x, w_in_attn, w_out_attn, rel_bias, rpb_2d, w_pw1, b_pw1, w_dw, b_dw, conv_ln_g, conv_ln_b, w_pw2, b_pw2, ln_mix_g, ln_mix_b, ln_ffn_g, ln_ffn_b, w_router, router_bias, w_gate, w_up, w_down):
    bsz, seq, d = x.shape
    n = bsz * seq
    h = x.reshape(n, d)
    wr_t = w_router.T.astype(F32)
    wr_hi = wr_t.astype(BF16)
    wr_lo = (wr_t - wr_hi.astype(F32)).astype(BF16)
    wr = jnp.concatenate([wr_hi, wr_lo], axis=0)
    rb = router_bias.reshape(N_EXPERTS, 1).astype(F32)
    n_dil_cols = len(DILATIONS) * GROUP_COLS
    q_cols = np.zeros((3 * d,), np.float32) + 1.0
    q_cols[0:n_dil_cols] = HEAD_DIM ** -0.5
    q_cols[3 * n_dil_cols:3 * n_dil_cols + GROUP_COLS] = HEAD_DIM ** -0.5

    for layer in range(DEPTH):
        i = layer // 2
        g_mix, b_mix = _vec(ln_mix_g[layer]), _vec(ln_mix_b[layer])
        if layer % 2 == 0:
            w_in = (w_in_attn[i] * q_cols[None, :]).astype(BF16)
            proj = _matmul(h, w_in)
            lhs = []
            for grp in range(len(DILATIONS)):
                lhs += _dilated_attention(proj, _dilated_bias(rel_bias, grp), grp, bsz, seq)
            lhs.append(_neighborhood_attention(proj, _na_bias(rpb_2d[i]), bsz, seq))
            x1, route = _post_mixer(_post_attn_kernel, "post_attn", lhs,
                                    [w_out_attn[i].astype(BF16)], h, g_mix, b_mix, wr, rb)
        else:
            u = _conv_front(h.reshape(bsz, seq, d), w_pw1[i].astype(BF16), _vec(b_pw1[i]),
                            w_dw[i].astype(F32), _vec(b_dw[i]), _vec(conv_ln_g[i]), _vec(conv_ln_b[i]))
            x1, route = _post_mixer(_post_conv_kernel, "post_conv", [u.reshape(n, d)],
                                    [w_pw2[i].astype(BF16), _vec(b_pw2[i])], h, g_mix, b_mix, wr, rb)
        h = _moe(x1, route, w_gate, w_up, w_down, layer,
                 _vec(ln_ffn_g[layer]), _vec(ln_ffn_b[layer]))
    return h.reshape(bsz, seq, d)
```

```python
import functools
import math

import jax
import jax.numpy as jnp
import numpy as np
from jax import lax
from jax.experimental import pallas as pl
from jax.experimental.pallas import tpu as pltpu

D_MODEL = 1024
HEAD_DIM = 64
GROUP_HEADS = 4
GROUP_COLS = GROUP_HEADS * HEAD_DIM
DILATIONS = (1, 4, 16)
QUERY_BLOCK = 64
GRID_W = 64
NA_ROWS = 8
NA_COLS = 16
T5_BUCKETS = 32
T5_MAX_DIST = 1024
CONV_WIDTH = 31
CONV_HALO = 16
N_EXPERTS = 16
GROUP_EXPERTS = 4
DEPTH = 4
ALPHA = (2 * DEPTH) ** 0.25
LN_EPS = 1e-5
NEG = -1e30

LANES = 128
ROW_CHUNKS = D_MODEL // LANES
GROUP_CHUNKS = GROUP_COLS // LANES
PAIR_ORDER = ((0, 1), (0, 2), (0, 3), (1, 3), (1, 2), (3, 2))
N_CLASSES = (N_EXPERTS // GROUP_EXPERTS) * len(PAIR_ORDER)
PACKED_SUBLANES = D_MODEL // (2 * LANES)
ROUTE_ROWS = 8
GATE_LANE = 1
SEQ_TILE = 2048
EXPERT_TILE = 256
COMBINE_TILE = 512
ROW_COPY_UNROLL = 8
VMEM_LIMIT = 56 * 1024 * 1024

F32 = jnp.float32
BF16 = jnp.bfloat16


def _params(semantics, vmem=VMEM_LIMIT):
    return pltpu.CompilerParams(dimension_semantics=semantics, vmem_limit_bytes=vmem)


MATMUL_SLAB = 512


def _pack_bf16_pair(lo, hi):
    lo_bits = pltpu.bitcast(lo.astype(BF16).astype(F32), jnp.uint32)
    hi_bits = pltpu.bitcast(hi.astype(BF16).astype(F32), jnp.uint32)
    return (lo_bits >> 16) | (hi_bits & jnp.uint32(0xFFFF0000))


def _unpack_bf16_pair(word):
    return pltpu.bitcast(word << 16, F32), pltpu.bitcast(word & jnp.uint32(0xFFFF0000), F32)


def _matmul_kernel(x_ref, w_ref, o_ref):
    xb = x_ref[...].astype(BF16)
    per_slab = MATMUL_SLAB // GROUP_COLS
    for s in range(w_ref.shape[1] // MATMUL_SLAB):
        y = jnp.dot(xb, w_ref[:, s * MATMUL_SLAB:(s + 1) * MATMUL_SLAB], preferred_element_type=F32)
        for c in range(per_slab):
            lo = y[:, (2 * c) * LANES:(2 * c + 1) * LANES]
            hi = y[:, (2 * c + 1) * LANES:(2 * c + 2) * LANES]
            o_ref[s * per_slab + c] = _pack_bf16_pair(lo, hi)


def _matmul(x, w, tm=1024):
    n, k = x.shape
    _, m = w.shape
    return pl.pallas_call(
        _matmul_kernel,
        grid=(n // tm,),
        in_specs=[pl.BlockSpec((tm, k), lambda i: (i, 0)),
                  pl.BlockSpec((k, m), lambda i: (0, 0))],
        out_specs=pl.BlockSpec((m // GROUP_COLS, tm, LANES), lambda i: (0, i, 0)),
        out_shape=jax.ShapeDtypeStruct((m // GROUP_COLS, n, LANES), jnp.uint32),
        compiler_params=_params(("parallel",)),
        name="qkv_proj",
    )(x, w)


def _head_masks(rows):
    lane = lax.broadcasted_iota(jnp.int32, (rows, GROUP_COLS), 1)
    return [(lane >= HEAD_DIM * h) & (lane < HEAD_DIM * (h + 1)) for h in range(GROUP_HEADS)]


def _stack_heads(q, masks):
    return jnp.concatenate([jnp.where(m, q, 0.0) for m in masks], axis=0)


def _unstack_heads(o4, masks, rows):
    out = jnp.where(masks[0], o4[0:rows], 0.0)
    for h in range(1, GROUP_HEADS):
        out = out + jnp.where(masks[h], o4[h * rows:(h + 1) * rows], 0.0)
    return out


def _softmax_pv(logits, v, with_lse):
    m = jnp.max(logits, axis=-1, keepdims=True)
    p = jnp.exp(logits - m)
    s = jnp.sum(p, axis=-1, keepdims=True)
    o4 = jnp.dot(p.astype(BF16), v, preferred_element_type=F32) * (1.0 / s)
    lse = (m + jnp.log(s)) if with_lse else None
    return o4, lse


def _dilated_kernel(q_ref, kp_ref, kc_ref, kn_ref, vp_ref, vc_ref, vn_ref, bias_ref,
                    o_ref, l_ref, *, dil):
    t = pl.program_id(1)
    chunk = QUERY_BLOCK * dil
    n_chunks = SEQ_TILE // chunk
    qb = QUERY_BLOCK
    masks = _head_masks(qb)
    col = lax.broadcasted_iota(jnp.int32, (GROUP_HEADS * qb, 3 * qb), 1)
    no_prev = jnp.where(col < qb, NEG, 0.0) * (t == 0).astype(F32)
    no_next = jnp.where(col >= 2 * qb, NEG, 0.0) * (t == pl.num_programs(1) - 1).astype(F32)
    bias = bias_ref[...]

    def sl(start):
        return pl.ds(start, qb) if dil == 1 else pl.ds(start, qb, stride=dil)

    def rows(ref, start):
        return jnp.concatenate(_unpack_bf16_pair(ref[0, sl(start), :]), axis=-1)

    def window(p_ref, c_ref, n_ref, c, r):
        parts = []
        for cc in (c - 1, c, c + 1):
            if cc < 0:
                parts.append(rows(p_ref, r))
            elif cc >= n_chunks:
                parts.append(rows(n_ref, r))
            else:
                parts.append(rows(c_ref, cc * chunk + r))
        return jnp.concatenate(parts, axis=0).astype(BF16)

    for c in range(n_chunks):
        for r in range(dil):
            base = c * chunk + r
            q4 = _stack_heads(rows(q_ref, base), masks).astype(BF16)
            k = window(kp_ref, kc_ref, kn_ref, c, r)
            v = window(vp_ref, vc_ref, vn_ref, c, r)
            logits = lax.dot_general(q4, k, (((1,), (1,)), ((), ())),
                                     preferred_element_type=F32) + bias
            if c == 0:
                logits = logits + no_prev
            if c == n_chunks - 1:
                logits = logits + no_next
            o4, lse = _softmax_pv(logits, v, True)
            o = _unstack_heads(o4, masks, qb)
            lb = _unstack_heads(jnp.broadcast_to(lse, (GROUP_HEADS * qb, GROUP_COLS)), masks, qb)
            o_ref[0, sl(base), :] = _pack_bf16_pair(o[:, 0:LANES], o[:, LANES:])
            for ch in range(GROUP_CHUNKS):
                l_ref[ch, sl(base), :] = lb[:, ch * LANES:(ch + 1) * LANES]


def _tile_spec(col_block, seq, chunks=1):
    tiles = seq // SEQ_TILE
    return pl.BlockSpec((chunks, SEQ_TILE, LANES), lambda b, t: (col_block, b * tiles + t, 0))


def _halo_specs(col_block, halo, seq):
    per_tile = SEQ_TILE // halo
    per_seq = seq // halo
    return [
        pl.BlockSpec((1, halo, LANES),
                     lambda b, t: (col_block, b * per_seq + jnp.maximum(t * per_tile - 1, 0), 0)),
        _tile_spec(col_block, seq),
        pl.BlockSpec((1, halo, LANES),
                     lambda b, t: (col_block, b * per_seq + jnp.minimum((t + 1) * per_tile, per_seq - 1), 0)),
    ]


def _dilated_attention(proj, bias, group, bsz, seq):
    dil = DILATIONS[group]
    n_groups = len(DILATIONS)
    q_col, k_col, v_col = group, n_groups + group, 2 * n_groups + group
    return pl.pallas_call(
        functools.partial(_dilated_kernel, dil=dil),
        grid=(bsz, seq // SEQ_TILE),
        in_specs=[_tile_spec(q_col, seq)]
        + _halo_specs(k_col, QUERY_BLOCK * dil, seq) + _halo_specs(v_col, QUERY_BLOCK * dil, seq)
        + [pl.BlockSpec(bias.shape, lambda b, t: (0, 0))],
        out_specs=[_tile_spec(0, seq), _tile_spec(0, seq, GROUP_CHUNKS)],
        out_shape=[jax.ShapeDtypeStruct((1, bsz * seq, LANES), jnp.uint32),
                   jax.ShapeDtypeStruct((GROUP_CHUNKS, bsz * seq, LANES), F32)],
        compiler_params=_params(("parallel", "arbitrary")),
        name=f"dilated_attn_d{dil}",
    )(proj, proj, proj, proj, proj, proj, proj, bias)


NA_HALO = 256
NA_KEYS = NA_ROWS * GRID_W


def _na_kernel(q_ref, kp_ref, kc_ref, kn_ref, vp_ref, vc_ref, vn_ref, bias_ref, o_ref,
               kwin, vwin, *, grid_rows):
    t = pl.program_id(1)
    tile_rows = SEQ_TILE // GRID_W
    half = NA_ROWS // 2
    for win, p_ref, c_ref, n_ref in ((kwin, kp_ref, kc_ref, kn_ref), (vwin, vp_ref, vc_ref, vn_ref)):
        for first, src in ((0, p_ref), (NA_HALO, c_ref), (NA_HALO + SEQ_TILE, n_ref)):
            n_rows = src.shape[1]
            for ch, part in enumerate(_unpack_bf16_pair(src[0])):
                win[first:first + n_rows, ch * LANES:(ch + 1) * LANES] = part.astype(BF16)
    masks = _head_masks(GRID_W)
    for i in range(tile_rows):
        row = t * tile_rows + i
        row0 = jnp.clip(row - half, 0, grid_rows - NA_ROWS)
        off = pl.multiple_of((row0 - (t * tile_rows - half)) * GRID_W, GRID_W)
        q = jnp.concatenate(_unpack_bf16_pair(q_ref[0, i * GRID_W:(i + 1) * GRID_W, :]), axis=-1)
        q4 = _stack_heads(q, masks).astype(BF16)
        k = kwin[pl.ds(off, NA_KEYS), :]
        v = vwin[pl.ds(off, NA_KEYS), :]
        logits = lax.dot_general(q4, k, (((1,), (1,)), ((), ())),
                                 preferred_element_type=F32) + bias_ref[row - row0]
        o4, _ = _softmax_pv(logits, v, False)
        o = _unstack_heads(o4, masks, GRID_W)
        o_ref[0, i * GRID_W:(i + 1) * GRID_W, :] = _pack_bf16_pair(o[:, 0:LANES], o[:, LANES:])


def _neighborhood_attention(proj, bias, bsz, seq):
    base = 3 * len(DILATIONS)
    win_rows = SEQ_TILE + 2 * NA_HALO
    return pl.pallas_call(
        functools.partial(_na_kernel, grid_rows=seq // GRID_W),
        grid=(bsz, seq // SEQ_TILE),
        in_specs=[_tile_spec(base, seq)]
        + _halo_specs(base + 1, NA_HALO, seq) + _halo_specs(base + 2, NA_HALO, seq)
        + [pl.BlockSpec(bias.shape, lambda b, t: (0, 0, 0))],
        out_specs=_tile_spec(0, seq),
        out_shape=jax.ShapeDtypeStruct((1, bsz * seq, LANES), jnp.uint32),
        scratch_shapes=[pltpu.VMEM((win_rows, GROUP_COLS), BF16),
                        pltpu.VMEM((win_rows, GROUP_COLS), BF16)],
        compiler_params=_params(("parallel", "arbitrary")),
        name="neighborhood_attn",
    )(proj, proj, proj, proj, proj, proj, proj, bias)


def _t5_bucket_table(dil):
    qb = QUERY_BLOCK
    rel = (np.arange(3 * qb)[None, :] - qb - np.arange(qb)[:, None])
    nb = T5_BUCKETS // 2
    max_exact = nb // 2
    n = np.abs(rel * dil)
    nf = np.maximum(n, 1).astype(np.float32)
    large = max_exact + (np.log(nf / np.float32(max_exact)) / np.float32(math.log(T5_MAX_DIST / max_exact))
                         * np.float32(nb - max_exact)).astype(np.int32)
    large = np.minimum(large, nb - 1)
    bucket = np.where(rel * dil > 0, nb, 0) + np.where(n < max_exact, n, large)
    return bucket.astype(np.int32), np.abs(rel) <= qb


def _dilated_bias(rel_bias, group):
    bucket, band = _t5_bucket_table(DILATIONS[group])
    tab = rel_bias[:, group * GROUP_HEADS:(group + 1) * GROUP_HEADS].astype(F32)
    onehot = (bucket[None, :, :] == np.arange(T5_BUCKETS)[:, None, None]).astype(np.float32)
    bias = jnp.einsum("bh,bqk->hqk", tab, onehot, precision=lax.Precision.HIGHEST)
    bias = jnp.where(band[None], bias, NEG)
    return bias.reshape(GROUP_HEADS * QUERY_BLOCK, 3 * QUERY_BLOCK)


def _na_bias(rpb):
    qc = np.arange(GRID_W)[:, None]
    kc = np.arange(GRID_W)[None, :]
    win0 = np.clip(qc - NA_COLS // 2, 0, GRID_W - NA_COLS)
    col_ok = (kc >= win0) & (kc < win0 + NA_COLS)
    dc = np.clip(kc - qc + NA_COLS - 1, 0, 2 * NA_COLS - 2)
    shift = np.arange(NA_ROWS)[:, None]
    kr = np.arange(NA_ROWS)[None, :]
    dr = kr - shift + NA_ROWS - 1
    pick_row = (dr[:, :, None] == np.arange(2 * NA_ROWS - 1)[None, None, :]).astype(np.float32)
    pick_col = (dc[None, :, :] == np.arange(2 * NA_COLS - 1)[:, None, None]).astype(np.float32)
    hi = lax.Precision.HIGHEST
    rows = jnp.einsum("hac,dka->hdkc", rpb.astype(F32), pick_row, precision=hi)
    b = jnp.einsum("hdkc,cqx->dhqkx", rows, pick_col, precision=hi)
    b = jnp.where(col_ok[None, None, :, None, :], b, NEG)
    return b.reshape(NA_ROWS, GROUP_HEADS * GRID_W, NA_KEYS)


def _layer_norm(z, g, b):
    mu = jnp.mean(z, axis=-1, keepdims=True)
    zc = z - mu
    var = jnp.mean(zc * zc, axis=-1, keepdims=True)
    return zc * lax.rsqrt(var + LN_EPS) * g + b


def _row(x, i):
    return x[i:i + 1, :]


def _route(logits, rbias):
    scores = jax.nn.sigmoid(logits)
    sel = scores + rbias
    ge = GROUP_EXPERTS
    n_groups = N_EXPERTS // ge
    best_sum, best_g = None, None
    for g in range(n_groups):
        vals = [_row(sel, g * ge + j) for j in range(ge)]
        top2 = None
        for a in range(ge):
            for b in range(a + 1, ge):
                pair = vals[a] + vals[b]
                top2 = pair if top2 is None else jnp.maximum(top2, pair)
        if best_sum is None:
            best_sum, best_g = top2, jnp.zeros_like(top2, dtype=jnp.int32)
        else:
            upd = top2 > best_sum
            best_g = jnp.where(upd, g, best_g)
            best_sum = jnp.where(upd, top2, best_sum)

    def in_group(x, j):
        out = _row(x, j)
        for g in range(1, n_groups):
            out = jnp.where(best_g == g, _row(x, g * ge + j), out)
        return out

    cand = [in_group(sel, j) for j in range(ge)]
    cand_score = [in_group(scores, j) for j in range(ge)]
    v1, j1, w1 = cand[0], jnp.zeros_like(best_g), cand_score[0]
    for j in range(1, ge):
        upd = cand[j] > v1
        v1 = jnp.where(upd, cand[j], v1)
        j1 = jnp.where(upd, j, j1)
        w1 = jnp.where(upd, cand_score[j], w1)
    v2 = jnp.full_like(v1, -jnp.inf)
    j2, w2 = jnp.zeros_like(best_g), jnp.zeros_like(w1)
    for j in range(ge):
        upd = (cand[j] > v2) & (j1 != j)
        v2 = jnp.where(upd, cand[j], v2)
        j2 = jnp.where(upd, j, j2)
        w2 = jnp.where(upd, cand_score[j], w2)
    wsum = w1 + w2
    g1, g2 = w1 / wsum, w2 / wsum
    lo, hi = jnp.minimum(j1, j2), jnp.maximum(j1, j2)
    pair = jnp.zeros_like(best_g)
    first = jnp.zeros_like(best_g)
    for q, (a, b) in enumerate(PAIR_ORDER):
        hit = (lo == min(a, b)) & (hi == max(a, b))
        pair = jnp.where(hit, q, pair)
        first = jnp.where(hit, a, first)
    is_first = j1 == first
    g_first = jnp.where(is_first, g1, g2)
    g_second = jnp.where(is_first, g2, g1)
    cls = (best_g * len(PAIR_ORDER) + pair).astype(F32)
    rows = [cls, g_first, g_second]
    return jnp.concatenate(rows + [jnp.zeros_like(w1)] * (ROUTE_ROWS - len(rows)), axis=0)


def _post_tail(m, x_ref, g_ref, b_ref, wr_ref, rb_ref, x1_ref, route_ref):
    xn = _layer_norm(ALPHA * x_ref[...] + m, g_ref[...], b_ref[...])
    x1_ref[...] = xn
    xh = xn.astype(BF16)
    xl = (xn - xh.astype(F32)).astype(BF16)
    wr = wr_ref[...]
    contract = (((1,), (1,)), ((), ()))
    a = lax.dot_general(wr, xh, contract, preferred_element_type=F32)
    c = lax.dot_general(wr[0:N_EXPERTS], xl, contract, preferred_element_type=F32)
    logits = a[0:N_EXPERTS] + a[N_EXPERTS:] + c
    route_ref[...] = _route(logits, rb_ref[...])


def _pack_token_rows(x, route, dst):
    tm = x.shape[0]
    for s in range(PACKED_SUBLANES):
        lo = x[:, (2 * s) * LANES:(2 * s + 1) * LANES]
        hi = x[:, (2 * s + 1) * LANES:(2 * s + 2) * LANES]
        dst[pl.ds(s, tm, stride=ROW_CHUNKS), :] = _pack_bf16_pair(lo, hi)
    padded = jnp.concatenate([route, jnp.zeros((LANES - route.shape[0], tm), F32)], axis=0)
    dst[pl.ds(PACKED_SUBLANES, tm, stride=ROW_CHUNKS), :] = pltpu.bitcast(padded.T, jnp.uint32)


def _zero_unused_sublanes(dst, tm):
    for s in range(PACKED_SUBLANES + 1, ROW_CHUNKS):
        dst[pl.ds(s, tm, stride=ROW_CHUNKS), :] = jnp.zeros((tm, LANES), jnp.uint32)


def _post_attn_kernel(o0_ref, l0_ref, o1_ref, l1_ref, o2_ref, l2_ref, ob_ref, w_ref,
                      x_ref, g_ref, b_ref, wr_ref, rb_ref, x1_ref, route_ref):
    def wide(ref):
        return jnp.concatenate([ref[ch] for ch in range(GROUP_CHUNKS)], axis=-1)

    def out(ref):
        return jnp.concatenate(_unpack_bf16_pair(ref[0]), axis=-1)

    l0, l1, l2 = wide(l0_ref), wide(l1_ref), wide(l2_ref)
    mx = jnp.maximum(jnp.maximum(l0, l1), l2)
    e0, e1, e2 = jnp.exp(l0 - mx), jnp.exp(l1 - mx), jnp.exp(l2 - mx)
    inv = 1.0 / (e0 + e1 + e2)
    lhs = jnp.concatenate([out(o0_ref) * (e0 * inv), out(o1_ref) * (e1 * inv),
                           out(o2_ref) * (e2 * inv), out(ob_ref)], axis=-1).astype(BF16)
    m = jnp.dot(lhs, w_ref[...], preferred_element_type=F32)
    _post_tail(m, x_ref, g_ref, b_ref, wr_ref, rb_ref, x1_ref, route_ref)


def _post_conv_kernel(u_ref, w_ref, wb_ref, x_ref, g_ref, b_ref, wr_ref, rb_ref,
                      x1_ref, route_ref):
    m = jnp.dot(u_ref[...], w_ref[...], preferred_element_type=F32) + wb_ref[...]
    _post_tail(m, x_ref, g_ref, b_ref, wr_ref, rb_ref, x1_ref, route_ref)


def _post_mixer(kernel, name, lhs_list, consts_front, x, g, b, wr, rb, tm=1024):
    n, d = x.shape
    row = lambda i: (i, 0)
    const = lambda i: (0, 0)
    in_specs = [pl.BlockSpec((tm, a.shape[1]), row) if a.ndim == 2
                else pl.BlockSpec((a.shape[0], tm, LANES), lambda i: (0, i, 0)) for a in lhs_list]
    in_specs += [pl.BlockSpec(c.shape, const) for c in consts_front]
    in_specs += [pl.BlockSpec((tm, d), row)]
    in_specs += [pl.BlockSpec(c.shape, const) for c in (g, b, wr, rb)]
    return pl.pallas_call(
        kernel,
        grid=(n // tm,),
        in_specs=in_specs,
        out_specs=[pl.BlockSpec((tm, d), row),
                   pl.BlockSpec((ROUTE_ROWS, tm), lambda i: (0, i))],
        out_shape=[jax.ShapeDtypeStruct((n, d), F32),
                   jax.ShapeDtypeStruct((ROUTE_ROWS, n), F32)],
        compiler_params=_params(("parallel",)),
        name=name,
    )(*lhs_list, *consts_front, x, g, b, wr, rb)


CONV_SLAB = 256


def _conv_front_kernel(xp_ref, xc_ref, xn_ref, w_ref, wb_ref, taps_ref, cb_ref, g_ref, b_ref,
                       o_ref, win, acc, *, ts):
    t = pl.program_id(1)
    has_prev = (t > 0).astype(F32)
    has_next = (t < pl.num_programs(1) - 1).astype(F32)
    rows = ts + 2 * CONV_HALO
    row = lax.broadcasted_iota(jnp.int32, (rows, 1), 0)
    keep = jnp.where(row < CONV_HALO, has_prev, jnp.where(row >= CONV_HALO + ts, has_next, 1.0))
    xw = jnp.concatenate([xp_ref[...], xc_ref[...], xn_ref[...]], axis=0).astype(BF16)
    half = w_ref.shape[1] // 2
    chunks_per_slab = CONV_SLAB // LANES
    for s in range(half // CONV_SLAB):
        cols = slice(s * CONV_SLAB, (s + 1) * CONV_SLAB)
        gate_cols = slice(half + s * CONV_SLAB, half + (s + 1) * CONV_SLAB)
        a = jnp.dot(xw, w_ref[:, cols], preferred_element_type=F32) + wb_ref[:, cols]
        gt = jnp.dot(xw, w_ref[:, gate_cols], preferred_element_type=F32) + wb_ref[:, gate_cols]
        u = a * jax.nn.sigmoid(gt) * keep
        for c in range(chunks_per_slab):
            j = s * chunks_per_slab + c
            win[j] = u[:, c * LANES:(c + 1) * LANES]
            conv = jnp.zeros((ts, LANES), F32)
            for tap in range(CONV_WIDTH):
                start = CONV_HALO - CONV_WIDTH // 2 + tap
                conv = conv + win[j, pl.ds(start, ts), :] * taps_ref[j, tap:tap + 1, :]
            acc[j] = conv
    u = jnp.concatenate([acc[j] for j in range(ROW_CHUNKS)], axis=-1) + cb_ref[...]
    y = _layer_norm(u, g_ref[...], b_ref[...])
    o_ref[...] = (y * jax.nn.sigmoid(y)).astype(o_ref.dtype)


def _conv_front(x, w_pw1, b_pw1, w_dw, b_dw, g, b, ts=256):
    bsz, seq, d = x.shape
    per_tile = ts // CONV_HALO
    last = seq // CONV_HALO - 1
    padded_taps = 2 * CONV_HALO
    taps = jnp.pad(w_dw, ((0, padded_taps - CONV_WIDTH), (0, 0)))
    taps = taps.reshape(padded_taps, ROW_CHUNKS, LANES).transpose(1, 0, 2)
    vec = pl.BlockSpec((1, d), lambda bb, t: (0, 0))
    return pl.pallas_call(
        functools.partial(_conv_front_kernel, ts=ts),
        grid=(bsz, seq // ts),
        in_specs=[pl.BlockSpec((None, CONV_HALO, d), lambda bb, t: (bb, jnp.maximum(t * per_tile - 1, 0), 0)),
                  pl.BlockSpec((None, ts, d), lambda bb, t: (bb, t, 0)),
                  pl.BlockSpec((None, CONV_HALO, d), lambda bb, t: (bb, jnp.minimum((t + 1) * per_tile, last), 0)),
                  pl.BlockSpec(w_pw1.shape, lambda bb, t: (0, 0)),
                  pl.BlockSpec(b_pw1.shape, lambda bb, t: (0, 0)),
                  pl.BlockSpec(taps.shape, lambda bb, t: (0, 0, 0)),
                  vec, vec, vec],
        out_specs=pl.BlockSpec((None, ts, d), lambda bb, t: (bb, t, 0)),
        out_shape=jax.ShapeDtypeStruct((bsz, seq, d), BF16),
        scratch_shapes=[pltpu.VMEM((ROW_CHUNKS, ts + 2 * CONV_HALO, LANES), F32),
                        pltpu.VMEM((ROW_CHUNKS, ts, LANES), F32)],
        compiler_params=_params(("parallel", "arbitrary")),
        name="conv_front",
    )(x, x, x, w_pw1, b_pw1, taps, b_dw, g, b)


def _row_gather_start(src_hbm, dst, sem, idx_ref, first, count):
    def body(r8, carry):
        for u in range(ROW_COPY_UNROLL):
            r = r8 * ROW_COPY_UNROLL + u
            tok = idx_ref[first + r]
            pltpu.make_async_copy(src_hbm.at[pl.ds(pl.multiple_of(tok * ROW_CHUNKS, ROW_CHUNKS), ROW_CHUNKS)],
                                  dst.at[pl.ds(pl.multiple_of(r * ROW_CHUNKS, ROW_CHUNKS), ROW_CHUNKS)],
                                  sem).start(priority=u % 2)
        return carry
    lax.fori_loop(0, count // ROW_COPY_UNROLL, body, 0)


def _row_gather_wait(src_hbm, dst, sem):
    pltpu.make_async_copy(src_hbm.at[pl.ds(0, dst.shape[0])], dst, sem).wait()


def _dispatch_kernel(dest_ref, tail_ref, n_tiles_ref, x_ref, route_ref, xs_hbm, zeros, stage, sem, zsem, *, tm):
    i = pl.program_id(0)
    n_steps = pl.num_programs(0)
    slot = i % 2
    tile_rows = EXPERT_TILE * ROW_CHUNKS
    all_tiles = xs_hbm.shape[0] // tile_rows

    def zero_tile(first_row):
        start = pl.multiple_of(first_row * ROW_CHUNKS, ROW_CHUNKS)
        return pltpu.make_async_copy(zeros, xs_hbm.at[pl.ds(start, tile_rows)], zsem)

    @pl.when(i == 0)
    def _():
        zeros[...] = jnp.zeros_like(zeros)
        for c in range(N_CLASSES):
            zero_tile(tail_ref[c]).start()
        for c in range(N_CLASSES):
            zero_tile(tail_ref[c]).wait()

        def unused(j, carry):
            cp = zero_tile(j * EXPERT_TILE)
            cp.start()
            cp.wait()
            return carry
        lax.fori_loop(n_tiles_ref[0], all_tiles, unused, 0)

    @pl.when(i < 2)
    def _():
        _zero_unused_sublanes(stage.at[slot], tm)

    _pack_token_rows(x_ref[...], route_ref[...], stage.at[slot])

    def body(r8, carry):
        for u in range(ROW_COPY_UNROLL):
            r = r8 * ROW_COPY_UNROLL + u
            d = dest_ref[i * tm + r]
            pltpu.make_async_copy(stage.at[slot, pl.ds(pl.multiple_of(r * ROW_CHUNKS, ROW_CHUNKS), ROW_CHUNKS)],
                                  xs_hbm.at[pl.ds(pl.multiple_of(d * ROW_CHUNKS, ROW_CHUNKS), ROW_CHUNKS)],
                                  sem.at[slot]).start(priority=u % 2)
        return carry
    lax.fori_loop(0, tm // ROW_COPY_UNROLL, body, 0)

    def wait_step(s):
        pltpu.make_async_copy(stage.at[s], xs_hbm.at[pl.ds(0, tm * ROW_CHUNKS)], sem.at[s]).wait()

    @pl.when(i > 0)
    def _():
        wait_step(1 - slot)

    @pl.when(i == n_steps - 1)
    def _():
        wait_step(slot)


def _dispatch(x1, route, dest, tail_start, n_tiles, n_rows, tm=512):
    n_tokens, d = x1.shape
    grid_spec = pltpu.PrefetchScalarGridSpec(
        num_scalar_prefetch=3,
        grid=(n_tokens // tm,),
        in_specs=[pl.BlockSpec((tm, d), lambda i, dst, tail, nt: (i, 0)),
                  pl.BlockSpec((ROUTE_ROWS, tm), lambda i, dst, tail, nt: (0, i))],
        out_specs=pl.BlockSpec(memory_space=pl.ANY),
        scratch_shapes=[pltpu.VMEM((EXPERT_TILE * ROW_CHUNKS, LANES), jnp.uint32),
                        pltpu.VMEM((2, tm * ROW_CHUNKS, LANES), jnp.uint32),
                        pltpu.SemaphoreType.DMA((2,)),
                        pltpu.SemaphoreType.DMA(())],
    )
    return pl.pallas_call(
        functools.partial(_dispatch_kernel, tm=tm),
        grid_spec=grid_spec,
        out_shape=jax.ShapeDtypeStruct((n_rows * ROW_CHUNKS, LANES), jnp.uint32),
        compiler_params=_params(("arbitrary",)),
        name="moe_dispatch",
    )(dest, tail_start, n_tiles, x1, route)


def _expert_kernel(run_ref, run_expert_ref, n_runs_ref, n_tiles_ref, x_ref,
                   wg_hbm, wu_hbm, wd_hbm, y_ref, stage_g, stage_u, stage_d,
                   wg1_bf, wu1_bf, wd1_bf, wg2_bf, wu2_bf, wd2_bf, sem, *, layer):
    i = pl.program_id(0)
    tm = EXPERT_TILE
    before = jnp.maximum(i - 1, 0)
    working = ((wg1_bf, wu1_bf, wd1_bf), (wg2_bf, wu2_bf, wd2_bf))

    def copies(role, run):
        slot = run % 2
        e = run_expert_ref[role, run]
        return [pltpu.make_async_copy(w.at[layer, e], st.at[role, slot], sem.at[role, slot])
                for w, st in ((wg_hbm, stage_g), (wu_hbm, stage_u), (wd_hbm, stage_d))]

    for role in range(2):
        run = run_ref[role, i]

        @pl.when(i == 0)
        def _():
            for cp in copies(role, 0):
                cp.start(priority=1)

        @pl.when((i == 0) | (run != run_ref[role, before]))
        def _():
            for cp in copies(role, run):
                cp.wait()
            slot = run % 2
            for st, wk in zip((stage_g, stage_u, stage_d), working[role]):
                wk[...] = st[role, slot].astype(BF16)

            @pl.when(run + 1 < n_runs_ref[role])
            def _():
                for cp in copies(role, run + 1):
                    cp.start(priority=1)

    @pl.when(i < n_tiles_ref[0])
    def _():
        halves = []
        for s in range(PACKED_SUBLANES):
            halves.extend(_unpack_bf16_pair(x_ref[pl.ds(s, tm, stride=ROW_CHUNKS), :]))
        x = jnp.concatenate(halves, axis=-1).astype(BF16)
        gates = pltpu.bitcast(x_ref[pl.ds(PACKED_SUBLANES, tm, stride=ROW_CHUNKS), :], F32)

        def ffn(wg, wu, wd):
            hg = jnp.dot(x, wg[...], preferred_element_type=F32)
            hu = jnp.dot(x, wu[...], preferred_element_type=F32)
            hid = (hg * jax.nn.sigmoid(hg) * hu).astype(BF16)
            return jnp.dot(hid, wd[...], preferred_element_type=F32)

        y = (gates[:, GATE_LANE:GATE_LANE + 1] * ffn(wg1_bf, wu1_bf, wd1_bf)
             + gates[:, GATE_LANE + 1:GATE_LANE + 2] * ffn(wg2_bf, wu2_bf, wd2_bf))
        for j in range(ROW_CHUNKS):
            y_ref[pl.ds(j, tm, stride=ROW_CHUNKS), :] = y[:, j * LANES:(j + 1) * LANES]

    @pl.when(i >= n_tiles_ref[0])
    def _():
        y_ref[...] = jnp.zeros_like(y_ref)


def _experts(xs, w_gate, w_up, w_down, layer, tile_run, run_expert, n_runs, n_tiles):
    tm = EXPERT_TILE
    n_rows = xs.shape[0] // ROW_CHUNKS
    d, de = w_gate.shape[2], w_gate.shape[3]
    row_map = lambda i, run, rexp, nr, nt: (jnp.minimum(i, jnp.maximum(nt[0] - 1, 0)), 0)
    hbm = pl.BlockSpec(memory_space=pl.ANY)
    grid_spec = pltpu.PrefetchScalarGridSpec(
        num_scalar_prefetch=4,
        grid=(n_rows // tm,),
        in_specs=[pl.BlockSpec((tm * ROW_CHUNKS, LANES), row_map), hbm, hbm, hbm],
        out_specs=pl.BlockSpec((tm * ROW_CHUNKS, LANES), lambda i, run, rexp, nr, nt: (i, 0)),
        scratch_shapes=[pltpu.VMEM((2, 2, d, de), F32), pltpu.VMEM((2, 2, d, de), F32),
                        pltpu.VMEM((2, 2, de, d), F32),
                        pltpu.VMEM((d, de), BF16), pltpu.VMEM((d, de), BF16), pltpu.VMEM((de, d), BF16),
                        pltpu.VMEM((d, de), BF16), pltpu.VMEM((d, de), BF16), pltpu.VMEM((de, d), BF16),
                        pltpu.SemaphoreType.DMA((2, 2))],
    )
    return pl.pallas_call(
        functools.partial(_expert_kernel, layer=layer),
        grid_spec=grid_spec,
        out_shape=jax.ShapeDtypeStruct((n_rows * ROW_CHUNKS, LANES), F32),
        compiler_params=_params(("arbitrary",)),
        name="moe_experts",
    )(tile_run, run_expert, n_runs, n_tiles, xs, w_gate, w_up, w_down)


def _combine_kernel(dest_ref, y_hbm, x_ref, g_ref, b_ref, o_ref, ybuf, sem):
    i = pl.program_id(0)
    n_steps = pl.num_programs(0)
    slot = i % 2
    tm = COMBINE_TILE

    @pl.when(i == 0)
    def _():
        _row_gather_start(y_hbm, ybuf.at[0], sem.at[0], dest_ref, 0, tm)

    @pl.when(i + 1 < n_steps)
    def _():
        _row_gather_start(y_hbm, ybuf.at[1 - slot], sem.at[1 - slot], dest_ref, (i + 1) * tm, tm)

    _row_gather_wait(y_hbm, ybuf.at[slot], sem.at[slot])
    f = jnp.concatenate([ybuf[slot, pl.ds(j, tm, stride=ROW_CHUNKS), :] for j in range(ROW_CHUNKS)], axis=-1)
    o_ref[...] = _layer_norm(ALPHA * x_ref[...] + f, g_ref[...], b_ref[...])


def _combine(y_rows, x1, dest, g, b):
    n, d = x1.shape
    tm = COMBINE_TILE
    grid_spec = pltpu.PrefetchScalarGridSpec(
        num_scalar_prefetch=1,
        grid=(n // tm,),
        in_specs=[pl.BlockSpec(memory_space=pl.ANY),
                  pl.BlockSpec((tm, d), lambda i, dst: (i, 0)),
                  pl.BlockSpec((1, d), lambda i, dst: (0, 0)),
                  pl.BlockSpec((1, d), lambda i, dst: (0, 0))],
        out_specs=pl.BlockSpec((tm, d), lambda i, dst: (i, 0)),
        scratch_shapes=[pltpu.VMEM((2, tm * ROW_CHUNKS, LANES), F32),
                        pltpu.SemaphoreType.DMA((2,))],
    )
    return pl.pallas_call(
        _combine_kernel,
        grid_spec=grid_spec,
        out_shape=jax.ShapeDtypeStruct((n, d), F32),
        compiler_params=_params(("arbitrary",)),
        name="moe_combine",
    )(dest, y_rows, x1, g, b)


def _dispatch_plan(route, n_tokens):
    tm = EXPERT_TILE
    blk = LANES
    n_rows = n_tokens + N_CLASSES * tm
    hi = lax.Precision.HIGHEST
    onehot = (route[0][:, None] == jnp.arange(N_CLASSES, dtype=F32)[None, :]).astype(F32)
    blocks = onehot.reshape(n_tokens // blk, blk, N_CLASSES)
    tri = np.tril(np.ones((blk, blk), np.float32))
    local = jnp.einsum("ij,bjk->bik", tri, blocks, precision=hi)
    totals = local[:, -1, :]
    strict = np.tril(np.ones((n_tokens // blk,) * 2, np.float32), -1)
    before = jnp.dot(strict, totals, precision=hi)
    rank = jnp.sum((local + before[:, None, :]) * blocks, axis=-1).reshape(n_tokens) - 1.0
    counts = before[-1] + totals[-1]
    padded = jnp.ceil(counts / tm) * tm
    seg_end = jnp.dot(np.tril(np.ones((N_CLASSES,) * 2, np.float32)), padded, precision=hi)
    seg_start = seg_end - padded
    dest = (jnp.sum(onehot * seg_start[None, :], axis=-1) + rank).astype(jnp.int32)
    tile_start = jnp.arange(n_rows // tm, dtype=F32) * tm
    tile_class = jnp.sum((seg_end[None, :] <= tile_start[:, None]).astype(jnp.int32), axis=-1)
    last_class = jnp.max(jnp.where(padded > 0, jnp.arange(N_CLASSES), 0))
    tile_class = jnp.minimum(tile_class, last_class)
    tile_hot = (tile_class[:, None] == np.arange(N_CLASSES)[None, :]).astype(jnp.int32)
    n_pairs = len(PAIR_ORDER)
    expert_of = np.array([[(c // n_pairs) * GROUP_EXPERTS + PAIR_ORDER[c % n_pairs][role]
                           for c in range(N_CLASSES)] for role in range(2)])
    tile_expert = jnp.sum(tile_hot[None, :, :] * expert_of[:, None, :], axis=-1)
    n_all = n_rows // tm
    starts = jnp.concatenate([jnp.ones((2, 1), jnp.int32),
                              (tile_expert[:, 1:] != tile_expert[:, :-1]).astype(jnp.int32)], axis=1)
    tile_run = jnp.cumsum(starts, axis=1) - 1
    run_hot = (tile_run[:, :, None] == np.arange(n_all)[None, None, :]).astype(jnp.int32)
    run_expert = jnp.sum(run_hot * (starts * tile_expert)[:, :, None], axis=1)
    n_runs = tile_run[:, -1] + 1
    n_tiles = (seg_end[-1] / tm).astype(jnp.int32).reshape(1)
    tail_start = jnp.maximum(seg_end - tm, 0.0).astype(jnp.int32)
    plan = (tile_run.astype(jnp.int32), run_expert.astype(jnp.int32), n_runs.astype(jnp.int32), n_tiles)
    return plan, tail_start, dest, n_rows


def _moe(x1, route, w_gate, w_up, w_down, layer, g, b):
    n = x1.shape[0]
    plan, tail_start, dest, n_rows = _dispatch_plan(route, n)
    xs = _dispatch(x1, route, dest, tail_start, plan[-1], n_rows)
    y_rows = _experts(xs, w_gate, w_up, w_down, layer, *plan)
    return _combine(y_rows, x1, dest, g, b)


def _vec(v):
    return v.reshape(1, -1).astype(F32)


def kernel(x, w_in_attn, w_out_attn, rel_bias, rpb_2d, w_pw1, b_pw1, w_dw, b_dw, conv_ln_g, conv_ln_b, w_pw2, b_pw2, ln_mix_g, ln_mix_b, ln_ffn_g, ln_ffn_b, w_router, router_bias, w_gate, w_up, w_down):
    bsz, seq, d = x.shape
    assert d == D_MODEL and seq % SEQ_TILE == 0 and seq % GRID_W == 0, x.shape
    assert w_gate.shape[:2] == (DEPTH, N_EXPERTS) and w_gate.dtype == F32, (w_gate.shape, w_gate.dtype)
    n = bsz * seq
    h = x.reshape(n, d)
    wr_t = w_router.T.astype(F32)
    wr_hi = wr_t.astype(BF16)
    wr_lo = (wr_t - wr_hi.astype(F32)).astype(BF16)
    wr = jnp.concatenate([wr_hi, wr_lo], axis=0)
    rb = router_bias.reshape(N_EXPERTS, 1).astype(F32)
    n_dil_cols = len(DILATIONS) * GROUP_COLS
    q_cols = np.zeros((3 * d,), np.float32) + 1.0
    q_cols[0:n_dil_cols] = HEAD_DIM ** -0.5
    q_cols[3 * n_dil_cols:3 * n_dil_cols + GROUP_COLS] = HEAD_DIM ** -0.5

    for layer in range(DEPTH):
        i = layer // 2
        g_mix, b_mix = _vec(ln_mix_g[layer]), _vec(ln_mix_b[layer])
        if layer % 2 == 0:
            w_in = (w_in_attn[i] * q_cols[None, :]).astype(BF16)
            proj = _matmul(h, w_in)
            lhs = []
            for grp in range(len(DILATIONS)):
                lhs += _dilated_attention(proj, _dilated_bias(rel_bias, grp), grp, bsz, seq)
            lhs.append(_neighborhood_attention(proj, _na_bias(rpb_2d[i]), bsz, seq))
            x1, route = _post_mixer(_post_attn_kernel, "post_attn", lhs,
                                    [w_out_attn[i].astype(BF16)], h, g_mix, b_mix, wr, rb)
        else:
            u = _conv_front(h.reshape(bsz, seq, d), w_pw1[i].astype(BF16), _vec(b_pw1[i]),
                            w_dw[i].astype(F32), _vec(b_dw[i]), _vec(conv_ln_g[i]), _vec(conv_ln_b[i]))
            x1, route = _post_mixer(_post_conv_kernel, "post_conv", [u.reshape(n, d)],
                                    [w_pw2[i].astype(BF16), _vec(b_pw2[i])], h, g_mix, b_mix, wr, rb)
        h = _moe(x1, route, w_gate, w_up, w_down, layer,
                 _vec(ln_ffn_g[layer]), _vec(ln_ffn_b[layer]))
    return h.reshape(bsz, seq, d)
```

```python
import functools
import math

import jax
import jax.numpy as jnp
import numpy as np
from jax import lax
from jax.experimental import pallas as pl
from jax.experimental.pallas import tpu as pltpu

D_MODEL = 1024
HEAD_DIM = 64
GROUP_HEADS = 4
GROUP_COLS = GROUP_HEADS * HEAD_DIM
DILATIONS = (1, 4, 16)
QUERY_BLOCK = 64
GRID_W = 64
NA_ROWS = 8
NA_COLS = 16
T5_BUCKETS = 32
T5_MAX_DIST = 1024
CONV_WIDTH = 31
CONV_HALO = 16
N_EXPERTS = 16
GROUP_EXPERTS = 4
DEPTH = 4
ALPHA = (2 * DEPTH) ** 0.25
LN_EPS = 1e-5
NEG = -1e30

LANES = 128
ROW_CHUNKS = D_MODEL // LANES
GROUP_CHUNKS = GROUP_COLS // LANES
PAIR_ORDER = ((0, 1), (0, 2), (0, 3), (1, 3), (1, 2), (3, 2))
N_CLASSES = (N_EXPERTS // GROUP_EXPERTS) * len(PAIR_ORDER)
PACKED_SUBLANES = D_MODEL // (2 * LANES)
ROUTE_ROWS = 8
GATE_LANE = 1
SEQ_TILE = 2048
EXPERT_TILE = 256
COMBINE_TILE = 512
ROW_COPY_UNROLL = 8
VMEM_LIMIT = 56 * 1024 * 1024

F32 = jnp.float32
BF16 = jnp.bfloat16


def _params(semantics, vmem=VMEM_LIMIT):
    return pltpu.CompilerParams(dimension_semantics=semantics, vmem_limit_bytes=vmem)


MATMUL_SLAB = 512


def _pack_bf16_pair(lo, hi):
    lo_bits = pltpu.bitcast(lo.astype(BF16).astype(F32), jnp.uint32)
    hi_bits = pltpu.bitcast(hi.astype(BF16).astype(F32), jnp.uint32)
    return (lo_bits >> 16) | (hi_bits & jnp.uint32(0xFFFF0000))


def _unpack_bf16_pair(word):
    return pltpu.bitcast(word << 16, F32), pltpu.bitcast(word & jnp.uint32(0xFFFF0000), F32)


def _matmul_kernel(x_ref, w_ref, o_ref):
    xb = x_ref[...].astype(BF16)
    per_slab = MATMUL_SLAB // GROUP_COLS
    for s in range(w_ref.shape[1] // MATMUL_SLAB):
        y = jnp.dot(xb, w_ref[:, s * MATMUL_SLAB:(s + 1) * MATMUL_SLAB], preferred_element_type=F32)
        for c in range(per_slab):
            lo = y[:, (2 * c) * LANES:(2 * c + 1) * LANES]
            hi = y[:, (2 * c + 1) * LANES:(2 * c + 2) * LANES]
            o_ref[s * per_slab + c] = _pack_bf16_pair(lo, hi)


def _matmul(x, w, tm=1024):
    n, k = x.shape
    _, m = w.shape
    return pl.pallas_call(
        _matmul_kernel,
        grid=(n // tm,),
        in_specs=[pl.BlockSpec((tm, k), lambda i: (i, 0)),
                  pl.BlockSpec((k, m), lambda i: (0, 0))],
        out_specs=pl.BlockSpec((m // GROUP_COLS, tm, LANES), lambda i: (0, i, 0)),
        out_shape=jax.ShapeDtypeStruct((m // GROUP_COLS, n, LANES), jnp.uint32),
        compiler_params=_params(("parallel",)),
        name="qkv_proj",
    )(x, w)


def _head_masks(rows):
    lane = lax.broadcasted_iota(jnp.int32, (rows, GROUP_COLS), 1)
    return [(lane >= HEAD_DIM * h) & (lane < HEAD_DIM * (h + 1)) for h in range(GROUP_HEADS)]


def _stack_heads(q, masks):
    return jnp.concatenate([jnp.where(m, q, 0.0) for m in masks], axis=0)


def _unstack_heads(o4, masks, rows):
    out = jnp.where(masks[0], o4[0:rows], 0.0)
    for h in range(1, GROUP_HEADS):
        out = out + jnp.where(masks[h], o4[h * rows:(h + 1) * rows], 0.0)
    return out


def _softmax_pv(logits, v, with_lse):
    m = jnp.max(logits, axis=-1, keepdims=True)
    p = jnp.exp(logits - m)
    s = jnp.sum(p, axis=-1, keepdims=True)
    o4 = jnp.dot(p.astype(BF16), v, preferred_element_type=F32) * (1.0 / s)
    lse = (m + jnp.log(s)) if with_lse else None
    return o4, lse


def _dilated_kernel(q_ref, kp_ref, kc_ref, kn_ref, vp_ref, vc_ref, vn_ref, bias_ref,
                    o_ref, l_ref, *, dil):
    t = pl.program_id(1)
    chunk = QUERY_BLOCK * dil
    n_chunks = SEQ_TILE // chunk
    qb = QUERY_BLOCK
    masks = _head_masks(qb)
    col = lax.broadcasted_iota(jnp.int32, (GROUP_HEADS * qb, 3 * qb), 1)
    no_prev = jnp.where(col < qb, NEG, 0.0) * (t == 0).astype(F32)
    no_next = jnp.where(col >= 2 * qb, NEG, 0.0) * (t == pl.num_programs(1) - 1).astype(F32)
    bias = bias_ref[...]

    def sl(start):
        return pl.ds(start, qb) if dil == 1 else pl.ds(start, qb, stride=dil)

    def rows(ref, start):
        return jnp.concatenate(_unpack_bf16_pair(ref[0, sl(start), :]), axis=-1)

    def window(p_ref, c_ref, n_ref, c, r):
        parts = []
        for cc in (c - 1, c, c + 1):
            if cc < 0:
                parts.append(rows(p_ref, r))
            elif cc >= n_chunks:
                parts.append(rows(n_ref, r))
            else:
                parts.append(rows(c_ref, cc * chunk + r))
        return jnp.concatenate(parts, axis=0).astype(BF16)

    for c in range(n_chunks):
        for r in range(dil):
            base = c * chunk + r
            q4 = _stack_heads(rows(q_ref, base), masks).astype(BF16)
            k = window(kp_ref, kc_ref, kn_ref, c, r)
            v = window(vp_ref, vc_ref, vn_ref, c, r)
            logits = lax.dot_general(q4, k, (((1,), (1,)), ((), ())),
                                     preferred_element_type=F32) + bias
            if c == 0:
                logits = logits + no_prev
            if c == n_chunks - 1:
                logits = logits + no_next
            o4, lse = _softmax_pv(logits, v, True)
            o = _unstack_heads(o4, masks, qb)
            lb = _unstack_heads(jnp.broadcast_to(lse, (GROUP_HEADS * qb, GROUP_COLS)), masks, qb)
            o_ref[0, sl(base), :] = _pack_bf16_pair(o[:, 0:LANES], o[:, LANES:])
            for ch in range(GROUP_CHUNKS):
                l_ref[ch, sl(base), :] = lb[:, ch * LANES:(ch + 1) * LANES]


def _tile_spec(col_block, seq, chunks=1):
    tiles = seq // SEQ_TILE
    return pl.BlockSpec((chunks, SEQ_TILE, LANES), lambda b, t: (col_block, b * tiles + t, 0))


def _halo_specs(col_block, halo, seq):
    per_tile = SEQ_TILE // halo
    per_seq = seq // halo
    return [
        pl.BlockSpec((1, halo, LANES),
                     lambda b, t: (col_block, b * per_seq + jnp.maximum(t * per_tile - 1, 0), 0)),
        _tile_spec(col_block, seq),
        pl.BlockSpec((1, halo, LANES),
                     lambda b, t: (col_block, b * per_seq + jnp.minimum((t + 1) * per_tile, per_seq - 1), 0)),
    ]


def _dilated_attention(proj, bias, group, bsz, seq):
    dil = DILATIONS[group]
    n_groups = len(DILATIONS)
    q_col, k_col, v_col = group, n_groups + group, 2 * n_groups + group
    return pl.pallas_call(
        functools.partial(_dilated_kernel, dil=dil),
        grid=(bsz, seq // SEQ_TILE),
        in_specs=[_tile_spec(q_col, seq)]
        + _halo_specs(k_col, QUERY_BLOCK * dil, seq) + _halo_specs(v_col, QUERY_BLOCK * dil, seq)
        + [pl.BlockSpec(bias.shape, lambda b, t: (0, 0))],
        out_specs=[_tile_spec(0, seq), _tile_spec(0, seq, GROUP_CHUNKS)],
        out_shape=[jax.ShapeDtypeStruct((1, bsz * seq, LANES), jnp.uint32),
                   jax.ShapeDtypeStruct((GROUP_CHUNKS, bsz * seq, LANES), F32)],
        compiler_params=_params(("parallel", "arbitrary")),
        name=f"dilated_attn_d{dil}",
    )(proj, proj, proj, proj, proj, proj, proj, bias)


NA_HALO = 256
NA_KEYS = NA_ROWS * GRID_W


def _na_kernel(q_ref, kp_ref, kc_ref, kn_ref, vp_ref, vc_ref, vn_ref, bias_ref, o_ref,
               kwin, vwin, *, grid_rows):
    t = pl.program_id(1)
    tile_rows = SEQ_TILE // GRID_W
    half = NA_ROWS // 2
    for win, p_ref, c_ref, n_ref in ((kwin, kp_ref, kc_ref, kn_ref), (vwin, vp_ref, vc_ref, vn_ref)):
        for first, src in ((0, p_ref), (NA_HALO, c_ref), (NA_HALO + SEQ_TILE, n_ref)):
            n_rows = src.shape[1]
            for ch, part in enumerate(_unpack_bf16_pair(src[0])):
                win[first:first + n_rows, ch * LANES:(ch + 1) * LANES] = part.astype(BF16)
    masks = _head_masks(GRID_W)
    for i in range(tile_rows):
        row = t * tile_rows + i
        row0 = jnp.clip(row - half, 0, grid_rows - NA_ROWS)
        off = pl.multiple_of((row0 - (t * tile_rows - half)) * GRID_W, GRID_W)
        q = jnp.concatenate(_unpack_bf16_pair(q_ref[0, i * GRID_W:(i + 1) * GRID_W, :]), axis=-1)
        q4 = _stack_heads(q, masks).astype(BF16)
        k = kwin[pl.ds(off, NA_KEYS), :]
        v = vwin[pl.ds(off, NA_KEYS), :]
        logits = lax.dot_general(q4, k, (((1,), (1,)), ((), ())),
                                 preferred_element_type=F32) + bias_ref[row - row0]
        o4, _ = _softmax_pv(logits, v, False)
        o = _unstack_heads(o4, masks, GRID_W)
        o_ref[0, i * GRID_W:(i + 1) * GRID_W, :] = _pack_bf16_pair(o[:, 0:LANES], o[:, LANES:])


def _neighborhood_attention(proj, bias, bsz, seq):
    base = 3 * len(DILATIONS)
    win_rows = SEQ_TILE + 2 * NA_HALO
    return pl.pallas_call(
        functools.partial(_na_kernel, grid_rows=seq // GRID_W),
        grid=(bsz, seq // SEQ_TILE),
        in_specs=[_tile_spec(base, seq)]
        + _halo_specs(base + 1, NA_HALO, seq) + _halo_specs(base + 2, NA_HALO, seq)
        + [pl.BlockSpec(bias.shape, lambda b, t: (0, 0, 0))],
        out_specs=_tile_spec(0, seq),
        out_shape=jax.ShapeDtypeStruct((1, bsz * seq, LANES), jnp.uint32),
        scratch_shapes=[pltpu.VMEM((win_rows, GROUP_COLS), BF16),
                        pltpu.VMEM((win_rows, GROUP_COLS), BF16)],
        compiler_params=_params(("parallel", "arbitrary")),
        name="neighborhood_attn",
    )(proj, proj, proj, proj, proj, proj, proj, bias)


def _t5_bucket_table(dil):
    qb = QUERY_BLOCK
    rel = (np.arange(3 * qb)[None, :] - qb - np.arange(qb)[:, None])
    nb = T5_BUCKETS // 2
    max_exact = nb // 2
    n = np.abs(rel * dil)
    nf = np.maximum(n, 1).astype(np.float32)
    large = max_exact + (np.log(nf / np.float32(max_exact)) / np.float32(math.log(T5_MAX_DIST / max_exact))
                         * np.float32(nb - max_exact)).astype(np.int32)
    large = np.minimum(large, nb - 1)
    bucket = np.where(rel * dil > 0, nb, 0) + np.where(n < max_exact, n, large)
    return bucket.astype(np.int32), np.abs(rel) <= qb


def _dilated_bias(rel_bias, group):
    bucket, band = _t5_bucket_table(DILATIONS[group])
    tab = rel_bias[:, group * GROUP_HEADS:(group + 1) * GROUP_HEADS].astype(F32)
    onehot = (bucket[None, :, :] == np.arange(T5_BUCKETS)[:, None, None]).astype(np.float32)
    bias = jnp.einsum("bh,bqk->hqk", tab, onehot, precision=lax.Precision.HIGHEST)
    bias = jnp.where(band[None], bias, NEG)
    return bias.reshape(GROUP_HEADS * QUERY_BLOCK, 3 * QUERY_BLOCK)


def _na_bias(rpb):
    qc = np.arange(GRID_W)[:, None]
    kc = np.arange(GRID_W)[None, :]
    win0 = np.clip(qc - NA_COLS // 2, 0, GRID_W - NA_COLS)
    col_ok = (kc >= win0) & (kc < win0 + NA_COLS)
    dc = np.clip(kc - qc + NA_COLS - 1, 0, 2 * NA_COLS - 2)
    shift = np.arange(NA_ROWS)[:, None]
    kr = np.arange(NA_ROWS)[None, :]
    dr = kr - shift + NA_ROWS - 1
    pick_row = (dr[:, :, None] == np.arange(2 * NA_ROWS - 1)[None, None, :]).astype(np.float32)
    pick_col = (dc[None, :, :] == np.arange(2 * NA_COLS - 1)[:, None, None]).astype(np.float32)
    hi = lax.Precision.HIGHEST
    rows = jnp.einsum("hac,dka->hdkc", rpb.astype(F32), pick_row, precision=hi)
    b = jnp.einsum("hdkc,cqx->dhqkx", rows, pick_col, precision=hi)
    b = jnp.where(col_ok[None, None, :, None, :], b, NEG)
    return b.reshape(NA_ROWS, GROUP_HEADS * GRID_W, NA_KEYS)


def _layer_norm(z, g, b):
    mu = jnp.mean(z, axis=-1, keepdims=True)
    zc = z - mu
    var = jnp.mean(zc * zc, axis=-1, keepdims=True)
    return zc * lax.rsqrt(var + LN_EPS) * g + b


def _row(x, i):
    return x[i:i + 1, :]


def _route(logits, rbias):
    scores = jax.nn.sigmoid(logits)
    sel = scores + rbias
    ge = GROUP_EXPERTS
    n_groups = N_EXPERTS // ge
    best_sum, best_g = None, None
    for g in range(n_groups):
        vals = [_row(sel, g * ge + j) for j in range(ge)]
        top2 = None
        for a in range(ge):
            for b in range(a + 1, ge):
                pair = vals[a] + vals[b]
                top2 = pair if top2 is None else jnp.maximum(top2, pair)
        if best_sum is None:
            best_sum, best_g = top2, jnp.zeros_like(top2, dtype=jnp.int32)
        else:
            upd = top2 > best_sum
            best_g = jnp.where(upd, g, best_g)
            best_sum = jnp.where(upd, top2, best_sum)

    def in_group(x, j):
        out = _row(x, j)
        for g in range(1, n_groups):
            out = jnp.where(best_g == g, _row(x, g * ge + j), out)
        return out

    cand = [in_group(sel, j) for j in range(ge)]
    cand_score = [in_group(scores, j) for j in range(ge)]
    v1, j1, w1 = cand[0], jnp.zeros_like(best_g), cand_score[0]
    for j in range(1, ge):
        upd = cand[j] > v1
        v1 = jnp.where(upd, cand[j], v1)
        j1 = jnp.where(upd, j, j1)
        w1 = jnp.where(upd, cand_score[j], w1)
    v2 = jnp.full_like(v1, -jnp.inf)
    j2, w2 = jnp.zeros_like(best_g), jnp.zeros_like(w1)
    for j in range(ge):
        upd = (cand[j] > v2) & (j1 != j)
        v2 = jnp.where(upd, cand[j], v2)
        j2 = jnp.where(upd, j, j2)
        w2 = jnp.where(upd, cand_score[j], w2)
    wsum = w1 + w2
    g1, g2 = w1 / wsum, w2 / wsum
    lo, hi = jnp.minimum(j1, j2), jnp.maximum(j1, j2)
    pair = jnp.zeros_like(best_g)
    first = jnp.zeros_like(best_g)
    for q, (a, b) in enumerate(PAIR_ORDER):
        hit = (lo == min(a, b)) & (hi == max(a, b))
        pair = jnp.where(hit, q, pair)
        first = jnp.where(hit, a, first)
    is_first = j1 == first
    g_first = jnp.where(is_first, g1, g2)
    g_second = jnp.where(is_first, g2, g1)
    cls = (best_g * len(PAIR_ORDER) + pair).astype(F32)
    rows = [cls, g_first, g_second]
    return jnp.concatenate(rows + [jnp.zeros_like(w1)] * (ROUTE_ROWS - len(rows)), axis=0)


def _post_tail(m, x_ref, g_ref, b_ref, wr_ref, rb_ref, x1_ref, route_ref):
    xn = _layer_norm(ALPHA * x_ref[...] + m, g_ref[...], b_ref[...])
    x1_ref[...] = xn
    xh = xn.astype(BF16)
    xl = (xn - xh.astype(F32)).astype(BF16)
    wr = wr_ref[...]
    contract = (((1,), (1,)), ((), ()))
    a = lax.dot_general(wr, xh, contract, preferred_element_type=F32)
    c = lax.dot_general(wr[0:N_EXPERTS], xl, contract, preferred_element_type=F32)
    logits = a[0:N_EXPERTS] + a[N_EXPERTS:] + c
    route_ref[...] = _route(logits, rb_ref[...])


def _pack_token_rows(x, route, dst):
    tm = x.shape[0]
    for s in range(PACKED_SUBLANES):
        lo = x[:, (2 * s) * LANES:(2 * s + 1) * LANES]
        hi = x[:, (2 * s + 1) * LANES:(2 * s + 2) * LANES]
        dst[pl.ds(s, tm, stride=ROW_CHUNKS), :] = _pack_bf16_pair(lo, hi)
    padded = jnp.concatenate([route, jnp.zeros((LANES - route.shape[0], tm), F32)], axis=0)
    dst[pl.ds(PACKED_SUBLANES, tm, stride=ROW_CHUNKS), :] = pltpu.bitcast(padded.T, jnp.uint32)


def _zero_unused_sublanes(dst, tm):
    for s in range(PACKED_SUBLANES + 1, ROW_CHUNKS):
        dst[pl.ds(s, tm, stride=ROW_CHUNKS), :] = jnp.zeros((tm, LANES), jnp.uint32)


def _post_attn_kernel(o0_ref, l0_ref, o1_ref, l1_ref, o2_ref, l2_ref, ob_ref, w_ref,
                      x_ref, g_ref, b_ref, wr_ref, rb_ref, x1_ref, route_ref):
    def wide(ref):
        return jnp.concatenate([ref[ch] for ch in range(GROUP_CHUNKS)], axis=-1)

    def out(ref):
        return jnp.concatenate(_unpack_bf16_pair(ref[0]), axis=-1)

    l0, l1, l2 = wide(l0_ref), wide(l1_ref), wide(l2_ref)
    mx = jnp.maximum(jnp.maximum(l0, l1), l2)
    e0, e1, e2 = jnp.exp(l0 - mx), jnp.exp(l1 - mx), jnp.exp(l2 - mx)
    inv = 1.0 / (e0 + e1 + e2)
    lhs = jnp.concatenate([out(o0_ref) * (e0 * inv), out(o1_ref) * (e1 * inv),
                           out(o2_ref) * (e2 * inv), out(ob_ref)], axis=-1).astype(BF16)
    m = jnp.dot(lhs, w_ref[...], preferred_element_type=F32)
    _post_tail(m, x_ref, g_ref, b_ref, wr_ref, rb_ref, x1_ref, route_ref)


def _post_conv_kernel(u_ref, w_ref, wb_ref, x_ref, g_ref, b_ref, wr_ref, rb_ref,
                      x1_ref, route_ref):
    m = jnp.dot(u_ref[...], w_ref[...], preferred_element_type=F32) + wb_ref[...]
    _post_tail(m, x_ref, g_ref, b_ref, wr_ref, rb_ref, x1_ref, route_ref)


def _post_mixer(kernel, name, lhs_list, consts_front, x, g, b, wr, rb, tm=1024):
    n, d = x.shape
    row = lambda i: (i, 0)
    const = lambda i: (0, 0)
    in_specs = [pl.BlockSpec((tm, a.shape[1]), row) if a.ndim == 2
                else pl.BlockSpec((a.shape[0], tm, LANES), lambda i: (0, i, 0)) for a in lhs_list]
    in_specs += [pl.BlockSpec(c.shape, const) for c in consts_front]
    in_specs += [pl.BlockSpec((tm, d), row)]
    in_specs += [pl.BlockSpec(c.shape, const) for c in (g, b, wr, rb)]
    return pl.pallas_call(
        kernel,
        grid=(n // tm,),
        in_specs=in_specs,
        out_specs=[pl.BlockSpec((tm, d), row),
                   pl.BlockSpec((ROUTE_ROWS, tm), lambda i: (0, i))],
        out_shape=[jax.ShapeDtypeStruct((n, d), F32),
                   jax.ShapeDtypeStruct((ROUTE_ROWS, n), F32)],
        compiler_params=_params(("parallel",)),
        name=name,
    )(*lhs_list, *consts_front, x, g, b, wr, rb)


CONV_SLAB = 256
CONV_ROWS = 256


def _conv_front_kernel(xp_ref, xc_ref, xn_ref, w_ref, wb_ref, taps_ref, cb_ref, g_ref, b_ref,
                       o_ref, win, acc, *, ts):
    t = pl.program_id(1)
    has_prev = (t > 0).astype(F32)
    has_next = (t < pl.num_programs(1) - 1).astype(F32)
    rows = ts + 2 * CONV_HALO
    row = lax.broadcasted_iota(jnp.int32, (rows, 1), 0)
    keep = jnp.where(row < CONV_HALO, has_prev, jnp.where(row >= CONV_HALO + ts, has_next, 1.0))
    xw = jnp.concatenate([xp_ref[...], xc_ref[...], xn_ref[...]], axis=0).astype(BF16)
    half = w_ref.shape[1] // 2
    chunks_per_slab = CONV_SLAB // LANES
    for s in range(half // CONV_SLAB):
        cols = slice(s * CONV_SLAB, (s + 1) * CONV_SLAB)
        gate_cols = slice(half + s * CONV_SLAB, half + (s + 1) * CONV_SLAB)
        a = jnp.dot(xw, w_ref[:, cols], preferred_element_type=F32) + wb_ref[:, cols]
        gt = jnp.dot(xw, w_ref[:, gate_cols], preferred_element_type=F32) + wb_ref[:, gate_cols]
        u = a * jax.nn.sigmoid(gt) * keep
        for c in range(chunks_per_slab):
            j = s * chunks_per_slab + c
            win[j] = u[:, c * LANES:(c + 1) * LANES]
            for r0 in range(0, ts, CONV_ROWS):
                conv = jnp.zeros((CONV_ROWS, LANES), F32)
                for tap in range(CONV_WIDTH):
                    start = r0 + CONV_HALO - CONV_WIDTH // 2 + tap
                    conv = conv + win[j, pl.ds(start, CONV_ROWS), :] * taps_ref[j, tap:tap + 1, :]
                acc[j, r0:r0 + CONV_ROWS, :] = conv
    u = jnp.concatenate([acc[j] for j in range(ROW_CHUNKS)], axis=-1) + cb_ref[...]
    y = _layer_norm(u, g_ref[...], b_ref[...])
    o_ref[...] = (y * jax.nn.sigmoid(y)).astype(o_ref.dtype)


def _conv_front(x, w_pw1, b_pw1, w_dw, b_dw, g, b, ts=1024):
    bsz, seq, d = x.shape
    per_tile = ts // CONV_HALO
    last = seq // CONV_HALO - 1
    padded_taps = 2 * CONV_HALO
    taps = jnp.pad(w_dw, ((0, padded_taps - CONV_WIDTH), (0, 0)))
    taps = taps.reshape(padded_taps, ROW_CHUNKS, LANES).transpose(1, 0, 2)
    vec = pl.BlockSpec((1, d), lambda bb, t: (0, 0))
    return pl.pallas_call(
        functools.partial(_conv_front_kernel, ts=ts),
        grid=(bsz, seq // ts),
        in_specs=[pl.BlockSpec((None, CONV_HALO, d), lambda bb, t: (bb, jnp.maximum(t * per_tile - 1, 0), 0)),
                  pl.BlockSpec((None, ts, d), lambda bb, t: (bb, t, 0)),
                  pl.BlockSpec((None, CONV_HALO, d), lambda bb, t: (bb, jnp.minimum((t + 1) * per_tile, last), 0)),
                  pl.BlockSpec(w_pw1.shape, lambda bb, t: (0, 0)),
                  pl.BlockSpec(b_pw1.shape, lambda bb, t: (0, 0)),
                  pl.BlockSpec(taps.shape, lambda bb, t: (0, 0, 0)),
                  vec, vec, vec],
        out_specs=pl.BlockSpec((None, ts, d), lambda bb, t: (bb, t, 0)),
        out_shape=jax.ShapeDtypeStruct((bsz, seq, d), BF16),
        scratch_shapes=[pltpu.VMEM((ROW_CHUNKS, ts + 2 * CONV_HALO, LANES), F32),
                        pltpu.VMEM((ROW_CHUNKS, ts, LANES), F32)],
        compiler_params=_params(("parallel", "arbitrary")),
        name="conv_front",
    )(x, x, x, w_pw1, b_pw1, taps, b_dw, g, b)


def _row_gather_start(src_hbm, dst, sem, idx_ref, first, count):
    def body(r8, carry):
        for u in range(ROW_COPY_UNROLL):
            r = r8 * ROW_COPY_UNROLL + u
            tok = idx_ref[first + r]
            pltpu.make_async_copy(src_hbm.at[pl.ds(pl.multiple_of(tok * ROW_CHUNKS, ROW_CHUNKS), ROW_CHUNKS)],
                                  dst.at[pl.ds(pl.multiple_of(r * ROW_CHUNKS, ROW_CHUNKS), ROW_CHUNKS)],
                                  sem).start(priority=u % 2)
        return carry
    lax.fori_loop(0, count // ROW_COPY_UNROLL, body, 0)


def _row_gather_wait(src_hbm, dst, sem):
    pltpu.make_async_copy(src_hbm.at[pl.ds(0, dst.shape[0])], dst, sem).wait()


def _dispatch_kernel(dest_ref, tail_ref, n_tiles_ref, x_ref, route_ref, xs_hbm, zeros, stage, sem, zsem, *, tm):
    i = pl.program_id(0)
    n_steps = pl.num_programs(0)
    slot = i % 2
    tile_rows = EXPERT_TILE * ROW_CHUNKS
    all_tiles = xs_hbm.shape[0] // tile_rows

    def zero_tile(first_row):
        start = pl.multiple_of(first_row * ROW_CHUNKS, ROW_CHUNKS)
        return pltpu.make_async_copy(zeros, xs_hbm.at[pl.ds(start, tile_rows)], zsem)

    @pl.when(i == 0)
    def _():
        zeros[...] = jnp.zeros_like(zeros)
        for c in range(N_CLASSES):
            zero_tile(tail_ref[c]).start()
        for c in range(N_CLASSES):
            zero_tile(tail_ref[c]).wait()

        def unused(j, carry):
            cp = zero_tile(j * EXPERT_TILE)
            cp.start()
            cp.wait()
            return carry
        lax.fori_loop(n_tiles_ref[0], all_tiles, unused, 0)

    @pl.when(i < 2)
    def _():
        _zero_unused_sublanes(stage.at[slot], tm)

    _pack_token_rows(x_ref[...], route_ref[...], stage.at[slot])

    def body(r8, carry):
        for u in range(ROW_COPY_UNROLL):
            r = r8 * ROW_COPY_UNROLL + u
            d = dest_ref[i * tm + r]
            pltpu.make_async_copy(stage.at[slot, pl.ds(pl.multiple_of(r * ROW_CHUNKS, ROW_CHUNKS), ROW_CHUNKS)],
                                  xs_hbm.at[pl.ds(pl.multiple_of(d * ROW_CHUNKS, ROW_CHUNKS), ROW_CHUNKS)],
                                  sem.at[slot]).start(priority=u % 2)
        return carry
    lax.fori_loop(0, tm // ROW_COPY_UNROLL, body, 0)

    def wait_step(s):
        pltpu.make_async_copy(stage.at[s], xs_hbm.at[pl.ds(0, tm * ROW_CHUNKS)], sem.at[s]).wait()

    @pl.when(i > 0)
    def _():
        wait_step(1 - slot)

    @pl.when(i == n_steps - 1)
    def _():
        wait_step(slot)


def _dispatch(x1, route, dest, tail_start, n_tiles, n_rows, tm=512):
    n_tokens, d = x1.shape
    grid_spec = pltpu.PrefetchScalarGridSpec(
        num_scalar_prefetch=3,
        grid=(n_tokens // tm,),
        in_specs=[pl.BlockSpec((tm, d), lambda i, dst, tail, nt: (i, 0)),
                  pl.BlockSpec((ROUTE_ROWS, tm), lambda i, dst, tail, nt: (0, i))],
        out_specs=pl.BlockSpec(memory_space=pl.ANY),
        scratch_shapes=[pltpu.VMEM((EXPERT_TILE * ROW_CHUNKS, LANES), jnp.uint32),
                        pltpu.VMEM((2, tm * ROW_CHUNKS, LANES), jnp.uint32),
                        pltpu.SemaphoreType.DMA((2,)),
                        pltpu.SemaphoreType.DMA(())],
    )
    return pl.pallas_call(
        functools.partial(_dispatch_kernel, tm=tm),
        grid_spec=grid_spec,
        out_shape=jax.ShapeDtypeStruct((n_rows * ROW_CHUNKS, LANES), jnp.uint32),
        compiler_params=_params(("arbitrary",)),
        name="moe_dispatch",
    )(dest, tail_start, n_tiles, x1, route)


def _expert_kernel(run_ref, run_expert_ref, n_runs_ref, n_tiles_ref, x_ref,
                   wg_hbm, wu_hbm, wd_hbm, y_ref, stage_g, stage_u, stage_d,
                   wg1_bf, wu1_bf, wd1_bf, wg2_bf, wu2_bf, wd2_bf, sem, *, layer):
    i = pl.program_id(0)
    tm = EXPERT_TILE
    before = jnp.maximum(i - 1, 0)
    working = ((wg1_bf, wu1_bf, wd1_bf), (wg2_bf, wu2_bf, wd2_bf))

    def copies(role, run):
        slot = run % 2
        e = run_expert_ref[role, run]
        return [pltpu.make_async_copy(w.at[layer, e], st.at[role, slot], sem.at[role, slot])
                for w, st in ((wg_hbm, stage_g), (wu_hbm, stage_u), (wd_hbm, stage_d))]

    for role in range(2):
        run = run_ref[role, i]

        @pl.when(i == 0)
        def _():
            for cp in copies(role, 0):
                cp.start(priority=1)

        @pl.when((i == 0) | (run != run_ref[role, before]))
        def _():
            for cp in copies(role, run):
                cp.wait()
            slot = run % 2
            for st, wk in zip((stage_g, stage_u, stage_d), working[role]):
                wk[...] = st[role, slot].astype(BF16)

            @pl.when(run + 1 < n_runs_ref[role])
            def _():
                for cp in copies(role, run + 1):
                    cp.start(priority=1)

    @pl.when(i < n_tiles_ref[0])
    def _():
        halves = []
        for s in range(PACKED_SUBLANES):
            halves.extend(_unpack_bf16_pair(x_ref[pl.ds(s, tm, stride=ROW_CHUNKS), :]))
        x = jnp.concatenate(halves, axis=-1).astype(BF16)
        gates = pltpu.bitcast(x_ref[pl.ds(PACKED_SUBLANES, tm, stride=ROW_CHUNKS), :], F32)

        def ffn(wg, wu, wd):
            hg = jnp.dot(x, wg[...], preferred_element_type=F32)
            hu = jnp.dot(x, wu[...], preferred_element_type=F32)
            hid = (hg * jax.nn.sigmoid(hg) * hu).astype(BF16)
            return jnp.dot(hid, wd[...], preferred_element_type=F32)

        y = (gates[:, GATE_LANE:GATE_LANE + 1] * ffn(wg1_bf, wu1_bf, wd1_bf)
             + gates[:, GATE_LANE + 1:GATE_LANE + 2] * ffn(wg2_bf, wu2_bf, wd2_bf))
        for j in range(ROW_CHUNKS):
            y_ref[pl.ds(j, tm, stride=ROW_CHUNKS), :] = y[:, j * LANES:(j + 1) * LANES]

    @pl.when(i >= n_tiles_ref[0])
    def _():
        y_ref[...] = jnp.zeros_like(y_ref)


def _experts(xs, w_gate, w_up, w_down, layer, tile_run, run_expert, n_runs, n_tiles):
    tm = EXPERT_TILE
    n_rows = xs.shape[0] // ROW_CHUNKS
    d, de = w_gate.shape[2], w_gate.shape[3]
    row_map = lambda i, run, rexp, nr, nt: (jnp.minimum(i, jnp.maximum(nt[0] - 1, 0)), 0)
    hbm = pl.BlockSpec(memory_space=pl.ANY)
    grid_spec = pltpu.PrefetchScalarGridSpec(
        num_scalar_prefetch=4,
        grid=(n_rows // tm,),
        in_specs=[pl.BlockSpec((tm * ROW_CHUNKS, LANES), row_map), hbm, hbm, hbm],
        out_specs=pl.BlockSpec((tm * ROW_CHUNKS, LANES), lambda i, run, rexp, nr, nt: (i, 0)),
        scratch_shapes=[pltpu.VMEM((2, 2, d, de), F32), pltpu.VMEM((2, 2, d, de), F32),
                        pltpu.VMEM((2, 2, de, d), F32),
                        pltpu.VMEM((d, de), BF16), pltpu.VMEM((d, de), BF16), pltpu.VMEM((de, d), BF16),
                        pltpu.VMEM((d, de), BF16), pltpu.VMEM((d, de), BF16), pltpu.VMEM((de, d), BF16),
                        pltpu.SemaphoreType.DMA((2, 2))],
    )
    return pl.pallas_call(
        functools.partial(_expert_kernel, layer=layer),
        grid_spec=grid_spec,
        out_shape=jax.ShapeDtypeStruct((n_rows * ROW_CHUNKS, LANES), F32),
        compiler_params=_params(("arbitrary",)),
        name="moe_experts",
    )(tile_run, run_expert, n_runs, n_tiles, xs, w_gate, w_up, w_down)


def _combine_kernel(dest_ref, y_hbm, x_ref, g_ref, b_ref, o_ref, ybuf, sem):
    i = pl.program_id(0)
    n_steps = pl.num_programs(0)
    slot = i % 2
    tm = COMBINE_TILE

    @pl.when(i == 0)
    def _():
        _row_gather_start(y_hbm, ybuf.at[0], sem.at[0], dest_ref, 0, tm)

    @pl.when(i + 1 < n_steps)
    def _():
        _row_gather_start(y_hbm, ybuf.at[1 - slot], sem.at[1 - slot], dest_ref, (i + 1) * tm, tm)

    _row_gather_wait(y_hbm, ybuf.at[slot], sem.at[slot])
    f = jnp.concatenate([ybuf[slot, pl.ds(j, tm, stride=ROW_CHUNKS), :] for j in range(ROW_CHUNKS)], axis=-1)
    o_ref[...] = _layer_norm(ALPHA * x_ref[...] + f, g_ref[...], b_ref[...])


def _combine(y_rows, x1, dest, g, b):
    n, d = x1.shape
    tm = COMBINE_TILE
    grid_spec = pltpu.PrefetchScalarGridSpec(
        num_scalar_prefetch=1,
        grid=(n // tm,),
        in_specs=[pl.BlockSpec(memory_space=pl.ANY),
                  pl.BlockSpec((tm, d), lambda i, dst: (i, 0)),
                  pl.BlockSpec((1, d), lambda i, dst: (0, 0)),
                  pl.BlockSpec((1, d), lambda i, dst: (0, 0))],
        out_specs=pl.BlockSpec((tm, d), lambda i, dst: (i, 0)),
        scratch_shapes=[pltpu.VMEM((2, tm * ROW_CHUNKS, LANES), F32),
                        pltpu.SemaphoreType.DMA((2,))],
    )
    return pl.pallas_call(
        _combine_kernel,
        grid_spec=grid_spec,
        out_shape=jax.ShapeDtypeStruct((n, d), F32),
        compiler_params=_params(("arbitrary",)),
        name="moe_combine",
    )(dest, y_rows, x1, g, b)


def _dispatch_plan(route, n_tokens):
    tm = EXPERT_TILE
    blk = LANES
    n_rows = n_tokens + N_CLASSES * tm
    hi = lax.Precision.HIGHEST
    onehot = (route[0][:, None] == jnp.arange(N_CLASSES, dtype=F32)[None, :]).astype(F32)
    blocks = onehot.reshape(n_tokens // blk, blk, N_CLASSES)
    tri = np.tril(np.ones((blk, blk), np.float32))
    local = jnp.einsum("ij,bjk->bik", tri, blocks, precision=hi)
    totals = local[:, -1, :]
    strict = np.tril(np.ones((n_tokens // blk,) * 2, np.float32), -1)
    before = jnp.dot(strict, totals, precision=hi)
    rank = jnp.sum((local + before[:, None, :]) * blocks, axis=-1).reshape(n_tokens) - 1.0
    counts = before[-1] + totals[-1]
    padded = jnp.ceil(counts / tm) * tm
    seg_end = jnp.dot(np.tril(np.ones((N_CLASSES,) * 2, np.float32)), padded, precision=hi)
    seg_start = seg_end - padded
    dest = (jnp.sum(onehot * seg_start[None, :], axis=-1) + rank).astype(jnp.int32)
    tile_start = jnp.arange(n_rows // tm, dtype=F32) * tm
    tile_class = jnp.sum((seg_end[None, :] <= tile_start[:, None]).astype(jnp.int32), axis=-1)
    last_class = jnp.max(jnp.where(padded > 0, jnp.arange(N_CLASSES), 0))
    tile_class = jnp.minimum(tile_class, last_class)
    tile_hot = (tile_class[:, None] == np.arange(N_CLASSES)[None, :]).astype(jnp.int32)
    n_pairs = len(PAIR_ORDER)
    expert_of = np.array([[(c // n_pairs) * GROUP_EXPERTS + PAIR_ORDER[c % n_pairs][role]
                           for c in range(N_CLASSES)] for role in range(2)])
    tile_expert = jnp.sum(tile_hot[None, :, :] * expert_of[:, None, :], axis=-1)
    n_all = n_rows // tm
    starts = jnp.concatenate([jnp.ones((2, 1), jnp.int32),
                              (tile_expert[:, 1:] != tile_expert[:, :-1]).astype(jnp.int32)], axis=1)
    tile_run = jnp.cumsum(starts, axis=1) - 1
    run_hot = (tile_run[:, :, None] == np.arange(n_all)[None, None, :]).astype(jnp.int32)
    run_expert = jnp.sum(run_hot * (starts * tile_expert)[:, :, None], axis=1)
    n_runs = tile_run[:, -1] + 1
    n_tiles = (seg_end[-1] / tm).astype(jnp.int32).reshape(1)
    tail_start = jnp.maximum(seg_end - tm, 0.0).astype(jnp.int32)
    plan = (tile_run.astype(jnp.int32), run_expert.astype(jnp.int32), n_runs.astype(jnp.int32), n_tiles)
    return plan, tail_start, dest, n_rows


def _moe(x1, route, w_gate, w_up, w_down, layer, g, b):
    n = x1.shape[0]
    plan, tail_start, dest, n_rows = _dispatch_plan(route, n)
    xs = _dispatch(x1, route, dest, tail_start, plan[-1], n_rows)
    y_rows = _experts(xs, w_gate, w_up, w_down, layer, *plan)
    return _combine(y_rows, x1, dest, g, b)


def _vec(v):
    return v.reshape(1, -1).astype(F32)


def kernel(x, w_in_attn, w_out_attn, rel_bias, rpb_2d, w_pw1, b_pw1, w_dw, b_dw, conv_ln_g, conv_ln_b, w_pw2, b_pw2, ln_mix_g, ln_mix_b, ln_ffn_g, ln_ffn_b, w_router, router_bias, w_gate, w_up, w_down):
    bsz, seq, d = x.shape
    assert d == D_MODEL and seq % SEQ_TILE == 0 and seq % GRID_W == 0, x.shape
    assert w_gate.shape[:2] == (DEPTH, N_EXPERTS) and w_gate.dtype == F32, (w_gate.shape, w_gate.dtype)
    n = bsz * seq
    h = x.reshape(n, d)
    wr_t = w_router.T.astype(F32)
    wr_hi = wr_t.astype(BF16)
    wr_lo = (wr_t - wr_hi.astype(F32)).astype(BF16)
    wr = jnp.concatenate([wr_hi, wr_lo], axis=0)
    rb = router_bias.reshape(N_EXPERTS, 1).astype(F32)
    n_dil_cols = len(DILATIONS) * GROUP_COLS
    q_cols = np.zeros((3 * d,), np.float32) + 1.0
    q_cols[0:n_dil_cols] = HEAD_DIM ** -0.5
    q_cols[3 * n_dil_cols:3 * n_dil_cols + GROUP_COLS] = HEAD_DIM ** -0.5

    for layer in range(DEPTH):
        i = layer // 2
        g_mix, b_mix = _vec(ln_mix_g[layer]), _vec(ln_mix_b[layer])
        if layer % 2 == 0:
            w_in = (w_in_attn[i] * q_cols[None, :]).astype(BF16)
            proj = _matmul(h, w_in)
            lhs = []
            for grp in range(len(DILATIONS)):
                lhs += _dilated_attention(proj, _dilated_bias(rel_bias, grp), grp, bsz, seq)
            lhs.append(_neighborhood_attention(proj, _na_bias(rpb_2d[i]), bsz, seq))
            x1, route = _post_mixer(_post_attn_kernel, "post_attn", lhs,
                                    [w_out_attn[i].astype(BF16)], h, g_mix, b_mix, wr, rb)
        else:
            u = _conv_front(h.reshape(bsz, seq, d), w_pw1[i].astype(BF16), _vec(b_pw1[i]),
                            w_dw[i].astype(F32), _vec(b_dw[i]), _vec(conv_ln_g[i]), _vec(conv_ln_b[i]))
            x1, route = _post_mixer(_post_conv_kernel, "post_conv", [u.reshape(n, d)],
                                    [w_pw2[i].astype(BF16), _vec(b_pw2[i])], h, g_mix, b_mix, wr, rb)
        h = _moe(x1, route, w_gate, w_up, w_down, layer,
                 _vec(ln_ffn_g[layer]), _vec(ln_ffn_b[layer]))
    return h.reshape(bsz, seq, d)
```

```python
import functools
import math

import jax
import jax.numpy as jnp
import numpy as np
from jax import lax
from jax.experimental import pallas as pl
from jax.experimental.pallas import tpu as pltpu

D_MODEL = 1024
HEAD_DIM = 64
GROUP_HEADS = 4
GROUP_COLS = GROUP_HEADS * HEAD_DIM
DILATIONS = (1, 4, 16)
QUERY_BLOCK = 64
GRID_W = 64
NA_ROWS = 8
NA_COLS = 16
T5_BUCKETS = 32
T5_MAX_DIST = 1024
CONV_WIDTH = 31
CONV_HALO = 16
N_EXPERTS = 16
GROUP_EXPERTS = 4
DEPTH = 4
ALPHA = (2 * DEPTH) ** 0.25
LN_EPS = 1e-5
NEG = -1e30

LANES = 128
ROW_CHUNKS = D_MODEL // LANES
GROUP_CHUNKS = GROUP_COLS // LANES
PAIR_ORDER = ((0, 1), (0, 2), (0, 3), (1, 3), (1, 2), (3, 2))
N_CLASSES = (N_EXPERTS // GROUP_EXPERTS) * len(PAIR_ORDER)
PACKED_SUBLANES = D_MODEL // (2 * LANES)
ROUTE_ROWS = 8
GATE_LANE = 1
SEQ_TILE = 2048
EXPERT_TILE = 256
COMBINE_TILE = 512
ROW_COPY_UNROLL = 8
VMEM_LIMIT = 56 * 1024 * 1024

F32 = jnp.float32
BF16 = jnp.bfloat16


def _params(semantics, vmem=VMEM_LIMIT):
    return pltpu.CompilerParams(dimension_semantics=semantics, vmem_limit_bytes=vmem)


MATMUL_SLAB = 512


def _pack_bf16_pair(lo, hi):
    lo_bits = pltpu.bitcast(lo.astype(BF16).astype(F32), jnp.uint32)
    hi_bits = pltpu.bitcast(hi.astype(BF16).astype(F32), jnp.uint32)
    return (lo_bits >> 16) | (hi_bits & jnp.uint32(0xFFFF0000))


def _unpack_bf16_pair(word):
    return pltpu.bitcast(word << 16, F32), pltpu.bitcast(word & jnp.uint32(0xFFFF0000), F32)


def _matmul_kernel(x_ref, w_ref, o_ref):
    xb = x_ref[...].astype(BF16)
    per_slab = MATMUL_SLAB // GROUP_COLS
    for s in range(w_ref.shape[1] // MATMUL_SLAB):
        y = jnp.dot(xb, w_ref[:, s * MATMUL_SLAB:(s + 1) * MATMUL_SLAB], preferred_element_type=F32)
        for c in range(per_slab):
            lo = y[:, (2 * c) * LANES:(2 * c + 1) * LANES]
            hi = y[:, (2 * c + 1) * LANES:(2 * c + 2) * LANES]
            o_ref[s * per_slab + c] = _pack_bf16_pair(lo, hi)


def _matmul(x, w, tm=1024):
    n, k = x.shape
    _, m = w.shape
    return pl.pallas_call(
        _matmul_kernel,
        grid=(n // tm,),
        in_specs=[pl.BlockSpec((tm, k), lambda i: (i, 0)),
                  pl.BlockSpec((k, m), lambda i: (0, 0))],
        out_specs=pl.BlockSpec((m // GROUP_COLS, tm, LANES), lambda i: (0, i, 0)),
        out_shape=jax.ShapeDtypeStruct((m // GROUP_COLS, n, LANES), jnp.uint32),
        compiler_params=_params(("parallel",)),
        name="qkv_proj",
    )(x, w)


def _head_masks(rows):
    lane = lax.broadcasted_iota(jnp.int32, (rows, GROUP_COLS), 1)
    return [(lane >= HEAD_DIM * h) & (lane < HEAD_DIM * (h + 1)) for h in range(GROUP_HEADS)]


def _stack_heads(q, masks):
    return jnp.concatenate([jnp.where(m, q, 0.0) for m in masks], axis=0)


def _unstack_heads(o4, masks, rows):
    out = jnp.where(masks[0], o4[0:rows], 0.0)
    for h in range(1, GROUP_HEADS):
        out = out + jnp.where(masks[h], o4[h * rows:(h + 1) * rows], 0.0)
    return out


def _softmax_pv(logits, v, with_lse):
    m = jnp.max(logits, axis=-1, keepdims=True)
    p = jnp.exp(logits - m)
    s = jnp.sum(p, axis=-1, keepdims=True)
    o4 = jnp.dot(p.astype(BF16), v, preferred_element_type=F32) * (1.0 / s)
    lse = (m + jnp.log(s)) if with_lse else None
    return o4, lse


def _dilated_kernel(q_ref, kp_ref, kc_ref, kn_ref, vp_ref, vc_ref, vn_ref, bias_ref,
                    o_ref, l_ref, *, dil):
    t = pl.program_id(1)
    chunk = QUERY_BLOCK * dil
    n_chunks = SEQ_TILE // chunk
    qb = QUERY_BLOCK
    masks = _head_masks(qb)
    col = lax.broadcasted_iota(jnp.int32, (GROUP_HEADS * qb, 3 * qb), 1)
    no_prev = jnp.where(col < qb, NEG, 0.0) * (t == 0).astype(F32)
    no_next = jnp.where(col >= 2 * qb, NEG, 0.0) * (t == pl.num_programs(1) - 1).astype(F32)
    bias = bias_ref[...]

    def sl(start):
        return pl.ds(start, qb) if dil == 1 else pl.ds(start, qb, stride=dil)

    def rows(ref, start):
        return jnp.concatenate(_unpack_bf16_pair(ref[0, sl(start), :]), axis=-1)

    def window(p_ref, c_ref, n_ref, c, r):
        parts = []
        for cc in (c - 1, c, c + 1):
            if cc < 0:
                parts.append(rows(p_ref, r))
            elif cc >= n_chunks:
                parts.append(rows(n_ref, r))
            else:
                parts.append(rows(c_ref, cc * chunk + r))
        return jnp.concatenate(parts, axis=0).astype(BF16)

    for c in range(n_chunks):
        for r in range(dil):
            base = c * chunk + r
            q4 = _stack_heads(rows(q_ref, base), masks).astype(BF16)
            k = window(kp_ref, kc_ref, kn_ref, c, r)
            v = window(vp_ref, vc_ref, vn_ref, c, r)
            logits = lax.dot_general(q4, k, (((1,), (1,)), ((), ())),
                                     preferred_element_type=F32) + bias
            if c == 0:
                logits = logits + no_prev
            if c == n_chunks - 1:
                logits = logits + no_next
            o4, lse = _softmax_pv(logits, v, True)
            o = _unstack_heads(o4, masks, qb)
            lb = _unstack_heads(jnp.broadcast_to(lse, (GROUP_HEADS * qb, GROUP_COLS)), masks, qb)
            o_ref[0, sl(base), :] = _pack_bf16_pair(o[:, 0:LANES], o[:, LANES:])
            for ch in range(GROUP_CHUNKS):
                l_ref[ch, sl(base), :] = lb[:, ch * LANES:(ch + 1) * LANES]


def _tile_spec(col_block, seq, chunks=1):
    tiles = seq // SEQ_TILE
    return pl.BlockSpec((chunks, SEQ_TILE, LANES), lambda b, t: (col_block, b * tiles + t, 0))


def _halo_specs(col_block, halo, seq):
    per_tile = SEQ_TILE // halo
    per_seq = seq // halo
    return [
        pl.BlockSpec((1, halo, LANES),
                     lambda b, t: (col_block, b * per_seq + jnp.maximum(t * per_tile - 1, 0), 0)),
        _tile_spec(col_block, seq),
        pl.BlockSpec((1, halo, LANES),
                     lambda b, t: (col_block, b * per_seq + jnp.minimum((t + 1) * per_tile, per_seq - 1), 0)),
    ]


def _dilated_attention(proj, bias, group, bsz, seq):
    dil = DILATIONS[group]
    n_groups = len(DILATIONS)
    q_col, k_col, v_col = group, n_groups + group, 2 * n_groups + group
    return pl.pallas_call(
        functools.partial(_dilated_kernel, dil=dil),
        grid=(bsz, seq // SEQ_TILE),
        in_specs=[_tile_spec(q_col, seq)]
        + _halo_specs(k_col, QUERY_BLOCK * dil, seq) + _halo_specs(v_col, QUERY_BLOCK * dil, seq)
        + [pl.BlockSpec(bias.shape, lambda b, t: (0, 0))],
        out_specs=[_tile_spec(0, seq), _tile_spec(0, seq, GROUP_CHUNKS)],
        out_shape=[jax.ShapeDtypeStruct((1, bsz * seq, LANES), jnp.uint32),
                   jax.ShapeDtypeStruct((GROUP_CHUNKS, bsz * seq, LANES), F32)],
        compiler_params=_params(("parallel", "arbitrary")),
        name=f"dilated_attn_d{dil}",
    )(proj, proj, proj, proj, proj, proj, proj, bias)


NA_HALO = 256
NA_KEYS = NA_ROWS * GRID_W


def _na_kernel(q_ref, kp_ref, kc_ref, kn_ref, vp_ref, vc_ref, vn_ref, bias_ref, o_ref,
               kwin, vwin, *, grid_rows):
    t = pl.program_id(1)
    tile_rows = SEQ_TILE // GRID_W
    half = NA_ROWS // 2
    for win, p_ref, c_ref, n_ref in ((kwin, kp_ref, kc_ref, kn_ref), (vwin, vp_ref, vc_ref, vn_ref)):
        for first, src in ((0, p_ref), (NA_HALO, c_ref), (NA_HALO + SEQ_TILE, n_ref)):
            n_rows = src.shape[1]
            for ch, part in enumerate(_unpack_bf16_pair(src[0])):
                win[first:first + n_rows, ch * LANES:(ch + 1) * LANES] = part.astype(BF16)
    masks = _head_masks(GRID_W)
    for i in range(tile_rows):
        row = t * tile_rows + i
        row0 = jnp.clip(row - half, 0, grid_rows - NA_ROWS)
        off = pl.multiple_of((row0 - (t * tile_rows - half)) * GRID_W, GRID_W)
        q = jnp.concatenate(_unpack_bf16_pair(q_ref[0, i * GRID_W:(i + 1) * GRID_W, :]), axis=-1)
        q4 = _stack_heads(q, masks).astype(BF16)
        k = kwin[pl.ds(off, NA_KEYS), :]
        v = vwin[pl.ds(off, NA_KEYS), :]
        logits = lax.dot_general(q4, k, (((1,), (1,)), ((), ())),
                                 preferred_element_type=F32) + bias_ref[row - row0]
        o4, _ = _softmax_pv(logits, v, False)
        o = _unstack_heads(o4, masks, GRID_W)
        o_ref[0, i * GRID_W:(i + 1) * GRID_W, :] = _pack_bf16_pair(o[:, 0:LANES], o[:, LANES:])


def _neighborhood_attention(proj, bias, bsz, seq):
    base = 3 * len(DILATIONS)
    win_rows = SEQ_TILE + 2 * NA_HALO
    return pl.pallas_call(
        functools.partial(_na_kernel, grid_rows=seq // GRID_W),
        grid=(bsz, seq // SEQ_TILE),
        in_specs=[_tile_spec(base, seq)]
        + _halo_specs(base + 1, NA_HALO, seq) + _halo_specs(base + 2, NA_HALO, seq)
        + [pl.BlockSpec(bias.shape, lambda b, t: (0, 0, 0))],
        out_specs=_tile_spec(0, seq),
        out_shape=jax.ShapeDtypeStruct((1, bsz * seq, LANES), jnp.uint32),
        scratch_shapes=[pltpu.VMEM((win_rows, GROUP_COLS), BF16),
                        pltpu.VMEM((win_rows, GROUP_COLS), BF16)],
        compiler_params=_params(("parallel", "arbitrary")),
        name="neighborhood_attn",
    )(proj, proj, proj, proj, proj, proj, proj, bias)


def _t5_bucket_table(dil):
    qb = QUERY_BLOCK
    rel = (np.arange(3 * qb)[None, :] - qb - np.arange(qb)[:, None])
    nb = T5_BUCKETS // 2
    max_exact = nb // 2
    n = np.abs(rel * dil)
    nf = np.maximum(n, 1).astype(np.float32)
    large = max_exact + (np.log(nf / np.float32(max_exact)) / np.float32(math.log(T5_MAX_DIST / max_exact))
                         * np.float32(nb - max_exact)).astype(np.int32)
    large = np.minimum(large, nb - 1)
    bucket = np.where(rel * dil > 0, nb, 0) + np.where(n < max_exact, n, large)
    return bucket.astype(np.int32), np.abs(rel) <= qb


def _dilated_bias(rel_bias, group):
    bucket, band = _t5_bucket_table(DILATIONS[group])
    tab = rel_bias[:, group * GROUP_HEADS:(group + 1) * GROUP_HEADS].astype(F32)
    onehot = (bucket[None, :, :] == np.arange(T5_BUCKETS)[:, None, None]).astype(np.float32)
    bias = jnp.einsum("bh,bqk->hqk", tab, onehot, precision=lax.Precision.HIGHEST)
    bias = jnp.where(band[None], bias, NEG)
    return bias.reshape(GROUP_HEADS * QUERY_BLOCK, 3 * QUERY_BLOCK)


def _na_bias(rpb):
    qc = np.arange(GRID_W)[:, None]
    kc = np.arange(GRID_W)[None, :]
    win0 = np.clip(qc - NA_COLS // 2, 0, GRID_W - NA_COLS)
    col_ok = (kc >= win0) & (kc < win0 + NA_COLS)
    dc = np.clip(kc - qc + NA_COLS - 1, 0, 2 * NA_COLS - 2)
    shift = np.arange(NA_ROWS)[:, None]
    kr = np.arange(NA_ROWS)[None, :]
    dr = kr - shift + NA_ROWS - 1
    pick_row = (dr[:, :, None] == np.arange(2 * NA_ROWS - 1)[None, None, :]).astype(np.float32)
    pick_col = (dc[None, :, :] == np.arange(2 * NA_COLS - 1)[:, None, None]).astype(np.float32)
    hi = lax.Precision.HIGHEST
    rows = jnp.einsum("hac,dka->hdkc", rpb.astype(F32), pick_row, precision=hi)
    b = jnp.einsum("hdkc,cqx->dhqkx", rows, pick_col, precision=hi)
    b = jnp.where(col_ok[None, None, :, None, :], b, NEG)
    return b.reshape(NA_ROWS, GROUP_HEADS * GRID_W, NA_KEYS)


def _layer_norm(z, g, b):
    mu = jnp.mean(z, axis=-1, keepdims=True)
    zc = z - mu
    var = jnp.mean(zc * zc, axis=-1, keepdims=True)
    return zc * lax.rsqrt(var + LN_EPS) * g + b


def _row(x, i):
    return x[i:i + 1, :]


def _route(logits, rbias):
    scores = jax.nn.sigmoid(logits)
    sel = scores + rbias
    ge = GROUP_EXPERTS
    n_groups = N_EXPERTS // ge
    best_sum, best_g = None, None
    for g in range(n_groups):
        vals = [_row(sel, g * ge + j) for j in range(ge)]
        top2 = None
        for a in range(ge):
            for b in range(a + 1, ge):
                pair = vals[a] + vals[b]
                top2 = pair if top2 is None else jnp.maximum(top2, pair)
        if best_sum is None:
            best_sum, best_g = top2, jnp.zeros_like(top2, dtype=jnp.int32)
        else:
            upd = top2 > best_sum
            best_g = jnp.where(upd, g, best_g)
            best_sum = jnp.where(upd, top2, best_sum)

    def in_group(x, j):
        out = _row(x, j)
        for g in range(1, n_groups):
            out = jnp.where(best_g == g, _row(x, g * ge + j), out)
        return out

    cand = [in_group(sel, j) for j in range(ge)]
    cand_score = [in_group(scores, j) for j in range(ge)]
    v1, j1, w1 = cand[0], jnp.zeros_like(best_g), cand_score[0]
    for j in range(1, ge):
        upd = cand[j] > v1
        v1 = jnp.where(upd, cand[j], v1)
        j1 = jnp.where(upd, j, j1)
        w1 = jnp.where(upd, cand_score[j], w1)
    v2 = jnp.full_like(v1, -jnp.inf)
    j2, w2 = jnp.zeros_like(best_g), jnp.zeros_like(w1)
    for j in range(ge):
        upd = (cand[j] > v2) & (j1 != j)
        v2 = jnp.where(upd, cand[j], v2)
        j2 = jnp.where(upd, j, j2)
        w2 = jnp.where(upd, cand_score[j], w2)
    wsum = w1 + w2
    g1, g2 = w1 / wsum, w2 / wsum
    lo, hi = jnp.minimum(j1, j2), jnp.maximum(j1, j2)
    pair = jnp.zeros_like(best_g)
    first = jnp.zeros_like(best_g)
    for q, (a, b) in enumerate(PAIR_ORDER):
        hit = (lo == min(a, b)) & (hi == max(a, b))
        pair = jnp.where(hit, q, pair)
        first = jnp.where(hit, a, first)
    is_first = j1 == first
    g_first = jnp.where(is_first, g1, g2)
    g_second = jnp.where(is_first, g2, g1)
    cls = (best_g * len(PAIR_ORDER) + pair).astype(F32)
    rows = [cls, g_first, g_second]
    return jnp.concatenate(rows + [jnp.zeros_like(w1)] * (ROUTE_ROWS - len(rows)), axis=0)


def _post_tail(m, x_ref, g_ref, b_ref, wr_ref, rb_ref, x1_ref, route_ref):
    xn = _layer_norm(ALPHA * x_ref[...] + m, g_ref[...], b_ref[...])
    x1_ref[...] = xn
    xh = xn.astype(BF16)
    xl = (xn - xh.astype(F32)).astype(BF16)
    wr = wr_ref[...]
    contract = (((1,), (1,)), ((), ()))
    a = lax.dot_general(wr, xh, contract, preferred_element_type=F32)
    c = lax.dot_general(wr[0:N_EXPERTS], xl, contract, preferred_element_type=F32)
    logits = a[0:N_EXPERTS] + a[N_EXPERTS:] + c
    route_ref[...] = _route(logits, rb_ref[...])


def _pack_token_rows(x, route, dst):
    tm = x.shape[0]
    for s in range(PACKED_SUBLANES):
        lo = x[:, (2 * s) * LANES:(2 * s + 1) * LANES]
        hi = x[:, (2 * s + 1) * LANES:(2 * s + 2) * LANES]
        dst[pl.ds(s, tm, stride=ROW_CHUNKS), :] = _pack_bf16_pair(lo, hi)
    padded = jnp.concatenate([route, jnp.zeros((LANES - route.shape[0], tm), F32)], axis=0)
    dst[pl.ds(PACKED_SUBLANES, tm, stride=ROW_CHUNKS), :] = pltpu.bitcast(padded.T, jnp.uint32)


def _zero_unused_sublanes(dst, tm):
    for s in range(PACKED_SUBLANES + 1, ROW_CHUNKS):
        dst[pl.ds(s, tm, stride=ROW_CHUNKS), :] = jnp.zeros((tm, LANES), jnp.uint32)


def _post_attn_kernel(o0_ref, l0_ref, o1_ref, l1_ref, o2_ref, l2_ref, ob_ref, w_ref,
                      x_ref, g_ref, b_ref, wr_ref, rb_ref, x1_ref, route_ref):
    def wide(ref):
        return jnp.concatenate([ref[ch] for ch in range(GROUP_CHUNKS)], axis=-1)

    def out(ref):
        return jnp.concatenate(_unpack_bf16_pair(ref[0]), axis=-1)

    l0, l1, l2 = wide(l0_ref), wide(l1_ref), wide(l2_ref)
    mx = jnp.maximum(jnp.maximum(l0, l1), l2)
    e0, e1, e2 = jnp.exp(l0 - mx), jnp.exp(l1 - mx), jnp.exp(l2 - mx)
    inv = 1.0 / (e0 + e1 + e2)
    lhs = jnp.concatenate([out(o0_ref) * (e0 * inv), out(o1_ref) * (e1 * inv),
                           out(o2_ref) * (e2 * inv), out(ob_ref)], axis=-1).astype(BF16)
    m = jnp.dot(lhs, w_ref[...], preferred_element_type=F32)
    _post_tail(m, x_ref, g_ref, b_ref, wr_ref, rb_ref, x1_ref, route_ref)


def _post_conv_kernel(u_ref, w_ref, wb_ref, x_ref, g_ref, b_ref, wr_ref, rb_ref,
                      x1_ref, route_ref):
    m = jnp.dot(u_ref[...], w_ref[...], preferred_element_type=F32) + wb_ref[...]
    _post_tail(m, x_ref, g_ref, b_ref, wr_ref, rb_ref, x1_ref, route_ref)


def _post_mixer(kernel, name, lhs_list, consts_front, x, g, b, wr, rb, tm=1024):
    n, d = x.shape
    row = lambda i: (i, 0)
    const = lambda i: (0, 0)
    in_specs = [pl.BlockSpec((tm, a.shape[1]), row) if a.ndim == 2
                else pl.BlockSpec((a.shape[0], tm, LANES), lambda i: (0, i, 0)) for a in lhs_list]
    in_specs += [pl.BlockSpec(c.shape, const) for c in consts_front]
    in_specs += [pl.BlockSpec((tm, d), row)]
    in_specs += [pl.BlockSpec(c.shape, const) for c in (g, b, wr, rb)]
    return pl.pallas_call(
        kernel,
        grid=(n // tm,),
        in_specs=in_specs,
        out_specs=[pl.BlockSpec((tm, d), row),
                   pl.BlockSpec((ROUTE_ROWS, tm), lambda i: (0, i))],
        out_shape=[jax.ShapeDtypeStruct((n, d), F32),
                   jax.ShapeDtypeStruct((ROUTE_ROWS, n), F32)],
        compiler_params=_params(("parallel",)),
        name=name,
    )(*lhs_list, *consts_front, x, g, b, wr, rb)


CONV_SLAB = 256
CONV_ROWS = 256


def _conv_front_kernel(xp_ref, xc_ref, xn_ref, w_ref, wb_ref, taps_ref, cb_ref, g_ref, b_ref,
                       o_ref, win, acc, *, ts):
    t = pl.program_id(1)
    has_prev = (t > 0).astype(F32)
    has_next = (t < pl.num_programs(1) - 1).astype(F32)
    rows = ts + 2 * CONV_HALO
    row = lax.broadcasted_iota(jnp.int32, (rows, 1), 0)
    keep = jnp.where(row < CONV_HALO, has_prev, jnp.where(row >= CONV_HALO + ts, has_next, 1.0))
    xw = jnp.concatenate([xp_ref[...], xc_ref[...], xn_ref[...]], axis=0).astype(BF16)
    half = w_ref.shape[1] // 2
    chunks_per_slab = CONV_SLAB // LANES
    for s in range(half // CONV_SLAB):
        cols = slice(s * CONV_SLAB, (s + 1) * CONV_SLAB)
        gate_cols = slice(half + s * CONV_SLAB, half + (s + 1) * CONV_SLAB)
        a = jnp.dot(xw, w_ref[:, cols], preferred_element_type=F32) + wb_ref[:, cols]
        gt = jnp.dot(xw, w_ref[:, gate_cols], preferred_element_type=F32) + wb_ref[:, gate_cols]
        u = a * jax.nn.sigmoid(gt) * keep
        for c in range(chunks_per_slab):
            j = s * chunks_per_slab + c
            win[j] = u[:, c * LANES:(c + 1) * LANES]
            for r0 in range(0, ts, CONV_ROWS):
                conv = jnp.zeros((CONV_ROWS, LANES), F32)
                for tap in range(CONV_WIDTH):
                    start = r0 + CONV_HALO - CONV_WIDTH // 2 + tap
                    conv = conv + win[j, pl.ds(start, CONV_ROWS), :] * taps_ref[j, tap:tap + 1, :]
                acc[j, r0:r0 + CONV_ROWS, :] = conv
    u = jnp.concatenate([acc[j] for j in range(ROW_CHUNKS)], axis=-1) + cb_ref[...]
    y = _layer_norm(u, g_ref[...], b_ref[...])
    o_ref[...] = (y * jax.nn.sigmoid(y)).astype(o_ref.dtype)


def _conv_front(x, w_pw1, b_pw1, w_dw, b_dw, g, b, ts=1024):
    bsz, seq, d = x.shape
    per_tile = ts // CONV_HALO
    last = seq // CONV_HALO - 1
    padded_taps = 2 * CONV_HALO
    taps = jnp.pad(w_dw, ((0, padded_taps - CONV_WIDTH), (0, 0)))
    taps = taps.reshape(padded_taps, ROW_CHUNKS, LANES).transpose(1, 0, 2)
    vec = pl.BlockSpec((1, d), lambda bb, t: (0, 0))
    return pl.pallas_call(
        functools.partial(_conv_front_kernel, ts=ts),
        grid=(bsz, seq // ts),
        in_specs=[pl.BlockSpec((None, CONV_HALO, d), lambda bb, t: (bb, jnp.maximum(t * per_tile - 1, 0), 0)),
                  pl.BlockSpec((None, ts, d), lambda bb, t: (bb, t, 0)),
                  pl.BlockSpec((None, CONV_HALO, d), lambda bb, t: (bb, jnp.minimum((t + 1) * per_tile, last), 0)),
                  pl.BlockSpec(w_pw1.shape, lambda bb, t: (0, 0)),
                  pl.BlockSpec(b_pw1.shape, lambda bb, t: (0, 0)),
                  pl.BlockSpec(taps.shape, lambda bb, t: (0, 0, 0)),
                  vec, vec, vec],
        out_specs=pl.BlockSpec((None, ts, d), lambda bb, t: (bb, t, 0)),
        out_shape=jax.ShapeDtypeStruct((bsz, seq, d), BF16),
        scratch_shapes=[pltpu.VMEM((ROW_CHUNKS, ts + 2 * CONV_HALO, LANES), F32),
                        pltpu.VMEM((ROW_CHUNKS, ts, LANES), F32)],
        compiler_params=_params(("parallel", "arbitrary")),
        name="conv_front",
    )(x, x, x, w_pw1, b_pw1, taps, b_dw, g, b)


def _row_gather_start(src_hbm, dst, sem, idx_ref, first, count):
    def body(r8, carry):
        for u in range(ROW_COPY_UNROLL):
            r = r8 * ROW_COPY_UNROLL + u
            tok = idx_ref[first + r]
            pltpu.make_async_copy(src_hbm.at[pl.ds(pl.multiple_of(tok * ROW_CHUNKS, ROW_CHUNKS), ROW_CHUNKS)],
                                  dst.at[pl.ds(pl.multiple_of(r * ROW_CHUNKS, ROW_CHUNKS), ROW_CHUNKS)],
                                  sem).start(priority=u % 2)
        return carry
    lax.fori_loop(0, count // ROW_COPY_UNROLL, body, 0)


def _row_gather_wait(src_hbm, dst, sem):
    pltpu.make_async_copy(src_hbm.at[pl.ds(0, dst.shape[0])], dst, sem).wait()


def _dispatch_kernel(dest_ref, tail_ref, n_tiles_ref, x_ref, route_ref, xs_hbm, zeros, stage, sem, zsem, *, tm):
    i = pl.program_id(0)
    n_steps = pl.num_programs(0)
    slot = i % 2
    tile_rows = EXPERT_TILE * ROW_CHUNKS
    all_tiles = xs_hbm.shape[0] // tile_rows

    def zero_tile(first_row):
        start = pl.multiple_of(first_row * ROW_CHUNKS, ROW_CHUNKS)
        return pltpu.make_async_copy(zeros, xs_hbm.at[pl.ds(start, tile_rows)], zsem)

    @pl.when(i == 0)
    def _():
        zeros[...] = jnp.zeros_like(zeros)

        def start_unused(j, carry):
            zero_tile(j * EXPERT_TILE).start()
            return carry

        def wait_unused(j, carry):
            zero_tile(j * EXPERT_TILE).wait()
            return carry

        for c in range(N_CLASSES):
            zero_tile(tail_ref[c]).start()
        lax.fori_loop(n_tiles_ref[0], all_tiles, start_unused, 0)
        for c in range(N_CLASSES):
            zero_tile(tail_ref[c]).wait()
        lax.fori_loop(n_tiles_ref[0], all_tiles, wait_unused, 0)

    @pl.when(i < 2)
    def _():
        _zero_unused_sublanes(stage.at[slot], tm)

    _pack_token_rows(x_ref[...], route_ref[...], stage.at[slot])

    def body(r8, carry):
        for u in range(ROW_COPY_UNROLL):
            r = r8 * ROW_COPY_UNROLL + u
            d = dest_ref[i * tm + r]
            pltpu.make_async_copy(stage.at[slot, pl.ds(pl.multiple_of(r * ROW_CHUNKS, ROW_CHUNKS), ROW_CHUNKS)],
                                  xs_hbm.at[pl.ds(pl.multiple_of(d * ROW_CHUNKS, ROW_CHUNKS), ROW_CHUNKS)],
                                  sem.at[slot]).start(priority=u % 2)
        return carry
    lax.fori_loop(0, tm // ROW_COPY_UNROLL, body, 0)

    def wait_step(s):
        pltpu.make_async_copy(stage.at[s], xs_hbm.at[pl.ds(0, tm * ROW_CHUNKS)], sem.at[s]).wait()

    @pl.when(i > 0)
    def _():
        wait_step(1 - slot)

    @pl.when(i == n_steps - 1)
    def _():
        wait_step(slot)


def _dispatch(x1, route, dest, tail_start, n_tiles, n_rows, tm=512):
    n_tokens, d = x1.shape
    grid_spec = pltpu.PrefetchScalarGridSpec(
        num_scalar_prefetch=3,
        grid=(n_tokens // tm,),
        in_specs=[pl.BlockSpec((tm, d), lambda i, dst, tail, nt: (i, 0)),
                  pl.BlockSpec((ROUTE_ROWS, tm), lambda i, dst, tail, nt: (0, i))],
        out_specs=pl.BlockSpec(memory_space=pl.ANY),
        scratch_shapes=[pltpu.VMEM((EXPERT_TILE * ROW_CHUNKS, LANES), jnp.uint32),
                        pltpu.VMEM((2, tm * ROW_CHUNKS, LANES), jnp.uint32),
                        pltpu.SemaphoreType.DMA((2,)),
                        pltpu.SemaphoreType.DMA(())],
    )
    return pl.pallas_call(
        functools.partial(_dispatch_kernel, tm=tm),
        grid_spec=grid_spec,
        out_shape=jax.ShapeDtypeStruct((n_rows * ROW_CHUNKS, LANES), jnp.uint32),
        compiler_params=_params(("arbitrary",)),
        name="moe_dispatch",
    )(dest, tail_start, n_tiles, x1, route)


def _expert_kernel(run_ref, run_expert_ref, n_runs_ref, n_tiles_ref, x_ref,
                   wg_hbm, wu_hbm, wd_hbm, y_ref, stage_g, stage_u, stage_d,
                   wg1_bf, wu1_bf, wd1_bf, wg2_bf, wu2_bf, wd2_bf, sem, *, layer):
    i = pl.program_id(0)
    tm = EXPERT_TILE
    before = jnp.maximum(i - 1, 0)
    working = ((wg1_bf, wu1_bf, wd1_bf), (wg2_bf, wu2_bf, wd2_bf))

    def copies(role, run):
        slot = run % 2
        e = run_expert_ref[role, run]
        return [pltpu.make_async_copy(w.at[layer, e], st.at[role, slot], sem.at[role, slot])
                for w, st in ((wg_hbm, stage_g), (wu_hbm, stage_u), (wd_hbm, stage_d))]

    for role in range(2):
        run = run_ref[role, i]

        @pl.when(i == 0)
        def _():
            for cp in copies(role, 0):
                cp.start(priority=1)

        @pl.when((i == 0) | (run != run_ref[role, before]))
        def _():
            for cp in copies(role, run):
                cp.wait()

            @pl.when(run + 1 < n_runs_ref[role])
            def _():
                for cp in copies(role, run + 1):
                    cp.start(priority=1)

            slot = run % 2
            for st, wk in zip((stage_g, stage_u, stage_d), working[role]):
                wk[...] = st[role, slot].astype(BF16)

    @pl.when(i < n_tiles_ref[0])
    def _():
        halves = []
        for s in range(PACKED_SUBLANES):
            halves.extend(_unpack_bf16_pair(x_ref[pl.ds(s, tm, stride=ROW_CHUNKS), :]))
        x = jnp.concatenate(halves, axis=-1).astype(BF16)
        gates = pltpu.bitcast(x_ref[pl.ds(PACKED_SUBLANES, tm, stride=ROW_CHUNKS), :], F32)

        def ffn(wg, wu, wd):
            hg = jnp.dot(x, wg[...], preferred_element_type=F32)
            hu = jnp.dot(x, wu[...], preferred_element_type=F32)
            hid = (hg * jax.nn.sigmoid(hg) * hu).astype(BF16)
            return jnp.dot(hid, wd[...], preferred_element_type=F32)

        y = (gates[:, GATE_LANE:GATE_LANE + 1] * ffn(wg1_bf, wu1_bf, wd1_bf)
             + gates[:, GATE_LANE + 1:GATE_LANE + 2] * ffn(wg2_bf, wu2_bf, wd2_bf))
        for j in range(ROW_CHUNKS):
            y_ref[pl.ds(j, tm, stride=ROW_CHUNKS), :] = y[:, j * LANES:(j + 1) * LANES]

    @pl.when(i >= n_tiles_ref[0])
    def _():
        y_ref[...] = jnp.zeros_like(y_ref)


def _experts(xs, w_gate, w_up, w_down, layer, tile_run, run_expert, n_runs, n_tiles):
    tm = EXPERT_TILE
    n_rows = xs.shape[0] // ROW_CHUNKS
    d, de = w_gate.shape[2], w_gate.shape[3]
    row_map = lambda i, run, rexp, nr, nt: (jnp.minimum(i, jnp.maximum(nt[0] - 1, 0)), 0)
    hbm = pl.BlockSpec(memory_space=pl.ANY)
    grid_spec = pltpu.PrefetchScalarGridSpec(
        num_scalar_prefetch=4,
        grid=(n_rows // tm,),
        in_specs=[pl.BlockSpec((tm * ROW_CHUNKS, LANES), row_map), hbm, hbm, hbm],
        out_specs=pl.BlockSpec((tm * ROW_CHUNKS, LANES), lambda i, run, rexp, nr, nt: (i, 0)),
        scratch_shapes=[pltpu.VMEM((2, 2, d, de), F32), pltpu.VMEM((2, 2, d, de), F32),
                        pltpu.VMEM((2, 2, de, d), F32),
                        pltpu.VMEM((d, de), BF16), pltpu.VMEM((d, de), BF16), pltpu.VMEM((de, d), BF16),
                        pltpu.VMEM((d, de), BF16), pltpu.VMEM((d, de), BF16), pltpu.VMEM((de, d), BF16),
                        pltpu.SemaphoreType.DMA((2, 2))],
    )
    return pl.pallas_call(
        functools.partial(_expert_kernel, layer=layer),
        grid_spec=grid_spec,
        out_shape=jax.ShapeDtypeStruct((n_rows * ROW_CHUNKS, LANES), F32),
        compiler_params=_params(("arbitrary",)),
        name="moe_experts",
    )(tile_run, run_expert, n_runs, n_tiles, xs, w_gate, w_up, w_down)


def _combine_kernel(dest_ref, y_hbm, x_ref, g_ref, b_ref, o_ref, ybuf, sem):
    i = pl.program_id(0)
    n_steps = pl.num_programs(0)
    slot = i % 2
    tm = COMBINE_TILE

    @pl.when(i == 0)
    def _():
        _row_gather_start(y_hbm, ybuf.at[0], sem.at[0], dest_ref, 0, tm)

    @pl.when(i + 1 < n_steps)
    def _():
        _row_gather_start(y_hbm, ybuf.at[1 - slot], sem.at[1 - slot], dest_ref, (i + 1) * tm, tm)

    _row_gather_wait(y_hbm, ybuf.at[slot], sem.at[slot])
    f = jnp.concatenate([ybuf[slot, pl.ds(j, tm, stride=ROW_CHUNKS), :] for j in range(ROW_CHUNKS)], axis=-1)
    o_ref[...] = _layer_norm(ALPHA * x_ref[...] + f, g_ref[...], b_ref[...])


def _combine(y_rows, x1, dest, g, b):
    n, d = x1.shape
    tm = COMBINE_TILE
    grid_spec = pltpu.PrefetchScalarGridSpec(
        num_scalar_prefetch=1,
        grid=(n // tm,),
        in_specs=[pl.BlockSpec(memory_space=pl.ANY),
                  pl.BlockSpec((tm, d), lambda i, dst: (i, 0)),
                  pl.BlockSpec((1, d), lambda i, dst: (0, 0)),
                  pl.BlockSpec((1, d), lambda i, dst: (0, 0))],
        out_specs=pl.BlockSpec((tm, d), lambda i, dst: (i, 0)),
        scratch_shapes=[pltpu.VMEM((2, tm * ROW_CHUNKS, LANES), F32),
                        pltpu.SemaphoreType.DMA((2,))],
    )
    return pl.pallas_call(
        _combine_kernel,
        grid_spec=grid_spec,
        out_shape=jax.ShapeDtypeStruct((n, d), F32),
        compiler_params=_params(("arbitrary",)),
        name="moe_combine",
    )(dest, y_rows, x1, g, b)


def _dispatch_plan(route, n_tokens):
    tm = EXPERT_TILE
    blk = LANES
    n_rows = n_tokens + N_CLASSES * tm
    hi = lax.Precision.HIGHEST
    onehot = (route[0][:, None] == jnp.arange(N_CLASSES, dtype=F32)[None, :]).astype(F32)
    blocks = onehot.reshape(n_tokens // blk, blk, N_CLASSES)
    tri = np.tril(np.ones((blk, blk), np.float32))
    local = jnp.einsum("ij,bjk->bik", tri, blocks, precision=hi)
    totals = local[:, -1, :]
    strict = np.tril(np.ones((n_tokens // blk,) * 2, np.float32), -1)
    before = jnp.dot(strict, totals, precision=hi)
    rank = jnp.sum((local + before[:, None, :]) * blocks, axis=-1).reshape(n_tokens) - 1.0
    counts = before[-1] + totals[-1]
    padded = jnp.ceil(counts / tm) * tm
    seg_end = jnp.dot(np.tril(np.ones((N_CLASSES,) * 2, np.float32)), padded, precision=hi)
    seg_start = seg_end - padded
    dest = (jnp.sum(onehot * seg_start[None, :], axis=-1) + rank).astype(jnp.int32)
    tile_start = jnp.arange(n_rows // tm, dtype=F32) * tm
    tile_class = jnp.sum((seg_end[None, :] <= tile_start[:, None]).astype(jnp.int32), axis=-1)
    last_class = jnp.max(jnp.where(padded > 0, jnp.arange(N_CLASSES), 0))
    tile_class = jnp.minimum(tile_class, last_class)
    tile_hot = (tile_class[:, None] == np.arange(N_CLASSES)[None, :]).astype(jnp.int32)
    n_pairs = len(PAIR_ORDER)
    expert_of = np.array([[(c // n_pairs) * GROUP_EXPERTS + PAIR_ORDER[c % n_pairs][role]
                           for c in range(N_CLASSES)] for role in range(2)])
    tile_expert = jnp.sum(tile_hot[None, :, :] * expert_of[:, None, :], axis=-1)
    n_all = n_rows // tm
    starts = jnp.concatenate([jnp.ones((2, 1), jnp.int32),
                              (tile_expert[:, 1:] != tile_expert[:, :-1]).astype(jnp.int32)], axis=1)
    tile_run = jnp.cumsum(starts, axis=1) - 1
    run_hot = (tile_run[:, :, None] == np.arange(n_all)[None, None, :]).astype(jnp.int32)
    run_expert = jnp.sum(run_hot * (starts * tile_expert)[:, :, None], axis=1)
    n_runs = tile_run[:, -1] + 1
    n_tiles = (seg_end[-1] / tm).astype(jnp.int32).reshape(1)
    tail_start = jnp.maximum(seg_end - tm, 0.0).astype(jnp.int32)
    plan = (tile_run.astype(jnp.int32), run_expert.astype(jnp.int32), n_runs.astype(jnp.int32), n_tiles)
    return plan, tail_start, dest, n_rows


def _moe(x1, route, w_gate, w_up, w_down, layer, g, b):
    n = x1.shape[0]
    plan, tail_start, dest, n_rows = _dispatch_plan(route, n)
    xs = _dispatch(x1, route, dest, tail_start, plan[-1], n_rows)
    y_rows = _experts(xs, w_gate, w_up, w_down, layer, *plan)
    return _combine(y_rows, x1, dest, g, b)


def _vec(v):
    return v.reshape(1, -1).astype(F32)


def kernel(x, w_in_attn, w_out_attn, rel_bias, rpb_2d, w_pw1, b_pw1, w_dw, b_dw, conv_ln_g, conv_ln_b, w_pw2, b_pw2, ln_mix_g, ln_mix_b, ln_ffn_g, ln_ffn_b, w_router, router_bias, w_gate, w_up, w_down):
    bsz, seq, d = x.shape
    assert d == D_MODEL and seq % SEQ_TILE == 0 and seq % GRID_W == 0, x.shape
    assert w_gate.shape[:2] == (DEPTH, N_EXPERTS) and w_gate.dtype == F32, (w_gate.shape, w_gate.dtype)
    n = bsz * seq
    h = x.reshape(n, d)
    wr_t = w_router.T.astype(F32)
    wr_hi = wr_t.astype(BF16)
    wr_lo = (wr_t - wr_hi.astype(F32)).astype(BF16)
    wr = jnp.concatenate([wr_hi, wr_lo], axis=0)
    rb = router_bias.reshape(N_EXPERTS, 1).astype(F32)
    n_dil_cols = len(DILATIONS) * GROUP_COLS
    q_cols = np.zeros((3 * d,), np.float32) + 1.0
    q_cols[0:n_dil_cols] = HEAD_DIM ** -0.5
    q_cols[3 * n_dil_cols:3 * n_dil_cols + GROUP_COLS] = HEAD_DIM ** -0.5

    for layer in range(DEPTH):
        i = layer // 2
        g_mix, b_mix = _vec(ln_mix_g[layer]), _vec(ln_mix_b[layer])
        if layer % 2 == 0:
            w_in = (w_in_attn[i] * q_cols[None, :]).astype(BF16)
            proj = _matmul(h, w_in)
            lhs = []
            for grp in range(len(DILATIONS)):
                lhs += _dilated_attention(proj, _dilated_bias(rel_bias, grp), grp, bsz, seq)
            lhs.append(_neighborhood_attention(proj, _na_bias(rpb_2d[i]), bsz, seq))
            x1, route = _post_mixer(_post_attn_kernel, "post_attn", lhs,
                                    [w_out_attn[i].astype(BF16)], h, g_mix, b_mix, wr, rb)
        else:
            u = _conv_front(h.reshape(bsz, seq, d), w_pw1[i].astype(BF16), _vec(b_pw1[i]),
                            w_dw[i].astype(F32), _vec(b_dw[i]), _vec(conv_ln_g[i]), _vec(conv_ln_b[i]))
            x1, route = _post_mixer(_post_conv_kernel, "post_conv", [u.reshape(n, d)],
                                    [w_pw2[i].astype(BF16), _vec(b_pw2[i])], h, g_mix, b_mix, wr, rb)
        h = _moe(x1, route, w_gate, w_up, w_down, layer,
                 _vec(ln_ffn_g[layer]), _vec(ln_ffn_b[layer]))
    return h.reshape(bsz, seq, d)
```
